```python
import math
import jax, jax.numpy as jnp
from jax import lax
import numpy as np

D_MODEL = 1024
BATCH = 16
SEQ = 4096
DEPTH = 2

CHUNK = 64
N_LEFT_CHUNKS = 8
D_S5 = D_MODEL // 4
S5_GROUP = 16
S5_GROUPS = D_S5 // S5_GROUP
S5_STATE = 64
D_ATT = D_MODEL // 2
ATT_HEAD_DIM = 64
ATT_HEADS = D_ATT // ATT_HEAD_DIM
MAX_REL = 128
D_CONV = D_MODEL // 4
CONV_WIDTH = 31
N_BRANCHES = 3
D_FF = 2816
EPS = 1e-6
DT_MIN = 1e-3
DT_MAX = 1e-1

SPLIT_POINTS = [D_S5, D_S5 + D_ATT, D_S5 + 2 * D_ATT, D_S5 + 3 * D_ATT,
                D_S5 + 3 * D_ATT + 2 * D_CONV]
IN_COLS = D_S5 + 3 * D_ATT + 2 * D_CONV + N_BRANCHES * D_MODEL

kernel_name = "hybrid_s5_chunkattn_conformer_conv_gated"


def rmsnorm(x, g):
    xf = x.astype(jnp.float32)
    y = xf * lax.rsqrt(jnp.mean(xf * xf, axis=-1, keepdims=True) + EPS)
    return (y * g.astype(jnp.float32)).astype(x.dtype)


def layernorm(x, g, b):
    xf = x.astype(jnp.float32)
    mu = jnp.mean(xf, axis=-1, keepdims=True)
    var = jnp.mean(jnp.square(xf - mu), axis=-1, keepdims=True)
    y = (xf - mu) * lax.rsqrt(var + EPS)
    return (y * g.astype(jnp.float32) + b.astype(jnp.float32)).astype(x.dtype)


def swiglu_ffn(x, w_up, w_down):
    a, b = jnp.split(x @ w_up, 2, axis=-1)
    return (jax.nn.silu(a) * b) @ w_down


def _complex_linear_combine(e1, e2):
    a1r, a1i, b1r, b1i = e1
    a2r, a2i, b2r, b2i = e2
    ar = a2r * a1r - a2i * a1i
    ai = a2r * a1i + a2i * a1r
    br = a2r * b1r - a2i * b1i + b2r
    bi = a2r * b1i + a2i * b1r + b2i
    return (ar, ai, br, bi)


def s5_mixer(u, lambda_re, lambda_im, log_dt, b_re, b_im, c_re, c_im, d_skip, w_glu):
    bsz, seq, _ = u.shape
    f32 = jnp.float32
    uf = u.astype(f32).reshape(bsz, seq, S5_GROUPS, S5_GROUP)
    lr = jnp.minimum(lambda_re.astype(f32), -1e-4)
    li = lambda_im.astype(f32)
    dt = jnp.exp(log_dt.astype(f32))[:, None]
    mag = jnp.exp(lr * dt)
    ar = mag * jnp.cos(li * dt)
    ai = mag * jnp.sin(li * dt)
    den = lr * lr + li * li
    coef_r = ((ar - 1.0) * lr + ai * li) / den
    coef_i = (ai * lr - (ar - 1.0) * li) / den
    br = b_re.astype(f32)
    bi = b_im.astype(f32)
    bbar_r = coef_r[..., None] * br - coef_i[..., None] * bi
    bbar_i = coef_r[..., None] * bi + coef_i[..., None] * br
    bu_r = jnp.einsum('bsgc,gpc->bsgp', uf, bbar_r)
    bu_i = jnp.einsum('bsgc,gpc->bsgp', uf, bbar_i)
    a_r = jnp.broadcast_to(ar, bu_r.shape)
    a_i = jnp.broadcast_to(ai, bu_i.shape)
    _, _, xr, xi = lax.associative_scan(_complex_linear_combine, (a_r, a_i, bu_r, bu_i), axis=1)
    y = (jnp.einsum('bsgp,gcp->bsgc', xr, c_re.astype(f32))
         - jnp.einsum('bsgp,gcp->bsgc', xi, c_im.astype(f32))
         + d_skip.astype(f32).reshape(S5_GROUPS, S5_GROUP) * uf)
    y = jax.nn.gelu(y.reshape(bsz, seq, D_S5)).astype(u.dtype)
    a, g = jnp.split(y @ w_glu, 2, axis=-1)
    return a * jax.nn.sigmoid(g)


def chunked_attention(q, k, v, q_gain, k_gain, rel_bias):
    bsz, seq, _ = q.shape
    n_chunks = seq // CHUNK
    q = rmsnorm(q.reshape(bsz, seq, ATT_HEADS, ATT_HEAD_DIM), q_gain)
    k = rmsnorm(k.reshape(bsz, seq, ATT_HEADS, ATT_HEAD_DIM), k_gain)
    v = v.reshape(bsz, seq, ATT_HEADS, ATT_HEAD_DIM)
    pad = N_LEFT_CHUNKS * CHUNK
    band = (N_LEFT_CHUNKS + 1) * CHUNK
    k_pad = jnp.pad(k, ((0, 0), (pad, 0), (0, 0), (0, 0)))
    v_pad = jnp.pad(v, ((0, 0), (pad, 0), (0, 0), (0, 0)))
    qi = jnp.arange(CHUNK)[:, None]
    kj = jnp.arange(band)[None, :]
    rel = jnp.clip(pad + qi - kj, -MAX_REL, MAX_REL) + MAX_REL
    bias = rel_bias.astype(jnp.float32)[:, rel]
    scale = ATT_HEAD_DIM ** -0.5

    def one_chunk(c):
        start = c * CHUNK
        qc = lax.dynamic_slice_in_dim(q, start, CHUNK, axis=1)
        kc = lax.dynamic_slice_in_dim(k_pad, start, band, axis=1)
        vc = lax.dynamic_slice_in_dim(v_pad, start, band, axis=1)
        s = jnp.einsum('bqhd,bkhd->bhqk', qc, kc).astype(jnp.float32) * scale + bias[None]
        valid = kj >= (pad - start)
        s = jnp.where(valid[None, None], s, -1e30)
        p = jax.nn.softmax(s, axis=-1).astype(vc.dtype)
        return jnp.einsum('bhqk,bkhd->bqhd', p, vc)

    out = lax.map(one_chunk, jnp.arange(n_chunks))
    return out.transpose(1, 0, 2, 3, 4).reshape(bsz, seq, D_ATT)


def conv_module(z, w_dw, b_dw, ln_g, ln_b, w_pw):
    a, g = jnp.split(z, 2, axis=-1)
    h = a * jax.nn.sigmoid(g)
    h = lax.conv_general_dilated(
        h, w_dw[:, None, :], window_strides=(1,), padding=[(CONV_WIDTH - 1, 0)],
        dimension_numbers=('NWC', 'WIO', 'NWC'), feature_group_count=D_CONV) + b_dw
    h = jax.nn.silu(layernorm(h, ln_g, ln_b))
    return h @ w_pw


def _fwd_setup_inputs(seed: int = 0) -> dict:
    key = jax.random.key(seed)
    ks = jax.random.split(key, 32)
    L = DEPTH

    def nrm(k, shape, scale):
        return jax.random.normal(k, shape, jnp.float32) * scale

    n_idx = jnp.arange(S5_STATE, dtype=jnp.float32)
    lam_im = jnp.broadcast_to(math.pi * n_idx, (L, S5_GROUPS, S5_STATE))
    return {
        "x": nrm(ks[0], (BATCH, SEQ, D_MODEL), 1.0),
        "ffn1_norm": 1.0 + nrm(ks[1], (L, D_MODEL), 0.02),
        "ffn1_w_up": nrm(ks[2], (L, D_MODEL, 2 * D_FF), D_MODEL ** -0.5),
        "ffn1_w_down": nrm(ks[3], (L, D_FF, D_MODEL), D_FF ** -0.5),
        "mix_norm": 1.0 + nrm(ks[4], (L, D_MODEL), 0.02),
        "w_in": nrm(ks[5], (L, D_MODEL, IN_COLS), D_MODEL ** -0.5),
        "b_gate": nrm(ks[6], (L, N_BRANCHES * D_MODEL), 0.01),
        "s5_lambda_re": -0.5 + nrm(ks[7], (L, S5_GROUPS, S5_STATE), 0.01),
        "s5_lambda_im": lam_im + nrm(ks[8], (L, S5_GROUPS, S5_STATE), 0.01),
        "s5_log_dt": jax.random.uniform(ks[9], (L, S5_GROUPS), jnp.float32,
                                        math.log(DT_MIN), math.log(DT_MAX)),
        "s5_b_re": nrm(ks[10], (L, S5_GROUPS, S5_STATE, S5_GROUP), (2 * S5_GROUP) ** -0.5),
        "s5_b_im": nrm(ks[11], (L, S5_GROUPS, S5_STATE, S5_GROUP), (2 * S5_GROUP) ** -0.5),
        "s5_c_re": nrm(ks[12], (L, S5_GROUPS, S5_GROUP, S5_STATE), (2 * S5_STATE) ** -0.5),
        "s5_c_im": nrm(ks[13], (L, S5_GROUPS, S5_GROUP, S5_STATE), (2 * S5_STATE) ** -0.5),
        "s5_d": nrm(ks[14], (L, D_S5), 1.0),
        "s5_w_glu": nrm(ks[15], (L, D_S5, 2 * D_S5), D_S5 ** -0.5),
        "w_br_s5": nrm(ks[16], (L, D_S5, D_MODEL), D_S5 ** -0.5),
        "attn_q_gain": 1.0 + nrm(ks[17], (L, ATT_HEAD_DIM), 0.02),
        "attn_k_gain": 1.0 + nrm(ks[18], (L, ATT_HEAD_DIM), 0.02),
        "attn_rel_bias": nrm(ks[19], (L, ATT_HEADS, 2 * MAX_REL + 1), 0.1),
        "w_br_attn": nrm(ks[20], (L, D_ATT, D_MODEL), D_ATT ** -0.5),
        "conv_w_dw": nrm(ks[21], (L, CONV_WIDTH, D_CONV), CONV_WIDTH ** -0.5),
        "conv_b_dw": nrm(ks[22], (L, D_CONV), 0.01),
        "conv_ln_g": 1.0 + nrm(ks[23], (L, D_CONV), 0.02),
        "conv_ln_b": nrm(ks[24], (L, D_CONV), 0.01),
        "w_br_conv": nrm(ks[25], (L, D_CONV, D_MODEL), D_CONV ** -0.5),
        "w_out": nrm(ks[26], (L, D_MODEL, D_MODEL), D_MODEL ** -0.5),
        "ffn2_norm": 1.0 + nrm(ks[27], (L, D_MODEL), 0.02),
        "ffn2_w_up": nrm(ks[28], (L, D_MODEL, 2 * D_FF), D_MODEL ** -0.5),
        "ffn2_w_down": nrm(ks[29], (L, D_FF, D_MODEL), D_FF ** -0.5),
    }


def _fwd_reference(x, ffn1_norm, ffn1_w_up, ffn1_w_down, mix_norm, w_in, b_gate,
              s5_lambda_re, s5_lambda_im, s5_log_dt, s5_b_re, s5_b_im, s5_c_re, s5_c_im,
              s5_d, s5_w_glu, w_br_s5, attn_q_gain, attn_k_gain, attn_rel_bias, w_br_attn,
              conv_w_dw, conv_b_dw, conv_ln_g, conv_ln_b, w_br_conv, w_out,
              ffn2_norm, ffn2_w_up, ffn2_w_down):
    for l in range(DEPTH):
        x = x + 0.5 * swiglu_ffn(rmsnorm(x, ffn1_norm[l]), ffn1_w_up[l], ffn1_w_down[l])

        h = rmsnorm(x, mix_norm[l])
        proj = h @ w_in[l]
        u_s5, q, k, v, z_conv, gate_logits = jnp.split(proj, SPLIT_POINTS, axis=-1)

        y_s5 = s5_mixer(u_s5, s5_lambda_re[l], s5_lambda_im[l], s5_log_dt[l],
                        s5_b_re[l], s5_b_im[l], s5_c_re[l], s5_c_im[l],
                        s5_d[l], s5_w_glu[l]) @ w_br_s5[l]
        y_attn = chunked_attention(q, k, v, attn_q_gain[l], attn_k_gain[l],
                                   attn_rel_bias[l]) @ w_br_attn[l]
        y_conv = conv_module(z_conv, conv_w_dw[l], conv_b_dw[l], conv_ln_g[l],
                             conv_ln_b[l], w_br_conv[l])

        gates = jax.nn.sigmoid(gate_logits + b_gate[l])
        g_s5, g_attn, g_conv = jnp.split(gates, N_BRANCHES, axis=-1)
        merged = g_s5 * y_s5 + g_attn * y_attn + g_conv * y_conv
        x = x + merged @ w_out[l]

        x = x + 0.5 * swiglu_ffn(rmsnorm(x, ffn2_norm[l]), ffn2_w_up[l], ffn2_w_down[l])
    return x


import jax as _jax
import jax.numpy as _jnp

TWIN_FORMAT = 'train_step'
FWD_PARAMS = ['x', 'ffn1_norm', 'ffn1_w_up', 'ffn1_w_down', 'mix_norm', 'w_in', 'b_gate', 's5_lambda_re', 's5_lambda_im', 's5_log_dt', 's5_b_re', 's5_b_im', 's5_c_re', 's5_c_im', 's5_d', 's5_w_glu', 'w_br_s5', 'attn_q_gain', 'attn_k_gain', 'attn_rel_bias', 'w_br_attn', 'conv_w_dw', 'conv_b_dw', 'conv_ln_g', 'conv_ln_b', 'w_br_conv', 'w_out', 'ffn2_norm', 'ffn2_w_up', 'ffn2_w_down']
TWIN_WEIGHTS = ['ffn1_norm', 'ffn1_w_up', 'ffn1_w_down', 'mix_norm', 'w_in', 'b_gate', 's5_lambda_re', 's5_lambda_im', 's5_log_dt', 's5_b_re', 's5_b_im', 's5_c_re', 's5_c_im', 's5_d', 's5_w_glu', 'w_br_s5', 'attn_q_gain', 'attn_k_gain', 'attn_rel_bias', 'w_br_attn', 'conv_w_dw', 'conv_b_dw', 'conv_ln_g', 'conv_ln_b', 'w_br_conv', 'w_out', 'ffn2_norm', 'ffn2_w_up', 'ffn2_w_down']
TWIN_DIFF_INPUT = 'x'
TWIN_INPUTS = ['x', 'ffn1_norm', 'ffn1_w_up', 'ffn1_w_down', 'mix_norm', 'w_in', 'b_gate', 's5_lambda_re', 's5_lambda_im', 's5_log_dt', 's5_b_re', 's5_b_im', 's5_c_re', 's5_c_im', 's5_d', 's5_w_glu', 'w_br_s5', 'attn_q_gain', 'attn_k_gain', 'attn_rel_bias', 'w_br_attn', 'conv_w_dw', 'conv_b_dw', 'conv_ln_g', 'conv_ln_b', 'w_br_conv', 'w_out', 'ffn2_norm', 'ffn2_w_up', 'ffn2_w_down', 'loss_target', 'm_ffn1_norm', 'm_ffn1_w_up', 'm_ffn1_w_down', 'm_mix_norm', 'm_w_in', 'm_b_gate', 'm_s5_lambda_re', 'm_s5_lambda_im', 'm_s5_log_dt', 'm_s5_b_re', 'm_s5_b_im', 'm_s5_c_re', 'm_s5_c_im', 'm_s5_d', 'm_s5_w_glu', 'm_w_br_s5', 'm_attn_q_gain', 'm_attn_k_gain', 'm_attn_rel_bias', 'm_w_br_attn', 'm_conv_w_dw', 'm_conv_b_dw', 'm_conv_ln_g', 'm_conv_ln_b', 'm_w_br_conv', 'm_w_out', 'm_ffn2_norm', 'm_ffn2_w_up', 'm_ffn2_w_down', 'v_ffn1_norm', 'v_ffn1_w_up', 'v_ffn1_w_down', 'v_mix_norm', 'v_w_in', 'v_b_gate', 'v_s5_lambda_re', 'v_s5_lambda_im', 'v_s5_log_dt', 'v_s5_b_re', 'v_s5_b_im', 'v_s5_c_re', 'v_s5_c_im', 'v_s5_d', 'v_s5_w_glu', 'v_w_br_s5', 'v_attn_q_gain', 'v_attn_k_gain', 'v_attn_rel_bias', 'v_w_br_attn', 'v_conv_w_dw', 'v_conv_b_dw', 'v_conv_ln_g', 'v_conv_ln_b', 'v_w_br_conv', 'v_w_out', 'v_ffn2_norm', 'v_ffn2_w_up', 'v_ffn2_w_down']
TWIN_OUTPUTS = ['loss', 'grad_x', 'grad_ffn1_norm', 'grad_ffn1_w_up', 'grad_ffn1_w_down', 'grad_mix_norm', 'grad_w_in', 'grad_b_gate', 'grad_s5_lambda_re', 'grad_s5_lambda_im', 'grad_s5_log_dt', 'grad_s5_b_re', 'grad_s5_b_im', 'grad_s5_c_re', 'grad_s5_c_im', 'grad_s5_d', 'grad_s5_w_glu', 'grad_w_br_s5', 'grad_attn_q_gain', 'grad_attn_k_gain', 'grad_attn_rel_bias', 'grad_w_br_attn', 'grad_conv_w_dw', 'grad_conv_b_dw', 'grad_conv_ln_g', 'grad_conv_ln_b', 'grad_w_br_conv', 'grad_w_out', 'grad_ffn2_norm', 'grad_ffn2_w_up', 'grad_ffn2_w_down', 'delta_ffn1_norm', 'delta_ffn1_w_up', 'delta_ffn1_w_down', 'delta_mix_norm', 'delta_w_in', 'delta_b_gate', 'delta_s5_lambda_re', 'delta_s5_lambda_im', 'delta_s5_log_dt', 'delta_s5_b_re', 'delta_s5_b_im', 'delta_s5_c_re', 'delta_s5_c_im', 'delta_s5_d', 'delta_s5_w_glu', 'delta_w_br_s5', 'delta_attn_q_gain', 'delta_attn_k_gain', 'delta_attn_rel_bias', 'delta_w_br_attn', 'delta_conv_w_dw', 'delta_conv_b_dw', 'delta_conv_ln_g', 'delta_conv_ln_b', 'delta_w_br_conv', 'delta_w_out', 'delta_ffn2_norm', 'delta_ffn2_w_up', 'delta_ffn2_w_down', 'new_m_ffn1_norm', 'new_m_ffn1_w_up', 'new_m_ffn1_w_down', 'new_m_mix_norm', 'new_m_w_in', 'new_m_b_gate', 'new_m_s5_lambda_re', 'new_m_s5_lambda_im', 'new_m_s5_log_dt', 'new_m_s5_b_re', 'new_m_s5_b_im', 'new_m_s5_c_re', 'new_m_s5_c_im', 'new_m_s5_d', 'new_m_s5_w_glu', 'new_m_w_br_s5', 'new_m_attn_q_gain', 'new_m_attn_k_gain', 'new_m_attn_rel_bias', 'new_m_w_br_attn', 'new_m_conv_w_dw', 'new_m_conv_b_dw', 'new_m_conv_ln_g', 'new_m_conv_ln_b', 'new_m_w_br_conv', 'new_m_w_out', 'new_m_ffn2_norm', 'new_m_ffn2_w_up', 'new_m_ffn2_w_down', 'new_v_ffn1_norm', 'new_v_ffn1_w_up', 'new_v_ffn1_w_down', 'new_v_mix_norm', 'new_v_w_in', 'new_v_b_gate', 'new_v_s5_lambda_re', 'new_v_s5_lambda_im', 'new_v_s5_log_dt', 'new_v_s5_b_re', 'new_v_s5_b_im', 'new_v_s5_c_re', 'new_v_s5_c_im', 'new_v_s5_d', 'new_v_s5_w_glu', 'new_v_w_br_s5', 'new_v_attn_q_gain', 'new_v_attn_k_gain', 'new_v_attn_rel_bias', 'new_v_w_br_attn', 'new_v_conv_w_dw', 'new_v_conv_b_dw', 'new_v_conv_ln_g', 'new_v_conv_ln_b', 'new_v_w_br_conv', 'new_v_w_out', 'new_v_ffn2_norm', 'new_v_ffn2_w_up', 'new_v_ffn2_w_down']
TWIN_LEAF_KINDS = {'loss': 'loss', 'grad_x': 'grad_x', 'grad_ffn1_norm': 'grad_w', 'grad_ffn1_w_up': 'grad_w', 'grad_ffn1_w_down': 'grad_w', 'grad_mix_norm': 'grad_w', 'grad_w_in': 'grad_w', 'grad_b_gate': 'grad_w', 'grad_s5_lambda_re': 'grad_w', 'grad_s5_lambda_im': 'grad_w', 'grad_s5_log_dt': 'grad_w', 'grad_s5_b_re': 'grad_w', 'grad_s5_b_im': 'grad_w', 'grad_s5_c_re': 'grad_w', 'grad_s5_c_im': 'grad_w', 'grad_s5_d': 'grad_w', 'grad_s5_w_glu': 'grad_w', 'grad_w_br_s5': 'grad_w', 'grad_attn_q_gain': 'grad_w', 'grad_attn_k_gain': 'grad_w', 'grad_attn_rel_bias': 'grad_w', 'grad_w_br_attn': 'grad_w', 'grad_conv_w_dw': 'grad_w', 'grad_conv_b_dw': 'grad_w', 'grad_conv_ln_g': 'grad_w', 'grad_conv_ln_b': 'grad_w', 'grad_w_br_conv': 'grad_w', 'grad_w_out': 'grad_w', 'grad_ffn2_norm': 'grad_w', 'grad_ffn2_w_up': 'grad_w', 'grad_ffn2_w_down': 'grad_w', 'delta_ffn1_norm': 'delta_w', 'delta_ffn1_w_up': 'delta_w', 'delta_ffn1_w_down': 'delta_w', 'delta_mix_norm': 'delta_w', 'delta_w_in': 'delta_w', 'delta_b_gate': 'delta_w', 'delta_s5_lambda_re': 'delta_w', 'delta_s5_lambda_im': 'delta_w', 'delta_s5_log_dt': 'delta_w', 'delta_s5_b_re': 'delta_w', 'delta_s5_b_im': 'delta_w', 'delta_s5_c_re': 'delta_w', 'delta_s5_c_im': 'delta_w', 'delta_s5_d': 'delta_w', 'delta_s5_w_glu': 'delta_w', 'delta_w_br_s5': 'delta_w', 'delta_attn_q_gain': 'delta_w', 'delta_attn_k_gain': 'delta_w', 'delta_attn_rel_bias': 'delta_w', 'delta_w_br_attn': 'delta_w', 'delta_conv_w_dw': 'delta_w', 'delta_conv_b_dw': 'delta_w', 'delta_conv_ln_g': 'delta_w', 'delta_conv_ln_b': 'delta_w', 'delta_w_br_conv': 'delta_w', 'delta_w_out': 'delta_w', 'delta_ffn2_norm': 'delta_w', 'delta_ffn2_w_up': 'delta_w', 'delta_ffn2_w_down': 'delta_w', 'new_m_ffn1_norm': 'new_m', 'new_m_ffn1_w_up': 'new_m', 'new_m_ffn1_w_down': 'new_m', 'new_m_mix_norm': 'new_m', 'new_m_w_in': 'new_m', 'new_m_b_gate': 'new_m', 'new_m_s5_lambda_re': 'new_m', 'new_m_s5_lambda_im': 'new_m', 'new_m_s5_log_dt': 'new_m', 'new_m_s5_b_re': 'new_m', 'new_m_s5_b_im': 'new_m', 'new_m_s5_c_re': 'new_m', 'new_m_s5_c_im': 'new_m', 'new_m_s5_d': 'new_m', 'new_m_s5_w_glu': 'new_m', 'new_m_w_br_s5': 'new_m', 'new_m_attn_q_gain': 'new_m', 'new_m_attn_k_gain': 'new_m', 'new_m_attn_rel_bias': 'new_m', 'new_m_w_br_attn': 'new_m', 'new_m_conv_w_dw': 'new_m', 'new_m_conv_b_dw': 'new_m', 'new_m_conv_ln_g': 'new_m', 'new_m_conv_ln_b': 'new_m', 'new_m_w_br_conv': 'new_m', 'new_m_w_out': 'new_m', 'new_m_ffn2_norm': 'new_m', 'new_m_ffn2_w_up': 'new_m', 'new_m_ffn2_w_down': 'new_m', 'new_v_ffn1_norm': 'new_v', 'new_v_ffn1_w_up': 'new_v', 'new_v_ffn1_w_down': 'new_v', 'new_v_mix_norm': 'new_v', 'new_v_w_in': 'new_v', 'new_v_b_gate': 'new_v', 'new_v_s5_lambda_re': 'new_v', 'new_v_s5_lambda_im': 'new_v', 'new_v_s5_log_dt': 'new_v', 'new_v_s5_b_re': 'new_v', 'new_v_s5_b_im': 'new_v', 'new_v_s5_c_re': 'new_v', 'new_v_s5_c_im': 'new_v', 'new_v_s5_d': 'new_v', 'new_v_s5_w_glu': 'new_v', 'new_v_w_br_s5': 'new_v', 'new_v_attn_q_gain': 'new_v', 'new_v_attn_k_gain': 'new_v', 'new_v_attn_rel_bias': 'new_v', 'new_v_w_br_attn': 'new_v', 'new_v_conv_w_dw': 'new_v', 'new_v_conv_b_dw': 'new_v', 'new_v_conv_ln_g': 'new_v', 'new_v_conv_ln_b': 'new_v', 'new_v_w_br_conv': 'new_v', 'new_v_w_out': 'new_v', 'new_v_ffn2_norm': 'new_v', 'new_v_ffn2_w_up': 'new_v', 'new_v_ffn2_w_down': 'new_v'}


def _forward(args):
    return _fwd_reference(*[args[k] for k in FWD_PARAMS])


def _output_shape():
    out = _jax.eval_shape(lambda: _forward(_fwd_setup_inputs(0)))
    return out.shape, out.dtype

N_MICROBATCH = 1
ADAM_LR = 0.001
ADAM_B1 = 0.9
ADAM_B2 = 0.999
ADAM_EPS = 1e-08
ADAM_WD = 0.01
ADAM_STEP = 10
PER_EXAMPLE_BATCH_AXIS = {'x': 0, 'loss_target': 0}
SHARED_INPUTS = []
_WEIGHT_DTYPES = {'ffn1_norm': _jnp.float32, 'ffn1_w_up': _jnp.float32, 'ffn1_w_down': _jnp.float32, 'mix_norm': _jnp.float32, 'w_in': _jnp.float32, 'b_gate': _jnp.float32, 's5_lambda_re': _jnp.float32, 's5_lambda_im': _jnp.float32, 's5_log_dt': _jnp.float32, 's5_b_re': _jnp.float32, 's5_b_im': _jnp.float32, 's5_c_re': _jnp.float32, 's5_c_im': _jnp.float32, 's5_d': _jnp.float32, 's5_w_glu': _jnp.float32, 'w_br_s5': _jnp.float32, 'attn_q_gain': _jnp.float32, 'attn_k_gain': _jnp.float32, 'attn_rel_bias': _jnp.float32, 'w_br_attn': _jnp.float32, 'conv_w_dw': _jnp.float32, 'conv_b_dw': _jnp.float32, 'conv_ln_g': _jnp.float32, 'conv_ln_b': _jnp.float32, 'w_br_conv': _jnp.float32, 'w_out': _jnp.float32, 'ffn2_norm': _jnp.float32, 'ffn2_w_up': _jnp.float32, 'ffn2_w_down': _jnp.float32}
MOMENT_SCALE = {'ffn1_norm': 1.220240e+01, 'ffn1_w_up': 1.318879e-01, 'ffn1_w_down': 2.208108e-01, 'mix_norm': 3.265124e+00, 'w_in': 2.737871e-01, 'b_gate': 1.560869e+00, 's5_lambda_re': 6.395572e-02, 's5_lambda_im': 6.035991e-02, 's5_log_dt': 1.001725e+01, 's5_b_re': 4.802805e-02, 's5_b_im': 4.215301e-02, 's5_c_re': 7.599448e-02, 's5_c_im': 1.038016e-01, 's5_d': 1.146364e+01, 's5_w_glu': 2.751320e+00, 'w_br_s5': 1.469642e+00, 'attn_q_gain': 6.537019e-01, 'attn_k_gain': 6.530536e-01, 'attn_rel_bias': 2.608844e-02, 'w_br_attn': 4.927421e-01, 'conv_w_dw': 9.048759e-01, 'conv_b_dw': 1.736186e+01, 'conv_ln_g': 3.451271e+01, 'conv_ln_b': 2.438086e+01, 'w_br_conv': 1.826038e+00, 'w_out': 1.818113e+00, 'ffn2_norm': 1.228363e+01, 'ffn2_w_up': 1.512978e-01, 'ffn2_w_down': 2.332623e-01}


def _to_microbatches(a, axis):
    t = _jnp.moveaxis(a, axis, 0)
    t = t.reshape((N_MICROBATCH, t.shape[0] // N_MICROBATCH) + t.shape[1:])
    return _jnp.moveaxis(t, 1, axis + 1)


def setup_inputs(seed: int = 0) -> dict:
    inp = _fwd_setup_inputs(seed)
    key = _jax.random.fold_in(_jax.random.key(seed), 7919)
    shape, _ = _output_shape()
    out = dict(inp)
    out["loss_target"] = _jax.random.normal(_jax.random.fold_in(key, 0), shape, _jnp.float32)
    for i, name in enumerate(TWIN_WEIGHTS):
        w = inp[name].astype(_jnp.float32)
        if MOMENT_SCALE is None:
            s = _jnp.sqrt(_jnp.mean(_jnp.square(w)) + 1e-30)
        else:
            s = MOMENT_SCALE[name]
        km, kv = _jax.random.split(_jax.random.fold_in(key, i + 1))
        out[name] = w
        out["m_" + name] = s * _jax.random.normal(km, w.shape, _jnp.float32)
        out["v_" + name] = (s * s) * _jax.random.uniform(kv, w.shape, _jnp.float32, 0.5, 1.5)
    if N_MICROBATCH > 1:
        for name, axis in PER_EXAMPLE_BATCH_AXIS.items():
            out[name] = _to_microbatches(out[name], axis)
    return {'x': out['x'], 'ffn1_norm': out['ffn1_norm'], 'ffn1_w_up': out['ffn1_w_up'], 'ffn1_w_down': out['ffn1_w_down'], 'mix_norm': out['mix_norm'], 'w_in': out['w_in'], 'b_gate': out['b_gate'], 's5_lambda_re': out['s5_lambda_re'], 's5_lambda_im': out['s5_lambda_im'], 's5_log_dt': out['s5_log_dt'], 's5_b_re': out['s5_b_re'], 's5_b_im': out['s5_b_im'], 's5_c_re': out['s5_c_re'], 's5_c_im': out['s5_c_im'], 's5_d': out['s5_d'], 's5_w_glu': out['s5_w_glu'], 'w_br_s5': out['w_br_s5'], 'attn_q_gain': out['attn_q_gain'], 'attn_k_gain': out['attn_k_gain'], 'attn_rel_bias': out['attn_rel_bias'], 'w_br_attn': out['w_br_attn'], 'conv_w_dw': out['conv_w_dw'], 'conv_b_dw': out['conv_b_dw'], 'conv_ln_g': out['conv_ln_g'], 'conv_ln_b': out['conv_ln_b'], 'w_br_conv': out['w_br_conv'], 'w_out': out['w_out'], 'ffn2_norm': out['ffn2_norm'], 'ffn2_w_up': out['ffn2_w_up'], 'ffn2_w_down': out['ffn2_w_down'], 'loss_target': out['loss_target'], 'm_ffn1_norm': out['m_ffn1_norm'], 'm_ffn1_w_up': out['m_ffn1_w_up'], 'm_ffn1_w_down': out['m_ffn1_w_down'], 'm_mix_norm': out['m_mix_norm'], 'm_w_in': out['m_w_in'], 'm_b_gate': out['m_b_gate'], 'm_s5_lambda_re': out['m_s5_lambda_re'], 'm_s5_lambda_im': out['m_s5_lambda_im'], 'm_s5_log_dt': out['m_s5_log_dt'], 'm_s5_b_re': out['m_s5_b_re'], 'm_s5_b_im': out['m_s5_b_im'], 'm_s5_c_re': out['m_s5_c_re'], 'm_s5_c_im': out['m_s5_c_im'], 'm_s5_d': out['m_s5_d'], 'm_s5_w_glu': out['m_s5_w_glu'], 'm_w_br_s5': out['m_w_br_s5'], 'm_attn_q_gain': out['m_attn_q_gain'], 'm_attn_k_gain': out['m_attn_k_gain'], 'm_attn_rel_bias': out['m_attn_rel_bias'], 'm_w_br_attn': out['m_w_br_attn'], 'm_conv_w_dw': out['m_conv_w_dw'], 'm_conv_b_dw': out['m_conv_b_dw'], 'm_conv_ln_g': out['m_conv_ln_g'], 'm_conv_ln_b': out['m_conv_ln_b'], 'm_w_br_conv': out['m_w_br_conv'], 'm_w_out': out['m_w_out'], 'm_ffn2_norm': out['m_ffn2_norm'], 'm_ffn2_w_up': out['m_ffn2_w_up'], 'm_ffn2_w_down': out['m_ffn2_w_down'], 'v_ffn1_norm': out['v_ffn1_norm'], 'v_ffn1_w_up': out['v_ffn1_w_up'], 'v_ffn1_w_down': out['v_ffn1_w_down'], 'v_mix_norm': out['v_mix_norm'], 'v_w_in': out['v_w_in'], 'v_b_gate': out['v_b_gate'], 'v_s5_lambda_re': out['v_s5_lambda_re'], 'v_s5_lambda_im': out['v_s5_lambda_im'], 'v_s5_log_dt': out['v_s5_log_dt'], 'v_s5_b_re': out['v_s5_b_re'], 'v_s5_b_im': out['v_s5_b_im'], 'v_s5_c_re': out['v_s5_c_re'], 'v_s5_c_im': out['v_s5_c_im'], 'v_s5_d': out['v_s5_d'], 'v_s5_w_glu': out['v_s5_w_glu'], 'v_w_br_s5': out['v_w_br_s5'], 'v_attn_q_gain': out['v_attn_q_gain'], 'v_attn_k_gain': out['v_attn_k_gain'], 'v_attn_rel_bias': out['v_attn_rel_bias'], 'v_w_br_attn': out['v_w_br_attn'], 'v_conv_w_dw': out['v_conv_w_dw'], 'v_conv_b_dw': out['v_conv_b_dw'], 'v_conv_ln_g': out['v_conv_ln_g'], 'v_conv_ln_b': out['v_conv_ln_b'], 'v_w_br_conv': out['v_w_br_conv'], 'v_w_out': out['v_w_out'], 'v_ffn2_norm': out['v_ffn2_norm'], 'v_ffn2_w_up': out['v_ffn2_w_up'], 'v_ffn2_w_down': out['v_ffn2_w_down']}


def _loss(weights, diff, rest, loss_target):
    with _jax.named_scope("forward"):
        args = {**rest, TWIN_DIFF_INPUT: diff, **{k: w.astype(_WEIGHT_DTYPES[k]) for k, w in weights.items()}}
        y = _forward(args)
    with _jax.named_scope("loss_head"):
        err = _jnp.square(y.astype(_jnp.float32) - loss_target)
        return 0.5 * _jnp.sum(_jnp.mean(err, axis=-1)) if err.ndim else 0.5 * err


def _adamw(w, g, m, v):
    m = ADAM_B1 * m + (1.0 - ADAM_B1) * g
    v = ADAM_B2 * v + (1.0 - ADAM_B2) * _jnp.square(g)
    m_hat = m / (1.0 - ADAM_B1 ** ADAM_STEP)
    v_hat = v / (1.0 - ADAM_B2 ** ADAM_STEP)
    delta = -ADAM_LR * (m_hat / (_jnp.sqrt(v_hat) + ADAM_EPS) + ADAM_WD * w)
    return delta, m, v


def reference(x, ffn1_norm, ffn1_w_up, ffn1_w_down, mix_norm, w_in, b_gate, s5_lambda_re, s5_lambda_im, s5_log_dt, s5_b_re, s5_b_im, s5_c_re, s5_c_im, s5_d, s5_w_glu, w_br_s5, attn_q_gain, attn_k_gain, attn_rel_bias, w_br_attn, conv_w_dw, conv_b_dw, conv_ln_g, conv_ln_b, w_br_conv, w_out, ffn2_norm, ffn2_w_up, ffn2_w_down, loss_target, m_ffn1_norm, m_ffn1_w_up, m_ffn1_w_down, m_mix_norm, m_w_in, m_b_gate, m_s5_lambda_re, m_s5_lambda_im, m_s5_log_dt, m_s5_b_re, m_s5_b_im, m_s5_c_re, m_s5_c_im, m_s5_d, m_s5_w_glu, m_w_br_s5, m_attn_q_gain, m_attn_k_gain, m_attn_rel_bias, m_w_br_attn, m_conv_w_dw, m_conv_b_dw, m_conv_ln_g, m_conv_ln_b, m_w_br_conv, m_w_out, m_ffn2_norm, m_ffn2_w_up, m_ffn2_w_down, v_ffn1_norm, v_ffn1_w_up, v_ffn1_w_down, v_mix_norm, v_w_in, v_b_gate, v_s5_lambda_re, v_s5_lambda_im, v_s5_log_dt, v_s5_b_re, v_s5_b_im, v_s5_c_re, v_s5_c_im, v_s5_d, v_s5_w_glu, v_w_br_s5, v_attn_q_gain, v_attn_k_gain, v_attn_rel_bias, v_w_br_attn, v_conv_w_dw, v_conv_b_dw, v_conv_ln_g, v_conv_ln_b, v_w_br_conv, v_w_out, v_ffn2_norm, v_ffn2_w_up, v_ffn2_w_down):
    given = dict(x=x, ffn1_norm=ffn1_norm, ffn1_w_up=ffn1_w_up, ffn1_w_down=ffn1_w_down, mix_norm=mix_norm, w_in=w_in, b_gate=b_gate, s5_lambda_re=s5_lambda_re, s5_lambda_im=s5_lambda_im, s5_log_dt=s5_log_dt, s5_b_re=s5_b_re, s5_b_im=s5_b_im, s5_c_re=s5_c_re, s5_c_im=s5_c_im, s5_d=s5_d, s5_w_glu=s5_w_glu, w_br_s5=w_br_s5, attn_q_gain=attn_q_gain, attn_k_gain=attn_k_gain, attn_rel_bias=attn_rel_bias, w_br_attn=w_br_attn, conv_w_dw=conv_w_dw, conv_b_dw=conv_b_dw, conv_ln_g=conv_ln_g, conv_ln_b=conv_ln_b, w_br_conv=w_br_conv, w_out=w_out, ffn2_norm=ffn2_norm, ffn2_w_up=ffn2_w_up, ffn2_w_down=ffn2_w_down, loss_target=loss_target, m_ffn1_norm=m_ffn1_norm, m_ffn1_w_up=m_ffn1_w_up, m_ffn1_w_down=m_ffn1_w_down, m_mix_norm=m_mix_norm, m_w_in=m_w_in, m_b_gate=m_b_gate, m_s5_lambda_re=m_s5_lambda_re, m_s5_lambda_im=m_s5_lambda_im, m_s5_log_dt=m_s5_log_dt, m_s5_b_re=m_s5_b_re, m_s5_b_im=m_s5_b_im, m_s5_c_re=m_s5_c_re, m_s5_c_im=m_s5_c_im, m_s5_d=m_s5_d, m_s5_w_glu=m_s5_w_glu, m_w_br_s5=m_w_br_s5, m_attn_q_gain=m_attn_q_gain, m_attn_k_gain=m_attn_k_gain, m_attn_rel_bias=m_attn_rel_bias, m_w_br_attn=m_w_br_attn, m_conv_w_dw=m_conv_w_dw, m_conv_b_dw=m_conv_b_dw, m_conv_ln_g=m_conv_ln_g, m_conv_ln_b=m_conv_ln_b, m_w_br_conv=m_w_br_conv, m_w_out=m_w_out, m_ffn2_norm=m_ffn2_norm, m_ffn2_w_up=m_ffn2_w_up, m_ffn2_w_down=m_ffn2_w_down, v_ffn1_norm=v_ffn1_norm, v_ffn1_w_up=v_ffn1_w_up, v_ffn1_w_down=v_ffn1_w_down, v_mix_norm=v_mix_norm, v_w_in=v_w_in, v_b_gate=v_b_gate, v_s5_lambda_re=v_s5_lambda_re, v_s5_lambda_im=v_s5_lambda_im, v_s5_log_dt=v_s5_log_dt, v_s5_b_re=v_s5_b_re, v_s5_b_im=v_s5_b_im, v_s5_c_re=v_s5_c_re, v_s5_c_im=v_s5_c_im, v_s5_d=v_s5_d, v_s5_w_glu=v_s5_w_glu, v_w_br_s5=v_w_br_s5, v_attn_q_gain=v_attn_q_gain, v_attn_k_gain=v_attn_k_gain, v_attn_rel_bias=v_attn_rel_bias, v_w_br_attn=v_w_br_attn, v_conv_w_dw=v_conv_w_dw, v_conv_b_dw=v_conv_b_dw, v_conv_ln_g=v_conv_ln_g, v_conv_ln_b=v_conv_ln_b, v_w_br_conv=v_w_br_conv, v_w_out=v_w_out, v_ffn2_norm=v_ffn2_norm, v_ffn2_w_up=v_ffn2_w_up, v_ffn2_w_down=v_ffn2_w_down)
    weights = {n: given[n] for n in TWIN_WEIGHTS}
    shared = {n: given[n] for n in SHARED_INPUTS}
    per_example = {n: given[n] for n in ['x']}
    grad_fn = _jax.value_and_grad(_loss, argnums=(0, 1))

    def one_microbatch(ex, loss_target):
        ex = dict(ex)
        diff = ex.pop(TWIN_DIFF_INPUT)
        return grad_fn(weights, diff, {**shared, **ex}, loss_target)

    if N_MICROBATCH == 1:
        loss, (grad_w, grad_x) = one_microbatch(per_example, given["loss_target"])
    else:
        def body(carry, xs):
            loss_sum, grad_sum = carry
            l_k, (gw_k, gx_k) = one_microbatch(xs[0], xs[1])
            with _jax.named_scope("update"):
                return (loss_sum + l_k, _jax.tree.map(_jnp.add, grad_sum, gw_k)), gx_k

        init = (_jnp.zeros((), _jnp.float32), _jax.tree.map(_jnp.zeros_like, weights))
        (loss, grad_w), grad_x = _jax.lax.scan(body, init, (per_example, given["loss_target"]))
    with _jax.named_scope("update"):
        delta_w, new_m, new_v = {}, {}, {}
        for n in TWIN_WEIGHTS:
            delta_w[n], new_m[n], new_v[n] = _adamw(weights[n], grad_w[n], given["m_" + n], given["v_" + n])
    return (loss, grad_x, *[grad_w[n] for n in TWIN_WEIGHTS], *[delta_w[n] for n in TWIN_WEIGHTS],
            *[new_m[n] for n in TWIN_WEIGHTS], *[new_v[n] for n in TWIN_WEIGHTS])
```

```python
import math

import numpy as np
import jax
import jax.numpy as jnp
from jax import lax
from jax.experimental import pallas as pl
from jax.experimental.pallas import tpu as pltpu

F32 = jnp.float32
BF16 = jnp.bfloat16

CHUNK = 64
N_LEFT = 8
QBLK = CHUNK * N_LEFT
HEAD = 64
MAX_REL = 128
S5_GROUP = 16
S5_STATE = 64
CONV_W = 31
HALO = 32
EPS = 1e-6
NEG = -1e30
ADAM_LR, ADAM_B1, ADAM_B2, ADAM_EPS, ADAM_WD, ADAM_STEP = 0.001, 0.9, 0.999, 1e-08, 0.01, 10
N_DEV = 8
VMEM_LIMIT = 56 * 1024 * 1024


def _call(body, **kw):
    return pl.pallas_call(body, **kw)


def _params(**kw):
    return pltpu.CompilerParams(vmem_limit_bytes=VMEM_LIMIT, **kw)


def _tile(n, cap, unit=128):
    if n <= cap:
        return n
    d = (cap // unit) * unit
    while d >= unit:
        if n % d == 0:
            return d
        d -= unit
    raise ValueError(f"no tile for {n} under {cap}")


def _dot(a, b):
    return jnp.dot(a, b, preferred_element_type=F32)


def _dot_nt(a, b):
    return lax.dot_general(a, b, (((1,), (1,)), ((), ())), preferred_element_type=F32)


def _dot_tn(a, b):
    return lax.dot_general(a, b, (((0,), (0,)), ((), ())), preferred_element_type=F32)


def _sig(x):
    return 1.0 / (1.0 + jnp.exp(-x))


def _rms_fwd(x, g):
    rs = lax.rsqrt(jnp.mean(x * x, axis=-1, keepdims=True) + EPS)
    xhat = x * rs
    return xhat * g, xhat, rs


def _rms_bwd(dh, xhat, rs, g):
    dxh = dh * g
    dx = rs * (dxh - xhat * jnp.mean(dxh * xhat, axis=-1, keepdims=True))
    return dx, dh * xhat


_GELU_C = math.sqrt(2.0 / math.pi)


def _gelu(x):
    return 0.5 * x * (1.0 + jnp.tanh(_GELU_C * (x + 0.044715 * x * x * x)))


def _gelu_grad(x):
    t = jnp.tanh(_GELU_C * (x + 0.044715 * x * x * x))
    return 0.5 * (1.0 + t) + 0.5 * x * (1.0 - t * t) * _GELU_C * (1.0 + 3.0 * 0.044715 * x * x)


def _acc_out(ref, val, first):
    @pl.when(first)
    def _():
        ref[...] = val

    @pl.when(jnp.logical_not(first))
    def _():
        ref[...] += val


FF_CHUNKS = N_DEV // 2


def ffn_fwd(x, g, w_up, w_down, layer):
    n, d = x.shape
    nn = w_up.shape[3]
    tm = _tile(n, 512, 8)

    def body(x_ref, g_ref, wa_ref, wb_ref, wd_ref, xo_ref, a_ref, b_ref, hn_ref, acc_ref):
        j = pl.program_id(1)

        @pl.when(j == 0)
        def _():
            h, _, _ = _rms_fwd(x_ref[...], g_ref[...])
            hn_ref[...] = h.astype(BF16)
            acc_ref[...] = jnp.zeros_like(acc_ref)

        hn = hn_ref[...]
        a = _dot(hn, wa_ref[0, 0])
        b = _dot(hn, wb_ref[0, 0])
        a_ref[0] = a.astype(BF16)
        b_ref[0] = b.astype(BF16)
        act = a * _sig(a) * b
        acc_ref[...] += _dot(act.astype(BF16), wd_ref[0])

        @pl.when(j == FF_CHUNKS - 1)
        def _():
            xo_ref[...] = x_ref[...] + 0.5 * acc_ref[...]

    return _call(
        body,
        name="ffn_fwd",
        grid=(n // tm, FF_CHUNKS),
        in_specs=[
            pl.BlockSpec((tm, d), lambda i, j: (i, 0)),
            pl.BlockSpec((1, d), lambda i, j: (0, 0)),
            pl.BlockSpec((1, 1, d, nn), lambda i, j: (j, layer, 0, 0)),
            pl.BlockSpec((1, 1, d, nn), lambda i, j: (j + FF_CHUNKS, layer, 0, 0)),
            pl.BlockSpec((1, nn, d), lambda i, j: (layer, j, 0)),
        ],
        out_specs=[
            pl.BlockSpec((tm, d), lambda i, j: (i, 0)),
            pl.BlockSpec((1, tm, nn), lambda i, j: (j, i, 0)),
            pl.BlockSpec((1, tm, nn), lambda i, j: (j, i, 0)),
        ],
        out_shape=[
            jax.ShapeDtypeStruct((n, d), F32),
            jax.ShapeDtypeStruct((FF_CHUNKS, n, nn), BF16),
            jax.ShapeDtypeStruct((FF_CHUNKS, n, nn), BF16),
        ],
        scratch_shapes=[pltpu.VMEM((tm, d), BF16), pltpu.VMEM((tm, d), F32)],
        compiler_params=_params(),
    )(x, g, w_up, w_up, w_down)


def ffn_bwd(dy, x, g, a, b, w_up, w_down, layer):
    n, d = x.shape
    nn = w_up.shape[3]
    tm = _tile(n, 512, 8)

    def body(dy_ref, x_ref, g_ref, a_ref, b_ref, wa_ref, wb_ref, wd_ref,
             dx_ref, dg_ref, hn_ref, dyb_ref, da_ref, db_ref, act_ref, dyb_s, dh_ref):
        i, j = pl.program_id(0), pl.program_id(1)

        @pl.when(j == 0)
        def _():
            h, _, _ = _rms_fwd(x_ref[...], g_ref[...])
            hn_ref[...] = h.astype(BF16)
            dyb = (0.5 * dy_ref[...]).astype(BF16)
            dyb_ref[...] = dyb
            dyb_s[...] = dyb
            dh_ref[...] = jnp.zeros_like(dh_ref)

        dact = _dot_nt(dyb_s[...], wd_ref[0])
        a32 = a_ref[0].astype(F32)
        b32 = b_ref[0].astype(F32)
        s = _sig(a32)
        sil = a32 * s
        da = (dact * b32 * (s * (1.0 + a32 * (1.0 - s)))).astype(BF16)
        db = (dact * sil).astype(BF16)
        da_ref[0] = da
        db_ref[0] = db
        act_ref[0] = (sil * b32).astype(BF16)
        dh_ref[...] += _dot_nt(da, wa_ref[0, 0]) + _dot_nt(db, wb_ref[0, 0])

        @pl.when(j == FF_CHUNKS - 1)
        def _():
            gg = g_ref[...]
            _, xhat, rs = _rms_fwd(x_ref[...], gg)
            dxn, dgr = _rms_bwd(dh_ref[...], xhat, rs, gg)
            dx_ref[...] = dy_ref[...] + dxn
            _acc_out(dg_ref, jnp.sum(dgr, axis=0, keepdims=True), i == 0)

    tok = pl.BlockSpec((tm, d), lambda i, j: (i, 0))
    chunk = pl.BlockSpec((1, tm, nn), lambda i, j: (j, i, 0))
    vec = pl.BlockSpec((1, d), lambda i, j: (0, 0))
    chunks = jax.ShapeDtypeStruct((FF_CHUNKS, n, nn), BF16)
    return _call(
        body,
        name="ffn_bwd",
        grid=(n // tm, FF_CHUNKS),
        in_specs=[
            tok, tok, vec, chunk, chunk,
            pl.BlockSpec((1, 1, d, nn), lambda i, j: (j, layer, 0, 0)),
            pl.BlockSpec((1, 1, d, nn), lambda i, j: (j + FF_CHUNKS, layer, 0, 0)),
            pl.BlockSpec((1, nn, d), lambda i, j: (layer, j, 0)),
        ],
        out_specs=[tok, vec, tok, tok, chunk, chunk, chunk],
        out_shape=[
            jax.ShapeDtypeStruct((n, d), F32),
            jax.ShapeDtypeStruct((1, d), F32),
            jax.ShapeDtypeStruct((n, d), BF16),
            jax.ShapeDtypeStruct((n, d), BF16),
            chunks, chunks, chunks,
        ],
        scratch_shapes=[pltpu.VMEM((tm, d), BF16), pltpu.VMEM((tm, d), F32)],
        compiler_params=_params(),
    )(dy, x, g, a, b, w_up, w_up, w_down)


def wgrad(a, b, name, out4, mode, *, g0=0, layer=0, prev=None, a_cols=None):
    a3 = a if a.ndim == 3 else a[None]
    b3 = b if b.ndim == 3 else b[None]
    sa, n, ka = a3.shape
    sb, _, kb = b3.shape
    a0 = 0
    if a_cols is not None:
        a0, ka = a_cols
    k1, k2 = sa * ka, sb * kb
    depth, groups, rr, cc = out4
    t1 = _tile(math.gcd(ka, rr), 1024)
    t2 = _tile(math.gcd(kb, cc), 1024)
    tn = _tile(n, 1024, 8)
    na, nb = ka // t1, kb // t2
    if mode == "col":
        assert rr == k1 and k2 % cc == 0 and g0 + k2 // cc <= groups
        per = cc // t2
        omap = lambda i, j, k: (layer, g0 + j // per, i, j % per)
    else:
        assert cc == k2 and k1 % rr == 0 and g0 + k1 // rr <= groups
        per = rr // t1
        omap = lambda i, j, k: (layer, g0 + i // per, i % per, j)

    def body(a_ref, b_ref, *rest):
        o_ref = rest[-1]

        @pl.when(pl.program_id(2) == 0)
        def _():
            o_ref[...] = jnp.zeros_like(o_ref)

        o_ref[0, 0] += _dot_tn(a_ref[0].astype(BF16), b_ref[0].astype(BF16))

    in_specs = [
        pl.BlockSpec((1, tn, t1), lambda i, j, k: (i // na, k, a0 // t1 + i % na)),
        pl.BlockSpec((1, tn, t2), lambda i, j, k: (j // nb, k, j % nb)),
    ]
    args = [a3, b3]
    extra = {}
    if prev is not None:
        in_specs.append(pl.BlockSpec(memory_space=pl.ANY))
        args.append(prev)
        extra["input_output_aliases"] = {2: 0}
    return _call(
        body,
        name=name,
        grid=(k1 // t1, k2 // t2, n // tn),
        in_specs=in_specs,
        out_specs=pl.BlockSpec((1, 1, t1, t2), omap),
        out_shape=jax.ShapeDtypeStruct(out4, F32),
        compiler_params=_params(),
        **extra,
    )(*args)


def proj_fwd(x, g, w, layer):
    n, d = x.shape
    c = w.shape[2]
    tm, tc = _tile(n, 512, 8), _tile(c, 768)

    def body(x_ref, g_ref, w_ref, o_ref, hn_ref):
        @pl.when(pl.program_id(1) == 0)
        def _():
            h, _, _ = _rms_fwd(x_ref[...], g_ref[...])
            hn_ref[...] = h.astype(BF16)

        o_ref[...] = _dot(hn_ref[...], w_ref[0])

    return _call(
        body,
        name="proj_fwd",
        grid=(n // tm, c // tc),
        in_specs=[
            pl.BlockSpec((tm, d), lambda i, j: (i, 0)),
            pl.BlockSpec((1, d), lambda i, j: (0, 0)),
            pl.BlockSpec((1, d, tc), lambda i, j: (layer, 0, j)),
        ],
        out_specs=pl.BlockSpec((tm, tc), lambda i, j: (i, j)),
        out_shape=jax.ShapeDtypeStruct((n, c), F32),
        scratch_shapes=[pltpu.VMEM((tm, d), BF16)],
        compiler_params=_params(),
    )(x, g, w)


def proj_bwd(dproj, dres, x, g, w, layer):
    n, d = x.shape
    c = w.shape[2]
    tm, tc = _tile(n, 512, 8), _tile(c, 768)
    nc = c // tc

    def body(dp_ref, dr_ref, x_ref, g_ref, w_ref, dx_ref, dg_ref, hn_ref, dh_ref):
        i, j = pl.program_id(0), pl.program_id(1)

        @pl.when(j == 0)
        def _():
            dh_ref[...] = jnp.zeros_like(dh_ref)

        dh_ref[...] += _dot_nt(dp_ref[...], w_ref[0])

        @pl.when(j == nc - 1)
        def _():
            gg = g_ref[...]
            h, xhat, rs = _rms_fwd(x_ref[...], gg)
            hn_ref[...] = h.astype(BF16)
            dxn, dgr = _rms_bwd(dh_ref[...], xhat, rs, gg)
            dx_ref[...] = dr_ref[...] + dxn
            _acc_out(dg_ref, jnp.sum(dgr, axis=0, keepdims=True), i == 0)

    return _call(
        body,
        name="proj_bwd",
        grid=(n // tm, nc),
        in_specs=[
            pl.BlockSpec((tm, tc), lambda i, j: (i, j)),
            pl.BlockSpec((tm, d), lambda i, j: (i, 0)),
            pl.BlockSpec((tm, d), lambda i, j: (i, 0)),
            pl.BlockSpec((1, d), lambda i, j: (0, 0)),
            pl.BlockSpec((1, d, tc), lambda i, j: (layer, 0, j)),
        ],
        out_specs=[
            pl.BlockSpec((tm, d), lambda i, j: (i, 0)),
            pl.BlockSpec((1, d), lambda i, j: (0, 0)),
            pl.BlockSpec((tm, d), lambda i, j: (i, 0)),
        ],
        out_shape=[
            jax.ShapeDtypeStruct((n, d), F32),
            jax.ShapeDtypeStruct((1, d), F32),
            jax.ShapeDtypeStruct((n, d), BF16),
        ],
        scratch_shapes=[pltpu.VMEM((tm, d), F32)],
        compiler_params=_params(),
    )(dproj, dres, x, g, w)


S5_TS = 512


def _cmul(ar, ai, br, bi):
    return ar * br - ai * bi, ar * bi + ai * br


def _s5_tables(ar, ai, reverse):
    gp = ar.shape[1]
    if reverse:
        ai = -ai
    a1r, a1i = jnp.broadcast_to(ar, (8, gp)), jnp.broadcast_to(ai, (8, gp))
    a2r, a2i = _cmul(a1r, a1i, a1r, a1i)
    a4r, a4i = _cmul(a2r, a2i, a2r, a2i)
    a8r, a8i = _cmul(a4r, a4i, a4r, a4i)
    row = lax.broadcasted_iota(jnp.int32, (8, gp), 0)
    e = (8 - row) if reverse else (row + 1)
    pr, pi = jnp.ones((8, gp), F32), jnp.zeros((8, gp), F32)
    for bit, (fr, fi) in ((1, (a1r, a1i)), (2, (a2r, a2i)), (4, (a4r, a4i)), (8, (a8r, a8i))):
        nr, ni = _cmul(pr, pi, fr, fi)
        on = (e & bit) != 0
        pr, pi = jnp.where(on, nr, pr), jnp.where(on, ni, pi)
    return (a1r, a1i, a2r, a2i, a4r, a4i, pr, pi)


def _s5_scan(xr_ref, xi_ref, tab_ref, cr_ref, ci_ref, ts, reverse):
    gp = xr_ref.shape[1]
    nt = ts // 8
    row = lax.broadcasted_iota(jnp.int32, (8, gp), 0)

    def shifted(v, s):
        if reverse:
            return jnp.where(row < 8 - s, pltpu.roll(v, 8 - s, 0), 0.0)
        return jnp.where(row >= s, pltpu.roll(v, s, 0), 0.0)

    def step(k, carry):
        cr, ci = carry
        t = (nt - 1 - k) if reverse else k
        r0 = pl.multiple_of(t * 8, 8)
        br = xr_ref[pl.ds(r0, 8), :]
        bi = xi_ref[pl.ds(r0, 8), :]
        for q, s in enumerate((1, 2, 4)):
            fr, fi = tab_ref[2 * q], tab_ref[2 * q + 1]
            sr, si = shifted(br, s), shifted(bi, s)
            mr, mi = _cmul(fr, fi, sr, si)
            br, bi = br + mr, bi + mi
        mr, mi = _cmul(tab_ref[6], tab_ref[7], cr, ci)
        br, bi = br + mr, bi + mi
        xr_ref[pl.ds(r0, 8), :] = br
        xi_ref[pl.ds(r0, 8), :] = bi
        edge = 0 if reverse else 7
        return (jnp.broadcast_to(br[edge:edge + 1, :], (8, gp)),
                jnp.broadcast_to(bi[edge:edge + 1, :], (8, gp)))

    cr, ci = lax.fori_loop(0, nt, step, (cr_ref[...], ci_ref[...]))
    cr_ref[...] = cr
    ci_ref[...] = ci


def s5_fwd(proj, ucol, seq, ar, ai, bm_r, bm_i, cm_r, cm_i, dskip, w_glu, layer):
    n = proj.shape[0]
    ds, gp = bm_r.shape
    ts = min(S5_TS, seq)
    nt = seq // ts

    def body(u_ref, ar_ref, ai_ref, bmr_ref, bmi_ref, cmr_ref, cmi_ref, d_ref, wg_ref,
             out_ref, xr_ref, xi_ref, yp_ref, tab_ref, cr_ref, ci_ref):
        @pl.when(pl.program_id(1) == 0)
        def _():
            for q, v in enumerate(_s5_tables(ar_ref[...], ai_ref[...], False)):
                tab_ref[q] = v
            cr_ref[...] = jnp.zeros_like(cr_ref)
            ci_ref[...] = jnp.zeros_like(ci_ref)

        u = u_ref[...]
        ub = u.astype(BF16)
        xr_ref[...] = _dot(ub, bmr_ref[...])
        xi_ref[...] = _dot(ub, bmi_ref[...])
        _s5_scan(xr_ref, xi_ref, tab_ref, cr_ref, ci_ref, ts, False)
        y = (_dot(xr_ref[...].astype(BF16), cmr_ref[...]) - _dot(xi_ref[...].astype(BF16), cmi_ref[...])
             + d_ref[...] * u)
        yp_ref[...] = y
        z = _dot(_gelu(y).astype(BF16), wg_ref[0])
        out_ref[...] = z[:, :ds] * _sig(z[:, ds:])

    full = lambda shape: pl.BlockSpec(shape, lambda b, t: (0, 0))
    return _call(
        body,
        name="s5_fwd",
        grid=(n // seq, nt),
        in_specs=[
            pl.BlockSpec((ts, ds), lambda b, t: (b * nt + t, ucol)),
            full((1, gp)), full((1, gp)), full((ds, gp)), full((ds, gp)), full((gp, ds)), full((gp, ds)),
            full((1, ds)), pl.BlockSpec((1, ds, 2 * ds), lambda b, t: (layer, 0, 0)),
        ],
        out_specs=[
            pl.BlockSpec((ts, ds), lambda b, t: (b * nt + t, 0)),
            pl.BlockSpec((ts, gp), lambda b, t: (b * nt + t, 0)),
            pl.BlockSpec((ts, gp), lambda b, t: (b * nt + t, 0)),
            pl.BlockSpec((ts, ds), lambda b, t: (b * nt + t, 0)),
        ],
        out_shape=[
            jax.ShapeDtypeStruct((n, ds), F32),
            jax.ShapeDtypeStruct((n, gp), F32),
            jax.ShapeDtypeStruct((n, gp), F32),
            jax.ShapeDtypeStruct((n, ds), F32),
        ],
        scratch_shapes=[pltpu.VMEM((8, 8, gp), F32), pltpu.VMEM((8, gp), F32), pltpu.VMEM((8, gp), F32)],
        compiler_params=_params(),
    )(proj, ar, ai, bm_r, bm_i, cm_r, cm_i, dskip, w_glu)


def s5_bwd(dout, ypre, proj, ucol, xr, xi, seq, ar, ai, bm_r, bm_i, cm_r, cm_i, dskip, w_glu, layer):
    n = proj.shape[0]
    ds, gp = bm_r.shape
    ts = min(S5_TS, seq)
    nt = seq // ts

    def body(do_ref, yp_ref, u_ref, xr_ref, xi_ref, hr_ref, hi_ref, ar_ref, ai_ref, bmr_ref, bmi_ref,
             cmr_ref, cmi_ref, d_ref, wg_ref,
             du_ref, gr_ref, gi_ref, dyb_ref, glb_ref, dzb_ref, dar_ref, dai_ref, dd_ref,
             tab_ref, cr_ref, ci_ref):
        b, t = pl.program_id(0), pl.program_id(1)
        first = jnp.logical_and(b == 0, t == 0)

        @pl.when(t == 0)
        def _():
            for q, v in enumerate(_s5_tables(ar_ref[...], ai_ref[...], True)):
                tab_ref[q] = v
            cr_ref[...] = jnp.zeros_like(cr_ref)
            ci_ref[...] = jnp.zeros_like(ci_ref)

        yp = yp_ref[...]
        u = u_ref[...]
        gl = _gelu(yp).astype(BF16)
        glb_ref[...] = gl
        z = _dot(gl, wg_ref[0])
        za, sg = z[:, :ds], _sig(z[:, ds:])
        do = do_ref[...]
        da = (do * sg).astype(BF16)
        dg = (do * za * sg * (1.0 - sg)).astype(BF16)
        dzb_ref[:, :ds] = da
        dzb_ref[:, ds:] = dg
        dgl = _dot_nt(da, wg_ref[0, :, :ds]) + _dot_nt(dg, wg_ref[0, :, ds:])
        dyp = dgl * _gelu_grad(yp)
        dypb = dyp.astype(BF16)
        dyb_ref[...] = dypb
        _acc_out(dd_ref, jnp.sum(dyp * u, axis=0, keepdims=True), first)

        gr_ref[...] = _dot_nt(dypb, cmr_ref[...])
        gi_ref[...] = -_dot_nt(dypb, cmi_ref[...])
        _s5_scan(gr_ref, gi_ref, tab_ref, cr_ref, ci_ref, ts, True)
        gr, gi = gr_ref[...], gi_ref[...]
        du_ref[...] = d_ref[...] * dyp + _dot_nt(gr.astype(BF16), bmr_ref[...]) + _dot_nt(gi.astype(BF16), bmi_ref[...])

        row = lax.broadcasted_iota(jnp.int32, (ts, gp), 0)
        live = jnp.where(t == nt - 1, 0.0, 1.0)
        pr = jnp.broadcast_to(hr_ref[7:8, :] * live, (ts, gp))
        pi = jnp.broadcast_to(hi_ref[7:8, :] * live, (ts, gp))
        sr = jnp.where(row == 0, pr, pltpu.roll(xr_ref[...], 1, 0))
        si = jnp.where(row == 0, pi, pltpu.roll(xi_ref[...], 1, 0))
        _acc_out(dar_ref, jnp.sum(gr * sr + gi * si, axis=0, keepdims=True), first)
        _acc_out(dai_ref, jnp.sum(gi * sr - gr * si, axis=0, keepdims=True), first)

    full = lambda shape: pl.BlockSpec(shape, lambda b, t: (0, 0))
    blk = lambda w, col=0: pl.BlockSpec((ts, w), lambda b, t: (b * nt + nt - 1 - t, col))
    halo = pl.BlockSpec((8, gp), lambda b, t: (jnp.maximum((b * seq + (nt - 1 - t) * ts) // 8 - 1, 0), 0))
    return _call(
        body,
        name="s5_bwd",
        grid=(n // seq, nt),
        in_specs=[
            blk(ds), blk(ds), blk(ds, ucol), blk(gp), blk(gp), halo, halo,
            full((1, gp)), full((1, gp)), full((ds, gp)), full((ds, gp)), full((gp, ds)), full((gp, ds)),
            full((1, ds)), pl.BlockSpec((1, ds, 2 * ds), lambda b, t: (layer, 0, 0)),
        ],
        out_specs=[
            blk(ds), blk(gp), blk(gp), blk(ds), blk(ds), blk(2 * ds),
            full((1, gp)), full((1, gp)), full((1, ds)),
        ],
        out_shape=[
            jax.ShapeDtypeStruct((n, ds), F32),
            jax.ShapeDtypeStruct((n, gp), F32),
            jax.ShapeDtypeStruct((n, gp), F32),
            jax.ShapeDtypeStruct((n, ds), BF16),
            jax.ShapeDtypeStruct((n, ds), BF16),
            jax.ShapeDtypeStruct((n, 2 * ds), BF16),
            jax.ShapeDtypeStruct((1, gp), F32),
            jax.ShapeDtypeStruct((1, gp), F32),
            jax.ShapeDtypeStruct((1, ds), F32),
        ],
        scratch_shapes=[pltpu.VMEM((8, 8, gp), F32), pltpu.VMEM((8, gp), F32), pltpu.VMEM((8, gp), F32)],
        compiler_params=_params(),
    )(dout, ypre, proj, xr, xi, xr, xi, ar, ai, bm_r, bm_i, cm_r, cm_i, dskip, w_glu)


BAND = QBLK + CHUNK
NCH = QBLK // CHUNK


def _attn_specs(nq):
    cur = pl.BlockSpec((1, QBLK, HEAD), lambda h, b, i: (h, b * nq + i, 0))
    prev = pl.BlockSpec((1, QBLK, HEAD), lambda h, b, i: (h, b * nq + jnp.maximum(i - 1, 0), 0))
    vec = pl.BlockSpec((1, 1, 2 * QBLK), lambda h, b, i: (h, 0, 0))
    gain = pl.BlockSpec((1, HEAD), lambda h, b, i: (0, 0))
    return cur, prev, vec, gain


def _attn_build_table(tv, bias_ref, tab_ref):
    w = 2 * QBLK
    for qi in range(CHUNK):
        bias_ref[qi:qi + 1, :] = pltpu.roll(tv, (qi - (CHUNK - 1)) % w, 1)
    bias = bias_ref[...]
    lane = lax.broadcasted_iota(jnp.int32, (CHUNK, w), 1)
    for c in range(NCH):
        rolled = bias if c == 0 else pltpu.roll(bias, CHUNK * c, 1)
        ok = jnp.logical_and(lane >= CHUNK * c, lane < CHUNK * c + BAND)
        tab_ref[CHUNK * c:CHUNK * (c + 1), :] = jnp.where(ok, rolled, NEG)


def _attn_reduce_table(dt_ref, bias_ref):
    w = 2 * QBLK
    acc = dt_ref[0:CHUNK, :]
    for c in range(1, NCH):
        acc = acc + pltpu.roll(dt_ref[CHUNK * c:CHUNK * (c + 1), :], w - CHUNK * c, 1)
    bias_ref[...] = acc
    out = jnp.zeros((1, w), F32)
    for qi in range(CHUNK):
        out = out + pltpu.roll(bias_ref[qi:qi + 1, :], ((CHUNK - 1) - qi) % w, 1)
    return out


def _attn_probs(q, kp, kc, gq, gk, table, first_block):
    qn, qhat, qrs = _rms_fwd(q, gq)
    k = jnp.concatenate([kp, kc], axis=0)
    kn, khat, krs = _rms_fwd(k, gk)
    s = _dot_nt(qn.astype(BF16), kn.astype(BF16)) * (HEAD ** -0.5) + table
    col = lax.broadcasted_iota(jnp.int32, s.shape, 1)
    s = jnp.where(jnp.logical_and(first_block, col < QBLK), NEG, s)
    e = jnp.exp(s - jnp.max(s, axis=-1, keepdims=True))
    p = e / jnp.sum(e, axis=-1, keepdims=True)
    return p, (qn, qhat, qrs), (kn, khat, krs)


def attn_fwd(q, k, v, tv, gq, gk, seq):
    h, n, _ = q.shape
    nq = seq // QBLK
    cur, prev, vec, gain = _attn_specs(nq)

    def body(q_ref, kp_ref, kc_ref, vp_ref, vc_ref, tv_ref, gq_ref, gk_ref, o_ref, bias_ref, tab_ref):
        @pl.when(jnp.logical_and(pl.program_id(1) == 0, pl.program_id(2) == 0))
        def _():
            _attn_build_table(tv_ref[0], bias_ref, tab_ref)

        p, _, _ = _attn_probs(q_ref[0], kp_ref[0], kc_ref[0], gq_ref[...], gk_ref[...], tab_ref[...],
                              pl.program_id(2) == 0)
        vv = jnp.concatenate([vp_ref[0], vc_ref[0]], axis=0).astype(BF16)
        o_ref[0] = _dot(p.astype(BF16), vv)

    return _call(
        body,
        name="attn_fwd",
        grid=(h, n // seq, nq),
        in_specs=[cur, prev, cur, prev, cur, vec, gain, gain],
        out_specs=cur,
        out_shape=jax.ShapeDtypeStruct((h, n, HEAD), F32),
        scratch_shapes=[pltpu.VMEM((CHUNK, 2 * QBLK), F32), pltpu.VMEM((QBLK, 2 * QBLK), F32)],
        compiler_params=_params(),
    )(q, k, k, v, v, tv, gq, gk)


def attn_bwd(do, q, k, v, tv, gq, gk, seq):
    h, n, _ = q.shape
    nb = n // seq
    nq = seq // QBLK
    cur, prev, vec, gain = _attn_specs(nq)

    def body(do_ref, q_ref, kp_ref, kc_ref, vp_ref, vc_ref, tv_ref, gq_ref, gk_ref,
             dq_ref, dkp_ref, dkc_ref, dvp_ref, dvc_ref, dtv_ref, dgq_ref, dgk_ref, bias_ref, tab_ref, dt_ref):
        hh, b, i = pl.program_id(0), pl.program_id(1), pl.program_id(2)
        head_start = jnp.logical_and(b == 0, i == 0)

        @pl.when(head_start)
        def _():
            _attn_build_table(tv_ref[0], bias_ref, tab_ref)

        gq_, gk_ = gq_ref[...], gk_ref[...]
        p, (qn, qhat, qrs), (kn, khat, krs) = _attn_probs(
            q_ref[0], kp_ref[0], kc_ref[0], gq_, gk_, tab_ref[...], i == 0)
        dob = do_ref[0].astype(BF16)
        vv = jnp.concatenate([vp_ref[0], vc_ref[0]], axis=0).astype(BF16)
        dv = _dot_tn(p.astype(BF16), dob)
        dp = _dot_nt(dob, vv)
        ds = p * (dp - jnp.sum(p * dp, axis=-1, keepdims=True))
        _acc_out(dt_ref, ds, head_start)
        dsb = (ds * (HEAD ** -0.5)).astype(BF16)
        dqn = _dot(dsb, kn.astype(BF16))
        dkn = _dot_tn(dsb, qn.astype(BF16))
        dq, dgq_rows = _rms_bwd(dqn, qhat, qrs, gq_)
        dk, dgk_rows = _rms_bwd(dkn, khat, krs, gk_)
        dq_ref[0] = dq
        dkp_ref[0] = dk[:QBLK]
        dkc_ref[0] = dk[QBLK:]
        dvp_ref[0] = dv[:QBLK]
        dvc_ref[0] = dv[QBLK:]
        first = jnp.logical_and(hh == 0, head_start)
        _acc_out(dgq_ref, jnp.sum(dgq_rows, axis=0, keepdims=True), first)
        _acc_out(dgk_ref, jnp.sum(dgk_rows, axis=0, keepdims=True), first)

        @pl.when(jnp.logical_and(b == nb - 1, i == nq - 1))
        def _():
            dtv_ref[0] = _attn_reduce_table(dt_ref, bias_ref)

    hm = jax.ShapeDtypeStruct((h, n, HEAD), F32)
    return _call(
        body,
        name="attn_bwd",
        grid=(h, nb, nq),
        in_specs=[cur, cur, prev, cur, prev, cur, vec, gain, gain],
        out_specs=[cur, cur, cur, cur, cur, vec, gain, gain],
        out_shape=[hm, hm, hm, hm, hm, jax.ShapeDtypeStruct(tv.shape, F32),
                   jax.ShapeDtypeStruct((1, HEAD), F32), jax.ShapeDtypeStruct((1, HEAD), F32)],
        scratch_shapes=[pltpu.VMEM((CHUNK, 2 * QBLK), F32), pltpu.VMEM((QBLK, 2 * QBLK), F32),
                        pltpu.VMEM((QBLK, 2 * QBLK), F32)],
        compiler_params=_params(),
    )(do, q, k, k, v, v, tv, gq, gk)


CONV_TC = 512


def _ln_fwd(h1, g, b):
    mu = jnp.mean(h1, axis=-1, keepdims=True)
    xc = h1 - mu
    rs = lax.rsqrt(jnp.mean(xc * xc, axis=-1, keepdims=True) + EPS)
    yhat = xc * rs
    return yhat * g + b, yhat, rs


def _glu(z, dc):
    return z[:, :dc] * _sig(z[:, dc:])


def conv_fwd(proj, zcol, seq, w, bdw, lng, lnb):
    n = proj.shape[0]
    dc = w.shape[1]
    tc = min(CONV_TC, seq)
    nt = seq // tc

    def body(z_ref, zp_ref, w_ref, b_ref, g_ref, lb_ref, h1_ref, o_ref, ext_ref):
        live = jnp.where(pl.program_id(1) == 0, 0.0, 1.0)
        ext_ref[pl.ds(0, HALO), :] = _glu(zp_ref[...], dc) * live
        ext_ref[pl.ds(HALO, tc), :] = _glu(z_ref[...], dc)
        acc = jnp.zeros((tc, dc), F32) + b_ref[...]
        for j in range(CONV_W):
            acc = acc + w_ref[j:j + 1, :] * ext_ref[pl.ds(HALO - (CONV_W - 1) + j, tc), :]
        h1_ref[...] = acc
        ln, _, _ = _ln_fwd(acc, g_ref[...], lb_ref[...])
        o_ref[...] = ln * _sig(ln)

    full = lambda shape: pl.BlockSpec(shape, lambda b, t: (0, 0))
    return _call(
        body,
        name="conv_fwd",
        grid=(n // seq, nt),
        in_specs=[
            pl.BlockSpec((tc, 2 * dc), lambda b, t: (b * nt + t, zcol)),
            pl.BlockSpec((HALO, 2 * dc), lambda b, t: (jnp.maximum((b * seq + t * tc) // HALO - 1, 0), zcol)),
            full((HALO, dc)), full((1, dc)), full((1, dc)), full((1, dc)),
        ],
        out_specs=[
            pl.BlockSpec((tc, dc), lambda b, t: (b * nt + t, 0)),
            pl.BlockSpec((tc, dc), lambda b, t: (b * nt + t, 0)),
        ],
        out_shape=[jax.ShapeDtypeStruct((n, dc), F32), jax.ShapeDtypeStruct((n, dc), F32)],
        scratch_shapes=[pltpu.VMEM((tc + HALO, dc), F32)],
        compiler_params=_params(),
    )(proj, proj, w, bdw, lng, lnb)


def conv_bwd(dco, h1, proj, zcol, seq, w, bdw, lng, lnb):
    n = proj.shape[0]
    dc = w.shape[1]
    tc = min(CONV_TC, seq)
    nt = seq // tc
    nrow = n // HALO

    def body(do_ref, don_ref, h1_ref, h1n_ref, z_ref, zp_ref, w_ref, g_ref, lb_ref,
             dz_ref, dw_ref, db_ref, dg_ref, dlb_ref, ext_ref, dext_ref):
        b, t = pl.program_id(0), pl.program_id(1)
        first = jnp.logical_and(b == 0, t == 0)
        g, lb = g_ref[...], lb_ref[...]

        def dh1_of(do, h1):
            ln, yhat, rs = _ln_fwd(h1, g, lb)
            s = _sig(ln)
            dln = do * (s * (1.0 + ln * (1.0 - s)))
            dyh = dln * g
            dh1 = rs * (dyh - jnp.mean(dyh, axis=-1, keepdims=True)
                        - yhat * jnp.mean(dyh * yhat, axis=-1, keepdims=True))
            return dh1, dln, yhat

        dh1, dln, yhat = dh1_of(do_ref[...], h1_ref[...])
        dh1n, _, _ = dh1_of(don_ref[...], h1n_ref[...])
        _acc_out(dg_ref, jnp.sum(dln * yhat, axis=0, keepdims=True), first)
        _acc_out(dlb_ref, jnp.sum(dln, axis=0, keepdims=True), first)
        _acc_out(db_ref, jnp.sum(dh1, axis=0, keepdims=True), first)

        dext_ref[pl.ds(0, tc), :] = dh1
        dext_ref[pl.ds(tc, HALO), :] = dh1n * jnp.where(t == nt - 1, 0.0, 1.0)
        z = z_ref[...]
        ext_ref[pl.ds(0, HALO), :] = _glu(zp_ref[...], dc) * jnp.where(t == 0, 0.0, 1.0)
        ext_ref[pl.ds(HALO, tc), :] = _glu(z, dc)

        @pl.when(first)
        def _():
            dw_ref[...] = jnp.zeros_like(dw_ref)

        dh0 = jnp.zeros((tc, dc), F32)
        for j in range(CONV_W):
            dh0 = dh0 + w_ref[j:j + 1, :] * dext_ref[pl.ds(CONV_W - 1 - j, tc), :]
            dw_ref[j:j + 1, :] += jnp.sum(dh1 * ext_ref[pl.ds(HALO - (CONV_W - 1) + j, tc), :],
                                          axis=0, keepdims=True)
        za, sg = z[:, :dc], _sig(z[:, dc:])
        dz_ref[:, :dc] = dh0 * sg
        dz_ref[:, dc:] = dh0 * za * sg * (1.0 - sg)

    full = lambda shape: pl.BlockSpec(shape, lambda b, t: (0, 0))
    cur = lambda wd, col=0: pl.BlockSpec((tc, wd), lambda b, t: (b * nt + t, col))
    nxt = pl.BlockSpec((HALO, dc), lambda b, t: (jnp.minimum((b * seq + (t + 1) * tc) // HALO, nrow - 1), 0))
    return _call(
        body,
        name="conv_bwd",
        grid=(n // seq, nt),
        in_specs=[
            cur(dc), nxt, cur(dc), nxt, cur(2 * dc, zcol),
            pl.BlockSpec((HALO, 2 * dc), lambda b, t: (jnp.maximum((b * seq + t * tc) // HALO - 1, 0), zcol)),
            full((HALO, dc)), full((1, dc)), full((1, dc)),
        ],
        out_specs=[cur(2 * dc), full((HALO, dc)), full((1, dc)), full((1, dc)), full((1, dc))],
        out_shape=[
            jax.ShapeDtypeStruct((n, 2 * dc), F32),
            jax.ShapeDtypeStruct((HALO, dc), F32),
            jax.ShapeDtypeStruct((1, dc), F32),
            jax.ShapeDtypeStruct((1, dc), F32),
            jax.ShapeDtypeStruct((1, dc), F32),
        ],
        scratch_shapes=[pltpu.VMEM((tc + HALO, dc), F32), pltpu.VMEM((tc + HALO, dc), F32)],
        compiler_params=_params(),
    )(dco, dco, h1, h1, proj, proj, w, lng, lnb)


def _merge_common(l0, l1, l2, bg, so, ao, co, wbs, wba, wbc, d):
    ys = _dot(so.astype(BF16), wbs)
    ya = _dot(ao.astype(BF16), wba)
    yc = _dot(co.astype(BF16), wbc)
    gs = _sig(l0 + bg[:, :d])
    ga = _sig(l1 + bg[:, d:2 * d])
    gc = _sig(l2 + bg[:, 2 * d:])
    return (ys, ya, yc), (gs, ga, gc)


def _merge_specs(tm, d, dss, da, dc, layer):
    row = lambda w, col=0: pl.BlockSpec((tm, w), lambda i: (i, col))
    full = lambda r, c: pl.BlockSpec((r, c), lambda i: (0, 0))
    stacked = lambda r: pl.BlockSpec((1, r, d), lambda i: (layer, 0, 0))
    acts = [row(d, 0), row(d, 1), row(d, 2), full(1, 3 * d), row(dss), row(da), row(dc)]
    weights = [stacked(dss), stacked(da), stacked(dc), stacked(d)]
    return row, full, acts, weights


def merge_fwd(x, proj, bg, so, ao, co, wbs, wba, wbc, wout, layer):
    n, d = x.shape
    tm = _tile(n, 256, 8)
    row, full, acts, weights = _merge_specs(tm, d, so.shape[1], ao.shape[1], co.shape[1], layer)

    def body(x_ref, l0_ref, l1_ref, l2_ref, bg_ref, so_ref, ao_ref, co_ref,
             wbs_ref, wba_ref, wbc_ref, wo_ref, o_ref):
        (ys, ya, yc), (gs, ga, gc) = _merge_common(
            l0_ref[...], l1_ref[...], l2_ref[...], bg_ref[...], so_ref[...], ao_ref[...], co_ref[...],
            wbs_ref[0], wba_ref[0], wbc_ref[0], d)
        merged = gs * ys + ga * ya + gc * yc
        o_ref[...] = x_ref[...] + _dot(merged.astype(BF16), wo_ref[0])

    return _call(
        body,
        name="merge_fwd",
        grid=(n // tm,),
        in_specs=[row(d)] + acts + weights,
        out_specs=row(d),
        out_shape=jax.ShapeDtypeStruct((n, d), F32),
        compiler_params=_params(),
    )(x, proj, proj, proj, bg, so, ao, co, wbs, wba, wbc, wout)


def merge_bwd(dx, proj, bg, so, ao, co, wbs, wba, wbc, wout, layer):
    n, d = dx.shape
    dss, da, dc = so.shape[1], ao.shape[1], co.shape[1]
    tm = _tile(n, 256, 8)
    row, full, acts, weights = _merge_specs(tm, d, dss, da, dc, layer)

    def body(dx_ref, l0_ref, l1_ref, l2_ref, bg_ref, so_ref, ao_ref, co_ref, wbs_ref, wba_ref, wbc_ref, wo_ref,
             dl_ref, dso_ref, dao_ref, dco_ref, dbg_ref, mg_ref, dxb_ref, dys_ref, dya_ref, dyc_ref):
        wbs, wba, wbc = wbs_ref[0], wba_ref[0], wbc_ref[0]
        (ys, ya, yc), (gs, ga, gc) = _merge_common(
            l0_ref[...], l1_ref[...], l2_ref[...], bg_ref[...], so_ref[...], ao_ref[...], co_ref[...],
            wbs, wba, wbc, d)
        mg_ref[...] = (gs * ys + ga * ya + gc * yc).astype(BF16)
        dxb = dx_ref[...].astype(BF16)
        dxb_ref[...] = dxb
        dm = _dot_nt(dxb, wo_ref[0])
        first = pl.program_id(0) == 0
        for k, (y, g, w, dy_ref, db_ref) in enumerate((
                (ys, gs, wbs, dys_ref, dso_ref), (ya, ga, wba, dya_ref, dao_ref), (yc, gc, wbc, dyc_ref, dco_ref))):
            dl = dm * y * g * (1.0 - g)
            dl_ref[:, k * d:(k + 1) * d] = dl.astype(BF16)
            _acc_out(dbg_ref.at[:, k * d:(k + 1) * d], jnp.sum(dl, axis=0, keepdims=True), first)
            dy = (dm * g).astype(BF16)
            dy_ref[...] = dy
            db_ref[...] = _dot_nt(dy, w)

    bf = lambda w: jax.ShapeDtypeStruct((n, w), BF16)
    return _call(
        body,
        name="merge_bwd",
        grid=(n // tm,),
        in_specs=[row(d)] + acts + weights,
        out_specs=[row(3 * d), row(dss), row(da), row(dc), full(1, 3 * d),
                   row(d), row(d), row(d), row(d), row(d)],
        out_shape=[bf(3 * d), jax.ShapeDtypeStruct((n, dss), F32), jax.ShapeDtypeStruct((n, da), F32),
                   jax.ShapeDtypeStruct((n, dc), F32), jax.ShapeDtypeStruct((1, 3 * d), F32),
                   bf(d), bf(d), bf(d), bf(d), bf(d)],
        compiler_params=_params(),
    )(dx, proj, proj, proj, bg, so, ao, co, wbs, wba, wbc, wout)


def loss_head(y, target):
    n, d = y.shape
    tm = _tile(n, 512, 8)

    def body(y_ref, t_ref, dy_ref, l_ref):
        e = y_ref[...] - t_ref[...]
        dy_ref[...] = e * (1.0 / d)
        part = 0.5 * jnp.sum(jnp.sum(e * e, axis=-1, keepdims=True) * (1.0 / d), axis=0, keepdims=True)
        _acc_out(l_ref, part, pl.program_id(0) == 0)

    return _call(
        body,
        name="loss_head",
        grid=(n // tm,),
        in_specs=[pl.BlockSpec((tm, d), lambda i: (i, 0)), pl.BlockSpec((tm, d), lambda i: (i, 0))],
        out_specs=[pl.BlockSpec((tm, d), lambda i: (i, 0)), pl.BlockSpec((1, 1), lambda i: (0, 0))],
        out_shape=[jax.ShapeDtypeStruct((n, d), F32), jax.ShapeDtypeStruct((1, 1), F32)],
        compiler_params=_params(),
    )(y, target)


def _mesh_pos():
    return lax.axis_index("x"), lax.axis_index("y"), lax.axis_index("c")


def all_gather(shards):
    na = len(shards)

    def body(*refs):
        x_refs, out_refs = refs[:na], refs[na:2 * na]
        send_sems, recv_sems, local_sems = refs[2 * na:]
        x, y, c = _mesh_pos()
        me, sibling = (x, y, c), (x, y, 1 - c)
        chips = [(1 - x, y), (x, 1 - y), (1 - x, 1 - y)]

        def slot(a, px, py, pc):
            return out_refs[a].at[4 * px + 2 * py + pc]

        def copy(a, k, block, to, src=None):
            return pltpu.make_async_remote_copy(
                src_ref=slot(a, *block) if src is None else src, dst_ref=slot(a, *block),
                send_sem=send_sems.at[a, k], recv_sem=recv_sems.at[a, k],
                device_id=to, device_id_type=pl.DeviceIdType.MESH)

        mine = [pltpu.make_async_copy(x_refs[a], slot(a, *me), local_sems.at[a]) for a in range(na)]
        for cp in mine:
            cp.start()
        first = []
        for a in range(na):
            first.append(copy(a, 0, me, sibling, src=x_refs[a]))
            first += [copy(a, 1 + j, me, (*chip, c), src=x_refs[a]) for j, chip in enumerate(chips)]
        for cp in first:
            cp.start()
        passed = []
        for j, chip in enumerate(chips):
            for a in range(na):
                copy(a, 1 + j, (*chip, c), me).wait_recv()
                fwd = copy(a, 4 + j, (*chip, c), sibling)
                fwd.start()
                passed.append(fwd)
        for a in range(na):
            copy(a, 0, sibling, me).wait_recv()
            for j, chip in enumerate(chips):
                copy(a, 4 + j, (*chip, 1 - c), me).wait_recv()
        for cp in first + passed:
            cp.wait_send()
        for cp in mine:
            cp.wait()

    anyspec = pl.BlockSpec(memory_space=pl.ANY)
    return _call(
        body,
        name="all_gather",
        out_shape=[jax.ShapeDtypeStruct((N_DEV,) + s.shape, s.dtype) for s in shards],
        in_specs=[anyspec] * na,
        out_specs=[anyspec] * na,
        scratch_shapes=[pltpu.SemaphoreType.DMA((na, 7)), pltpu.SemaphoreType.DMA((na, 7)),
                        pltpu.SemaphoreType.DMA((na,))],
    )(*shards)


def grad_exchange(slabs, small):
    na = len(slabs)

    def body(*refs):
        s_refs, sm_ref = refs[:na], refs[na]
        r_refs, rsm_ref = refs[na + 1:2 * na + 1], refs[2 * na + 1]
        send_sems, recv_sems, local_sems = refs[2 * na + 2:]
        x, y, c = _mesh_pos()
        me = 4 * x + 2 * y + c

        def peer(k):
            px = (1 - x) if k & 4 else x
            py = (1 - y) if k & 2 else y
            pc = (1 - c) if k & 1 else c
            return (px, py, pc), 4 * px + 2 * py + pc

        def src_dst(a, pid, slot):
            if a == na:
                return sm_ref, rsm_ref.at[slot]
            return s_refs[a].at[:, pid], r_refs[a].at[slot]

        def copy(a, k):
            to, pid = peer(k)
            src, dst = src_dst(a, pid, me)
            return pltpu.make_async_remote_copy(
                src_ref=src, dst_ref=dst, send_sem=send_sems.at[a, k - 1], recv_sem=recv_sems.at[a, k - 1],
                device_id=to, device_id_type=pl.DeviceIdType.MESH)

        def arrival(a, k):
            _, pid = peer(k)
            src, dst = src_dst(a, pid, pid)
            return pltpu.make_async_remote_copy(
                src_ref=src, dst_ref=dst, send_sem=send_sems.at[a, k - 1], recv_sem=recv_sems.at[a, k - 1],
                device_id=(x, y, c), device_id_type=pl.DeviceIdType.MESH)

        mine = []
        for a in range(na + 1):
            src, dst = src_dst(a, me, me)
            mine.append(pltpu.make_async_copy(src, dst, local_sems.at[a]))
        for cp in mine:
            cp.start()
        sends = [copy(a, k) for k in range(1, N_DEV) for a in range(na + 1)]
        for cp in sends:
            cp.start()
        for k in range(1, N_DEV):
            for a in range(na + 1):
                arrival(a, k).wait_recv()
        for cp in sends:
            cp.wait_send()
        for cp in mine:
            cp.wait()

    anyspec = pl.BlockSpec(memory_space=pl.ANY)
    outs = [jax.ShapeDtypeStruct((N_DEV, s.shape[0]) + s.shape[2:], s.dtype) for s in slabs]
    outs.append(jax.ShapeDtypeStruct((N_DEV,) + small.shape, small.dtype))
    return _call(
        body,
        name="grad_exchange",
        out_shape=outs,
        in_specs=[anyspec] * (na + 1),
        out_specs=[anyspec] * (na + 1),
        scratch_shapes=[pltpu.SemaphoreType.DMA((na + 1, 7)), pltpu.SemaphoreType.DMA((na + 1, 7)),
                        pltpu.SemaphoreType.DMA((na + 1,))],
    )(*slabs, small)


ADAM_BLOCK = 256 * 1024


def sum_adamw(recv, w, m, v, name):
    shape = w.shape
    cols = shape[-1]
    rows = w.size // cols
    tr = _tile(rows, max(8, ADAM_BLOCK // cols // 8 * 8), 8)
    c1 = 1.0 / (1.0 - ADAM_B1 ** ADAM_STEP)
    c2 = 1.0 / (1.0 - ADAM_B2 ** ADAM_STEP)

    def body(r_ref, w_ref, m_ref, v_ref, g_ref, d_ref, mo_ref, vo_ref):
        g = r_ref[0]
        for s in range(1, N_DEV):
            g = g + r_ref[s]
        mn = ADAM_B1 * m_ref[...] + (1.0 - ADAM_B1) * g
        vn = ADAM_B2 * v_ref[...] + (1.0 - ADAM_B2) * (g * g)
        g_ref[...] = g
        mo_ref[...] = mn
        vo_ref[...] = vn
        d_ref[...] = -ADAM_LR * ((mn * c1) / (jnp.sqrt(vn * c2) + ADAM_EPS) + ADAM_WD * w_ref[...])

    flat = pl.BlockSpec((tr, cols), lambda i: (i, 0))
    out = jax.ShapeDtypeStruct((rows, cols), F32)
    res = _call(
        body,
        name=name,
        grid=(rows // tr,),
        in_specs=[pl.BlockSpec((N_DEV, tr, cols), lambda i: (0, i, 0)), flat, flat, flat],
        out_specs=[flat, flat, flat, flat],
        out_shape=[out, out, out, out],
        compiler_params=_params(),
    )(recv.reshape(N_DEV, rows, cols), w.reshape(rows, cols), m.reshape(rows, cols), v.reshape(rows, cols))
    return [r.reshape(shape) for r in res]


def _attn_bias_vector(rel_bias):
    h = rel_bias.shape[0]
    n_far = BAND - MAX_REL
    n_near = BAND + CHUNK - 1 - n_far
    far = jnp.broadcast_to(rel_bias[:, 2 * MAX_REL:], (h, n_far))
    near = rel_bias[:, 2 * MAX_REL - n_near:2 * MAX_REL][:, ::-1]
    pad = jnp.zeros((h, 2 * QBLK - n_far - n_near), F32)
    return jnp.concatenate([far, near, pad], axis=1)[:, None, :]


def _s5_prepare(lre, lim, ldt, bre, bim, cre, cim):
    g, p = lre.shape
    lr = jnp.minimum(lre, -1e-4)
    dt = jnp.exp(ldt)[:, None]
    mag = jnp.exp(lr * dt)
    ar = mag * jnp.cos(lim * dt)
    ai = mag * jnp.sin(lim * dt)
    den = lr * lr + lim * lim
    coef_r = ((ar - 1.0) * lr + ai * lim) / den
    coef_i = (ai * lr - (ar - 1.0) * lim) / den
    bbar_r = coef_r[..., None] * bre - coef_i[..., None] * bim
    bbar_i = coef_r[..., None] * bim + coef_i[..., None] * bre
    eye = jnp.eye(g, dtype=F32)
    bd_in = lambda b: jnp.einsum("gpc,gh->gchp", b, eye).reshape(g * S5_GROUP, g * p)
    bd_out = lambda c: jnp.einsum("gcp,gh->gphc", c, eye).reshape(g * p, g * S5_GROUP)
    return (ar.reshape(1, g * p), ai.reshape(1, g * p), bd_in(bbar_r), bd_in(bbar_i), bd_out(cre), bd_out(cim))


SHARDED = ("ffn1_w_up", "ffn1_w_down", "w_in", "s5_w_glu", "w_br_s5", "w_br_attn", "conv_w_dw", "w_br_conv",
           "w_out", "ffn2_w_up", "ffn2_w_down")
WEIGHTS = ("ffn1_norm", "ffn1_w_up", "ffn1_w_down", "mix_norm", "w_in", "b_gate", "s5_lambda_re", "s5_lambda_im",
           "s5_log_dt", "s5_b_re", "s5_b_im", "s5_c_re", "s5_c_im", "s5_d", "s5_w_glu", "w_br_s5", "attn_q_gain",
           "attn_k_gain", "attn_rel_bias", "w_br_attn", "conv_w_dw", "conv_b_dw", "conv_ln_g", "conv_ln_b",
           "w_br_conv", "w_out", "ffn2_norm", "ffn2_w_up", "ffn2_w_down")
SMALL = tuple(nm for nm in WEIGHTS if nm not in SHARDED)
SMALL_LANES = 1024


def _cols_full(g):
    _, depth, k, nn = g.shape
    return g.transpose(1, 2, 0, 3).reshape(depth, k, N_DEV * nn)


def _rows_full(g):
    _, depth, r, cc = g.shape
    return g.transpose(1, 0, 2, 3).reshape(depth, N_DEV * r, cc)


def _cols_slabs(gfull):
    depth, k, c8 = gfull.shape
    return gfull.reshape(depth, k, N_DEV, c8 // N_DEV).transpose(0, 2, 1, 3)


def _heads(a, h):
    n = a.shape[0]
    return a.reshape(n, h, HEAD).transpose(1, 0, 2)


def _unheads(a):
    h, n, _ = a.shape
    return a.transpose(1, 0, 2).reshape(n, h * HEAD)


def _small_pack(t):
    flat = jnp.concatenate([t[nm].reshape(-1) for nm in SMALL])
    rows = -(-flat.shape[0] // SMALL_LANES)
    rows = -(-rows // 8) * 8
    return jnp.pad(flat, (0, rows * SMALL_LANES - flat.shape[0])).reshape(rows, SMALL_LANES)


def _small_unpack(flat, like):
    flat = flat.reshape(-1)
    out, off = {}, 0
    for nm in SMALL:
        out[nm] = flat[off:off + like[nm].size].reshape(like[nm].shape)
        off += like[nm].size
    return out


def _step(x, target, w, m, v):
    bsz, seq, d = x.shape
    n = bsz * seq
    depth = w["ffn1_norm"].shape[0]
    da, dss, dc = d // 2, d // 4, d // 4
    heads = da // HEAD
    gp = dss // S5_GROUP * S5_STATE
    mid = 3 * da + 2 * dc
    q0, k0, v0, z0, u0 = 3 * d, 3 * d + da, 3 * d + 2 * da, 3 * d + 3 * da, 3 * d + mid
    to_kernel_cols = lambda a: jnp.concatenate([a[..., dss + mid:], a[..., dss:dss + mid], a[..., :dss]], axis=-1)
    to_ref_cols = lambda a: jnp.concatenate([a[..., 3 * d + mid:], a[..., 3 * d:3 * d + mid], a[..., :3 * d]], axis=-1)

    gathered = dict(zip(SHARDED, all_gather(
        [w[nm] if nm == "conv_w_dw" else w[nm].astype(BF16) for nm in SHARDED])))
    full = {nm: gathered[nm] for nm in ("ffn1_w_up", "ffn2_w_up")}
    for nm in ("ffn1_w_down", "ffn2_w_down", "w_out"):
        full[nm] = _rows_full(gathered[nm])
    for nm in ("s5_w_glu", "w_br_s5", "w_br_attn", "w_br_conv", "conv_w_dw"):
        full[nm] = _cols_full(gathered[nm])
    full["w_in"] = to_kernel_cols(_cols_full(gathered["w_in"]))
    nff = full["ffn1_w_up"].shape[3]
    conv_w = jnp.pad(full["conv_w_dw"], ((0, 0), (0, HALO - CONV_W), (0, 0)))

    row = lambda a: a.reshape(1, -1)
    saved = []
    xin = x.reshape(n, d)
    for l in range(depth):
        s = {"x0": xin}
        s["x1"], s["a1"], s["b1"] = ffn_fwd(xin, row(w["ffn1_norm"][l]), full["ffn1_w_up"], full["ffn1_w_down"], l)
        proj = proj_fwd(s["x1"], row(w["mix_norm"][l]), full["w_in"], l)
        s["proj"] = proj
        prep_in = (w["s5_lambda_re"][l], w["s5_lambda_im"][l], w["s5_log_dt"][l], w["s5_b_re"][l], w["s5_b_im"][l],
                   w["s5_c_re"][l], w["s5_c_im"][l])
        (ar, ai, bm_r, bm_i, cm_r, cm_i), s["prep_vjp"] = jax.vjp(_s5_prepare, *prep_in)
        s["s5p"] = (ar, ai, bm_r.astype(BF16), bm_i.astype(BF16), cm_r.astype(BF16), cm_i.astype(BF16),
                    row(w["s5_d"][l]), full["s5_w_glu"], l)
        s["so"], s["xr"], s["xi"], s["ypre"] = s5_fwd(proj, u0 // dss, seq, *s["s5p"])
        tv, s["tv_vjp"] = jax.vjp(_attn_bias_vector, w["attn_rel_bias"][l])
        s["qkv"] = tuple(_heads(proj[:, c0:c0 + da], heads) for c0 in (q0, k0, v0))
        s["attnp"] = (tv, row(w["attn_q_gain"][l]), row(w["attn_k_gain"][l]))
        s["ao"] = _unheads(attn_fwd(*s["qkv"], *s["attnp"], seq))
        s["convp"] = (conv_w[l], row(w["conv_b_dw"][l]), row(w["conv_ln_g"][l]), row(w["conv_ln_b"][l]))
        s["h1"], s["co"] = conv_fwd(proj, z0 // (2 * dc), seq, *s["convp"])
        s["mergep"] = (row(w["b_gate"][l]), s["so"], s["ao"], s["co"], full["w_br_s5"], full["w_br_attn"],
                       full["w_br_conv"], full["w_out"], l)
        s["x2"] = merge_fwd(s["x1"], proj, *s["mergep"])
        xin, s["a2"], s["b2"] = ffn_fwd(s["x2"], row(w["ffn2_norm"][l]), full["ffn2_w_up"], full["ffn2_w_down"], l)
        saved.append(s)

    dx, loss = loss_head(xin, target.reshape(n, d))
    loss = lax.psum(loss[0, 0], ("x", "y", "c"))

    small_g = {nm: [None] * depth for nm in SMALL}
    slab = {nm: None for nm in SHARDED}
    conv_g = [None] * depth

    def ffn_grads(which, l, hn, dyb, dab_a, dab_b, act):
        up, down = which + "_w_up", which + "_w_down"
        up4, down4 = (depth, N_DEV, d, nff), (depth, FF_CHUNKS, nff, d)
        slab[up] = wgrad(hn, dab_a, "wg_ffn_up", up4, "col", layer=l, prev=slab[up])
        slab[up] = wgrad(hn, dab_b, "wg_ffn_up", up4, "col", g0=FF_CHUNKS, layer=l, prev=slab[up])
        slab[down] = wgrad(act, dyb, "wg_ffn_down", down4, "row", layer=l, prev=slab[down])

    for l in reversed(range(depth)):
        s = saved[l]
        dx, dg, hn, dyb, dab_a, dab_b, act = ffn_bwd(dx, s["x2"], row(w["ffn2_norm"][l]), s["a2"], s["b2"],
                                                     full["ffn2_w_up"], full["ffn2_w_down"], l)
        small_g["ffn2_norm"][l] = dg
        ffn_grads("ffn2", l, hn, dyb, dab_a, dab_b, act)

        dlog, dso, dao, dco, dbg, mg, dxb, dys, dya, dyc = merge_bwd(dx, s["proj"], *s["mergep"])
        small_g["b_gate"][l] = dbg
        slab["w_out"] = wgrad(mg, dxb, "wg_out", (depth, N_DEV, d // N_DEV, d), "row", layer=l, prev=slab["w_out"])
        for nm, act_in, dy_br in (("w_br_s5", s["so"], dys), ("w_br_attn", s["ao"], dya), ("w_br_conv", s["co"], dyc)):
            slab[nm] = wgrad(act_in, dy_br, "wg_" + nm, (depth, N_DEV, act_in.shape[1], d // N_DEV), "col",
                             layer=l, prev=slab[nm])

        dz, dwdw, dbdw, dlng, dlnb = conv_bwd(dco, s["h1"], s["proj"], z0 // (2 * dc), seq, *s["convp"])
        conv_g[l] = dwdw[:CONV_W]
        small_g["conv_b_dw"][l], small_g["conv_ln_g"][l], small_g["conv_ln_b"][l] = dbdw, dlng, dlnb

        dq, dkp, dkc, dvp, dvc, dtv, dgq, dgk = attn_bwd(_heads(dao, heads), *s["qkv"], *s["attnp"], seq)
        small_g["attn_q_gain"][l], small_g["attn_k_gain"][l] = dgq, dgk
        small_g["attn_rel_bias"][l] = s["tv_vjp"](dtv)[0]

        def from_prev(cur, prev):
            prev = prev.reshape(heads, bsz, seq, HEAD)
            prev = jnp.concatenate([prev[:, :, QBLK:], jnp.zeros_like(prev[:, :, :QBLK])], axis=2)
            return cur + prev.reshape(heads, n, HEAD)

        dk, dv = from_prev(dkc, dkp), from_prev(dvc, dvp)

        du, gr, gi, dyb5, glb, dzb, dar, dai, dd = s5_bwd(dso, s["ypre"], s["proj"], u0 // dss, s["xr"], s["xi"],
                                                          seq, *s["s5p"])
        small_g["s5_d"][l] = dd
        slab["s5_w_glu"] = wgrad(glb, dzb, "wg_s5_glu", (depth, 1, dss, 2 * dss), "col", layer=l, prev=slab["s5_w_glu"])
        one = lambda k1, k2: (1, 1, k1, k2)
        dcm_r = wgrad(s["xr"], dyb5, "wg_s5_c", one(gp, dss), "col")[0, 0]
        dcm_i = -wgrad(s["xi"], dyb5, "wg_s5_c", one(gp, dss), "col")[0, 0]
        dbm_r = wgrad(s["proj"], gr, "wg_s5_b", one(dss, gp), "col", a_cols=(u0, dss))[0, 0]
        dbm_i = wgrad(s["proj"], gi, "wg_s5_b", one(dss, gp), "col", a_cols=(u0, dss))[0, 0]
        pg = s["prep_vjp"]((dar, dai, dbm_r, dbm_i, dcm_r, dcm_i))
        for nm, gval in zip(("s5_lambda_re", "s5_lambda_im", "s5_log_dt", "s5_b_re", "s5_b_im", "s5_c_re", "s5_c_im"), pg):
            small_g[nm][l] = gval

        dproj = jnp.concatenate([dlog, _unheads(dq).astype(BF16), _unheads(dk).astype(BF16),
                                 _unheads(dv).astype(BF16), dz.astype(BF16), du.astype(BF16)], axis=1)
        dx, dgm, hn = proj_bwd(dproj, dx, s["x1"], row(w["mix_norm"][l]), full["w_in"], l)
        small_g["mix_norm"][l] = dgm
        slab["w_in"] = wgrad(hn, dproj, "wg_in", (depth, 1, d, 3 * d + mid + dss), "col", layer=l, prev=slab["w_in"])

        dx, dg, hn, dyb, dab_a, dab_b, act = ffn_bwd(dx, s["x0"], row(w["ffn1_norm"][l]), s["a1"], s["b1"],
                                                     full["ffn1_w_up"], full["ffn1_w_down"], l)
        small_g["ffn1_norm"][l] = dg
        ffn_grads("ffn1", l, hn, dyb, dab_a, dab_b, act)

    for nm in ("ffn1_w_down", "ffn2_w_down"):
        slab[nm] = slab[nm].reshape(depth, N_DEV, nff // 2, d)
    slab["w_in"] = _cols_slabs(to_ref_cols(slab["w_in"][:, 0]))
    slab["s5_w_glu"] = _cols_slabs(slab["s5_w_glu"][:, 0])
    slab["conv_w_dw"] = _cols_slabs(jnp.stack(conv_g))
    small_flat = _small_pack({nm: jnp.stack([g.reshape(w[nm].shape[1:]) for g in small_g[nm]]) for nm in SMALL})

    recv = grad_exchange([slab[nm] for nm in SHARDED], small_flat)
    outs = {}
    for nm, r in zip(SHARDED, recv[:-1]):
        outs[nm] = sum_adamw(r, w[nm], m[nm], v[nm], "adamw_" + nm)
    packed = sum_adamw(recv[-1], _small_pack(w), _small_pack(m), _small_pack(v), "adamw_small")
    unpacked = [_small_unpack(p, w) for p in packed]
    for nm in SMALL:
        outs[nm] = [u[nm] for u in unpacked]
    return loss, dx.reshape(x.shape), outs


def kernel(x, ffn1_norm, ffn1_w_up, ffn1_w_down, mix_norm, w_in, b_gate, s5_lambda_re, s5_lambda_im, s5_log_dt, s5_b_re, s5_b_im, s5_c_re, s5_c_im, s5_d, s5_w_glu, w_br_s5, attn_q_gain, attn_k_gain, attn_rel_bias, w_br_attn, conv_w_dw, conv_b_dw, conv_ln_g, conv_ln_b, w_br_conv, w_out, ffn2_norm, ffn2_w_up, ffn2_w_down, loss_target, m_ffn1_norm, m_ffn1_w_up, m_ffn1_w_down, m_mix_norm, m_w_in, m_b_gate, m_s5_lambda_re, m_s5_lambda_im, m_s5_log_dt, m_s5_b_re, m_s5_b_im, m_s5_c_re, m_s5_c_im, m_s5_d, m_s5_w_glu, m_w_br_s5, m_attn_q_gain, m_attn_k_gain, m_attn_rel_bias, m_w_br_attn, m_conv_w_dw, m_conv_b_dw, m_conv_ln_g, m_conv_ln_b, m_w_br_conv, m_w_out, m_ffn2_norm, m_ffn2_w_up, m_ffn2_w_down, v_ffn1_norm, v_ffn1_w_up, v_ffn1_w_down, v_mix_norm, v_w_in, v_b_gate, v_s5_lambda_re, v_s5_lambda_im, v_s5_log_dt, v_s5_b_re, v_s5_b_im, v_s5_c_re, v_s5_c_im, v_s5_d, v_s5_w_glu, v_w_br_s5, v_attn_q_gain, v_attn_k_gain, v_attn_rel_bias, v_w_br_attn, v_conv_w_dw, v_conv_b_dw, v_conv_ln_g, v_conv_ln_b, v_w_br_conv, v_w_out, v_ffn2_norm, v_ffn2_w_up, v_ffn2_w_down):
    args = locals()
    w = {nm: args[nm] for nm in WEIGHTS}
    m = {nm: args["m_" + nm] for nm in WEIGHTS}
    v = {nm: args["v_" + nm] for nm in WEIGHTS}
    loss, gx, outs = _step(x, loss_target, w, m, v)
    return (loss, gx, *[outs[nm][0] for nm in WEIGHTS], *[outs[nm][1] for nm in WEIGHTS],
            *[outs[nm][2] for nm in WEIGHTS], *[outs[nm][3] for nm in WEIGHTS])
```

```python
import math

import numpy as np
import jax
import jax.numpy as jnp
from jax import lax
from jax.experimental import pallas as pl
from jax.experimental.pallas import tpu as pltpu

F32 = jnp.float32
BF16 = jnp.bfloat16

CHUNK = 64
N_LEFT = 8
QBLK = CHUNK * N_LEFT
HEAD = 64
MAX_REL = 128
S5_GROUP = 16
S5_STATE = 64
CONV_W = 31
HALO = 32
EPS = 1e-6
NEG = -1e30
ADAM_LR, ADAM_B1, ADAM_B2, ADAM_EPS, ADAM_WD, ADAM_STEP = 0.001, 0.9, 0.999, 1e-08, 0.01, 10
N_DEV = 8
VMEM_LIMIT = 56 * 1024 * 1024


def _call(body, **kw):
    return pl.pallas_call(body, **kw)


def _params(**kw):
    return pltpu.CompilerParams(vmem_limit_bytes=VMEM_LIMIT, **kw)


def _tile(n, cap, unit=128):
    if n <= cap:
        return n
    d = (cap // unit) * unit
    while d >= unit:
        if n % d == 0:
            return d
        d -= unit
    raise ValueError(f"no tile for {n} under {cap}")


def _dot(a, b):
    return jnp.dot(a, b, preferred_element_type=F32)


def _dot_nt(a, b):
    return lax.dot_general(a, b, (((1,), (1,)), ((), ())), preferred_element_type=F32)


def _dot_tn(a, b):
    return lax.dot_general(a, b, (((0,), (0,)), ((), ())), preferred_element_type=F32)


def _sig(x):
    return 1.0 / (1.0 + jnp.exp(-x))


def _rms_fwd(x, g):
    rs = lax.rsqrt(jnp.mean(x * x, axis=-1, keepdims=True) + EPS)
    xhat = x * rs
    return xhat * g, xhat, rs


def _rms_bwd(dh, xhat, rs, g):
    dxh = dh * g
    dx = rs * (dxh - xhat * jnp.mean(dxh * xhat, axis=-1, keepdims=True))
    return dx, dh * xhat


_GELU_C = math.sqrt(2.0 / math.pi)


def _gelu(x):
    return 0.5 * x * (1.0 + jnp.tanh(_GELU_C * (x + 0.044715 * x * x * x)))


def _gelu_grad(x):
    t = jnp.tanh(_GELU_C * (x + 0.044715 * x * x * x))
    return 0.5 * (1.0 + t) + 0.5 * x * (1.0 - t * t) * _GELU_C * (1.0 + 3.0 * 0.044715 * x * x)


def _acc_out(ref, val, first):
    @pl.when(first)
    def _():
        ref[...] = val

    @pl.when(jnp.logical_not(first))
    def _():
        ref[...] += val


FF_CHUNKS = N_DEV // 2


def ffn_fwd(x, g, w_up, w_down, layer):
    n, d = x.shape
    nn = w_up.shape[3]
    tm = _tile(n, 512, 8)

    def body(x_ref, g_ref, wa_ref, wb_ref, wd_ref, xo_ref, a_ref, b_ref, hn_ref, acc_ref):
        j = pl.program_id(1)

        @pl.when(j == 0)
        def _():
            h, _, _ = _rms_fwd(x_ref[...], g_ref[...])
            hn_ref[...] = h.astype(BF16)
            acc_ref[...] = jnp.zeros_like(acc_ref)

        hn = hn_ref[...]
        a = _dot(hn, wa_ref[0, 0])
        b = _dot(hn, wb_ref[0, 0])
        a_ref[0] = a.astype(BF16)
        b_ref[0] = b.astype(BF16)
        act = a * _sig(a) * b
        acc_ref[...] += _dot(act.astype(BF16), wd_ref[0])

        @pl.when(j == FF_CHUNKS - 1)
        def _():
            xo_ref[...] = x_ref[...] + 0.5 * acc_ref[...]

    return _call(
        body,
        name="ffn_fwd",
        grid=(n // tm, FF_CHUNKS),
        in_specs=[
            pl.BlockSpec((tm, d), lambda i, j: (i, 0)),
            pl.BlockSpec((1, d), lambda i, j: (0, 0)),
            pl.BlockSpec((1, 1, d, nn), lambda i, j: (j, layer, 0, 0)),
            pl.BlockSpec((1, 1, d, nn), lambda i, j: (j + FF_CHUNKS, layer, 0, 0)),
            pl.BlockSpec((1, nn, d), lambda i, j: (layer, j, 0)),
        ],
        out_specs=[
            pl.BlockSpec((tm, d), lambda i, j: (i, 0)),
            pl.BlockSpec((1, tm, nn), lambda i, j: (j, i, 0)),
            pl.BlockSpec((1, tm, nn), lambda i, j: (j, i, 0)),
        ],
        out_shape=[
            jax.ShapeDtypeStruct((n, d), F32),
            jax.ShapeDtypeStruct((FF_CHUNKS, n, nn), BF16),
            jax.ShapeDtypeStruct((FF_CHUNKS, n, nn), BF16),
        ],
        scratch_shapes=[pltpu.VMEM((tm, d), BF16), pltpu.VMEM((tm, d), F32)],
        compiler_params=_params(),
    )(x, g, w_up, w_up, w_down)


def ffn_bwd(dy, x, g, a, b, w_up, w_down, layer):
    n, d = x.shape
    nn = w_up.shape[3]
    tm = _tile(n, 512, 8)

    def body(dy_ref, x_ref, g_ref, a_ref, b_ref, wa_ref, wb_ref, wd_ref,
             dx_ref, dg_ref, hn_ref, dyb_ref, da_ref, db_ref, act_ref, dyb_s, dh_ref):
        i, j = pl.program_id(0), pl.program_id(1)

        @pl.when(j == 0)
        def _():
            h, _, _ = _rms_fwd(x_ref[...], g_ref[...])
            hn_ref[...] = h.astype(BF16)
            dyb = (0.5 * dy_ref[...]).astype(BF16)
            dyb_ref[...] = dyb
            dyb_s[...] = dyb
            dh_ref[...] = jnp.zeros_like(dh_ref)

        dact = _dot_nt(dyb_s[...], wd_ref[0])
        a32 = a_ref[0].astype(F32)
        b32 = b_ref[0].astype(F32)
        s = _sig(a32)
        sil = a32 * s
        da = (dact * b32 * (s * (1.0 + a32 * (1.0 - s)))).astype(BF16)
        db = (dact * sil).astype(BF16)
        da_ref[0] = da
        db_ref[0] = db
        act_ref[0] = (sil * b32).astype(BF16)
        dh_ref[...] += _dot_nt(da, wa_ref[0, 0]) + _dot_nt(db, wb_ref[0, 0])

        @pl.when(j == FF_CHUNKS - 1)
        def _():
            gg = g_ref[...]
            _, xhat, rs = _rms_fwd(x_ref[...], gg)
            dxn, dgr = _rms_bwd(dh_ref[...], xhat, rs, gg)
            dx_ref[...] = dy_ref[...] + dxn
            _acc_out(dg_ref, jnp.sum(dgr, axis=0, keepdims=True), i == 0)

    tok = pl.BlockSpec((tm, d), lambda i, j: (i, 0))
    chunk = pl.BlockSpec((1, tm, nn), lambda i, j: (j, i, 0))
    vec = pl.BlockSpec((1, d), lambda i, j: (0, 0))
    chunks = jax.ShapeDtypeStruct((FF_CHUNKS, n, nn), BF16)
    return _call(
        body,
        name="ffn_bwd",
        grid=(n // tm, FF_CHUNKS),
        in_specs=[
            tok, tok, vec, chunk, chunk,
            pl.BlockSpec((1, 1, d, nn), lambda i, j: (j, layer, 0, 0)),
            pl.BlockSpec((1, 1, d, nn), lambda i, j: (j + FF_CHUNKS, layer, 0, 0)),
            pl.BlockSpec((1, nn, d), lambda i, j: (layer, j, 0)),
        ],
        out_specs=[tok, vec, tok, tok, chunk, chunk, chunk],
        out_shape=[
            jax.ShapeDtypeStruct((n, d), F32),
            jax.ShapeDtypeStruct((1, d), F32),
            jax.ShapeDtypeStruct((n, d), BF16),
            jax.ShapeDtypeStruct((n, d), BF16),
            chunks, chunks, chunks,
        ],
        scratch_shapes=[pltpu.VMEM((tm, d), BF16), pltpu.VMEM((tm, d), F32)],
        compiler_params=_params(),
    )(dy, x, g, a, b, w_up, w_up, w_down)


def wgrad(a, b, name, out4, mode, *, g0=0, layer=0, prev=None, a_cols=None, dtype=BF16):
    a3 = a if a.ndim == 3 else a[None]
    b3 = b if b.ndim == 3 else b[None]
    sa, n, ka = a3.shape
    sb, _, kb = b3.shape
    a0 = 0
    if a_cols is not None:
        a0, ka = a_cols
    k1, k2 = sa * ka, sb * kb
    depth, groups, rr, cc = out4
    t1 = _tile(math.gcd(ka, rr), 1024)
    t2 = _tile(math.gcd(kb, cc), 1024)
    tn = _tile(n, 1024, 8)
    na, nb = ka // t1, kb // t2
    if mode == "col":
        assert rr == k1 and k2 % cc == 0 and g0 + k2 // cc <= groups
        per = cc // t2
        omap = lambda i, j, k: (layer, g0 + j // per, i, j % per)
    else:
        assert cc == k2 and k1 % rr == 0 and g0 + k1 // rr <= groups
        per = rr // t1
        omap = lambda i, j, k: (layer, g0 + i // per, i % per, j)

    nk = n // tn

    def body(a_ref, b_ref, *rest):
        o_ref, acc_ref = rest[-2], rest[-1]
        k = pl.program_id(2)

        @pl.when(k == 0)
        def _():
            acc_ref[...] = jnp.zeros_like(acc_ref)

        acc_ref[...] += _dot_tn(a_ref[0].astype(BF16), b_ref[0].astype(BF16))

        @pl.when(k == nk - 1)
        def _():
            o_ref[0, 0] = acc_ref[...].astype(dtype)

    in_specs = [
        pl.BlockSpec((1, tn, t1), lambda i, j, k: (i // na, k, a0 // t1 + i % na)),
        pl.BlockSpec((1, tn, t2), lambda i, j, k: (j // nb, k, j % nb)),
    ]
    args = [a3, b3]
    extra = {}
    if prev is not None:
        in_specs.append(pl.BlockSpec(memory_space=pl.ANY))
        args.append(prev)
        extra["input_output_aliases"] = {2: 0}
    return _call(
        body,
        name=name,
        grid=(k1 // t1, k2 // t2, nk),
        in_specs=in_specs,
        out_specs=pl.BlockSpec((1, 1, t1, t2), omap),
        out_shape=jax.ShapeDtypeStruct(out4, dtype),
        scratch_shapes=[pltpu.VMEM((t1, t2), F32)],
        compiler_params=_params(),
        **extra,
    )(*args)


def proj_fwd(x, g, w, layer):
    n, d = x.shape
    c = w.shape[2]
    tm, tc = _tile(n, 512, 8), _tile(c, 768)

    def body(x_ref, g_ref, w_ref, o_ref, hn_ref):
        @pl.when(pl.program_id(1) == 0)
        def _():
            h, _, _ = _rms_fwd(x_ref[...], g_ref[...])
            hn_ref[...] = h.astype(BF16)

        o_ref[...] = _dot(hn_ref[...], w_ref[0])

    return _call(
        body,
        name="proj_fwd",
        grid=(n // tm, c // tc),
        in_specs=[
            pl.BlockSpec((tm, d), lambda i, j: (i, 0)),
            pl.BlockSpec((1, d), lambda i, j: (0, 0)),
            pl.BlockSpec((1, d, tc), lambda i, j: (layer, 0, j)),
        ],
        out_specs=pl.BlockSpec((tm, tc), lambda i, j: (i, j)),
        out_shape=jax.ShapeDtypeStruct((n, c), F32),
        scratch_shapes=[pltpu.VMEM((tm, d), BF16)],
        compiler_params=_params(),
    )(x, g, w)


def proj_bwd(dproj, dres, x, g, w, layer):
    n, d = x.shape
    c = w.shape[2]
    tm, tc = _tile(n, 512, 8), _tile(c, 768)
    nc = c // tc

    def body(dp_ref, dr_ref, x_ref, g_ref, w_ref, dx_ref, dg_ref, hn_ref, dh_ref):
        i, j = pl.program_id(0), pl.program_id(1)

        @pl.when(j == 0)
        def _():
            dh_ref[...] = jnp.zeros_like(dh_ref)

        dh_ref[...] += _dot_nt(dp_ref[...], w_ref[0])

        @pl.when(j == nc - 1)
        def _():
            gg = g_ref[...]
            h, xhat, rs = _rms_fwd(x_ref[...], gg)
            hn_ref[...] = h.astype(BF16)
            dxn, dgr = _rms_bwd(dh_ref[...], xhat, rs, gg)
            dx_ref[...] = dr_ref[...] + dxn
            _acc_out(dg_ref, jnp.sum(dgr, axis=0, keepdims=True), i == 0)

    return _call(
        body,
        name="proj_bwd",
        grid=(n // tm, nc),
        in_specs=[
            pl.BlockSpec((tm, tc), lambda i, j: (i, j)),
            pl.BlockSpec((tm, d), lambda i, j: (i, 0)),
            pl.BlockSpec((tm, d), lambda i, j: (i, 0)),
            pl.BlockSpec((1, d), lambda i, j: (0, 0)),
            pl.BlockSpec((1, d, tc), lambda i, j: (layer, 0, j)),
        ],
        out_specs=[
            pl.BlockSpec((tm, d), lambda i, j: (i, 0)),
            pl.BlockSpec((1, d), lambda i, j: (0, 0)),
            pl.BlockSpec((tm, d), lambda i, j: (i, 0)),
        ],
        out_shape=[
            jax.ShapeDtypeStruct((n, d), F32),
            jax.ShapeDtypeStruct((1, d), F32),
            jax.ShapeDtypeStruct((n, d), BF16),
        ],
        scratch_shapes=[pltpu.VMEM((tm, d), F32)],
        compiler_params=_params(),
    )(dproj, dres, x, g, w)


S5_TS = 512


def _cmul(ar, ai, br, bi):
    return ar * br - ai * bi, ar * bi + ai * br


def _s5_tables(ar, ai, reverse):
    gp = ar.shape[1]
    if reverse:
        ai = -ai
    a1r, a1i = jnp.broadcast_to(ar, (8, gp)), jnp.broadcast_to(ai, (8, gp))
    a2r, a2i = _cmul(a1r, a1i, a1r, a1i)
    a4r, a4i = _cmul(a2r, a2i, a2r, a2i)
    a8r, a8i = _cmul(a4r, a4i, a4r, a4i)
    row = lax.broadcasted_iota(jnp.int32, (8, gp), 0)
    e = (8 - row) if reverse else (row + 1)
    pr, pi = jnp.ones((8, gp), F32), jnp.zeros((8, gp), F32)
    for bit, (fr, fi) in ((1, (a1r, a1i)), (2, (a2r, a2i)), (4, (a4r, a4i)), (8, (a8r, a8i))):
        nr, ni = _cmul(pr, pi, fr, fi)
        on = (e & bit) != 0
        pr, pi = jnp.where(on, nr, pr), jnp.where(on, ni, pi)
    return (a1r, a1i, a2r, a2i, a4r, a4i, pr, pi)


def _s5_scan(xr_ref, xi_ref, tab_ref, cr_ref, ci_ref, ts, reverse):
    gp = xr_ref.shape[1]
    nt = ts // 8
    row = lax.broadcasted_iota(jnp.int32, (8, gp), 0)

    def shifted(v, s):
        if reverse:
            return jnp.where(row < 8 - s, pltpu.roll(v, 8 - s, 0), 0.0)
        return jnp.where(row >= s, pltpu.roll(v, s, 0), 0.0)

    def step(k, carry):
        cr, ci = carry
        t = (nt - 1 - k) if reverse else k
        r0 = pl.multiple_of(t * 8, 8)
        br = xr_ref[pl.ds(r0, 8), :]
        bi = xi_ref[pl.ds(r0, 8), :]
        for q, s in enumerate((1, 2, 4)):
            fr, fi = tab_ref[2 * q], tab_ref[2 * q + 1]
            sr, si = shifted(br, s), shifted(bi, s)
            mr, mi = _cmul(fr, fi, sr, si)
            br, bi = br + mr, bi + mi
        mr, mi = _cmul(tab_ref[6], tab_ref[7], cr, ci)
        br, bi = br + mr, bi + mi
        xr_ref[pl.ds(r0, 8), :] = br
        xi_ref[pl.ds(r0, 8), :] = bi
        edge = 0 if reverse else 7
        return (jnp.broadcast_to(br[edge:edge + 1, :], (8, gp)),
                jnp.broadcast_to(bi[edge:edge + 1, :], (8, gp)))

    cr, ci = lax.fori_loop(0, nt, step, (cr_ref[...], ci_ref[...]))
    cr_ref[...] = cr
    ci_ref[...] = ci


def s5_fwd(proj, ucol, seq, ar, ai, bm_r, bm_i, cm_r, cm_i, dskip, w_glu, layer):
    n = proj.shape[0]
    ds, gp = bm_r.shape
    ts = min(S5_TS, seq)
    nt = seq // ts

    def body(u_ref, ar_ref, ai_ref, bmr_ref, bmi_ref, cmr_ref, cmi_ref, d_ref, wg_ref,
             out_ref, xr_ref, xi_ref, yp_ref, tab_ref, cr_ref, ci_ref):
        @pl.when(pl.program_id(1) == 0)
        def _():
            for q, v in enumerate(_s5_tables(ar_ref[...], ai_ref[...], False)):
                tab_ref[q] = v
            cr_ref[...] = jnp.zeros_like(cr_ref)
            ci_ref[...] = jnp.zeros_like(ci_ref)

        u = u_ref[...]
        ub = u.astype(BF16)
        xr_ref[...] = _dot(ub, bmr_ref[...])
        xi_ref[...] = _dot(ub, bmi_ref[...])
        _s5_scan(xr_ref, xi_ref, tab_ref, cr_ref, ci_ref, ts, False)
        y = (_dot(xr_ref[...].astype(BF16), cmr_ref[...]) - _dot(xi_ref[...].astype(BF16), cmi_ref[...])
             + d_ref[...] * u)
        yp_ref[...] = y
        z = _dot(_gelu(y).astype(BF16), wg_ref[0])
        out_ref[...] = z[:, :ds] * _sig(z[:, ds:])

    full = lambda shape: pl.BlockSpec(shape, lambda b, t: (0, 0))
    return _call(
        body,
        name="s5_fwd",
        grid=(n // seq, nt),
        in_specs=[
            pl.BlockSpec((ts, ds), lambda b, t: (b * nt + t, ucol)),
            full((1, gp)), full((1, gp)), full((ds, gp)), full((ds, gp)), full((gp, ds)), full((gp, ds)),
            full((1, ds)), pl.BlockSpec((1, ds, 2 * ds), lambda b, t: (layer, 0, 0)),
        ],
        out_specs=[
            pl.BlockSpec((ts, ds), lambda b, t: (b * nt + t, 0)),
            pl.BlockSpec((ts, gp), lambda b, t: (b * nt + t, 0)),
            pl.BlockSpec((ts, gp), lambda b, t: (b * nt + t, 0)),
            pl.BlockSpec((ts, ds), lambda b, t: (b * nt + t, 0)),
        ],
        out_shape=[
            jax.ShapeDtypeStruct((n, ds), F32),
            jax.ShapeDtypeStruct((n, gp), F32),
            jax.ShapeDtypeStruct((n, gp), F32),
            jax.ShapeDtypeStruct((n, ds), F32),
        ],
        scratch_shapes=[pltpu.VMEM((8, 8, gp), F32), pltpu.VMEM((8, gp), F32), pltpu.VMEM((8, gp), F32)],
        compiler_params=_params(),
    )(proj, ar, ai, bm_r, bm_i, cm_r, cm_i, dskip, w_glu)


def s5_bwd(dout, ypre, proj, ucol, xr, xi, seq, ar, ai, bm_r, bm_i, cm_r, cm_i, dskip, w_glu, layer):
    n = proj.shape[0]
    ds, gp = bm_r.shape
    ts = min(S5_TS, seq)
    nt = seq // ts

    def body(do_ref, yp_ref, u_ref, xr_ref, xi_ref, hr_ref, hi_ref, ar_ref, ai_ref, bmr_ref, bmi_ref,
             cmr_ref, cmi_ref, d_ref, wg_ref,
             du_ref, gr_ref, gi_ref, dyb_ref, glb_ref, dzb_ref, dar_ref, dai_ref, dd_ref,
             tab_ref, cr_ref, ci_ref):
        b, t = pl.program_id(0), pl.program_id(1)
        first = jnp.logical_and(b == 0, t == 0)

        @pl.when(t == 0)
        def _():
            for q, v in enumerate(_s5_tables(ar_ref[...], ai_ref[...], True)):
                tab_ref[q] = v
            cr_ref[...] = jnp.zeros_like(cr_ref)
            ci_ref[...] = jnp.zeros_like(ci_ref)

        yp = yp_ref[...]
        u = u_ref[...]
        gl = _gelu(yp).astype(BF16)
        glb_ref[...] = gl
        z = _dot(gl, wg_ref[0])
        za, sg = z[:, :ds], _sig(z[:, ds:])
        do = do_ref[...]
        da = (do * sg).astype(BF16)
        dg = (do * za * sg * (1.0 - sg)).astype(BF16)
        dzb_ref[:, :ds] = da
        dzb_ref[:, ds:] = dg
        dgl = _dot_nt(da, wg_ref[0, :, :ds]) + _dot_nt(dg, wg_ref[0, :, ds:])
        dyp = dgl * _gelu_grad(yp)
        dypb = dyp.astype(BF16)
        dyb_ref[...] = dypb
        _acc_out(dd_ref, jnp.sum(dyp * u, axis=0, keepdims=True), first)

        gr_ref[...] = _dot_nt(dypb, cmr_ref[...])
        gi_ref[...] = -_dot_nt(dypb, cmi_ref[...])
        _s5_scan(gr_ref, gi_ref, tab_ref, cr_ref, ci_ref, ts, True)
        gr, gi = gr_ref[...], gi_ref[...]
        du_ref[...] = d_ref[...] * dyp + _dot_nt(gr.astype(BF16), bmr_ref[...]) + _dot_nt(gi.astype(BF16), bmi_ref[...])

        row = lax.broadcasted_iota(jnp.int32, (ts, gp), 0)
        live = jnp.where(t == nt - 1, 0.0, 1.0)
        pr = jnp.broadcast_to(hr_ref[7:8, :] * live, (ts, gp))
        pi = jnp.broadcast_to(hi_ref[7:8, :] * live, (ts, gp))
        sr = jnp.where(row == 0, pr, pltpu.roll(xr_ref[...], 1, 0))
        si = jnp.where(row == 0, pi, pltpu.roll(xi_ref[...], 1, 0))
        _acc_out(dar_ref, jnp.sum(gr * sr + gi * si, axis=0, keepdims=True), first)
        _acc_out(dai_ref, jnp.sum(gi * sr - gr * si, axis=0, keepdims=True), first)

    full = lambda shape: pl.BlockSpec(shape, lambda b, t: (0, 0))
    blk = lambda w, col=0: pl.BlockSpec((ts, w), lambda b, t: (b * nt + nt - 1 - t, col))
    halo = pl.BlockSpec((8, gp), lambda b, t: (jnp.maximum((b * seq + (nt - 1 - t) * ts) // 8 - 1, 0), 0))
    return _call(
        body,
        name="s5_bwd",
        grid=(n // seq, nt),
        in_specs=[
            blk(ds), blk(ds), blk(ds, ucol), blk(gp), blk(gp), halo, halo,
            full((1, gp)), full((1, gp)), full((ds, gp)), full((ds, gp)), full((gp, ds)), full((gp, ds)),
            full((1, ds)), pl.BlockSpec((1, ds, 2 * ds), lambda b, t: (layer, 0, 0)),
        ],
        out_specs=[
            blk(ds), blk(gp), blk(gp), blk(ds), blk(ds), blk(2 * ds),
            full((1, gp)), full((1, gp)), full((1, ds)),
        ],
        out_shape=[
            jax.ShapeDtypeStruct((n, ds), F32),
            jax.ShapeDtypeStruct((n, gp), F32),
            jax.ShapeDtypeStruct((n, gp), F32),
            jax.ShapeDtypeStruct((n, ds), BF16),
            jax.ShapeDtypeStruct((n, ds), BF16),
            jax.ShapeDtypeStruct((n, 2 * ds), BF16),
            jax.ShapeDtypeStruct((1, gp), F32),
            jax.ShapeDtypeStruct((1, gp), F32),
            jax.ShapeDtypeStruct((1, ds), F32),
        ],
        scratch_shapes=[pltpu.VMEM((8, 8, gp), F32), pltpu.VMEM((8, gp), F32), pltpu.VMEM((8, gp), F32)],
        compiler_params=_params(),
    )(dout, ypre, proj, xr, xi, xr, xi, ar, ai, bm_r, bm_i, cm_r, cm_i, dskip, w_glu)


BAND = QBLK + CHUNK
NCH = QBLK // CHUNK


def _attn_specs(nq):
    cur = pl.BlockSpec((1, QBLK, HEAD), lambda h, b, i: (h, b * nq + i, 0))
    prev = pl.BlockSpec((1, QBLK, HEAD), lambda h, b, i: (h, b * nq + jnp.maximum(i - 1, 0), 0))
    vec = pl.BlockSpec((1, 1, 2 * QBLK), lambda h, b, i: (h, 0, 0))
    gain = pl.BlockSpec((1, HEAD), lambda h, b, i: (0, 0))
    return cur, prev, vec, gain


def _attn_build_table(tv, bias_ref, tab_ref):
    w = 2 * QBLK
    for qi in range(CHUNK):
        bias_ref[qi:qi + 1, :] = pltpu.roll(tv, (qi - (CHUNK - 1)) % w, 1)
    bias = bias_ref[...]
    lane = lax.broadcasted_iota(jnp.int32, (CHUNK, w), 1)
    for c in range(NCH):
        rolled = bias if c == 0 else pltpu.roll(bias, CHUNK * c, 1)
        ok = jnp.logical_and(lane >= CHUNK * c, lane < CHUNK * c + BAND)
        tab_ref[CHUNK * c:CHUNK * (c + 1), :] = jnp.where(ok, rolled, NEG)


def _attn_reduce_table(dt_ref, bias_ref):
    w = 2 * QBLK
    acc = dt_ref[0:CHUNK, :]
    for c in range(1, NCH):
        acc = acc + pltpu.roll(dt_ref[CHUNK * c:CHUNK * (c + 1), :], w - CHUNK * c, 1)
    bias_ref[...] = acc
    out = jnp.zeros((1, w), F32)
    for qi in range(CHUNK):
        out = out + pltpu.roll(bias_ref[qi:qi + 1, :], ((CHUNK - 1) - qi) % w, 1)
    return out


def _attn_probs(q, kp, kc, gq, gk, table, first_block):
    qn, qhat, qrs = _rms_fwd(q, gq)
    k = jnp.concatenate([kp, kc], axis=0)
    kn, khat, krs = _rms_fwd(k, gk)
    s = _dot_nt(qn.astype(BF16), kn.astype(BF16)) * (HEAD ** -0.5) + table
    col = lax.broadcasted_iota(jnp.int32, s.shape, 1)
    s = jnp.where(jnp.logical_and(first_block, col < QBLK), NEG, s)
    e = jnp.exp(s - jnp.max(s, axis=-1, keepdims=True))
    p = e * (1.0 / jnp.sum(e, axis=-1, keepdims=True))
    return p, (qn, qhat, qrs), (kn, khat, krs)


def attn_fwd(q, k, v, tv, gq, gk, seq):
    h, n, _ = q.shape
    nq = seq // QBLK
    cur, prev, vec, gain = _attn_specs(nq)

    def body(q_ref, kp_ref, kc_ref, vp_ref, vc_ref, tv_ref, gq_ref, gk_ref, o_ref, bias_ref, tab_ref):
        @pl.when(jnp.logical_and(pl.program_id(1) == 0, pl.program_id(2) == 0))
        def _():
            _attn_build_table(tv_ref[0], bias_ref, tab_ref)

        p, _, _ = _attn_probs(q_ref[0], kp_ref[0], kc_ref[0], gq_ref[...], gk_ref[...], tab_ref[...],
                              pl.program_id(2) == 0)
        vv = jnp.concatenate([vp_ref[0], vc_ref[0]], axis=0).astype(BF16)
        o_ref[0] = _dot(p.astype(BF16), vv)

    return _call(
        body,
        name="attn_fwd",
        grid=(h, n // seq, nq),
        in_specs=[cur, prev, cur, prev, cur, vec, gain, gain],
        out_specs=cur,
        out_shape=jax.ShapeDtypeStruct((h, n, HEAD), F32),
        scratch_shapes=[pltpu.VMEM((CHUNK, 2 * QBLK), F32), pltpu.VMEM((QBLK, 2 * QBLK), F32)],
        compiler_params=_params(),
    )(q, k, k, v, v, tv, gq, gk)


def attn_bwd(do, q, k, v, tv, gq, gk, seq):
    h, n, _ = q.shape
    nb = n // seq
    nq = seq // QBLK
    cur, prev, vec, gain = _attn_specs(nq)

    def body(do_ref, q_ref, kp_ref, kc_ref, vp_ref, vc_ref, tv_ref, gq_ref, gk_ref,
             dq_ref, dkp_ref, dkc_ref, dvp_ref, dvc_ref, dtv_ref, dgq_ref, dgk_ref, bias_ref, tab_ref, dt_ref):
        hh, b, i = pl.program_id(0), pl.program_id(1), pl.program_id(2)
        head_start = jnp.logical_and(b == 0, i == 0)

        @pl.when(head_start)
        def _():
            _attn_build_table(tv_ref[0], bias_ref, tab_ref)

        gq_, gk_ = gq_ref[...], gk_ref[...]
        p, (qn, qhat, qrs), (kn, khat, krs) = _attn_probs(
            q_ref[0], kp_ref[0], kc_ref[0], gq_, gk_, tab_ref[...], i == 0)
        dob = do_ref[0].astype(BF16)
        vv = jnp.concatenate([vp_ref[0], vc_ref[0]], axis=0).astype(BF16)
        dv = _dot_tn(p.astype(BF16), dob)
        dp = _dot_nt(dob, vv)
        ds = p * (dp - jnp.sum(p * dp, axis=-1, keepdims=True))
        _acc_out(dt_ref, ds, head_start)
        dsb = (ds * (HEAD ** -0.5)).astype(BF16)
        dqn = _dot(dsb, kn.astype(BF16))
        dkn = _dot_tn(dsb, qn.astype(BF16))
        dq, dgq_rows = _rms_bwd(dqn, qhat, qrs, gq_)
        dk, dgk_rows = _rms_bwd(dkn, khat, krs, gk_)
        dq_ref[0] = dq
        dkp_ref[0] = dk[:QBLK]
        dkc_ref[0] = dk[QBLK:]
        dvp_ref[0] = dv[:QBLK]
        dvc_ref[0] = dv[QBLK:]
        first = jnp.logical_and(hh == 0, head_start)
        _acc_out(dgq_ref, jnp.sum(dgq_rows, axis=0, keepdims=True), first)
        _acc_out(dgk_ref, jnp.sum(dgk_rows, axis=0, keepdims=True), first)

        @pl.when(jnp.logical_and(b == nb - 1, i == nq - 1))
        def _():
            dtv_ref[0] = _attn_reduce_table(dt_ref, bias_ref)

    hm = jax.ShapeDtypeStruct((h, n, HEAD), F32)
    return _call(
        body,
        name="attn_bwd",
        grid=(h, nb, nq),
        in_specs=[cur, cur, prev, cur, prev, cur, vec, gain, gain],
        out_specs=[cur, cur, cur, cur, cur, vec, gain, gain],
        out_shape=[hm, hm, hm, hm, hm, jax.ShapeDtypeStruct(tv.shape, F32),
                   jax.ShapeDtypeStruct((1, HEAD), F32), jax.ShapeDtypeStruct((1, HEAD), F32)],
        scratch_shapes=[pltpu.VMEM((CHUNK, 2 * QBLK), F32), pltpu.VMEM((QBLK, 2 * QBLK), F32),
                        pltpu.VMEM((QBLK, 2 * QBLK), F32)],
        compiler_params=_params(),
    )(do, q, k, k, v, v, tv, gq, gk)


CONV_TC = 512


def _ln_fwd(h1, g, b):
    mu = jnp.mean(h1, axis=-1, keepdims=True)
    xc = h1 - mu
    rs = lax.rsqrt(jnp.mean(xc * xc, axis=-1, keepdims=True) + EPS)
    yhat = xc * rs
    return yhat * g + b, yhat, rs


def _glu(z, dc):
    return z[:, :dc] * _sig(z[:, dc:])


def conv_fwd(proj, zcol, seq, w, bdw, lng, lnb):
    n = proj.shape[0]
    dc = w.shape[1]
    tc = min(CONV_TC, seq)
    nt = seq // tc

    def body(z_ref, zp_ref, w_ref, b_ref, g_ref, lb_ref, h1_ref, o_ref, ext_ref):
        live = jnp.where(pl.program_id(1) == 0, 0.0, 1.0)
        ext_ref[pl.ds(0, HALO), :] = _glu(zp_ref[...], dc) * live
        ext_ref[pl.ds(HALO, tc), :] = _glu(z_ref[...], dc)
        acc = jnp.zeros((tc, dc), F32) + b_ref[...]
        for j in range(CONV_W):
            acc = acc + w_ref[j:j + 1, :] * ext_ref[pl.ds(HALO - (CONV_W - 1) + j, tc), :]
        h1_ref[...] = acc
        ln, _, _ = _ln_fwd(acc, g_ref[...], lb_ref[...])
        o_ref[...] = ln * _sig(ln)

    full = lambda shape: pl.BlockSpec(shape, lambda b, t: (0, 0))
    return _call(
        body,
        name="conv_fwd",
        grid=(n // seq, nt),
        in_specs=[
            pl.BlockSpec((tc, 2 * dc), lambda b, t: (b * nt + t, zcol)),
            pl.BlockSpec((HALO, 2 * dc), lambda b, t: (jnp.maximum((b * seq + t * tc) // HALO - 1, 0), zcol)),
            full((HALO, dc)), full((1, dc)), full((1, dc)), full((1, dc)),
        ],
        out_specs=[
            pl.BlockSpec((tc, dc), lambda b, t: (b * nt + t, 0)),
            pl.BlockSpec((tc, dc), lambda b, t: (b * nt + t, 0)),
        ],
        out_shape=[jax.ShapeDtypeStruct((n, dc), F32), jax.ShapeDtypeStruct((n, dc), F32)],
        scratch_shapes=[pltpu.VMEM((tc + HALO, dc), F32)],
        compiler_params=_params(),
    )(proj, proj, w, bdw, lng, lnb)


def conv_bwd(dco, h1, proj, zcol, seq, w, bdw, lng, lnb):
    n = proj.shape[0]
    dc = w.shape[1]
    tc = min(CONV_TC, seq)
    nt = seq // tc
    nrow = n // HALO

    def body(do_ref, don_ref, h1_ref, h1n_ref, z_ref, zp_ref, w_ref, g_ref, lb_ref,
             dz_ref, dw_ref, db_ref, dg_ref, dlb_ref, ext_ref, dext_ref):
        b, t = pl.program_id(0), pl.program_id(1)
        first = jnp.logical_and(b == 0, t == 0)
        g, lb = g_ref[...], lb_ref[...]

        def dh1_of(do, h1):
            ln, yhat, rs = _ln_fwd(h1, g, lb)
            s = _sig(ln)
            dln = do * (s * (1.0 + ln * (1.0 - s)))
            dyh = dln * g
            dh1 = rs * (dyh - jnp.mean(dyh, axis=-1, keepdims=True)
                        - yhat * jnp.mean(dyh * yhat, axis=-1, keepdims=True))
            return dh1, dln, yhat

        dh1, dln, yhat = dh1_of(do_ref[...], h1_ref[...])
        dh1n, _, _ = dh1_of(don_ref[...], h1n_ref[...])
        _acc_out(dg_ref, jnp.sum(dln * yhat, axis=0, keepdims=True), first)
        _acc_out(dlb_ref, jnp.sum(dln, axis=0, keepdims=True), first)
        _acc_out(db_ref, jnp.sum(dh1, axis=0, keepdims=True), first)

        dext_ref[pl.ds(0, tc), :] = dh1
        dext_ref[pl.ds(tc, HALO), :] = dh1n * jnp.where(t == nt - 1, 0.0, 1.0)
        z = z_ref[...]
        ext_ref[pl.ds(0, HALO), :] = _glu(zp_ref[...], dc) * jnp.where(t == 0, 0.0, 1.0)
        ext_ref[pl.ds(HALO, tc), :] = _glu(z, dc)

        @pl.when(first)
        def _():
            dw_ref[...] = jnp.zeros_like(dw_ref)

        dh0 = jnp.zeros((tc, dc), F32)
        for j in range(CONV_W):
            dh0 = dh0 + w_ref[j:j + 1, :] * dext_ref[pl.ds(CONV_W - 1 - j, tc), :]
            dw_ref[j:j + 1, :] += jnp.sum(dh1 * ext_ref[pl.ds(HALO - (CONV_W - 1) + j, tc), :],
                                          axis=0, keepdims=True)
        za, sg = z[:, :dc], _sig(z[:, dc:])
        dz_ref[:, :dc] = dh0 * sg
        dz_ref[:, dc:] = dh0 * za * sg * (1.0 - sg)

    full = lambda shape: pl.BlockSpec(shape, lambda b, t: (0, 0))
    cur = lambda wd, col=0: pl.BlockSpec((tc, wd), lambda b, t: (b * nt + t, col))
    nxt = pl.BlockSpec((HALO, dc), lambda b, t: (jnp.minimum((b * seq + (t + 1) * tc) // HALO, nrow - 1), 0))
    return _call(
        body,
        name="conv_bwd",
        grid=(n // seq, nt),
        in_specs=[
            cur(dc), nxt, cur(dc), nxt, cur(2 * dc, zcol),
            pl.BlockSpec((HALO, 2 * dc), lambda b, t: (jnp.maximum((b * seq + t * tc) // HALO - 1, 0), zcol)),
            full((HALO, dc)), full((1, dc)), full((1, dc)),
        ],
        out_specs=[cur(2 * dc), full((HALO, dc)), full((1, dc)), full((1, dc)), full((1, dc))],
        out_shape=[
            jax.ShapeDtypeStruct((n, 2 * dc), F32),
            jax.ShapeDtypeStruct((HALO, dc), F32),
            jax.ShapeDtypeStruct((1, dc), F32),
            jax.ShapeDtypeStruct((1, dc), F32),
            jax.ShapeDtypeStruct((1, dc), F32),
        ],
        scratch_shapes=[pltpu.VMEM((tc + HALO, dc), F32), pltpu.VMEM((tc + HALO, dc), F32)],
        compiler_params=_params(),
    )(dco, dco, h1, h1, proj, proj, w, lng, lnb)


def _merge_common(l0, l1, l2, bg, so, ao, co, wbs, wba, wbc, d):
    ys = _dot(so.astype(BF16), wbs)
    ya = _dot(ao.astype(BF16), wba)
    yc = _dot(co.astype(BF16), wbc)
    gs = _sig(l0 + bg[:, :d])
    ga = _sig(l1 + bg[:, d:2 * d])
    gc = _sig(l2 + bg[:, 2 * d:])
    return (ys, ya, yc), (gs, ga, gc)


def _merge_specs(tm, d, dss, da, dc, layer):
    row = lambda w, col=0: pl.BlockSpec((tm, w), lambda i: (i, col))
    full = lambda r, c: pl.BlockSpec((r, c), lambda i: (0, 0))
    stacked = lambda r: pl.BlockSpec((1, r, d), lambda i: (layer, 0, 0))
    acts = [row(d, 0), row(d, 1), row(d, 2), full(1, 3 * d), row(dss), row(da), row(dc)]
    weights = [stacked(dss), stacked(da), stacked(dc), stacked(d)]
    return row, full, acts, weights


def merge_fwd(x, proj, bg, so, ao, co, wbs, wba, wbc, wout, layer):
    n, d = x.shape
    tm = _tile(n, 256, 8)
    row, full, acts, weights = _merge_specs(tm, d, so.shape[1], ao.shape[1], co.shape[1], layer)

    def body(x_ref, l0_ref, l1_ref, l2_ref, bg_ref, so_ref, ao_ref, co_ref,
             wbs_ref, wba_ref, wbc_ref, wo_ref, o_ref):
        (ys, ya, yc), (gs, ga, gc) = _merge_common(
            l0_ref[...], l1_ref[...], l2_ref[...], bg_ref[...], so_ref[...], ao_ref[...], co_ref[...],
            wbs_ref[0], wba_ref[0], wbc_ref[0], d)
        merged = gs * ys + ga * ya + gc * yc
        o_ref[...] = x_ref[...] + _dot(merged.astype(BF16), wo_ref[0])

    return _call(
        body,
        name="merge_fwd",
        grid=(n // tm,),
        in_specs=[row(d)] + acts + weights,
        out_specs=row(d),
        out_shape=jax.ShapeDtypeStruct((n, d), F32),
        compiler_params=_params(),
    )(x, proj, proj, proj, bg, so, ao, co, wbs, wba, wbc, wout)


def merge_bwd(dx, proj, bg, so, ao, co, wbs, wba, wbc, wout, layer):
    n, d = dx.shape
    dss, da, dc = so.shape[1], ao.shape[1], co.shape[1]
    tm = _tile(n, 256, 8)
    row, full, acts, weights = _merge_specs(tm, d, dss, da, dc, layer)

    def body(dx_ref, l0_ref, l1_ref, l2_ref, bg_ref, so_ref, ao_ref, co_ref, wbs_ref, wba_ref, wbc_ref, wo_ref,
             dl_ref, dso_ref, dao_ref, dco_ref, dbg_ref, mg_ref, dxb_ref, dys_ref, dya_ref, dyc_ref):
        wbs, wba, wbc = wbs_ref[0], wba_ref[0], wbc_ref[0]
        (ys, ya, yc), (gs, ga, gc) = _merge_common(
            l0_ref[...], l1_ref[...], l2_ref[...], bg_ref[...], so_ref[...], ao_ref[...], co_ref[...],
            wbs, wba, wbc, d)
        mg_ref[...] = (gs * ys + ga * ya + gc * yc).astype(BF16)
        dxb = dx_ref[...].astype(BF16)
        dxb_ref[...] = dxb
        dm = _dot_nt(dxb, wo_ref[0])
        first = pl.program_id(0) == 0
        for k, (y, g, w, dy_ref, db_ref) in enumerate((
                (ys, gs, wbs, dys_ref, dso_ref), (ya, ga, wba, dya_ref, dao_ref), (yc, gc, wbc, dyc_ref, dco_ref))):
            dl = dm * y * g * (1.0 - g)
            dl_ref[:, k * d:(k + 1) * d] = dl.astype(BF16)
            _acc_out(dbg_ref.at[:, k * d:(k + 1) * d], jnp.sum(dl, axis=0, keepdims=True), first)
            dy = (dm * g).astype(BF16)
            dy_ref[...] = dy
            db_ref[...] = _dot_nt(dy, w)

    bf = lambda w: jax.ShapeDtypeStruct((n, w), BF16)
    return _call(
        body,
        name="merge_bwd",
        grid=(n // tm,),
        in_specs=[row(d)] + acts + weights,
        out_specs=[row(3 * d), row(dss), row(da), row(dc), full(1, 3 * d),
                   row(d), row(d), row(d), row(d), row(d)],
        out_shape=[bf(3 * d), jax.ShapeDtypeStruct((n, dss), F32), jax.ShapeDtypeStruct((n, da), F32),
                   jax.ShapeDtypeStruct((n, dc), F32), jax.ShapeDtypeStruct((1, 3 * d), F32),
                   bf(d), bf(d), bf(d), bf(d), bf(d)],
        compiler_params=_params(),
    )(dx, proj, proj, proj, bg, so, ao, co, wbs, wba, wbc, wout)


def loss_head(y, target):
    n, d = y.shape
    tm = _tile(n, 512, 8)

    def body(y_ref, t_ref, dy_ref, l_ref):
        e = y_ref[...] - t_ref[...]
        dy_ref[...] = e * (1.0 / d)
        part = 0.5 * jnp.sum(jnp.sum(e * e, axis=-1, keepdims=True) * (1.0 / d), axis=0, keepdims=True)
        _acc_out(l_ref, part, pl.program_id(0) == 0)

    return _call(
        body,
        name="loss_head",
        grid=(n // tm,),
        in_specs=[pl.BlockSpec((tm, d), lambda i: (i, 0)), pl.BlockSpec((tm, d), lambda i: (i, 0))],
        out_specs=[pl.BlockSpec((tm, d), lambda i: (i, 0)), pl.BlockSpec((1, 1), lambda i: (0, 0))],
        out_shape=[jax.ShapeDtypeStruct((n, d), F32), jax.ShapeDtypeStruct((1, 1), F32)],
        compiler_params=_params(),
    )(y, target)


def _mesh_pos():
    return lax.axis_index("x"), lax.axis_index("y"), lax.axis_index("c")


def all_gather(shards):
    na = len(shards)

    def body(*refs):
        x_refs, out_refs = refs[:na], refs[na:2 * na]
        send_sems, recv_sems, local_sems = refs[2 * na:]
        x, y, c = _mesh_pos()
        me, sibling = (x, y, c), (x, y, 1 - c)
        chips = [(1 - x, y), (x, 1 - y), (1 - x, 1 - y)]

        def slot(a, px, py, pc):
            return out_refs[a].at[4 * px + 2 * py + pc]

        def copy(a, k, block, to, src=None):
            return pltpu.make_async_remote_copy(
                src_ref=slot(a, *block) if src is None else src, dst_ref=slot(a, *block),
                send_sem=send_sems.at[a, k], recv_sem=recv_sems.at[a, k],
                device_id=to, device_id_type=pl.DeviceIdType.MESH)

        mine = [pltpu.make_async_copy(x_refs[a], slot(a, *me), local_sems.at[a]) for a in range(na)]
        for cp in mine:
            cp.start()
        first = []
        for a in range(na):
            first.append(copy(a, 0, me, sibling, src=x_refs[a]))
            first += [copy(a, 1 + j, me, (*chip, c), src=x_refs[a]) for j, chip in enumerate(chips)]
        for cp in first:
            cp.start()
        passed = []
        for j, chip in enumerate(chips):
            for a in range(na):
                copy(a, 1 + j, (*chip, c), me).wait_recv()
                fwd = copy(a, 4 + j, (*chip, c), sibling)
                fwd.start()
                passed.append(fwd)
        for a in range(na):
            copy(a, 0, sibling, me).wait_recv()
            for j, chip in enumerate(chips):
                copy(a, 4 + j, (*chip, 1 - c), me).wait_recv()
        for cp in first + passed:
            cp.wait_send()
        for cp in mine:
            cp.wait()

    anyspec = pl.BlockSpec(memory_space=pl.ANY)
    return _call(
        body,
        name="all_gather",
        out_shape=[jax.ShapeDtypeStruct((N_DEV,) + s.shape, s.dtype) for s in shards],
        in_specs=[anyspec] * na,
        out_specs=[anyspec] * na,
        scratch_shapes=[pltpu.SemaphoreType.DMA((na, 7)), pltpu.SemaphoreType.DMA((na, 7)),
                        pltpu.SemaphoreType.DMA((na,))],
    )(*shards)


def grad_exchange(slabs, small):
    na = len(slabs)

    def body(*refs):
        s_refs, sm_ref = refs[:na], refs[na]
        r_refs, rsm_ref = refs[na + 1:2 * na + 1], refs[2 * na + 1]
        send_sems, recv_sems, local_sems = refs[2 * na + 2:]
        x, y, c = _mesh_pos()
        me = 4 * x + 2 * y + c

        def peer(k):
            px = (1 - x) if k & 4 else x
            py = (1 - y) if k & 2 else y
            pc = (1 - c) if k & 1 else c
            return (px, py, pc), 4 * px + 2 * py + pc

        def src_dst(a, pid, slot):
            if a == na:
                return sm_ref, rsm_ref.at[slot]
            return s_refs[a].at[:, pid], r_refs[a].at[slot]

        def copy(a, k):
            to, pid = peer(k)
            src, dst = src_dst(a, pid, me)
            return pltpu.make_async_remote_copy(
                src_ref=src, dst_ref=dst, send_sem=send_sems.at[a, k - 1], recv_sem=recv_sems.at[a, k - 1],
                device_id=to, device_id_type=pl.DeviceIdType.MESH)

        def arrival(a, k):
            _, pid = peer(k)
            src, dst = src_dst(a, pid, pid)
            return pltpu.make_async_remote_copy(
                src_ref=src, dst_ref=dst, send_sem=send_sems.at[a, k - 1], recv_sem=recv_sems.at[a, k - 1],
                device_id=(x, y, c), device_id_type=pl.DeviceIdType.MESH)

        mine = []
        for a in range(na + 1):
            src, dst = src_dst(a, me, me)
            mine.append(pltpu.make_async_copy(src, dst, local_sems.at[a]))
        for cp in mine:
            cp.start()
        sends = [copy(a, k) for k in range(1, N_DEV) for a in range(na + 1)]
        for cp in sends:
            cp.start()
        for k in range(1, N_DEV):
            for a in range(na + 1):
                arrival(a, k).wait_recv()
        for cp in sends:
            cp.wait_send()
        for cp in mine:
            cp.wait()

    anyspec = pl.BlockSpec(memory_space=pl.ANY)
    outs = [jax.ShapeDtypeStruct((N_DEV, s.shape[0]) + s.shape[2:], s.dtype) for s in slabs]
    outs.append(jax.ShapeDtypeStruct((N_DEV,) + small.shape, small.dtype))
    return _call(
        body,
        name="grad_exchange",
        out_shape=outs,
        in_specs=[anyspec] * (na + 1),
        out_specs=[anyspec] * (na + 1),
        scratch_shapes=[pltpu.SemaphoreType.DMA((na + 1, 7)), pltpu.SemaphoreType.DMA((na + 1, 7)),
                        pltpu.SemaphoreType.DMA((na + 1,))],
    )(*slabs, small)


ADAM_BLOCK = 256 * 1024


def sum_adamw(recv, w, m, v, name):
    shape = w.shape
    cols = shape[-1]
    rows = w.size // cols
    tr = _tile(rows, max(8, ADAM_BLOCK // cols // 8 * 8), 8)
    c1 = 1.0 / (1.0 - ADAM_B1 ** ADAM_STEP)
    c2 = 1.0 / (1.0 - ADAM_B2 ** ADAM_STEP)

    def body(r_ref, w_ref, m_ref, v_ref, g_ref, d_ref, mo_ref, vo_ref):
        g = r_ref[0].astype(F32)
        for s in range(1, N_DEV):
            g = g + r_ref[s].astype(F32)
        mn = ADAM_B1 * m_ref[...] + (1.0 - ADAM_B1) * g
        vn = ADAM_B2 * v_ref[...] + (1.0 - ADAM_B2) * (g * g)
        g_ref[...] = g
        mo_ref[...] = mn
        vo_ref[...] = vn
        d_ref[...] = -ADAM_LR * ((mn * c1) / (jnp.sqrt(vn * c2) + ADAM_EPS) + ADAM_WD * w_ref[...])

    flat = pl.BlockSpec((tr, cols), lambda i: (i, 0))
    out = jax.ShapeDtypeStruct((rows, cols), F32)
    res = _call(
        body,
        name=name,
        grid=(rows // tr,),
        in_specs=[pl.BlockSpec((N_DEV, tr, cols), lambda i: (0, i, 0)), flat, flat, flat],
        out_specs=[flat, flat, flat, flat],
        out_shape=[out, out, out, out],
        compiler_params=_params(),
    )(recv.reshape(N_DEV, rows, cols), w.reshape(rows, cols), m.reshape(rows, cols), v.reshape(rows, cols))
    return [r.reshape(shape) for r in res]


def _attn_bias_vector(rel_bias):
    h = rel_bias.shape[0]
    n_far = BAND - MAX_REL
    n_near = BAND + CHUNK - 1 - n_far
    far = jnp.broadcast_to(rel_bias[:, 2 * MAX_REL:], (h, n_far))
    near = rel_bias[:, 2 * MAX_REL - n_near:2 * MAX_REL][:, ::-1]
    pad = jnp.zeros((h, 2 * QBLK - n_far - n_near), F32)
    return jnp.concatenate([far, near, pad], axis=1)[:, None, :]


def _s5_prepare(lre, lim, ldt, bre, bim, cre, cim):
    g, p = lre.shape
    lr = jnp.minimum(lre, -1e-4)
    dt = jnp.exp(ldt)[:, None]
    mag = jnp.exp(lr * dt)
    ar = mag * jnp.cos(lim * dt)
    ai = mag * jnp.sin(lim * dt)
    den = lr * lr + lim * lim
    coef_r = ((ar - 1.0) * lr + ai * lim) / den
    coef_i = (ai * lr - (ar - 1.0) * lim) / den
    bbar_r = coef_r[..., None] * bre - coef_i[..., None] * bim
    bbar_i = coef_r[..., None] * bim + coef_i[..., None] * bre
    eye = jnp.eye(g, dtype=F32)
    bd_in = lambda b: jnp.einsum("gpc,gh->gchp", b, eye).reshape(g * S5_GROUP, g * p)
    bd_out = lambda c: jnp.einsum("gcp,gh->gphc", c, eye).reshape(g * p, g * S5_GROUP)
    return (ar.reshape(1, g * p), ai.reshape(1, g * p), bd_in(bbar_r), bd_in(bbar_i), bd_out(cre), bd_out(cim))


SHARDED = ("ffn1_w_up", "ffn1_w_down", "w_in", "s5_w_glu", "w_br_s5", "w_br_attn", "conv_w_dw", "w_br_conv",
           "w_out", "ffn2_w_up", "ffn2_w_down")
WEIGHTS = ("ffn1_norm", "ffn1_w_up", "ffn1_w_down", "mix_norm", "w_in", "b_gate", "s5_lambda_re", "s5_lambda_im",
           "s5_log_dt", "s5_b_re", "s5_b_im", "s5_c_re", "s5_c_im", "s5_d", "s5_w_glu", "w_br_s5", "attn_q_gain",
           "attn_k_gain", "attn_rel_bias", "w_br_attn", "conv_w_dw", "conv_b_dw", "conv_ln_g", "conv_ln_b",
           "w_br_conv", "w_out", "ffn2_norm", "ffn2_w_up", "ffn2_w_down")
SMALL = tuple(nm for nm in WEIGHTS if nm not in SHARDED)
SMALL_LANES = 1024


def _cols_full(g):
    _, depth, k, nn = g.shape
    return g.transpose(1, 2, 0, 3).reshape(depth, k, N_DEV * nn)


def _rows_full(g):
    _, depth, r, cc = g.shape
    return g.transpose(1, 0, 2, 3).reshape(depth, N_DEV * r, cc)


def _cols_slabs(gfull):
    depth, k, c8 = gfull.shape
    return gfull.reshape(depth, k, N_DEV, c8 // N_DEV).transpose(0, 2, 1, 3)


def _heads(a, h):
    n = a.shape[0]
    return a.reshape(n, h, HEAD).transpose(1, 0, 2)


def _unheads(a):
    h, n, _ = a.shape
    return a.transpose(1, 0, 2).reshape(n, h * HEAD)


def _small_pack(t):
    flat = jnp.concatenate([t[nm].reshape(-1) for nm in SMALL])
    rows = -(-flat.shape[0] // SMALL_LANES)
    rows = -(-rows // 8) * 8
    return jnp.pad(flat, (0, rows * SMALL_LANES - flat.shape[0])).reshape(rows, SMALL_LANES)


def _small_unpack(flat, like):
    flat = flat.reshape(-1)
    out, off = {}, 0
    for nm in SMALL:
        out[nm] = flat[off:off + like[nm].size].reshape(like[nm].shape)
        off += like[nm].size
    return out


def _step(x, target, w, m, v):
    bsz, seq, d = x.shape
    n = bsz * seq
    depth = w["ffn1_norm"].shape[0]
    da, dss, dc = d // 2, d // 4, d // 4
    heads = da // HEAD
    gp = dss // S5_GROUP * S5_STATE
    mid = 3 * da + 2 * dc
    q0, k0, v0, z0, u0 = 3 * d, 3 * d + da, 3 * d + 2 * da, 3 * d + 3 * da, 3 * d + mid
    to_kernel_cols = lambda a: jnp.concatenate([a[..., dss + mid:], a[..., dss:dss + mid], a[..., :dss]], axis=-1)
    to_ref_cols = lambda a: jnp.concatenate([a[..., 3 * d + mid:], a[..., 3 * d:3 * d + mid], a[..., :3 * d]], axis=-1)

    gathered = dict(zip(SHARDED, all_gather(
        [w[nm] if nm == "conv_w_dw" else w[nm].astype(BF16) for nm in SHARDED])))
    full = {nm: gathered[nm] for nm in ("ffn1_w_up", "ffn2_w_up")}
    for nm in ("ffn1_w_down", "ffn2_w_down", "w_out"):
        full[nm] = _rows_full(gathered[nm])
    for nm in ("s5_w_glu", "w_br_s5", "w_br_attn", "w_br_conv", "conv_w_dw"):
        full[nm] = _cols_full(gathered[nm])
    full["w_in"] = to_kernel_cols(_cols_full(gathered["w_in"]))
    nff = full["ffn1_w_up"].shape[3]
    conv_w = jnp.pad(full["conv_w_dw"], ((0, 0), (0, HALO - CONV_W), (0, 0)))

    row = lambda a: a.reshape(1, -1)
    saved = []
    xin = x.reshape(n, d)
    for l in range(depth):
        s = {"x0": xin}
        s["x1"], s["a1"], s["b1"] = ffn_fwd(xin, row(w["ffn1_norm"][l]), full["ffn1_w_up"], full["ffn1_w_down"], l)
        proj = proj_fwd(s["x1"], row(w["mix_norm"][l]), full["w_in"], l)
        s["proj"] = proj
        prep_in = (w["s5_lambda_re"][l], w["s5_lambda_im"][l], w["s5_log_dt"][l], w["s5_b_re"][l], w["s5_b_im"][l],
                   w["s5_c_re"][l], w["s5_c_im"][l])
        (ar, ai, bm_r, bm_i, cm_r, cm_i), s["prep_vjp"] = jax.vjp(_s5_prepare, *prep_in)
        s["s5p"] = (ar, ai, bm_r.astype(BF16), bm_i.astype(BF16), cm_r.astype(BF16), cm_i.astype(BF16),
                    row(w["s5_d"][l]), full["s5_w_glu"], l)
        s["so"], s["xr"], s["xi"], s["ypre"] = s5_fwd(proj, u0 // dss, seq, *s["s5p"])
        tv, s["tv_vjp"] = jax.vjp(_attn_bias_vector, w["attn_rel_bias"][l])
        s["qkv"] = tuple(_heads(proj[:, c0:c0 + da], heads) for c0 in (q0, k0, v0))
        s["attnp"] = (tv, row(w["attn_q_gain"][l]), row(w["attn_k_gain"][l]))
        s["ao"] = _unheads(attn_fwd(*s["qkv"], *s["attnp"], seq))
        s["convp"] = (conv_w[l], row(w["conv_b_dw"][l]), row(w["conv_ln_g"][l]), row(w["conv_ln_b"][l]))
        s["h1"], s["co"] = conv_fwd(proj, z0 // (2 * dc), seq, *s["convp"])
        s["mergep"] = (row(w["b_gate"][l]), s["so"], s["ao"], s["co"], full["w_br_s5"], full["w_br_attn"],
                       full["w_br_conv"], full["w_out"], l)
        s["x2"] = merge_fwd(s["x1"], proj, *s["mergep"])
        xin, s["a2"], s["b2"] = ffn_fwd(s["x2"], row(w["ffn2_norm"][l]), full["ffn2_w_up"], full["ffn2_w_down"], l)
        saved.append(s)

    dx, loss = loss_head(xin, target.reshape(n, d))
    loss = lax.psum(loss[0, 0], ("x", "y", "c"))

    small_g = {nm: [None] * depth for nm in SMALL}
    slab = {nm: None for nm in SHARDED}
    conv_g = [None] * depth

    def ffn_grads(which, l, hn, dyb, dab_a, dab_b, act):
        up, down = which + "_w_up", which + "_w_down"
        up4, down4 = (depth, N_DEV, d, nff), (depth, FF_CHUNKS, nff, d)
        slab[up] = wgrad(hn, dab_a, "wg_ffn_up", up4, "col", layer=l, prev=slab[up])
        slab[up] = wgrad(hn, dab_b, "wg_ffn_up", up4, "col", g0=FF_CHUNKS, layer=l, prev=slab[up])
        slab[down] = wgrad(act, dyb, "wg_ffn_down", down4, "row", layer=l, prev=slab[down])

    for l in reversed(range(depth)):
        s = saved[l]
        dx, dg, hn, dyb, dab_a, dab_b, act = ffn_bwd(dx, s["x2"], row(w["ffn2_norm"][l]), s["a2"], s["b2"],
                                                     full["ffn2_w_up"], full["ffn2_w_down"], l)
        small_g["ffn2_norm"][l] = dg
        ffn_grads("ffn2", l, hn, dyb, dab_a, dab_b, act)

        dlog, dso, dao, dco, dbg, mg, dxb, dys, dya, dyc = merge_bwd(dx, s["proj"], *s["mergep"])
        small_g["b_gate"][l] = dbg
        slab["w_out"] = wgrad(mg, dxb, "wg_out", (depth, N_DEV, d // N_DEV, d), "row", layer=l, prev=slab["w_out"])
        for nm, act_in, dy_br in (("w_br_s5", s["so"], dys), ("w_br_attn", s["ao"], dya), ("w_br_conv", s["co"], dyc)):
            slab[nm] = wgrad(act_in, dy_br, "wg_" + nm, (depth, N_DEV, act_in.shape[1], d // N_DEV), "col",
                             layer=l, prev=slab[nm])

        dz, dwdw, dbdw, dlng, dlnb = conv_bwd(dco, s["h1"], s["proj"], z0 // (2 * dc), seq, *s["convp"])
        conv_g[l] = dwdw[:CONV_W]
        small_g["conv_b_dw"][l], small_g["conv_ln_g"][l], small_g["conv_ln_b"][l] = dbdw, dlng, dlnb

        dq, dkp, dkc, dvp, dvc, dtv, dgq, dgk = attn_bwd(_heads(dao, heads), *s["qkv"], *s["attnp"], seq)
        small_g["attn_q_gain"][l], small_g["attn_k_gain"][l] = dgq, dgk
        small_g["attn_rel_bias"][l] = s["tv_vjp"](dtv)[0]

        def from_prev(cur, prev):
            prev = prev.reshape(heads, bsz, seq, HEAD)
            prev = jnp.concatenate([prev[:, :, QBLK:], jnp.zeros_like(prev[:, :, :QBLK])], axis=2)
            return cur + prev.reshape(heads, n, HEAD)

        dk, dv = from_prev(dkc, dkp), from_prev(dvc, dvp)

        du, gr, gi, dyb5, glb, dzb, dar, dai, dd = s5_bwd(dso, s["ypre"], s["proj"], u0 // dss, s["xr"], s["xi"],
                                                          seq, *s["s5p"])
        small_g["s5_d"][l] = dd
        slab["s5_w_glu"] = wgrad(glb, dzb, "wg_s5_glu", (depth, 1, dss, 2 * dss), "col", layer=l, prev=slab["s5_w_glu"])
        one = lambda k1, k2: (1, 1, k1, k2)
        dcm_r = wgrad(s["xr"], dyb5, "wg_s5_c", one(gp, dss), "col", dtype=F32)[0, 0]
        dcm_i = -wgrad(s["xi"], dyb5, "wg_s5_c", one(gp, dss), "col", dtype=F32)[0, 0]
        dbm_r = wgrad(s["proj"], gr, "wg_s5_b", one(dss, gp), "col", a_cols=(u0, dss), dtype=F32)[0, 0]
        dbm_i = wgrad(s["proj"], gi, "wg_s5_b", one(dss, gp), "col", a_cols=(u0, dss), dtype=F32)[0, 0]
        pg = s["prep_vjp"]((dar, dai, dbm_r, dbm_i, dcm_r, dcm_i))
        for nm, gval in zip(("s5_lambda_re", "s5_lambda_im", "s5_log_dt", "s5_b_re", "s5_b_im", "s5_c_re", "s5_c_im"), pg):
            small_g[nm][l] = gval

        dproj = jnp.concatenate([dlog, _unheads(dq).astype(BF16), _unheads(dk).astype(BF16),
                                 _unheads(dv).astype(BF16), dz.astype(BF16), du.astype(BF16)], axis=1)
        dx, dgm, hn = proj_bwd(dproj, dx, s["x1"], row(w["mix_norm"][l]), full["w_in"], l)
        small_g["mix_norm"][l] = dgm
        slab["w_in"] = wgrad(hn, dproj, "wg_in", (depth, 1, d, 3 * d + mid + dss), "col", layer=l, prev=slab["w_in"])

        dx, dg, hn, dyb, dab_a, dab_b, act = ffn_bwd(dx, s["x0"], row(w["ffn1_norm"][l]), s["a1"], s["b1"],
                                                     full["ffn1_w_up"], full["ffn1_w_down"], l)
        small_g["ffn1_norm"][l] = dg
        ffn_grads("ffn1", l, hn, dyb, dab_a, dab_b, act)

    for nm in ("ffn1_w_down", "ffn2_w_down"):
        slab[nm] = slab[nm].reshape(depth, N_DEV, nff // 2, d)
    slab["w_in"] = _cols_slabs(to_ref_cols(slab["w_in"][:, 0]))
    slab["s5_w_glu"] = _cols_slabs(slab["s5_w_glu"][:, 0])
    slab["conv_w_dw"] = _cols_slabs(jnp.stack(conv_g)).astype(BF16)
    small_flat = _small_pack({nm: jnp.stack([g.reshape(w[nm].shape[1:]) for g in small_g[nm]]) for nm in SMALL})

    recv = grad_exchange([slab[nm] for nm in SHARDED], small_flat)
    outs = {}
    for nm, r in zip(SHARDED, recv[:-1]):
        outs[nm] = sum_adamw(r, w[nm], m[nm], v[nm], "adamw_" + nm)
    packed = sum_adamw(recv[-1], _small_pack(w), _small_pack(m), _small_pack(v), "adamw_small")
    unpacked = [_small_unpack(p, w) for p in packed]
    for nm in SMALL:
        outs[nm] = [u[nm] for u in unpacked]
    return loss, dx.reshape(x.shape), outs


def kernel(x, ffn1_norm, ffn1_w_up, ffn1_w_down, mix_norm, w_in, b_gate, s5_lambda_re, s5_lambda_im, s5_log_dt, s5_b_re, s5_b_im, s5_c_re, s5_c_im, s5_d, s5_w_glu, w_br_s5, attn_q_gain, attn_k_gain, attn_rel_bias, w_br_attn, conv_w_dw, conv_b_dw, conv_ln_g, conv_ln_b, w_br_conv, w_out, ffn2_norm, ffn2_w_up, ffn2_w_down, loss_target, m_ffn1_norm, m_ffn1_w_up, m_ffn1_w_down, m_mix_norm, m_w_in, m_b_gate, m_s5_lambda_re, m_s5_lambda_im, m_s5_log_dt, m_s5_b_re, m_s5_b_im, m_s5_c_re, m_s5_c_im, m_s5_d, m_s5_w_glu, m_w_br_s5, m_attn_q_gain, m_attn_k_gain, m_attn_rel_bias, m_w_br_attn, m_conv_w_dw, m_conv_b_dw, m_conv_ln_g, m_conv_ln_b, m_w_br_conv, m_w_out, m_ffn2_norm, m_ffn2_w_up, m_ffn2_w_down, v_ffn1_norm, v_ffn1_w_up, v_ffn1_w_down, v_mix_norm, v_w_in, v_b_gate, v_s5_lambda_re, v_s5_lambda_im, v_s5_log_dt, v_s5_b_re, v_s5_b_im, v_s5_c_re, v_s5_c_im, v_s5_d, v_s5_w_glu, v_w_br_s5, v_attn_q_gain, v_attn_k_gain, v_attn_rel_bias, v_w_br_attn, v_conv_w_dw, v_conv_b_dw, v_conv_ln_g, v_conv_ln_b, v_w_br_conv, v_w_out, v_ffn2_norm, v_ffn2_w_up, v_ffn2_w_down):
    args = locals()
    w = {nm: args[nm] for nm in WEIGHTS}
    m = {nm: args["m_" + nm] for nm in WEIGHTS}
    v = {nm: args["v_" + nm] for nm in WEIGHTS}
    loss, gx, outs = _step(x, loss_target, w, m, v)
    return (loss, gx, *[outs[nm][0] for nm in WEIGHTS], *[outs[nm][1] for nm in WEIGHTS],
            *[outs[nm][2] for nm in WEIGHTS], *[outs[nm][3] for nm in WEIGHTS])
```

```python
import functools
import math

import numpy as np
import jax
import jax.numpy as jnp
from jax import lax
from jax.experimental import pallas as pl
from jax.experimental.pallas import tpu as pltpu

F32 = jnp.float32
BF16 = jnp.bfloat16

CHUNK = 64
N_LEFT = 8
QBLK = CHUNK * N_LEFT
HEAD = 64
MAX_REL = 128
S5_GROUP = 16
S5_STATE = 64
CONV_W = 31
HALO = 32
EPS = 1e-6
NEG = -1e30
ADAM_LR, ADAM_B1, ADAM_B2, ADAM_EPS, ADAM_WD, ADAM_STEP = 0.001, 0.9, 0.999, 1e-08, 0.01, 10
N_DEV = 8
VMEM_LIMIT = 56 * 1024 * 1024


def _call(body, **kw):
    return pl.pallas_call(body, **kw)


def _params(**kw):
    return pltpu.CompilerParams(vmem_limit_bytes=VMEM_LIMIT, **kw)


def _tile(n, cap, unit=128):
    if n <= cap:
        return n
    d = (cap // unit) * unit
    while d >= unit:
        if n % d == 0:
            return d
        d -= unit
    raise ValueError(f"no tile for {n} under {cap}")


def _dot(a, b):
    return jnp.dot(a, b, preferred_element_type=F32)


def _dot_nt(a, b):
    return lax.dot_general(a, b, (((1,), (1,)), ((), ())), preferred_element_type=F32)


def _dot_tn(a, b):
    return lax.dot_general(a, b, (((0,), (0,)), ((), ())), preferred_element_type=F32)


def _sig(x):
    return 1.0 / (1.0 + jnp.exp(-x))


def _rms_fwd(x, g):
    rs = lax.rsqrt(jnp.mean(x * x, axis=-1, keepdims=True) + EPS)
    xhat = x * rs
    return xhat * g, xhat, rs


def _rms_bwd(dh, xhat, rs, g):
    dxh = dh * g
    dx = rs * (dxh - xhat * jnp.mean(dxh * xhat, axis=-1, keepdims=True))
    return dx, dh * xhat


_GELU_C = math.sqrt(2.0 / math.pi)


def _gelu(x):
    return 0.5 * x * (1.0 + jnp.tanh(_GELU_C * (x + 0.044715 * x * x * x)))


def _gelu_grad(x):
    t = jnp.tanh(_GELU_C * (x + 0.044715 * x * x * x))
    return 0.5 * (1.0 + t) + 0.5 * x * (1.0 - t * t) * _GELU_C * (1.0 + 3.0 * 0.044715 * x * x)


def _acc_out(ref, val, first):
    @pl.when(first)
    def _():
        ref[...] = val

    @pl.when(jnp.logical_not(first))
    def _():
        ref[...] += val


FF_CHUNKS = N_DEV // 2


def ffn_fwd(x, g, w_up, w_down, layer, comm=None):
    n, d = x.shape
    nn = w_up.shape[3]
    tm = _tile(n, 512, 8)

    def body(x_ref, g_ref, wa_ref, wb_ref, wd_ref, xo_ref, a_ref, b_ref, hn_ref, acc_ref):
        j = pl.program_id(1)

        @pl.when(j == 0)
        def _():
            h, _, _ = _rms_fwd(x_ref[...], g_ref[...])
            hn_ref[...] = h.astype(BF16)
            acc_ref[...] = jnp.zeros_like(acc_ref)

        hn = hn_ref[...]
        a = _dot(hn, wa_ref[0, 0])
        b = _dot(hn, wb_ref[0, 0])
        a_ref[0] = a.astype(BF16)
        b_ref[0] = b.astype(BF16)
        act = a * _sig(a) * b
        acc_ref[...] += _dot(act.astype(BF16), wd_ref[0])

        @pl.when(j == FF_CHUNKS - 1)
        def _():
            xo_ref[...] = x_ref[...] + 0.5 * acc_ref[...]

    return _hosted_call(
        body, comm,
        name="ffn_fwd",
        grid=(n // tm, FF_CHUNKS),
        in_specs=[
            pl.BlockSpec((tm, d), lambda i, j: (i, 0)),
            pl.BlockSpec((1, d), lambda i, j: (0, 0)),
            pl.BlockSpec((1, 1, d, nn), lambda i, j: (j, layer, 0, 0)),
            pl.BlockSpec((1, 1, d, nn), lambda i, j: (j + FF_CHUNKS, layer, 0, 0)),
            pl.BlockSpec((1, nn, d), lambda i, j: (layer, j, 0)),
        ],
        out_specs=[
            pl.BlockSpec((tm, d), lambda i, j: (i, 0)),
            pl.BlockSpec((1, tm, nn), lambda i, j: (j, i, 0)),
            pl.BlockSpec((1, tm, nn), lambda i, j: (j, i, 0)),
        ],
        out_shape=[
            jax.ShapeDtypeStruct((n, d), F32),
            jax.ShapeDtypeStruct((FF_CHUNKS, n, nn), BF16),
            jax.ShapeDtypeStruct((FF_CHUNKS, n, nn), BF16),
        ],
        scratch_shapes=[pltpu.VMEM((tm, d), BF16), pltpu.VMEM((tm, d), F32)],
        args=(x, g, w_up, w_up, w_down),
    )


def ffn_bwd(dy, x, g, a, b, w_up, w_down, layer, comm=None):
    n, d = x.shape
    nn = w_up.shape[3]
    tm = _tile(n, 512, 8)

    def body(dy_ref, x_ref, g_ref, a_ref, b_ref, wa_ref, wb_ref, wd_ref,
             dx_ref, dg_ref, hn_ref, dyb_ref, da_ref, db_ref, act_ref, dyb_s, dh_ref):
        i, j = pl.program_id(0), pl.program_id(1)

        @pl.when(j == 0)
        def _():
            h, _, _ = _rms_fwd(x_ref[...], g_ref[...])
            hn_ref[...] = h.astype(BF16)
            dyb = (0.5 * dy_ref[...]).astype(BF16)
            dyb_ref[...] = dyb
            dyb_s[...] = dyb
            dh_ref[...] = jnp.zeros_like(dh_ref)

        dact = _dot_nt(dyb_s[...], wd_ref[0])
        a32 = a_ref[0].astype(F32)
        b32 = b_ref[0].astype(F32)
        s = _sig(a32)
        sil = a32 * s
        da = (dact * b32 * (s * (1.0 + a32 * (1.0 - s)))).astype(BF16)
        db = (dact * sil).astype(BF16)
        da_ref[0] = da
        db_ref[0] = db
        act_ref[0] = (sil * b32).astype(BF16)
        dh_ref[...] += _dot_nt(da, wa_ref[0, 0]) + _dot_nt(db, wb_ref[0, 0])

        @pl.when(j == FF_CHUNKS - 1)
        def _():
            gg = g_ref[...]
            _, xhat, rs = _rms_fwd(x_ref[...], gg)
            dxn, dgr = _rms_bwd(dh_ref[...], xhat, rs, gg)
            dx_ref[...] = dy_ref[...] + dxn
            _acc_out(dg_ref, jnp.sum(dgr, axis=0, keepdims=True), i == 0)

    tok = pl.BlockSpec((tm, d), lambda i, j: (i, 0))
    chunk = pl.BlockSpec((1, tm, nn), lambda i, j: (j, i, 0))
    vec = pl.BlockSpec((1, d), lambda i, j: (0, 0))
    chunks = jax.ShapeDtypeStruct((FF_CHUNKS, n, nn), BF16)
    return _hosted_call(
        body, comm,
        name="ffn_bwd",
        grid=(n // tm, FF_CHUNKS),
        in_specs=[
            tok, tok, vec, chunk, chunk,
            pl.BlockSpec((1, 1, d, nn), lambda i, j: (j, layer, 0, 0)),
            pl.BlockSpec((1, 1, d, nn), lambda i, j: (j + FF_CHUNKS, layer, 0, 0)),
            pl.BlockSpec((1, nn, d), lambda i, j: (layer, j, 0)),
        ],
        out_specs=[tok, vec, tok, tok, chunk, chunk, chunk],
        out_shape=[
            jax.ShapeDtypeStruct((n, d), F32),
            jax.ShapeDtypeStruct((1, d), F32),
            jax.ShapeDtypeStruct((n, d), BF16),
            jax.ShapeDtypeStruct((n, d), BF16),
            chunks, chunks, chunks,
        ],
        scratch_shapes=[pltpu.VMEM((tm, d), BF16), pltpu.VMEM((tm, d), F32)],
        args=(dy, x, g, a, b, w_up, w_up, w_down),
    )


def wgrad(a, b, name, out4, mode, *, g0=0, layer=0, prev=None, a_cols=None, dtype=BF16):
    a3 = a if a.ndim == 3 else a[None]
    b3 = b if b.ndim == 3 else b[None]
    sa, n, ka = a3.shape
    sb, _, kb = b3.shape
    a0 = 0
    if a_cols is not None:
        a0, ka = a_cols
    k1, k2 = sa * ka, sb * kb
    depth, groups, rr, cc = out4
    t1 = _tile(math.gcd(ka, rr), 1024)
    t2 = _tile(math.gcd(kb, cc), 1024)
    tn = _tile(n, 1024, 8)
    na, nb = ka // t1, kb // t2
    if mode == "col":
        assert rr == k1 and k2 % cc == 0 and g0 + k2 // cc <= groups
        per = cc // t2
        omap = lambda i, j, k: (layer, g0 + j // per, i, j % per)
    else:
        assert cc == k2 and k1 % rr == 0 and g0 + k1 // rr <= groups
        per = rr // t1
        omap = lambda i, j, k: (layer, g0 + i // per, i % per, j)

    nk = n // tn

    def body(a_ref, b_ref, *rest):
        o_ref, acc_ref = rest[-2], rest[-1]
        k = pl.program_id(2)

        @pl.when(k == 0)
        def _():
            acc_ref[...] = jnp.zeros_like(acc_ref)

        acc_ref[...] += _dot_tn(a_ref[0].astype(BF16), b_ref[0].astype(BF16))

        @pl.when(k == nk - 1)
        def _():
            o_ref[0, 0] = acc_ref[...].astype(dtype)

    in_specs = [
        pl.BlockSpec((1, tn, t1), lambda i, j, k: (i // na, k, a0 // t1 + i % na)),
        pl.BlockSpec((1, tn, t2), lambda i, j, k: (j // nb, k, j % nb)),
    ]
    args = [a3, b3]
    extra = {}
    if prev is not None:
        in_specs.append(pl.BlockSpec(memory_space=pl.ANY))
        args.append(prev)
        extra["input_output_aliases"] = {2: 0}
    return _call(
        body,
        name=name,
        grid=(k1 // t1, k2 // t2, nk),
        in_specs=in_specs,
        out_specs=pl.BlockSpec((1, 1, t1, t2), omap),
        out_shape=jax.ShapeDtypeStruct(out4, dtype),
        scratch_shapes=[pltpu.VMEM((t1, t2), F32)],
        compiler_params=_params(),
        **extra,
    )(*args)


def proj_fwd(x, g, w, layer):
    n, d = x.shape
    c = w.shape[2]
    tm, tc = _tile(n, 512, 8), _tile(c, 768)

    def body(x_ref, g_ref, w_ref, o_ref, hn_ref):
        @pl.when(pl.program_id(1) == 0)
        def _():
            h, _, _ = _rms_fwd(x_ref[...], g_ref[...])
            hn_ref[...] = h.astype(BF16)

        o_ref[...] = _dot(hn_ref[...], w_ref[0])

    return _call(
        body,
        name="proj_fwd",
        grid=(n // tm, c // tc),
        in_specs=[
            pl.BlockSpec((tm, d), lambda i, j: (i, 0)),
            pl.BlockSpec((1, d), lambda i, j: (0, 0)),
            pl.BlockSpec((1, d, tc), lambda i, j: (layer, 0, j)),
        ],
        out_specs=pl.BlockSpec((tm, tc), lambda i, j: (i, j)),
        out_shape=jax.ShapeDtypeStruct((n, c), F32),
        scratch_shapes=[pltpu.VMEM((tm, d), BF16)],
        compiler_params=_params(),
    )(x, g, w)


def proj_bwd(dproj, dres, x, g, w, layer):
    n, d = x.shape
    c = w.shape[2]
    tm, tc = _tile(n, 512, 8), _tile(c, 768)
    nc = c // tc

    def body(dp_ref, dr_ref, x_ref, g_ref, w_ref, dx_ref, dg_ref, hn_ref, dh_ref):
        i, j = pl.program_id(0), pl.program_id(1)

        @pl.when(j == 0)
        def _():
            dh_ref[...] = jnp.zeros_like(dh_ref)

        dh_ref[...] += _dot_nt(dp_ref[...], w_ref[0])

        @pl.when(j == nc - 1)
        def _():
            gg = g_ref[...]
            h, xhat, rs = _rms_fwd(x_ref[...], gg)
            hn_ref[...] = h.astype(BF16)
            dxn, dgr = _rms_bwd(dh_ref[...], xhat, rs, gg)
            dx_ref[...] = dr_ref[...] + dxn
            _acc_out(dg_ref, jnp.sum(dgr, axis=0, keepdims=True), i == 0)

    return _call(
        body,
        name="proj_bwd",
        grid=(n // tm, nc),
        in_specs=[
            pl.BlockSpec((tm, tc), lambda i, j: (i, j)),
            pl.BlockSpec((tm, d), lambda i, j: (i, 0)),
            pl.BlockSpec((tm, d), lambda i, j: (i, 0)),
            pl.BlockSpec((1, d), lambda i, j: (0, 0)),
            pl.BlockSpec((1, d, tc), lambda i, j: (layer, 0, j)),
        ],
        out_specs=[
            pl.BlockSpec((tm, d), lambda i, j: (i, 0)),
            pl.BlockSpec((1, d), lambda i, j: (0, 0)),
            pl.BlockSpec((tm, d), lambda i, j: (i, 0)),
        ],
        out_shape=[
            jax.ShapeDtypeStruct((n, d), F32),
            jax.ShapeDtypeStruct((1, d), F32),
            jax.ShapeDtypeStruct((n, d), BF16),
        ],
        scratch_shapes=[pltpu.VMEM((tm, d), F32)],
        compiler_params=_params(),
    )(dproj, dres, x, g, w)


S5_TS = 512


def _cmul(ar, ai, br, bi):
    return ar * br - ai * bi, ar * bi + ai * br


def _s5_tables(ar, ai, reverse):
    gp = ar.shape[1]
    if reverse:
        ai = -ai
    a1r, a1i = jnp.broadcast_to(ar, (8, gp)), jnp.broadcast_to(ai, (8, gp))
    a2r, a2i = _cmul(a1r, a1i, a1r, a1i)
    a4r, a4i = _cmul(a2r, a2i, a2r, a2i)
    a8r, a8i = _cmul(a4r, a4i, a4r, a4i)
    row = lax.broadcasted_iota(jnp.int32, (8, gp), 0)
    e = (8 - row) if reverse else (row + 1)
    pr, pi = jnp.ones((8, gp), F32), jnp.zeros((8, gp), F32)
    for bit, (fr, fi) in ((1, (a1r, a1i)), (2, (a2r, a2i)), (4, (a4r, a4i)), (8, (a8r, a8i))):
        nr, ni = _cmul(pr, pi, fr, fi)
        on = (e & bit) != 0
        pr, pi = jnp.where(on, nr, pr), jnp.where(on, ni, pi)
    return (a1r, a1i, a2r, a2i, a4r, a4i, pr, pi)


def _s5_scan(xr_ref, xi_ref, tab_ref, cr_ref, ci_ref, ts, reverse):
    gp = xr_ref.shape[1]
    nt = ts // 8
    row = lax.broadcasted_iota(jnp.int32, (8, gp), 0)

    def shifted(v, s):
        if reverse:
            return jnp.where(row < 8 - s, pltpu.roll(v, 8 - s, 0), 0.0)
        return jnp.where(row >= s, pltpu.roll(v, s, 0), 0.0)

    def step(k, carry):
        cr, ci = carry
        t = (nt - 1 - k) if reverse else k
        r0 = pl.multiple_of(t * 8, 8)
        br = xr_ref[pl.ds(r0, 8), :]
        bi = xi_ref[pl.ds(r0, 8), :]
        for q, s in enumerate((1, 2, 4)):
            fr, fi = tab_ref[2 * q], tab_ref[2 * q + 1]
            sr, si = shifted(br, s), shifted(bi, s)
            mr, mi = _cmul(fr, fi, sr, si)
            br, bi = br + mr, bi + mi
        mr, mi = _cmul(tab_ref[6], tab_ref[7], cr, ci)
        br, bi = br + mr, bi + mi
        xr_ref[pl.ds(r0, 8), :] = br
        xi_ref[pl.ds(r0, 8), :] = bi
        edge = 0 if reverse else 7
        return (jnp.broadcast_to(br[edge:edge + 1, :], (8, gp)),
                jnp.broadcast_to(bi[edge:edge + 1, :], (8, gp)))

    cr, ci = lax.fori_loop(0, nt, step, (cr_ref[...], ci_ref[...]))
    cr_ref[...] = cr
    ci_ref[...] = ci


def s5_fwd(proj, ucol, seq, ar, ai, bm_r, bm_i, cm_r, cm_i, dskip, w_glu, layer):
    n = proj.shape[0]
    ds, gp = bm_r.shape
    ts = min(S5_TS, seq)
    nt = seq // ts

    def body(u_ref, ar_ref, ai_ref, bmr_ref, bmi_ref, cmr_ref, cmi_ref, d_ref, wg_ref,
             out_ref, xr_ref, xi_ref, yp_ref, tab_ref, cr_ref, ci_ref):
        @pl.when(pl.program_id(1) == 0)
        def _():
            for q, v in enumerate(_s5_tables(ar_ref[...], ai_ref[...], False)):
                tab_ref[q] = v
            cr_ref[...] = jnp.zeros_like(cr_ref)
            ci_ref[...] = jnp.zeros_like(ci_ref)

        u = u_ref[...]
        ub = u.astype(BF16)
        xr_ref[...] = _dot(ub, bmr_ref[...])
        xi_ref[...] = _dot(ub, bmi_ref[...])
        _s5_scan(xr_ref, xi_ref, tab_ref, cr_ref, ci_ref, ts, False)
        y = (_dot(xr_ref[...].astype(BF16), cmr_ref[...]) - _dot(xi_ref[...].astype(BF16), cmi_ref[...])
             + d_ref[...] * u)
        yp_ref[...] = y
        z = _dot(_gelu(y).astype(BF16), wg_ref[0])
        out_ref[...] = z[:, :ds] * _sig(z[:, ds:])

    full = lambda shape: pl.BlockSpec(shape, lambda b, t: (0, 0))
    return _call(
        body,
        name="s5_fwd",
        grid=(n // seq, nt),
        in_specs=[
            pl.BlockSpec((ts, ds), lambda b, t: (b * nt + t, ucol)),
            full((1, gp)), full((1, gp)), full((ds, gp)), full((ds, gp)), full((gp, ds)), full((gp, ds)),
            full((1, ds)), pl.BlockSpec((1, ds, 2 * ds), lambda b, t: (layer, 0, 0)),
        ],
        out_specs=[
            pl.BlockSpec((ts, ds), lambda b, t: (b * nt + t, 0)),
            pl.BlockSpec((ts, gp), lambda b, t: (b * nt + t, 0)),
            pl.BlockSpec((ts, gp), lambda b, t: (b * nt + t, 0)),
            pl.BlockSpec((ts, ds), lambda b, t: (b * nt + t, 0)),
        ],
        out_shape=[
            jax.ShapeDtypeStruct((n, ds), F32),
            jax.ShapeDtypeStruct((n, gp), F32),
            jax.ShapeDtypeStruct((n, gp), F32),
            jax.ShapeDtypeStruct((n, ds), F32),
        ],
        scratch_shapes=[pltpu.VMEM((8, 8, gp), F32), pltpu.VMEM((8, gp), F32), pltpu.VMEM((8, gp), F32)],
        compiler_params=_params(),
    )(proj, ar, ai, bm_r, bm_i, cm_r, cm_i, dskip, w_glu)


def s5_bwd(dout, ypre, proj, ucol, xr, xi, seq, ar, ai, bm_r, bm_i, cm_r, cm_i, dskip, w_glu, layer):
    n = proj.shape[0]
    ds, gp = bm_r.shape
    ts = min(S5_TS, seq)
    nt = seq // ts

    def body(do_ref, yp_ref, u_ref, xr_ref, xi_ref, hr_ref, hi_ref, ar_ref, ai_ref, bmr_ref, bmi_ref,
             cmr_ref, cmi_ref, d_ref, wg_ref,
             du_ref, gr_ref, gi_ref, dyb_ref, glb_ref, dzb_ref, dar_ref, dai_ref, dd_ref,
             tab_ref, cr_ref, ci_ref):
        b, t = pl.program_id(0), pl.program_id(1)
        first = jnp.logical_and(b == 0, t == 0)

        @pl.when(t == 0)
        def _():
            for q, v in enumerate(_s5_tables(ar_ref[...], ai_ref[...], True)):
                tab_ref[q] = v
            cr_ref[...] = jnp.zeros_like(cr_ref)
            ci_ref[...] = jnp.zeros_like(ci_ref)

        yp = yp_ref[...]
        u = u_ref[...]
        gl = _gelu(yp).astype(BF16)
        glb_ref[...] = gl
        z = _dot(gl, wg_ref[0])
        za, sg = z[:, :ds], _sig(z[:, ds:])
        do = do_ref[...]
        da = (do * sg).astype(BF16)
        dg = (do * za * sg * (1.0 - sg)).astype(BF16)
        dzb_ref[:, :ds] = da
        dzb_ref[:, ds:] = dg
        dgl = _dot_nt(da, wg_ref[0, :, :ds]) + _dot_nt(dg, wg_ref[0, :, ds:])
        dyp = dgl * _gelu_grad(yp)
        dypb = dyp.astype(BF16)
        dyb_ref[...] = dypb
        _acc_out(dd_ref, jnp.sum(dyp * u, axis=0, keepdims=True), first)

        gr_ref[...] = _dot_nt(dypb, cmr_ref[...])
        gi_ref[...] = -_dot_nt(dypb, cmi_ref[...])
        _s5_scan(gr_ref, gi_ref, tab_ref, cr_ref, ci_ref, ts, True)
        gr, gi = gr_ref[...], gi_ref[...]
        du_ref[...] = d_ref[...] * dyp + _dot_nt(gr.astype(BF16), bmr_ref[...]) + _dot_nt(gi.astype(BF16), bmi_ref[...])

        row = lax.broadcasted_iota(jnp.int32, (ts, gp), 0)
        live = jnp.where(t == nt - 1, 0.0, 1.0)
        pr = jnp.broadcast_to(hr_ref[7:8, :] * live, (ts, gp))
        pi = jnp.broadcast_to(hi_ref[7:8, :] * live, (ts, gp))
        sr = jnp.where(row == 0, pr, pltpu.roll(xr_ref[...], 1, 0))
        si = jnp.where(row == 0, pi, pltpu.roll(xi_ref[...], 1, 0))
        _acc_out(dar_ref, jnp.sum(gr * sr + gi * si, axis=0, keepdims=True), first)
        _acc_out(dai_ref, jnp.sum(gi * sr - gr * si, axis=0, keepdims=True), first)

    full = lambda shape: pl.BlockSpec(shape, lambda b, t: (0, 0))
    blk = lambda w, col=0: pl.BlockSpec((ts, w), lambda b, t: (b * nt + nt - 1 - t, col))
    halo = pl.BlockSpec((8, gp), lambda b, t: (jnp.maximum((b * seq + (nt - 1 - t) * ts) // 8 - 1, 0), 0))
    return _call(
        body,
        name="s5_bwd",
        grid=(n // seq, nt),
        in_specs=[
            blk(ds), blk(ds), blk(ds, ucol), blk(gp), blk(gp), halo, halo,
            full((1, gp)), full((1, gp)), full((ds, gp)), full((ds, gp)), full((gp, ds)), full((gp, ds)),
            full((1, ds)), pl.BlockSpec((1, ds, 2 * ds), lambda b, t: (layer, 0, 0)),
        ],
        out_specs=[
            blk(ds), blk(gp), blk(gp), blk(ds), blk(ds), blk(2 * ds),
            full((1, gp)), full((1, gp)), full((1, ds)),
        ],
        out_shape=[
            jax.ShapeDtypeStruct((n, ds), F32),
            jax.ShapeDtypeStruct((n, gp), F32),
            jax.ShapeDtypeStruct((n, gp), F32),
            jax.ShapeDtypeStruct((n, ds), BF16),
            jax.ShapeDtypeStruct((n, ds), BF16),
            jax.ShapeDtypeStruct((n, 2 * ds), BF16),
            jax.ShapeDtypeStruct((1, gp), F32),
            jax.ShapeDtypeStruct((1, gp), F32),
            jax.ShapeDtypeStruct((1, ds), F32),
        ],
        scratch_shapes=[pltpu.VMEM((8, 8, gp), F32), pltpu.VMEM((8, gp), F32), pltpu.VMEM((8, gp), F32)],
        compiler_params=_params(),
    )(dout, ypre, proj, xr, xi, xr, xi, ar, ai, bm_r, bm_i, cm_r, cm_i, dskip, w_glu)


BAND = QBLK + CHUNK
NCH = QBLK // CHUNK


def _attn_specs(nq):
    cur = pl.BlockSpec((1, QBLK, HEAD), lambda h, b, i: (h, b * nq + i, 0))
    prev = pl.BlockSpec((1, QBLK, HEAD), lambda h, b, i: (h, b * nq + jnp.maximum(i - 1, 0), 0))
    vec = pl.BlockSpec((1, 1, 2 * QBLK), lambda h, b, i: (h, 0, 0))
    gain = pl.BlockSpec((1, HEAD), lambda h, b, i: (0, 0))
    return cur, prev, vec, gain


def _attn_build_table(tv, bias_ref, tab_ref):
    w = 2 * QBLK
    for qi in range(CHUNK):
        bias_ref[qi:qi + 1, :] = pltpu.roll(tv, (qi - (CHUNK - 1)) % w, 1)
    bias = bias_ref[...]
    lane = lax.broadcasted_iota(jnp.int32, (CHUNK, w), 1)
    for c in range(NCH):
        rolled = bias if c == 0 else pltpu.roll(bias, CHUNK * c, 1)
        ok = jnp.logical_and(lane >= CHUNK * c, lane < CHUNK * c + BAND)
        tab_ref[CHUNK * c:CHUNK * (c + 1), :] = jnp.where(ok, rolled, NEG)


def _attn_reduce_table(dt_ref, bias_ref):
    w = 2 * QBLK
    acc = dt_ref[0:CHUNK, :]
    for c in range(1, NCH):
        acc = acc + pltpu.roll(dt_ref[CHUNK * c:CHUNK * (c + 1), :], w - CHUNK * c, 1)
    bias_ref[...] = acc
    out = jnp.zeros((1, w), F32)
    for qi in range(CHUNK):
        out = out + pltpu.roll(bias_ref[qi:qi + 1, :], ((CHUNK - 1) - qi) % w, 1)
    return out


def _attn_probs(q, kp, kc, gq, gk, table, first_block):
    qn, qhat, qrs = _rms_fwd(q, gq)
    k = jnp.concatenate([kp, kc], axis=0)
    kn, khat, krs = _rms_fwd(k, gk)
    s = _dot_nt(qn.astype(BF16), kn.astype(BF16)) * (HEAD ** -0.5) + table
    col = lax.broadcasted_iota(jnp.int32, s.shape, 1)
    s = jnp.where(jnp.logical_and(first_block, col < QBLK), NEG, s)
    e = jnp.exp(s - jnp.max(s, axis=-1, keepdims=True))
    p = e * (1.0 / jnp.sum(e, axis=-1, keepdims=True))
    return p, (qn, qhat, qrs), (kn, khat, krs)


def attn_fwd(q, k, v, tv, gq, gk, seq, comm=None):
    h, n, _ = q.shape
    nq = seq // QBLK
    cur, prev, vec, gain = _attn_specs(nq)

    def body(q_ref, kp_ref, kc_ref, vp_ref, vc_ref, tv_ref, gq_ref, gk_ref, o_ref, bias_ref, tab_ref):
        @pl.when(jnp.logical_and(pl.program_id(1) == 0, pl.program_id(2) == 0))
        def _():
            _attn_build_table(tv_ref[0], bias_ref, tab_ref)

        p, _, _ = _attn_probs(q_ref[0], kp_ref[0], kc_ref[0], gq_ref[...], gk_ref[...], tab_ref[...],
                              pl.program_id(2) == 0)
        vv = jnp.concatenate([vp_ref[0], vc_ref[0]], axis=0).astype(BF16)
        o_ref[0] = _dot(p.astype(BF16), vv)

    return _hosted_call(
        body, comm,
        name="attn_fwd",
        grid=(h, n // seq, nq),
        in_specs=[cur, prev, cur, prev, cur, vec, gain, gain],
        out_specs=[cur],
        out_shape=[jax.ShapeDtypeStruct((h, n, HEAD), F32)],
        scratch_shapes=[pltpu.VMEM((CHUNK, 2 * QBLK), F32), pltpu.VMEM((QBLK, 2 * QBLK), F32)],
        args=(q, k, k, v, v, tv, gq, gk),
    )


def attn_bwd(do, q, k, v, tv, gq, gk, seq, comm=None):
    h, n, _ = q.shape
    nb = n // seq
    nq = seq // QBLK
    cur, prev, vec, gain = _attn_specs(nq)

    def body(do_ref, q_ref, kp_ref, kc_ref, vp_ref, vc_ref, tv_ref, gq_ref, gk_ref,
             dq_ref, dkp_ref, dkc_ref, dvp_ref, dvc_ref, dtv_ref, dgq_ref, dgk_ref, bias_ref, tab_ref, dt_ref):
        hh, b, i = pl.program_id(0), pl.program_id(1), pl.program_id(2)
        head_start = jnp.logical_and(b == 0, i == 0)

        @pl.when(head_start)
        def _():
            _attn_build_table(tv_ref[0], bias_ref, tab_ref)

        gq_, gk_ = gq_ref[...], gk_ref[...]
        p, (qn, qhat, qrs), (kn, khat, krs) = _attn_probs(
            q_ref[0], kp_ref[0], kc_ref[0], gq_, gk_, tab_ref[...], i == 0)
        dob = do_ref[0].astype(BF16)
        vv = jnp.concatenate([vp_ref[0], vc_ref[0]], axis=0).astype(BF16)
        dv = _dot_tn(p.astype(BF16), dob)
        dp = _dot_nt(dob, vv)
        ds = p * (dp - jnp.sum(p * dp, axis=-1, keepdims=True))
        _acc_out(dt_ref, ds, head_start)
        dsb = (ds * (HEAD ** -0.5)).astype(BF16)
        dqn = _dot(dsb, kn.astype(BF16))
        dkn = _dot_tn(dsb, qn.astype(BF16))
        dq, dgq_rows = _rms_bwd(dqn, qhat, qrs, gq_)
        dk, dgk_rows = _rms_bwd(dkn, khat, krs, gk_)
        dq_ref[0] = dq
        dkp_ref[0] = dk[:QBLK]
        dkc_ref[0] = dk[QBLK:]
        dvp_ref[0] = dv[:QBLK]
        dvc_ref[0] = dv[QBLK:]
        first = jnp.logical_and(hh == 0, head_start)
        _acc_out(dgq_ref, jnp.sum(dgq_rows, axis=0, keepdims=True), first)
        _acc_out(dgk_ref, jnp.sum(dgk_rows, axis=0, keepdims=True), first)

        @pl.when(jnp.logical_and(b == nb - 1, i == nq - 1))
        def _():
            dtv_ref[0] = _attn_reduce_table(dt_ref, bias_ref)

    hm = jax.ShapeDtypeStruct((h, n, HEAD), F32)
    return _hosted_call(
        body, comm,
        name="attn_bwd",
        grid=(h, nb, nq),
        in_specs=[cur, cur, prev, cur, prev, cur, vec, gain, gain],
        out_specs=[cur, cur, cur, cur, cur, vec, gain, gain],
        out_shape=[hm, hm, hm, hm, hm, jax.ShapeDtypeStruct(tv.shape, F32),
                   jax.ShapeDtypeStruct((1, HEAD), F32), jax.ShapeDtypeStruct((1, HEAD), F32)],
        scratch_shapes=[pltpu.VMEM((CHUNK, 2 * QBLK), F32), pltpu.VMEM((QBLK, 2 * QBLK), F32),
                        pltpu.VMEM((QBLK, 2 * QBLK), F32)],
        args=(do, q, k, k, v, v, tv, gq, gk),
    )


CONV_TC = 512


def _ln_fwd(h1, g, b):
    mu = jnp.mean(h1, axis=-1, keepdims=True)
    xc = h1 - mu
    rs = lax.rsqrt(jnp.mean(xc * xc, axis=-1, keepdims=True) + EPS)
    yhat = xc * rs
    return yhat * g + b, yhat, rs


def _glu(z, dc):
    return z[:, :dc] * _sig(z[:, dc:])


def conv_fwd(proj, zcol, seq, w, bdw, lng, lnb):
    n = proj.shape[0]
    dc = w.shape[1]
    tc = min(CONV_TC, seq)
    nt = seq // tc

    def body(z_ref, zp_ref, w_ref, b_ref, g_ref, lb_ref, h1_ref, o_ref, ext_ref):
        live = jnp.where(pl.program_id(1) == 0, 0.0, 1.0)
        ext_ref[pl.ds(0, HALO), :] = _glu(zp_ref[...], dc) * live
        ext_ref[pl.ds(HALO, tc), :] = _glu(z_ref[...], dc)
        acc = jnp.zeros((tc, dc), F32) + b_ref[...]
        for j in range(CONV_W):
            acc = acc + w_ref[j:j + 1, :] * ext_ref[pl.ds(HALO - (CONV_W - 1) + j, tc), :]
        h1_ref[...] = acc
        ln, _, _ = _ln_fwd(acc, g_ref[...], lb_ref[...])
        o_ref[...] = ln * _sig(ln)

    full = lambda shape: pl.BlockSpec(shape, lambda b, t: (0, 0))
    return _call(
        body,
        name="conv_fwd",
        grid=(n // seq, nt),
        in_specs=[
            pl.BlockSpec((tc, 2 * dc), lambda b, t: (b * nt + t, zcol)),
            pl.BlockSpec((HALO, 2 * dc), lambda b, t: (jnp.maximum((b * seq + t * tc) // HALO - 1, 0), zcol)),
            full((HALO, dc)), full((1, dc)), full((1, dc)), full((1, dc)),
        ],
        out_specs=[
            pl.BlockSpec((tc, dc), lambda b, t: (b * nt + t, 0)),
            pl.BlockSpec((tc, dc), lambda b, t: (b * nt + t, 0)),
        ],
        out_shape=[jax.ShapeDtypeStruct((n, dc), F32), jax.ShapeDtypeStruct((n, dc), F32)],
        scratch_shapes=[pltpu.VMEM((tc + HALO, dc), F32)],
        compiler_params=_params(),
    )(proj, proj, w, bdw, lng, lnb)


def conv_bwd(dco, h1, proj, zcol, seq, w, bdw, lng, lnb):
    n = proj.shape[0]
    dc = w.shape[1]
    tc = min(CONV_TC, seq)
    nt = seq // tc
    nrow = n // HALO

    def body(do_ref, don_ref, h1_ref, h1n_ref, z_ref, zp_ref, w_ref, g_ref, lb_ref,
             dz_ref, dw_ref, db_ref, dg_ref, dlb_ref, ext_ref, dext_ref):
        b, t = pl.program_id(0), pl.program_id(1)
        first = jnp.logical_and(b == 0, t == 0)
        g, lb = g_ref[...], lb_ref[...]

        def dh1_of(do, h1):
            ln, yhat, rs = _ln_fwd(h1, g, lb)
            s = _sig(ln)
            dln = do * (s * (1.0 + ln * (1.0 - s)))
            dyh = dln * g
            dh1 = rs * (dyh - jnp.mean(dyh, axis=-1, keepdims=True)
                        - yhat * jnp.mean(dyh * yhat, axis=-1, keepdims=True))
            return dh1, dln, yhat

        dh1, dln, yhat = dh1_of(do_ref[...], h1_ref[...])
        dh1n, _, _ = dh1_of(don_ref[...], h1n_ref[...])
        _acc_out(dg_ref, jnp.sum(dln * yhat, axis=0, keepdims=True), first)
        _acc_out(dlb_ref, jnp.sum(dln, axis=0, keepdims=True), first)
        _acc_out(db_ref, jnp.sum(dh1, axis=0, keepdims=True), first)

        dext_ref[pl.ds(0, tc), :] = dh1
        dext_ref[pl.ds(tc, HALO), :] = dh1n * jnp.where(t == nt - 1, 0.0, 1.0)
        z = z_ref[...]
        ext_ref[pl.ds(0, HALO), :] = _glu(zp_ref[...], dc) * jnp.where(t == 0, 0.0, 1.0)
        ext_ref[pl.ds(HALO, tc), :] = _glu(z, dc)

        @pl.when(first)
        def _():
            dw_ref[...] = jnp.zeros_like(dw_ref)

        dh0 = jnp.zeros((tc, dc), F32)
        for j in range(CONV_W):
            dh0 = dh0 + w_ref[j:j + 1, :] * dext_ref[pl.ds(CONV_W - 1 - j, tc), :]
            dw_ref[j:j + 1, :] += jnp.sum(dh1 * ext_ref[pl.ds(HALO - (CONV_W - 1) + j, tc), :],
                                          axis=0, keepdims=True)
        za, sg = z[:, :dc], _sig(z[:, dc:])
        dz_ref[:, :dc] = dh0 * sg
        dz_ref[:, dc:] = dh0 * za * sg * (1.0 - sg)

    full = lambda shape: pl.BlockSpec(shape, lambda b, t: (0, 0))
    cur = lambda wd, col=0: pl.BlockSpec((tc, wd), lambda b, t: (b * nt + t, col))
    nxt = pl.BlockSpec((HALO, dc), lambda b, t: (jnp.minimum((b * seq + (t + 1) * tc) // HALO, nrow - 1), 0))
    return _call(
        body,
        name="conv_bwd",
        grid=(n // seq, nt),
        in_specs=[
            cur(dc), nxt, cur(dc), nxt, cur(2 * dc, zcol),
            pl.BlockSpec((HALO, 2 * dc), lambda b, t: (jnp.maximum((b * seq + t * tc) // HALO - 1, 0), zcol)),
            full((HALO, dc)), full((1, dc)), full((1, dc)),
        ],
        out_specs=[cur(2 * dc), full((HALO, dc)), full((1, dc)), full((1, dc)), full((1, dc))],
        out_shape=[
            jax.ShapeDtypeStruct((n, 2 * dc), F32),
            jax.ShapeDtypeStruct((HALO, dc), F32),
            jax.ShapeDtypeStruct((1, dc), F32),
            jax.ShapeDtypeStruct((1, dc), F32),
            jax.ShapeDtypeStruct((1, dc), F32),
        ],
        scratch_shapes=[pltpu.VMEM((tc + HALO, dc), F32), pltpu.VMEM((tc + HALO, dc), F32)],
        compiler_params=_params(),
    )(dco, dco, h1, h1, proj, proj, w, lng, lnb)


def _merge_common(l0, l1, l2, bg, so, ao, co, wbs, wba, wbc, d):
    ys = _dot(so.astype(BF16), wbs)
    ya = _dot(ao.astype(BF16), wba)
    yc = _dot(co.astype(BF16), wbc)
    gs = _sig(l0 + bg[:, :d])
    ga = _sig(l1 + bg[:, d:2 * d])
    gc = _sig(l2 + bg[:, 2 * d:])
    return (ys, ya, yc), (gs, ga, gc)


def _merge_specs(tm, d, dss, da, dc, layer):
    row = lambda w, col=0: pl.BlockSpec((tm, w), lambda i: (i, col))
    full = lambda r, c: pl.BlockSpec((r, c), lambda i: (0, 0))
    stacked = lambda r: pl.BlockSpec((1, r, d), lambda i: (layer, 0, 0))
    acts = [row(d, 0), row(d, 1), row(d, 2), full(1, 3 * d), row(dss), row(da), row(dc)]
    weights = [stacked(dss), stacked(da), stacked(dc), stacked(d)]
    return row, full, acts, weights


def merge_fwd(x, proj, bg, so, ao, co, wbs, wba, wbc, wout, layer):
    n, d = x.shape
    tm = _tile(n, 256, 8)
    row, full, acts, weights = _merge_specs(tm, d, so.shape[1], ao.shape[1], co.shape[1], layer)

    def body(x_ref, l0_ref, l1_ref, l2_ref, bg_ref, so_ref, ao_ref, co_ref,
             wbs_ref, wba_ref, wbc_ref, wo_ref, o_ref):
        (ys, ya, yc), (gs, ga, gc) = _merge_common(
            l0_ref[...], l1_ref[...], l2_ref[...], bg_ref[...], so_ref[...], ao_ref[...], co_ref[...],
            wbs_ref[0], wba_ref[0], wbc_ref[0], d)
        merged = gs * ys + ga * ya + gc * yc
        o_ref[...] = x_ref[...] + _dot(merged.astype(BF16), wo_ref[0])

    return _call(
        body,
        name="merge_fwd",
        grid=(n // tm,),
        in_specs=[row(d)] + acts + weights,
        out_specs=row(d),
        out_shape=jax.ShapeDtypeStruct((n, d), F32),
        compiler_params=_params(),
    )(x, proj, proj, proj, bg, so, ao, co, wbs, wba, wbc, wout)


def merge_bwd(dx, proj, bg, so, ao, co, wbs, wba, wbc, wout, layer):
    n, d = dx.shape
    dss, da, dc = so.shape[1], ao.shape[1], co.shape[1]
    tm = _tile(n, 256, 8)
    row, full, acts, weights = _merge_specs(tm, d, dss, da, dc, layer)

    def body(dx_ref, l0_ref, l1_ref, l2_ref, bg_ref, so_ref, ao_ref, co_ref, wbs_ref, wba_ref, wbc_ref, wo_ref,
             dl_ref, dso_ref, dao_ref, dco_ref, dbg_ref, mg_ref, dxb_ref, dys_ref, dya_ref, dyc_ref):
        wbs, wba, wbc = wbs_ref[0], wba_ref[0], wbc_ref[0]
        (ys, ya, yc), (gs, ga, gc) = _merge_common(
            l0_ref[...], l1_ref[...], l2_ref[...], bg_ref[...], so_ref[...], ao_ref[...], co_ref[...],
            wbs, wba, wbc, d)
        mg_ref[...] = (gs * ys + ga * ya + gc * yc).astype(BF16)
        dxb = dx_ref[...].astype(BF16)
        dxb_ref[...] = dxb
        dm = _dot_nt(dxb, wo_ref[0])
        first = pl.program_id(0) == 0
        for k, (y, g, w, dy_ref, db_ref) in enumerate((
                (ys, gs, wbs, dys_ref, dso_ref), (ya, ga, wba, dya_ref, dao_ref), (yc, gc, wbc, dyc_ref, dco_ref))):
            dl = dm * y * g * (1.0 - g)
            dl_ref[:, k * d:(k + 1) * d] = dl.astype(BF16)
            _acc_out(dbg_ref.at[:, k * d:(k + 1) * d], jnp.sum(dl, axis=0, keepdims=True), first)
            dy = (dm * g).astype(BF16)
            dy_ref[...] = dy
            db_ref[...] = _dot_nt(dy, w)

    bf = lambda w: jax.ShapeDtypeStruct((n, w), BF16)
    return _call(
        body,
        name="merge_bwd",
        grid=(n // tm,),
        in_specs=[row(d)] + acts + weights,
        out_specs=[row(3 * d), row(dss), row(da), row(dc), full(1, 3 * d),
                   row(d), row(d), row(d), row(d), row(d)],
        out_shape=[bf(3 * d), jax.ShapeDtypeStruct((n, dss), F32), jax.ShapeDtypeStruct((n, da), F32),
                   jax.ShapeDtypeStruct((n, dc), F32), jax.ShapeDtypeStruct((1, 3 * d), F32),
                   bf(d), bf(d), bf(d), bf(d), bf(d)],
        compiler_params=_params(),
    )(dx, proj, proj, proj, bg, so, ao, co, wbs, wba, wbc, wout)


def loss_head(y, target):
    n, d = y.shape
    tm = _tile(n, 512, 8)

    def body(y_ref, t_ref, dy_ref, l_ref):
        e = y_ref[...] - t_ref[...]
        dy_ref[...] = e * (1.0 / d)
        part = 0.5 * jnp.sum(jnp.sum(e * e, axis=-1, keepdims=True) * (1.0 / d), axis=0, keepdims=True)
        _acc_out(l_ref, part, pl.program_id(0) == 0)

    return _call(
        body,
        name="loss_head",
        grid=(n // tm,),
        in_specs=[pl.BlockSpec((tm, d), lambda i: (i, 0)), pl.BlockSpec((tm, d), lambda i: (i, 0))],
        out_specs=[pl.BlockSpec((tm, d), lambda i: (i, 0)), pl.BlockSpec((1, 1), lambda i: (0, 0))],
        out_shape=[jax.ShapeDtypeStruct((n, d), F32), jax.ShapeDtypeStruct((1, 1), F32)],
        compiler_params=_params(),
    )(y, target)


def _mesh_pos():
    return lax.axis_index("x"), lax.axis_index("y"), lax.axis_index("c")


ANY = pl.BlockSpec(memory_space=pl.ANY)


def _comm_sems(na):
    return [pltpu.SemaphoreType.DMA((na, 7)), pltpu.SemaphoreType.DMA((na, 7)), pltpu.SemaphoreType.DMA((na,))]


def _gather_plan(x_refs, out_refs, sems):
    na = len(x_refs)
    send_sems, recv_sems, local_sems = sems
    x, y, c = _mesh_pos()
    me, sibling = (x, y, c), (x, y, 1 - c)
    chips = [(1 - x, y), (x, 1 - y), (1 - x, 1 - y)]

    def slot(a, px, py, pc):
        return out_refs[a].at[4 * px + 2 * py + pc]

    def copy(a, k, block, to, src=None):
        return pltpu.make_async_remote_copy(
            src_ref=slot(a, *block) if src is None else src, dst_ref=slot(a, *block),
            send_sem=send_sems.at[a, k], recv_sem=recv_sems.at[a, k],
            device_id=to, device_id_type=pl.DeviceIdType.MESH)

    mine = [pltpu.make_async_copy(x_refs[a], slot(a, *me), local_sems.at[a]) for a in range(na)]
    first = []
    for a in range(na):
        first.append(copy(a, 0, me, sibling, src=x_refs[a]))
        first += [copy(a, 1 + j, me, (*chip, c), src=x_refs[a]) for j, chip in enumerate(chips)]

    def start():
        for cp in mine + first:
            cp.start()

    def finish():
        passed = []
        for j, chip in enumerate(chips):
            for a in range(na):
                copy(a, 1 + j, (*chip, c), me).wait_recv()
                fwd = copy(a, 4 + j, (*chip, c), sibling)
                fwd.start()
                passed.append(fwd)
        for a in range(na):
            copy(a, 0, sibling, me).wait_recv()
            for j, chip in enumerate(chips):
                copy(a, 4 + j, (*chip, 1 - c), me).wait_recv()
        for cp in first + passed:
            cp.wait_send()
        for cp in mine:
            cp.wait()

    return start, finish


def _gather_out(shards):
    return [jax.ShapeDtypeStruct((N_DEV,) + s.shape, s.dtype) for s in shards]


def all_gather(shards):
    na = len(shards)

    def body(*refs):
        start, finish = _gather_plan(refs[:na], refs[na:2 * na], refs[2 * na:])
        start()
        finish()

    return _call(
        body,
        name="all_gather",
        out_shape=_gather_out(shards),
        in_specs=[ANY] * na,
        out_specs=[ANY] * na,
        scratch_shapes=_comm_sems(na),
    )(*shards)


def _hosted_call(body, comm, *, name, grid, in_specs, out_specs, out_shape, scratch_shapes, args):
    if comm is None:
        res = _call(body, name=name, grid=grid, in_specs=in_specs, out_specs=out_specs, out_shape=out_shape,
                    scratch_shapes=scratch_shapes, compiler_params=_params())(*args)
        return res, []
    plan, arrays, c_out = comm
    n_in, n_out, n_scr, ci, co = len(in_specs), len(out_specs), len(scratch_shapes), len(arrays), len(c_out)

    def hosted(*refs):
        ins, cins = refs[:n_in], refs[n_in:n_in + ci]
        o0 = n_in + ci
        outs, couts = refs[o0:o0 + n_out], refs[o0 + n_out:o0 + n_out + co]
        s0 = o0 + n_out + co
        scr, sems = refs[s0:s0 + n_scr], refs[s0 + n_scr:]
        ids = [pl.program_id(ax) for ax in range(len(grid))]
        first = functools.reduce(jnp.logical_and, [i == 0 for i in ids])
        last = functools.reduce(jnp.logical_and, [i == g - 1 for i, g in zip(ids, grid)])
        start, finish = plan(cins, couts, sems)
        pl.when(first)(start)
        body(*ins, *outs, *scr)
        pl.when(last)(finish)

    res = _call(hosted, name=name + "_comm", grid=grid, in_specs=list(in_specs) + [ANY] * ci,
                out_specs=list(out_specs) + [ANY] * co, out_shape=list(out_shape) + list(c_out),
                scratch_shapes=list(scratch_shapes) + _comm_sems(max(ci, co)),
                compiler_params=_params())(*args, *arrays)
    return res[:n_out], res[n_out:]


def _exchange_plan(bcast=()):
    def plan(s_refs, r_refs, sems):
        na = len(s_refs)
        send_sems, recv_sems, local_sems = sems
        x, y, c = _mesh_pos()
        me = 4 * x + 2 * y + c

        def peer(k):
            px = (1 - x) if k & 4 else x
            py = (1 - y) if k & 2 else y
            pc = (1 - c) if k & 1 else c
            return (px, py, pc), 4 * px + 2 * py + pc

        def src_dst(a, pid, slot):
            return (s_refs[a] if a in bcast else s_refs[a].at[pid]), r_refs[a].at[slot]

        def copy(a, k):
            to, pid = peer(k)
            src, dst = src_dst(a, pid, me)
            return pltpu.make_async_remote_copy(
                src_ref=src, dst_ref=dst, send_sem=send_sems.at[a, k - 1], recv_sem=recv_sems.at[a, k - 1],
                device_id=to, device_id_type=pl.DeviceIdType.MESH)

        def arrival(a, k):
            _, pid = peer(k)
            src, dst = src_dst(a, pid, pid)
            return pltpu.make_async_remote_copy(
                src_ref=src, dst_ref=dst, send_sem=send_sems.at[a, k - 1], recv_sem=recv_sems.at[a, k - 1],
                device_id=(x, y, c), device_id_type=pl.DeviceIdType.MESH)

        mine = [pltpu.make_async_copy(*src_dst(a, me, me), local_sems.at[a]) for a in range(na)]
        sends = [copy(a, k) for k in range(1, N_DEV) for a in range(na)]

        def start():
            for cp in mine + sends:
                cp.start()

        def finish():
            for k in range(1, N_DEV):
                for a in range(na):
                    arrival(a, k).wait_recv()
            for cp in sends:
                cp.wait_send()
            for cp in mine:
                cp.wait()

        return start, finish

    return plan


def _exchange_out(slabs, bcast=()):
    return [jax.ShapeDtypeStruct(((N_DEV,) + s.shape) if a in bcast else s.shape, s.dtype)
            for a, s in enumerate(slabs)]


def grad_exchange(slabs, small):
    arrays = list(slabs) + [small]
    na = len(arrays)
    bcast = (na - 1,)

    def body(*refs):
        start, finish = _exchange_plan(bcast)(refs[:na], refs[na:2 * na], refs[2 * na:])
        start()
        finish()

    return _call(
        body,
        name="grad_exchange",
        out_shape=_exchange_out(arrays, bcast),
        in_specs=[ANY] * na,
        out_specs=[ANY] * na,
        scratch_shapes=_comm_sems(na),
    )(*arrays)


ADAM_BLOCK = 256 * 1024


def sum_adamw(recv, w, m, v, layer, prev, name):
    depth, rows, cols = w.shape
    tr = _tile(rows, max(8, ADAM_BLOCK // cols // 8 * 8), 8)
    c1 = 1.0 / (1.0 - ADAM_B1 ** ADAM_STEP)
    c2 = 1.0 / (1.0 - ADAM_B2 ** ADAM_STEP)

    def body(r_ref, w_ref, m_ref, v_ref, *rest):
        g_ref, d_ref, mo_ref, vo_ref = rest[-4:]
        g = r_ref[0].astype(F32)
        for s in range(1, N_DEV):
            g = g + r_ref[s].astype(F32)
        mn = ADAM_B1 * m_ref[0] + (1.0 - ADAM_B1) * g
        vn = ADAM_B2 * v_ref[0] + (1.0 - ADAM_B2) * (g * g)
        g_ref[0] = g
        mo_ref[0] = mn
        vo_ref[0] = vn
        d_ref[0] = -ADAM_LR * ((mn * c1) / (jnp.sqrt(vn * c2) + ADAM_EPS) + ADAM_WD * w_ref[0])

    blk = pl.BlockSpec((1, tr, cols), lambda i: (layer, i, 0))
    out = jax.ShapeDtypeStruct((depth, rows, cols), F32)
    in_specs = [pl.BlockSpec((N_DEV, tr, cols), lambda i: (0, i, 0)), blk, blk, blk]
    args = [recv, w, m, v]
    extra = {}
    if prev is not None:
        in_specs += [ANY] * 4
        args += list(prev)
        extra["input_output_aliases"] = {4 + j: j for j in range(4)}
    return _call(
        body,
        name=name,
        grid=(rows // tr,),
        in_specs=in_specs,
        out_specs=[blk, blk, blk, blk],
        out_shape=[out, out, out, out],
        compiler_params=_params(),
        **extra,
    )(*args)


def _attn_bias_vector(rel_bias):
    h = rel_bias.shape[0]
    n_far = BAND - MAX_REL
    n_near = BAND + CHUNK - 1 - n_far
    far = jnp.broadcast_to(rel_bias[:, 2 * MAX_REL:], (h, n_far))
    near = rel_bias[:, 2 * MAX_REL - n_near:2 * MAX_REL][:, ::-1]
    pad = jnp.zeros((h, 2 * QBLK - n_far - n_near), F32)
    return jnp.concatenate([far, near, pad], axis=1)[:, None, :]


def _s5_prepare(lre, lim, ldt, bre, bim, cre, cim):
    g, p = lre.shape
    lr = jnp.minimum(lre, -1e-4)
    dt = jnp.exp(ldt)[:, None]
    mag = jnp.exp(lr * dt)
    ar = mag * jnp.cos(lim * dt)
    ai = mag * jnp.sin(lim * dt)
    den = lr * lr + lim * lim
    coef_r = ((ar - 1.0) * lr + ai * lim) / den
    coef_i = (ai * lr - (ar - 1.0) * lim) / den
    bbar_r = coef_r[..., None] * bre - coef_i[..., None] * bim
    bbar_i = coef_r[..., None] * bim + coef_i[..., None] * bre
    eye = jnp.eye(g, dtype=F32)
    bd_in = lambda b: jnp.einsum("gpc,gh->gchp", b, eye).reshape(g * S5_GROUP, g * p)
    bd_out = lambda c: jnp.einsum("gcp,gh->gphc", c, eye).reshape(g * p, g * S5_GROUP)
    return (ar.reshape(1, g * p), ai.reshape(1, g * p), bd_in(bbar_r), bd_in(bbar_i), bd_out(cre), bd_out(cim))


SHARDED = ("ffn1_w_up", "ffn1_w_down", "w_in", "s5_w_glu", "w_br_s5", "w_br_attn", "conv_w_dw", "w_br_conv",
           "w_out", "ffn2_w_up", "ffn2_w_down")
WEIGHTS = ("ffn1_norm", "ffn1_w_up", "ffn1_w_down", "mix_norm", "w_in", "b_gate", "s5_lambda_re", "s5_lambda_im",
           "s5_log_dt", "s5_b_re", "s5_b_im", "s5_c_re", "s5_c_im", "s5_d", "s5_w_glu", "w_br_s5", "attn_q_gain",
           "attn_k_gain", "attn_rel_bias", "w_br_attn", "conv_w_dw", "conv_b_dw", "conv_ln_g", "conv_ln_b",
           "w_br_conv", "w_out", "ffn2_norm", "ffn2_w_up", "ffn2_w_down")
SMALL = tuple(nm for nm in WEIGHTS if nm not in SHARDED)
SMALL_LANES = 1024


def _cols_full(g):
    _, depth, k, nn = g.shape
    return g.transpose(1, 2, 0, 3).reshape(depth, k, N_DEV * nn)


def _rows_full(g):
    _, depth, r, cc = g.shape
    return g.transpose(1, 0, 2, 3).reshape(depth, N_DEV * r, cc)


def _cols_slabs(gfull):
    depth, k, c8 = gfull.shape
    return gfull.reshape(depth, k, N_DEV, c8 // N_DEV).transpose(0, 2, 1, 3)


def _heads(a, h):
    n = a.shape[0]
    return a.reshape(n, h, HEAD).transpose(1, 0, 2)


def _unheads(a):
    h, n, _ = a.shape
    return a.transpose(1, 0, 2).reshape(n, h * HEAD)


def _small_pack(t):
    flat = jnp.concatenate([t[nm].reshape(-1) for nm in SMALL])
    rows = -(-flat.shape[0] // SMALL_LANES)
    rows = -(-rows // 8) * 8
    return jnp.pad(flat, (0, rows * SMALL_LANES - flat.shape[0])).reshape(rows, SMALL_LANES)


def _small_unpack(flat, like):
    flat = flat.reshape(-1)
    out, off = {}, 0
    for nm in SMALL:
        out[nm] = flat[off:off + like[nm].size].reshape(like[nm].shape)
        off += like[nm].size
    return out


def _step(x, target, w, m, v):
    bsz, seq, d = x.shape
    n = bsz * seq
    depth = w["ffn1_norm"].shape[0]
    da, dss, dc = d // 2, d // 4, d // 4
    heads = da // HEAD
    gp = dss // S5_GROUP * S5_STATE
    mid = 3 * da + 2 * dc
    q0, k0, v0, z0, u0 = 3 * d, 3 * d + da, 3 * d + 2 * da, 3 * d + 3 * da, 3 * d + mid
    to_kernel_cols = lambda a: jnp.concatenate([a[..., dss + mid:], a[..., dss:dss + mid], a[..., :dss]], axis=-1)
    to_ref_cols = lambda a: jnp.concatenate([a[..., 3 * d + mid:], a[..., 3 * d:3 * d + mid], a[..., :3 * d]], axis=-1)

    shard = lambda nm: w[nm] if nm == "conv_w_dw" else w[nm].astype(BF16)
    mixer_names = ("w_in", "s5_w_glu", "w_br_s5", "w_br_attn", "w_br_conv", "conv_w_dw", "w_out")
    gather_of = lambda names: (_gather_plan, [shard(nm) for nm in names], _gather_out([shard(nm) for nm in names]))
    full = {}

    def take(names, arrays):
        for nm, g in zip(names, arrays):
            if nm in ("ffn1_w_up", "ffn2_w_up"):
                full[nm] = g
            elif nm in ("ffn1_w_down", "ffn2_w_down", "w_out"):
                full[nm] = _rows_full(g)
            elif nm == "w_in":
                full[nm] = to_kernel_cols(_cols_full(g))
            else:
                full[nm] = _cols_full(g)

    take(("ffn1_w_up", "ffn1_w_down"), all_gather([shard("ffn1_w_up"), shard("ffn1_w_down")]))
    nff = full["ffn1_w_up"].shape[3]

    row = lambda a: a.reshape(1, -1)
    saved = []
    xin = x.reshape(n, d)
    for l in range(depth):
        s = {"x0": xin}
        (s["x1"], s["a1"], s["b1"]), got = ffn_fwd(xin, row(w["ffn1_norm"][l]), full["ffn1_w_up"], full["ffn1_w_down"], l,
                                                   comm=gather_of(mixer_names) if l == 0 else None)
        if l == 0:
            take(mixer_names, got)
            conv_w = jnp.pad(full["conv_w_dw"], ((0, 0), (0, HALO - CONV_W), (0, 0)))
        proj = proj_fwd(s["x1"], row(w["mix_norm"][l]), full["w_in"], l)
        s["proj"] = proj
        prep_in = (w["s5_lambda_re"][l], w["s5_lambda_im"][l], w["s5_log_dt"][l], w["s5_b_re"][l], w["s5_b_im"][l],
                   w["s5_c_re"][l], w["s5_c_im"][l])
        (ar, ai, bm_r, bm_i, cm_r, cm_i), s["prep_vjp"] = jax.vjp(_s5_prepare, *prep_in)
        s["s5p"] = (ar, ai, bm_r.astype(BF16), bm_i.astype(BF16), cm_r.astype(BF16), cm_i.astype(BF16),
                    row(w["s5_d"][l]), full["s5_w_glu"], l)
        s["so"], s["xr"], s["xi"], s["ypre"] = s5_fwd(proj, u0 // dss, seq, *s["s5p"])
        tv, s["tv_vjp"] = jax.vjp(_attn_bias_vector, w["attn_rel_bias"][l])
        s["qkv"] = tuple(_heads(proj[:, c0:c0 + da], heads) for c0 in (q0, k0, v0))
        s["attnp"] = (tv, row(w["attn_q_gain"][l]), row(w["attn_k_gain"][l]))
        (ao,), got = attn_fwd(*s["qkv"], *s["attnp"], seq,
                              comm=gather_of(("ffn2_w_up", "ffn2_w_down")) if l == 0 else None)
        if l == 0:
            take(("ffn2_w_up", "ffn2_w_down"), got)
        s["ao"] = _unheads(ao)
        s["convp"] = (conv_w[l], row(w["conv_b_dw"][l]), row(w["conv_ln_g"][l]), row(w["conv_ln_b"][l]))
        s["h1"], s["co"] = conv_fwd(proj, z0 // (2 * dc), seq, *s["convp"])
        s["mergep"] = (row(w["b_gate"][l]), s["so"], s["ao"], s["co"], full["w_br_s5"], full["w_br_attn"],
                       full["w_br_conv"], full["w_out"], l)
        s["x2"] = merge_fwd(s["x1"], proj, *s["mergep"])
        (xin, s["a2"], s["b2"]), _ = ffn_fwd(s["x2"], row(w["ffn2_norm"][l]), full["ffn2_w_up"], full["ffn2_w_down"], l)
        saved.append(s)

    dx, loss = loss_head(xin, target.reshape(n, d))
    loss = lax.psum(loss[0, 0], ("x", "y", "c"))

    small_g = {nm: [None] * depth for nm in SMALL}
    slab = {nm: None for nm in SHARDED}
    conv_g = [None] * depth

    recv = {}

    def exchange_of(names, l):
        arrays = [slab.pop(nm) for nm in names]
        return [(nm, l) for nm in names], (_exchange_plan(), arrays, _exchange_out(arrays))

    def ffn_grads(which, hn, dyb, dab_a, dab_b, act):
        up, down = which + "_w_up", which + "_w_down"
        up4, down4 = (1, N_DEV, d, nff), (1, FF_CHUNKS, nff, d)
        half = wgrad(hn, dab_a, "wg_ffn_up", up4, "col")
        slab[up] = wgrad(hn, dab_b, "wg_ffn_up", up4, "col", g0=FF_CHUNKS, prev=half).reshape(up4[1:])
        slab[down] = wgrad(act, dyb, "wg_ffn_down", down4, "row").reshape(N_DEV, nff // 2, d)

    pending = None
    for l in reversed(range(depth)):
        s = saved[l]
        outs, got = ffn_bwd(dx, s["x2"], row(w["ffn2_norm"][l]), s["a2"], s["b2"],
                            full["ffn2_w_up"], full["ffn2_w_down"], l, comm=pending[1] if pending else None)
        if pending:
            recv.update(zip(pending[0], got))
        dx, dg, hn, dyb, dab_a, dab_b, act = outs
        small_g["ffn2_norm"][l] = dg
        ffn_grads("ffn2", hn, dyb, dab_a, dab_b, act)

        dlog, dso, dao, dco, dbg, mg, dxb, dys, dya, dyc = merge_bwd(dx, s["proj"], *s["mergep"])
        small_g["b_gate"][l] = dbg
        slab["w_out"] = wgrad(mg, dxb, "wg_out", (1, N_DEV, d // N_DEV, d), "row").reshape(N_DEV, d // N_DEV, d)
        for nm, act_in, dy_br in (("w_br_s5", s["so"], dys), ("w_br_attn", s["ao"], dya), ("w_br_conv", s["co"], dyc)):
            k_in = act_in.shape[1]
            slab[nm] = wgrad(act_in, dy_br, "wg_" + nm, (1, N_DEV, k_in, d // N_DEV), "col").reshape(
                N_DEV, k_in, d // N_DEV)

        dz, dwdw, dbdw, dlng, dlnb = conv_bwd(dco, s["h1"], s["proj"], z0 // (2 * dc), seq, *s["convp"])
        slab["conv_w_dw"] = _cols_slabs(dwdw[None, :CONV_W])[0].astype(BF16)
        small_g["conv_b_dw"][l], small_g["conv_ln_g"][l], small_g["conv_ln_b"][l] = dbdw, dlng, dlnb

        keys, comm = exchange_of(("ffn2_w_up", "ffn2_w_down"), l)
        outs, got = attn_bwd(_heads(dao, heads), *s["qkv"], *s["attnp"], seq, comm=comm)
        recv.update(zip(keys, got))
        dq, dkp, dkc, dvp, dvc, dtv, dgq, dgk = outs
        small_g["attn_q_gain"][l], small_g["attn_k_gain"][l] = dgq, dgk
        small_g["attn_rel_bias"][l] = s["tv_vjp"](dtv)[0]

        def from_prev(cur, prev):
            prev = prev.reshape(heads, bsz, seq, HEAD)
            prev = jnp.concatenate([prev[:, :, QBLK:], jnp.zeros_like(prev[:, :, :QBLK])], axis=2)
            return cur + prev.reshape(heads, n, HEAD)

        dk, dv = from_prev(dkc, dkp), from_prev(dvc, dvp)

        du, gr, gi, dyb5, glb, dzb, dar, dai, dd = s5_bwd(dso, s["ypre"], s["proj"], u0 // dss, s["xr"], s["xi"],
                                                          seq, *s["s5p"])
        small_g["s5_d"][l] = dd
        one = lambda k1, k2: (1, 1, k1, k2)
        slab["s5_w_glu"] = _cols_slabs(wgrad(glb, dzb, "wg_s5_glu", one(dss, 2 * dss), "col")[0])[0]
        dcm_r = wgrad(s["xr"], dyb5, "wg_s5_c", one(gp, dss), "col", dtype=F32)[0, 0]
        dcm_i = -wgrad(s["xi"], dyb5, "wg_s5_c", one(gp, dss), "col", dtype=F32)[0, 0]
        dbm_r = wgrad(s["proj"], gr, "wg_s5_b", one(dss, gp), "col", a_cols=(u0, dss), dtype=F32)[0, 0]
        dbm_i = wgrad(s["proj"], gi, "wg_s5_b", one(dss, gp), "col", a_cols=(u0, dss), dtype=F32)[0, 0]
        pg = s["prep_vjp"]((dar, dai, dbm_r, dbm_i, dcm_r, dcm_i))
        for nm, gval in zip(("s5_lambda_re", "s5_lambda_im", "s5_log_dt", "s5_b_re", "s5_b_im", "s5_c_re", "s5_c_im"), pg):
            small_g[nm][l] = gval

        dproj = jnp.concatenate([dlog, _unheads(dq).astype(BF16), _unheads(dk).astype(BF16),
                                 _unheads(dv).astype(BF16), dz.astype(BF16), du.astype(BF16)], axis=1)
        dx, dgm, hn = proj_bwd(dproj, dx, s["x1"], row(w["mix_norm"][l]), full["w_in"], l)
        small_g["mix_norm"][l] = dgm
        slab["w_in"] = _cols_slabs(to_ref_cols(wgrad(hn, dproj, "wg_in", one(d, 3 * d + mid + dss), "col")[0]))[0]

        keys, comm = exchange_of(mixer_names, l)
        outs, got = ffn_bwd(dx, s["x0"], row(w["ffn1_norm"][l]), s["a1"], s["b1"],
                            full["ffn1_w_up"], full["ffn1_w_down"], l, comm=comm)
        recv.update(zip(keys, got))
        dx, dg, hn, dyb, dab_a, dab_b, act = outs
        small_g["ffn1_norm"][l] = dg
        ffn_grads("ffn1", hn, dyb, dab_a, dab_b, act)
        pending = exchange_of(("ffn1_w_up", "ffn1_w_down"), l)

    small_flat = _small_pack({nm: jnp.stack([g.reshape(w[nm].shape[1:]) for g in small_g[nm]]) for nm in SMALL})
    *got, recv_small = grad_exchange(pending[1][1], small_flat)
    recv.update(zip(pending[0], got))

    outs = {}
    for nm in SHARDED:
        shp = w[nm].shape
        as3 = lambda t: t.reshape(shp[0], -1, shp[-1])
        bufs = None
        for l in reversed(range(depth)):
            bufs = sum_adamw(recv[(nm, l)], as3(w[nm]), as3(m[nm]), as3(v[nm]), l, bufs, "adamw_" + nm)
        outs[nm] = [b.reshape(shp) for b in bufs]
    packed = sum_adamw(recv_small, _small_pack(w)[None], _small_pack(m)[None], _small_pack(v)[None], 0, None,
                       "adamw_small")
    unpacked = [_small_unpack(p, w) for p in packed]
    for nm in SMALL:
        outs[nm] = [u[nm] for u in unpacked]
    return loss, dx.reshape(x.shape), outs


def kernel(x, ffn1_norm, ffn1_w_up, ffn1_w_down, mix_norm, w_in, b_gate, s5_lambda_re, s5_lambda_im, s5_log_dt, s5_b_re, s5_b_im, s5_c_re, s5_c_im, s5_d, s5_w_glu, w_br_s5, attn_q_gain, attn_k_gain, attn_rel_bias, w_br_attn, conv_w_dw, conv_b_dw, conv_ln_g, conv_ln_b, w_br_conv, w_out, ffn2_norm, ffn2_w_up, ffn2_w_down, loss_target, m_ffn1_norm, m_ffn1_w_up, m_ffn1_w_down, m_mix_norm, m_w_in, m_b_gate, m_s5_lambda_re, m_s5_lambda_im, m_s5_log_dt, m_s5_b_re, m_s5_b_im, m_s5_c_re, m_s5_c_im, m_s5_d, m_s5_w_glu, m_w_br_s5, m_attn_q_gain, m_attn_k_gain, m_attn_rel_bias, m_w_br_attn, m_conv_w_dw, m_conv_b_dw, m_conv_ln_g, m_conv_ln_b, m_w_br_conv, m_w_out, m_ffn2_norm, m_ffn2_w_up, m_ffn2_w_down, v_ffn1_norm, v_ffn1_w_up, v_ffn1_w_down, v_mix_norm, v_w_in, v_b_gate, v_s5_lambda_re, v_s5_lambda_im, v_s5_log_dt, v_s5_b_re, v_s5_b_im, v_s5_c_re, v_s5_c_im, v_s5_d, v_s5_w_glu, v_w_br_s5, v_attn_q_gain, v_attn_k_gain, v_attn_rel_bias, v_w_br_attn, v_conv_w_dw, v_conv_b_dw, v_conv_ln_g, v_conv_ln_b, v_w_br_conv, v_w_out, v_ffn2_norm, v_ffn2_w_up, v_ffn2_w_down):
    args = locals()
    w = {nm: args[nm] for nm in WEIGHTS}
    m = {nm: args["m_" + nm] for nm in WEIGHTS}
    v = {nm: args["v_" + nm] for nm in WEIGHTS}
    loss, gx, outs = _step(x, loss_target, w, m, v)
    return (loss, gx, *[outs[nm][0] for nm in WEIGHTS], *[outs[nm][1] for nm in WEIGHTS],
            *[outs[nm][2] for nm in WEIGHTS], *[outs[nm][3] for nm in WEIGHTS])
```

```python
import functools
import math

import numpy as np
import jax
import jax.numpy as jnp
from jax import lax
from jax.experimental import pallas as pl
from jax.experimental.pallas import tpu as pltpu

F32 = jnp.float32
BF16 = jnp.bfloat16

CHUNK = 64
N_LEFT = 8
QBLK = CHUNK * N_LEFT
HEAD = 64
MAX_REL = 128
S5_GROUP = 16
S5_STATE = 64
CONV_W = 31
HALO = 32
EPS = 1e-6
NEG = -1e30
ADAM_LR, ADAM_B1, ADAM_B2, ADAM_EPS, ADAM_WD, ADAM_STEP = 0.001, 0.9, 0.999, 1e-08, 0.01, 10
N_DEV = 8
VMEM_LIMIT = 56 * 1024 * 1024


def _call(body, **kw):
    return pl.pallas_call(body, **kw)


def _params(**kw):
    return pltpu.CompilerParams(vmem_limit_bytes=VMEM_LIMIT, **kw)


def _tile(n, cap, unit=128):
    if n <= cap:
        return n
    d = (cap // unit) * unit
    while d >= unit:
        if n % d == 0:
            return d
        d -= unit
    raise ValueError(f"no tile for {n} under {cap}")


def _dot(a, b):
    return jnp.dot(a, b, preferred_element_type=F32)


def _dot_nt(a, b):
    return lax.dot_general(a, b, (((1,), (1,)), ((), ())), preferred_element_type=F32)


def _dot_tn(a, b):
    return lax.dot_general(a, b, (((0,), (0,)), ((), ())), preferred_element_type=F32)


def _sig(x):
    return 1.0 / (1.0 + jnp.exp(-x))


def _rms_fwd(x, g):
    rs = lax.rsqrt(jnp.mean(x * x, axis=-1, keepdims=True) + EPS)
    xhat = x * rs
    return xhat * g, xhat, rs


def _rms_bwd(dh, xhat, rs, g):
    dxh = dh * g
    dx = rs * (dxh - xhat * jnp.mean(dxh * xhat, axis=-1, keepdims=True))
    return dx, dh * xhat


_GELU_C = math.sqrt(2.0 / math.pi)


def _gelu(x):
    return 0.5 * x * (1.0 + jnp.tanh(_GELU_C * (x + 0.044715 * x * x * x)))


def _gelu_grad(x):
    t = jnp.tanh(_GELU_C * (x + 0.044715 * x * x * x))
    return 0.5 * (1.0 + t) + 0.5 * x * (1.0 - t * t) * _GELU_C * (1.0 + 3.0 * 0.044715 * x * x)


def _acc_out(ref, val, first):
    @pl.when(first)
    def _():
        ref[...] = val

    @pl.when(jnp.logical_not(first))
    def _():
        ref[...] += val


FF_CHUNKS = N_DEV // 2


def ffn_fwd(x, g, w_up, w_down, layer, comm=None):
    n, d = x.shape
    nn = w_up.shape[3]
    tm = _tile(n, 512, 8)

    def body(x_ref, g_ref, wa_ref, wb_ref, wd_ref, xo_ref, a_ref, b_ref, hn_ref, acc_ref):
        j = pl.program_id(1)

        @pl.when(j == 0)
        def _():
            h, _, _ = _rms_fwd(x_ref[...], g_ref[...])
            hn_ref[...] = h.astype(BF16)
            acc_ref[...] = jnp.zeros_like(acc_ref)

        hn = hn_ref[...]
        a = _dot(hn, wa_ref[0, 0])
        b = _dot(hn, wb_ref[0, 0])
        a_ref[0] = a.astype(BF16)
        b_ref[0] = b.astype(BF16)
        act = a * _sig(a) * b
        acc_ref[...] += _dot(act.astype(BF16), wd_ref[0])

        @pl.when(j == FF_CHUNKS - 1)
        def _():
            xo_ref[...] = x_ref[...] + 0.5 * acc_ref[...]

    return _hosted_call(
        body, comm,
        name="ffn_fwd",
        grid=(n // tm, FF_CHUNKS),
        in_specs=[
            pl.BlockSpec((tm, d), lambda i, j: (i, 0)),
            pl.BlockSpec((1, d), lambda i, j: (0, 0)),
            pl.BlockSpec((1, 1, d, nn), lambda i, j: (j, layer, 0, 0)),
            pl.BlockSpec((1, 1, d, nn), lambda i, j: (j + FF_CHUNKS, layer, 0, 0)),
            pl.BlockSpec((1, nn, d), lambda i, j: (layer, j, 0)),
        ],
        out_specs=[
            pl.BlockSpec((tm, d), lambda i, j: (i, 0)),
            pl.BlockSpec((1, tm, nn), lambda i, j: (j, i, 0)),
            pl.BlockSpec((1, tm, nn), lambda i, j: (j, i, 0)),
        ],
        out_shape=[
            jax.ShapeDtypeStruct((n, d), F32),
            jax.ShapeDtypeStruct((FF_CHUNKS, n, nn), BF16),
            jax.ShapeDtypeStruct((FF_CHUNKS, n, nn), BF16),
        ],
        scratch_shapes=[pltpu.VMEM((tm, d), BF16), pltpu.VMEM((tm, d), F32)],
        args=(x, g, w_up, w_up, w_down),
    )


def ffn_bwd(dy, x, g, a, b, w_up, w_down, layer, comm=None):
    n, d = x.shape
    nn = w_up.shape[3]
    tm = _tile(n, 512, 8)

    def body(dy_ref, x_ref, g_ref, a_ref, b_ref, wa_ref, wb_ref, wd_ref,
             dx_ref, dg_ref, hn_ref, dyb_ref, da_ref, db_ref, act_ref, dyb_s, dh_ref):
        i, j = pl.program_id(0), pl.program_id(1)

        @pl.when(j == 0)
        def _():
            h, _, _ = _rms_fwd(x_ref[...], g_ref[...])
            hn_ref[...] = h.astype(BF16)
            dyb = (0.5 * dy_ref[...]).astype(BF16)
            dyb_ref[...] = dyb
            dyb_s[...] = dyb
            dh_ref[...] = jnp.zeros_like(dh_ref)

        dact = _dot_nt(dyb_s[...], wd_ref[0])
        a32 = a_ref[0].astype(F32)
        b32 = b_ref[0].astype(F32)
        s = _sig(a32)
        sil = a32 * s
        da = (dact * b32 * (s * (1.0 + a32 * (1.0 - s)))).astype(BF16)
        db = (dact * sil).astype(BF16)
        da_ref[0] = da
        db_ref[0] = db
        act_ref[0] = (sil * b32).astype(BF16)
        dh_ref[...] += _dot_nt(da, wa_ref[0, 0]) + _dot_nt(db, wb_ref[0, 0])

        @pl.when(j == FF_CHUNKS - 1)
        def _():
            gg = g_ref[...]
            _, xhat, rs = _rms_fwd(x_ref[...], gg)
            dxn, dgr = _rms_bwd(dh_ref[...], xhat, rs, gg)
            dx_ref[...] = dy_ref[...] + dxn
            _acc_out(dg_ref, jnp.sum(dgr, axis=0, keepdims=True), i == 0)

    tok = pl.BlockSpec((tm, d), lambda i, j: (i, 0))
    chunk = pl.BlockSpec((1, tm, nn), lambda i, j: (j, i, 0))
    vec = pl.BlockSpec((1, d), lambda i, j: (0, 0))
    chunks = jax.ShapeDtypeStruct((FF_CHUNKS, n, nn), BF16)
    return _hosted_call(
        body, comm,
        name="ffn_bwd",
        grid=(n // tm, FF_CHUNKS),
        in_specs=[
            tok, tok, vec, chunk, chunk,
            pl.BlockSpec((1, 1, d, nn), lambda i, j: (j, layer, 0, 0)),
            pl.BlockSpec((1, 1, d, nn), lambda i, j: (j + FF_CHUNKS, layer, 0, 0)),
            pl.BlockSpec((1, nn, d), lambda i, j: (layer, j, 0)),
        ],
        out_specs=[tok, vec, tok, tok, chunk, chunk, chunk],
        out_shape=[
            jax.ShapeDtypeStruct((n, d), F32),
            jax.ShapeDtypeStruct((1, d), F32),
            jax.ShapeDtypeStruct((n, d), BF16),
            jax.ShapeDtypeStruct((n, d), BF16),
            chunks, chunks, chunks,
        ],
        scratch_shapes=[pltpu.VMEM((tm, d), BF16), pltpu.VMEM((tm, d), F32)],
        args=(dy, x, g, a, b, w_up, w_up, w_down),
    )


def wgrad(a, b, name, out4, mode, *, g0=0, layer=0, prev=None, a_cols=None, dtype=BF16):
    a3 = a if a.ndim == 3 else a[None]
    b3 = b if b.ndim == 3 else b[None]
    sa, n, ka = a3.shape
    sb, _, kb = b3.shape
    a0 = 0
    if a_cols is not None:
        a0, ka = a_cols
    k1, k2 = sa * ka, sb * kb
    depth, groups, rr, cc = out4
    t1 = _tile(math.gcd(ka, rr), 1024)
    t2 = _tile(math.gcd(kb, cc), 1024)
    tn = _tile(n, 1024, 8)
    gpb = 1
    if mode == "col":
        assert rr == k1 and k2 % cc == 0 and g0 + k2 // cc <= groups
        if kb % cc == 0 and cc < kb <= 1024 and g0 % (kb // cc) == 0:
            t2, gpb = kb, kb // cc
        per = max(cc // t2, 1)
        oblock = (1, gpb, t1, min(t2, cc))
        omap = lambda i, j, k: (layer, (g0 + j // per) // gpb, i, j % per)
    else:
        assert cc == k2 and k1 % rr == 0 and g0 + k1 // rr <= groups
        if ka % rr == 0 and rr < ka <= 1024 and g0 % (ka // rr) == 0:
            t1, gpb = ka, ka // rr
        per = max(rr // t1, 1)
        oblock = (1, gpb, min(t1, rr), t2)
        omap = lambda i, j, k: (layer, (g0 + i // per) // gpb, i % per, j)
    na, nb = ka // t1, kb // t2
    nk = n // tn

    def body(a_ref, b_ref, *rest):
        o_ref, acc_ref = rest[-2], rest[-1]
        k = pl.program_id(2)

        @pl.when(k == 0)
        def _():
            acc_ref[...] = jnp.zeros_like(acc_ref)

        acc_ref[...] += _dot_tn(a_ref[0].astype(BF16), b_ref[0].astype(BF16))

        @pl.when(k == nk - 1)
        def _():
            for g in range(gpb):
                if gpb == 1:
                    o_ref[0, 0] = acc_ref[...].astype(dtype)
                elif mode == "col":
                    o_ref[0, g] = acc_ref[:, g * cc:(g + 1) * cc].astype(dtype)
                else:
                    o_ref[0, g] = acc_ref[g * rr:(g + 1) * rr, :].astype(dtype)

    in_specs = [
        pl.BlockSpec((1, tn, t1), lambda i, j, k: (i // na, k, a0 // t1 + i % na)),
        pl.BlockSpec((1, tn, t2), lambda i, j, k: (j // nb, k, j % nb)),
    ]
    args = [a3, b3]
    extra = {}
    if prev is not None:
        in_specs.append(pl.BlockSpec(memory_space=pl.ANY))
        args.append(prev)
        extra["input_output_aliases"] = {2: 0}
    return _call(
        body,
        name=name,
        grid=(k1 // t1, k2 // t2, nk),
        in_specs=in_specs,
        out_specs=pl.BlockSpec(oblock, omap),
        out_shape=jax.ShapeDtypeStruct(out4, dtype),
        scratch_shapes=[pltpu.VMEM((t1, t2), F32)],
        compiler_params=_params(),
        **extra,
    )(*args)


def proj_fwd(x, g, w, layer):
    n, d = x.shape
    c = w.shape[2]
    tm, tc = _tile(n, 512, 8), _tile(c, 768)

    def body(x_ref, g_ref, w_ref, o_ref, hn_ref):
        @pl.when(pl.program_id(1) == 0)
        def _():
            h, _, _ = _rms_fwd(x_ref[...], g_ref[...])
            hn_ref[...] = h.astype(BF16)

        o_ref[...] = _dot(hn_ref[...], w_ref[0])

    return _call(
        body,
        name="proj_fwd",
        grid=(n // tm, c // tc),
        in_specs=[
            pl.BlockSpec((tm, d), lambda i, j: (i, 0)),
            pl.BlockSpec((1, d), lambda i, j: (0, 0)),
            pl.BlockSpec((1, d, tc), lambda i, j: (layer, 0, j)),
        ],
        out_specs=pl.BlockSpec((tm, tc), lambda i, j: (i, j)),
        out_shape=jax.ShapeDtypeStruct((n, c), F32),
        scratch_shapes=[pltpu.VMEM((tm, d), BF16)],
        compiler_params=_params(),
    )(x, g, w)


def proj_bwd(dproj, dres, x, g, w, layer):
    n, d = x.shape
    c = w.shape[2]
    tm, tc = _tile(n, 512, 8), _tile(c, 768)
    nc = c // tc

    def body(dp_ref, dr_ref, x_ref, g_ref, w_ref, dx_ref, dg_ref, hn_ref, dh_ref):
        i, j = pl.program_id(0), pl.program_id(1)

        @pl.when(j == 0)
        def _():
            dh_ref[...] = jnp.zeros_like(dh_ref)

        dh_ref[...] += _dot_nt(dp_ref[...], w_ref[0])

        @pl.when(j == nc - 1)
        def _():
            gg = g_ref[...]
            h, xhat, rs = _rms_fwd(x_ref[...], gg)
            hn_ref[...] = h.astype(BF16)
            dxn, dgr = _rms_bwd(dh_ref[...], xhat, rs, gg)
            dx_ref[...] = dr_ref[...] + dxn
            _acc_out(dg_ref, jnp.sum(dgr, axis=0, keepdims=True), i == 0)

    return _call(
        body,
        name="proj_bwd",
        grid=(n // tm, nc),
        in_specs=[
            pl.BlockSpec((tm, tc), lambda i, j: (i, j)),
            pl.BlockSpec((tm, d), lambda i, j: (i, 0)),
            pl.BlockSpec((tm, d), lambda i, j: (i, 0)),
            pl.BlockSpec((1, d), lambda i, j: (0, 0)),
            pl.BlockSpec((1, d, tc), lambda i, j: (layer, 0, j)),
        ],
        out_specs=[
            pl.BlockSpec((tm, d), lambda i, j: (i, 0)),
            pl.BlockSpec((1, d), lambda i, j: (0, 0)),
            pl.BlockSpec((tm, d), lambda i, j: (i, 0)),
        ],
        out_shape=[
            jax.ShapeDtypeStruct((n, d), F32),
            jax.ShapeDtypeStruct((1, d), F32),
            jax.ShapeDtypeStruct((n, d), BF16),
        ],
        scratch_shapes=[pltpu.VMEM((tm, d), F32)],
        compiler_params=_params(),
    )(dproj, dres, x, g, w)


S5_TS = 512


def _cmul(ar, ai, br, bi):
    return ar * br - ai * bi, ar * bi + ai * br


def _s5_tables(ar, ai, reverse):
    gp = ar.shape[1]
    if reverse:
        ai = -ai
    a1r, a1i = jnp.broadcast_to(ar, (8, gp)), jnp.broadcast_to(ai, (8, gp))
    a2r, a2i = _cmul(a1r, a1i, a1r, a1i)
    a4r, a4i = _cmul(a2r, a2i, a2r, a2i)
    a8r, a8i = _cmul(a4r, a4i, a4r, a4i)
    row = lax.broadcasted_iota(jnp.int32, (8, gp), 0)
    e = (8 - row) if reverse else (row + 1)
    pr, pi = jnp.ones((8, gp), F32), jnp.zeros((8, gp), F32)
    for bit, (fr, fi) in ((1, (a1r, a1i)), (2, (a2r, a2i)), (4, (a4r, a4i)), (8, (a8r, a8i))):
        nr, ni = _cmul(pr, pi, fr, fi)
        on = (e & bit) != 0
        pr, pi = jnp.where(on, nr, pr), jnp.where(on, ni, pi)
    return (a1r, a1i, a2r, a2i, a4r, a4i, pr, pi)


def _s5_scan(xr_ref, xi_ref, tab_ref, cr_ref, ci_ref, ts, reverse):
    gp = xr_ref.shape[1]
    nt = ts // 8
    row = lax.broadcasted_iota(jnp.int32, (8, gp), 0)

    def shifted(v, s):
        if reverse:
            return jnp.where(row < 8 - s, pltpu.roll(v, 8 - s, 0), 0.0)
        return jnp.where(row >= s, pltpu.roll(v, s, 0), 0.0)

    def step(k, carry):
        cr, ci = carry
        t = (nt - 1 - k) if reverse else k
        r0 = pl.multiple_of(t * 8, 8)
        br = xr_ref[pl.ds(r0, 8), :]
        bi = xi_ref[pl.ds(r0, 8), :]
        for q, s in enumerate((1, 2, 4)):
            fr, fi = tab_ref[2 * q], tab_ref[2 * q + 1]
            sr, si = shifted(br, s), shifted(bi, s)
            mr, mi = _cmul(fr, fi, sr, si)
            br, bi = br + mr, bi + mi
        mr, mi = _cmul(tab_ref[6], tab_ref[7], cr, ci)
        br, bi = br + mr, bi + mi
        xr_ref[pl.ds(r0, 8), :] = br
        xi_ref[pl.ds(r0, 8), :] = bi
        edge = 0 if reverse else 7
        return (jnp.broadcast_to(br[edge:edge + 1, :], (8, gp)),
                jnp.broadcast_to(bi[edge:edge + 1, :], (8, gp)))

    cr, ci = lax.fori_loop(0, nt, step, (cr_ref[...], ci_ref[...]))
    cr_ref[...] = cr
    ci_ref[...] = ci


def s5_fwd(proj, ucol, seq, ar, ai, bm_r, bm_i, cm_r, cm_i, dskip, w_glu, layer):
    n = proj.shape[0]
    ds, gp = bm_r.shape
    ts = min(S5_TS, seq)
    nt = seq // ts

    def body(u_ref, ar_ref, ai_ref, bmr_ref, bmi_ref, cmr_ref, cmi_ref, d_ref, wg_ref,
             out_ref, xr_ref, xi_ref, yp_ref, tab_ref, cr_ref, ci_ref):
        @pl.when(pl.program_id(1) == 0)
        def _():
            for q, v in enumerate(_s5_tables(ar_ref[...], ai_ref[...], False)):
                tab_ref[q] = v
            cr_ref[...] = jnp.zeros_like(cr_ref)
            ci_ref[...] = jnp.zeros_like(ci_ref)

        u = u_ref[...]
        ub = u.astype(BF16)
        xr_ref[...] = _dot(ub, bmr_ref[...])
        xi_ref[...] = _dot(ub, bmi_ref[...])
        _s5_scan(xr_ref, xi_ref, tab_ref, cr_ref, ci_ref, ts, False)
        y = (_dot(xr_ref[...].astype(BF16), cmr_ref[...]) - _dot(xi_ref[...].astype(BF16), cmi_ref[...])
             + d_ref[...] * u)
        yp_ref[...] = y
        z = _dot(_gelu(y).astype(BF16), wg_ref[0])
        out_ref[...] = z[:, :ds] * _sig(z[:, ds:])

    full = lambda shape: pl.BlockSpec(shape, lambda b, t: (0, 0))
    return _call(
        body,
        name="s5_fwd",
        grid=(n // seq, nt),
        in_specs=[
            pl.BlockSpec((ts, ds), lambda b, t: (b * nt + t, ucol)),
            full((1, gp)), full((1, gp)), full((ds, gp)), full((ds, gp)), full((gp, ds)), full((gp, ds)),
            full((1, ds)), pl.BlockSpec((1, ds, 2 * ds), lambda b, t: (layer, 0, 0)),
        ],
        out_specs=[
            pl.BlockSpec((ts, ds), lambda b, t: (b * nt + t, 0)),
            pl.BlockSpec((ts, gp), lambda b, t: (b * nt + t, 0)),
            pl.BlockSpec((ts, gp), lambda b, t: (b * nt + t, 0)),
            pl.BlockSpec((ts, ds), lambda b, t: (b * nt + t, 0)),
        ],
        out_shape=[
            jax.ShapeDtypeStruct((n, ds), F32),
            jax.ShapeDtypeStruct((n, gp), F32),
            jax.ShapeDtypeStruct((n, gp), F32),
            jax.ShapeDtypeStruct((n, ds), F32),
        ],
        scratch_shapes=[pltpu.VMEM((8, 8, gp), F32), pltpu.VMEM((8, gp), F32), pltpu.VMEM((8, gp), F32)],
        compiler_params=_params(),
    )(proj, ar, ai, bm_r, bm_i, cm_r, cm_i, dskip, w_glu)


def s5_bwd(dout, ypre, proj, ucol, xr, xi, seq, ar, ai, bm_r, bm_i, cm_r, cm_i, dskip, w_glu, layer):
    n = proj.shape[0]
    ds, gp = bm_r.shape
    ts = min(S5_TS, seq)
    nt = seq // ts

    def body(do_ref, yp_ref, u_ref, xr_ref, xi_ref, hr_ref, hi_ref, ar_ref, ai_ref, bmr_ref, bmi_ref,
             cmr_ref, cmi_ref, d_ref, wg_ref,
             du_ref, gr_ref, gi_ref, dyb_ref, glb_ref, dzb_ref, dar_ref, dai_ref, dd_ref,
             tab_ref, cr_ref, ci_ref):
        b, t = pl.program_id(0), pl.program_id(1)
        first = jnp.logical_and(b == 0, t == 0)

        @pl.when(t == 0)
        def _():
            for q, v in enumerate(_s5_tables(ar_ref[...], ai_ref[...], True)):
                tab_ref[q] = v
            cr_ref[...] = jnp.zeros_like(cr_ref)
            ci_ref[...] = jnp.zeros_like(ci_ref)

        yp = yp_ref[...]
        u = u_ref[...]
        gl = _gelu(yp).astype(BF16)
        glb_ref[...] = gl
        z = _dot(gl, wg_ref[0])
        za, sg = z[:, :ds], _sig(z[:, ds:])
        do = do_ref[...]
        da = (do * sg).astype(BF16)
        dg = (do * za * sg * (1.0 - sg)).astype(BF16)
        dzb_ref[:, :ds] = da
        dzb_ref[:, ds:] = dg
        dgl = _dot_nt(da, wg_ref[0, :, :ds]) + _dot_nt(dg, wg_ref[0, :, ds:])
        dyp = dgl * _gelu_grad(yp)
        dypb = dyp.astype(BF16)
        dyb_ref[...] = dypb
        _acc_out(dd_ref, jnp.sum(dyp * u, axis=0, keepdims=True), first)

        gr_ref[...] = _dot_nt(dypb, cmr_ref[...])
        gi_ref[...] = -_dot_nt(dypb, cmi_ref[...])
        _s5_scan(gr_ref, gi_ref, tab_ref, cr_ref, ci_ref, ts, True)
        gr, gi = gr_ref[...], gi_ref[...]
        du_ref[...] = d_ref[...] * dyp + _dot_nt(gr.astype(BF16), bmr_ref[...]) + _dot_nt(gi.astype(BF16), bmi_ref[...])

        row = lax.broadcasted_iota(jnp.int32, (ts, gp), 0)
        live = jnp.where(t == nt - 1, 0.0, 1.0)
        pr = jnp.broadcast_to(hr_ref[7:8, :] * live, (ts, gp))
        pi = jnp.broadcast_to(hi_ref[7:8, :] * live, (ts, gp))
        sr = jnp.where(row == 0, pr, pltpu.roll(xr_ref[...], 1, 0))
        si = jnp.where(row == 0, pi, pltpu.roll(xi_ref[...], 1, 0))
        _acc_out(dar_ref, jnp.sum(gr * sr + gi * si, axis=0, keepdims=True), first)
        _acc_out(dai_ref, jnp.sum(gi * sr - gr * si, axis=0, keepdims=True), first)

    full = lambda shape: pl.BlockSpec(shape, lambda b, t: (0, 0))
    blk = lambda w, col=0: pl.BlockSpec((ts, w), lambda b, t: (b * nt + nt - 1 - t, col))
    halo = pl.BlockSpec((8, gp), lambda b, t: (jnp.maximum((b * seq + (nt - 1 - t) * ts) // 8 - 1, 0), 0))
    return _call(
        body,
        name="s5_bwd",
        grid=(n // seq, nt),
        in_specs=[
            blk(ds), blk(ds), blk(ds, ucol), blk(gp), blk(gp), halo, halo,
            full((1, gp)), full((1, gp)), full((ds, gp)), full((ds, gp)), full((gp, ds)), full((gp, ds)),
            full((1, ds)), pl.BlockSpec((1, ds, 2 * ds), lambda b, t: (layer, 0, 0)),
        ],
        out_specs=[
            blk(ds), blk(gp), blk(gp), blk(ds), blk(ds), blk(2 * ds),
            full((1, gp)), full((1, gp)), full((1, ds)),
        ],
        out_shape=[
            jax.ShapeDtypeStruct((n, ds), F32),
            jax.ShapeDtypeStruct((n, gp), F32),
            jax.ShapeDtypeStruct((n, gp), F32),
            jax.ShapeDtypeStruct((n, ds), BF16),
            jax.ShapeDtypeStruct((n, ds), BF16),
            jax.ShapeDtypeStruct((n, 2 * ds), BF16),
            jax.ShapeDtypeStruct((1, gp), F32),
            jax.ShapeDtypeStruct((1, gp), F32),
            jax.ShapeDtypeStruct((1, ds), F32),
        ],
        scratch_shapes=[pltpu.VMEM((8, 8, gp), F32), pltpu.VMEM((8, gp), F32), pltpu.VMEM((8, gp), F32)],
        compiler_params=_params(),
    )(dout, ypre, proj, xr, xi, xr, xi, ar, ai, bm_r, bm_i, cm_r, cm_i, dskip, w_glu)


BAND = QBLK + CHUNK
NCH = QBLK // CHUNK


def _attn_specs(nq):
    cur = pl.BlockSpec((1, QBLK, HEAD), lambda h, b, i: (h, b * nq + i, 0))
    prev = pl.BlockSpec((1, QBLK, HEAD), lambda h, b, i: (h, b * nq + jnp.maximum(i - 1, 0), 0))
    vec = pl.BlockSpec((1, 1, 2 * QBLK), lambda h, b, i: (h, 0, 0))
    gain = pl.BlockSpec((1, HEAD), lambda h, b, i: (0, 0))
    return cur, prev, vec, gain


def _attn_bias_rows(tv, bias_ref):
    w = 2 * QBLK
    for qi in range(CHUNK):
        bias_ref[qi:qi + 1, :] = pltpu.roll(tv, (qi - (CHUNK - 1)) % w, 1)


def _attn_bias_reduce(dbias_ref):
    w = 2 * QBLK
    out = jnp.zeros((1, w), F32)
    for qi in range(CHUNK):
        out = out + pltpu.roll(dbias_ref[qi:qi + 1, :], ((CHUNK - 1) - qi) % w, 1)
    return out


def _attn_chunk_probs(qc, kb, bias, c, first_block):
    s = _dot_nt(qc, kb) * (HEAD ** -0.5) + bias
    col = lax.broadcasted_iota(jnp.int32, s.shape, 1)
    s = jnp.where(jnp.logical_and(first_block, col < QBLK - CHUNK * c), NEG, s)
    e = jnp.exp(s - jnp.max(s, axis=-1, keepdims=True))
    return e * (1.0 / jnp.sum(e, axis=-1, keepdims=True))


def attn_fwd(q, k, v, tv, gq, gk, seq, comm=None):
    h, n, _ = q.shape
    nq = seq // QBLK
    cur, prev, vec, gain = _attn_specs(nq)

    def body(q_ref, kp_ref, kc_ref, vp_ref, vc_ref, tv_ref, gq_ref, gk_ref, o_ref, bias_ref):
        @pl.when(jnp.logical_and(pl.program_id(1) == 0, pl.program_id(2) == 0))
        def _():
            _attn_bias_rows(tv_ref[0], bias_ref)

        first_block = pl.program_id(2) == 0
        qn, _, _ = _rms_fwd(q_ref[0], gq_ref[...])
        kn, _, _ = _rms_fwd(jnp.concatenate([kp_ref[0], kc_ref[0]], axis=0), gk_ref[...])
        qnb, knb = qn.astype(BF16), kn.astype(BF16)
        vb = jnp.concatenate([vp_ref[0], vc_ref[0]], axis=0).astype(BF16)
        bias = bias_ref[:, :BAND]
        for c in range(NCH):
            lo = CHUNK * c
            p = _attn_chunk_probs(qnb[lo:lo + CHUNK], knb[lo:lo + BAND], bias, c, first_block)
            o_ref[0, lo:lo + CHUNK, :] = _dot(p.astype(BF16), vb[lo:lo + BAND])

    return _hosted_call(
        body, comm,
        name="attn_fwd",
        grid=(h, n // seq, nq),
        in_specs=[cur, prev, cur, prev, cur, vec, gain, gain],
        out_specs=[cur],
        out_shape=[jax.ShapeDtypeStruct((h, n, HEAD), F32)],
        scratch_shapes=[pltpu.VMEM((CHUNK, 2 * QBLK), F32)],
        args=(q, k, k, v, v, tv, gq, gk),
    )


def attn_bwd(do, q, k, v, tv, gq, gk, seq, comm=None):
    h, n, _ = q.shape
    nb = n // seq
    nq = seq // QBLK
    cur, prev, vec, gain = _attn_specs(nq)

    def body(do_ref, q_ref, kp_ref, kc_ref, vp_ref, vc_ref, tv_ref, gq_ref, gk_ref,
             dq_ref, dkp_ref, dkc_ref, dvp_ref, dvc_ref, dtv_ref, dgq_ref, dgk_ref,
             bias_ref, dbias_ref, dqn_ref, dkn_ref, dv_ref):
        hh, b, i = pl.program_id(0), pl.program_id(1), pl.program_id(2)
        head_start = jnp.logical_and(b == 0, i == 0)

        @pl.when(head_start)
        def _():
            _attn_bias_rows(tv_ref[0], bias_ref)
            dbias_ref[...] = jnp.zeros_like(dbias_ref)

        gq_, gk_ = gq_ref[...], gk_ref[...]
        qn, qhat, qrs = _rms_fwd(q_ref[0], gq_)
        kn, khat, krs = _rms_fwd(jnp.concatenate([kp_ref[0], kc_ref[0]], axis=0), gk_)
        qnb, knb = qn.astype(BF16), kn.astype(BF16)
        vb = jnp.concatenate([vp_ref[0], vc_ref[0]], axis=0).astype(BF16)
        dob = do_ref[0].astype(BF16)
        bias = bias_ref[:, :BAND]
        dkn_ref[...] = jnp.zeros_like(dkn_ref)
        dv_ref[...] = jnp.zeros_like(dv_ref)
        for c in range(NCH):
            lo = CHUNK * c
            qc, kb, doc = qnb[lo:lo + CHUNK], knb[lo:lo + BAND], dob[lo:lo + CHUNK]
            p = _attn_chunk_probs(qc, kb, bias, c, i == 0)
            dv_ref[lo:lo + BAND, :] += _dot_tn(p.astype(BF16), doc)
            dp = _dot_nt(doc, vb[lo:lo + BAND])
            ds = p * (dp - jnp.sum(p * dp, axis=-1, keepdims=True))
            dbias_ref[:, :BAND] += ds
            dsb = (ds * (HEAD ** -0.5)).astype(BF16)
            dqn_ref[lo:lo + CHUNK, :] = _dot(dsb, kb)
            dkn_ref[lo:lo + BAND, :] += _dot_tn(dsb, qc)
        dq, dgq_rows = _rms_bwd(dqn_ref[...], qhat, qrs, gq_)
        dk, dgk_rows = _rms_bwd(dkn_ref[...], khat, krs, gk_)
        dq_ref[0] = dq
        dkp_ref[0] = dk[:QBLK]
        dkc_ref[0] = dk[QBLK:]
        dvp_ref[0] = dv_ref[:QBLK, :]
        dvc_ref[0] = dv_ref[QBLK:, :]
        first = jnp.logical_and(hh == 0, head_start)
        _acc_out(dgq_ref, jnp.sum(dgq_rows, axis=0, keepdims=True), first)
        _acc_out(dgk_ref, jnp.sum(dgk_rows, axis=0, keepdims=True), first)

        @pl.when(jnp.logical_and(b == nb - 1, i == nq - 1))
        def _():
            dtv_ref[0] = _attn_bias_reduce(dbias_ref)

    hm = jax.ShapeDtypeStruct((h, n, HEAD), F32)
    return _hosted_call(
        body, comm,
        name="attn_bwd",
        grid=(h, nb, nq),
        in_specs=[cur, cur, prev, cur, prev, cur, vec, gain, gain],
        out_specs=[cur, cur, cur, cur, cur, vec, gain, gain],
        out_shape=[hm, hm, hm, hm, hm, jax.ShapeDtypeStruct(tv.shape, F32),
                   jax.ShapeDtypeStruct((1, HEAD), F32), jax.ShapeDtypeStruct((1, HEAD), F32)],
        scratch_shapes=[pltpu.VMEM((CHUNK, 2 * QBLK), F32), pltpu.VMEM((CHUNK, 2 * QBLK), F32),
                        pltpu.VMEM((QBLK, HEAD), F32), pltpu.VMEM((2 * QBLK, HEAD), F32),
                        pltpu.VMEM((2 * QBLK, HEAD), F32)],
        args=(do, q, k, k, v, v, tv, gq, gk),
    )


CONV_TC = 512


def _ln_fwd(h1, g, b):
    mu = jnp.mean(h1, axis=-1, keepdims=True)
    xc = h1 - mu
    rs = lax.rsqrt(jnp.mean(xc * xc, axis=-1, keepdims=True) + EPS)
    yhat = xc * rs
    return yhat * g + b, yhat, rs


def _glu(z, dc):
    return z[:, :dc] * _sig(z[:, dc:])


def conv_fwd(proj, zcol, seq, w, bdw, lng, lnb):
    n = proj.shape[0]
    dc = w.shape[1]
    tc = min(CONV_TC, seq)
    nt = seq // tc

    def body(z_ref, zp_ref, w_ref, b_ref, g_ref, lb_ref, h1_ref, o_ref, ext_ref):
        live = jnp.where(pl.program_id(1) == 0, 0.0, 1.0)
        ext_ref[pl.ds(0, HALO), :] = _glu(zp_ref[...], dc) * live
        ext_ref[pl.ds(HALO, tc), :] = _glu(z_ref[...], dc)
        acc = jnp.zeros((tc, dc), F32) + b_ref[...]
        for j in range(CONV_W):
            acc = acc + w_ref[j:j + 1, :] * ext_ref[pl.ds(HALO - (CONV_W - 1) + j, tc), :]
        h1_ref[...] = acc
        ln, _, _ = _ln_fwd(acc, g_ref[...], lb_ref[...])
        o_ref[...] = ln * _sig(ln)

    full = lambda shape: pl.BlockSpec(shape, lambda b, t: (0, 0))
    return _call(
        body,
        name="conv_fwd",
        grid=(n // seq, nt),
        in_specs=[
            pl.BlockSpec((tc, 2 * dc), lambda b, t: (b * nt + t, zcol)),
            pl.BlockSpec((HALO, 2 * dc), lambda b, t: (jnp.maximum((b * seq + t * tc) // HALO - 1, 0), zcol)),
            full((HALO, dc)), full((1, dc)), full((1, dc)), full((1, dc)),
        ],
        out_specs=[
            pl.BlockSpec((tc, dc), lambda b, t: (b * nt + t, 0)),
            pl.BlockSpec((tc, dc), lambda b, t: (b * nt + t, 0)),
        ],
        out_shape=[jax.ShapeDtypeStruct((n, dc), F32), jax.ShapeDtypeStruct((n, dc), F32)],
        scratch_shapes=[pltpu.VMEM((tc + HALO, dc), F32)],
        compiler_params=_params(),
    )(proj, proj, w, bdw, lng, lnb)


def conv_bwd(dco, h1, proj, zcol, seq, w, bdw, lng, lnb):
    n = proj.shape[0]
    dc = w.shape[1]
    tc = min(CONV_TC, seq)
    nt = seq // tc
    nrow = n // HALO

    def body(do_ref, don_ref, h1_ref, h1n_ref, z_ref, zp_ref, w_ref, g_ref, lb_ref,
             dz_ref, dw_ref, db_ref, dg_ref, dlb_ref, ext_ref, dext_ref):
        b, t = pl.program_id(0), pl.program_id(1)
        first = jnp.logical_and(b == 0, t == 0)
        g, lb = g_ref[...], lb_ref[...]

        def dh1_of(do, h1):
            ln, yhat, rs = _ln_fwd(h1, g, lb)
            s = _sig(ln)
            dln = do * (s * (1.0 + ln * (1.0 - s)))
            dyh = dln * g
            dh1 = rs * (dyh - jnp.mean(dyh, axis=-1, keepdims=True)
                        - yhat * jnp.mean(dyh * yhat, axis=-1, keepdims=True))
            return dh1, dln, yhat

        dh1, dln, yhat = dh1_of(do_ref[...], h1_ref[...])
        dh1n, _, _ = dh1_of(don_ref[...], h1n_ref[...])
        _acc_out(dg_ref, jnp.sum(dln * yhat, axis=0, keepdims=True), first)
        _acc_out(dlb_ref, jnp.sum(dln, axis=0, keepdims=True), first)
        _acc_out(db_ref, jnp.sum(dh1, axis=0, keepdims=True), first)

        dext_ref[pl.ds(0, tc), :] = dh1
        dext_ref[pl.ds(tc, HALO), :] = dh1n * jnp.where(t == nt - 1, 0.0, 1.0)
        z = z_ref[...]
        ext_ref[pl.ds(0, HALO), :] = _glu(zp_ref[...], dc) * jnp.where(t == 0, 0.0, 1.0)
        ext_ref[pl.ds(HALO, tc), :] = _glu(z, dc)

        @pl.when(first)
        def _():
            dw_ref[...] = jnp.zeros_like(dw_ref)

        dh0 = jnp.zeros((tc, dc), F32)
        for j in range(CONV_W):
            dh0 = dh0 + w_ref[j:j + 1, :] * dext_ref[pl.ds(CONV_W - 1 - j, tc), :]
            dw_ref[j:j + 1, :] += jnp.sum(dh1 * ext_ref[pl.ds(HALO - (CONV_W - 1) + j, tc), :],
                                          axis=0, keepdims=True)
        za, sg = z[:, :dc], _sig(z[:, dc:])
        dz_ref[:, :dc] = dh0 * sg
        dz_ref[:, dc:] = dh0 * za * sg * (1.0 - sg)

    full = lambda shape: pl.BlockSpec(shape, lambda b, t: (0, 0))
    cur = lambda wd, col=0: pl.BlockSpec((tc, wd), lambda b, t: (b * nt + t, col))
    nxt = pl.BlockSpec((HALO, dc), lambda b, t: (jnp.minimum((b * seq + (t + 1) * tc) // HALO, nrow - 1), 0))
    return _call(
        body,
        name="conv_bwd",
        grid=(n // seq, nt),
        in_specs=[
            cur(dc), nxt, cur(dc), nxt, cur(2 * dc, zcol),
            pl.BlockSpec((HALO, 2 * dc), lambda b, t: (jnp.maximum((b * seq + t * tc) // HALO - 1, 0), zcol)),
            full((HALO, dc)), full((1, dc)), full((1, dc)),
        ],
        out_specs=[cur(2 * dc), full((HALO, dc)), full((1, dc)), full((1, dc)), full((1, dc))],
        out_shape=[
            jax.ShapeDtypeStruct((n, 2 * dc), F32),
            jax.ShapeDtypeStruct((HALO, dc), F32),
            jax.ShapeDtypeStruct((1, dc), F32),
            jax.ShapeDtypeStruct((1, dc), F32),
            jax.ShapeDtypeStruct((1, dc), F32),
        ],
        scratch_shapes=[pltpu.VMEM((tc + HALO, dc), F32), pltpu.VMEM((tc + HALO, dc), F32)],
        compiler_params=_params(),
    )(dco, dco, h1, h1, proj, proj, w, lng, lnb)


def _merge_common(l0, l1, l2, bg, so, ao, co, wbs, wba, wbc, d):
    ys = _dot(so.astype(BF16), wbs)
    ya = _dot(ao.astype(BF16), wba)
    yc = _dot(co.astype(BF16), wbc)
    gs = _sig(l0 + bg[:, :d])
    ga = _sig(l1 + bg[:, d:2 * d])
    gc = _sig(l2 + bg[:, 2 * d:])
    return (ys, ya, yc), (gs, ga, gc)


def _merge_specs(tm, d, dss, da, dc, layer):
    row = lambda w, col=0: pl.BlockSpec((tm, w), lambda i: (i, col))
    full = lambda r, c: pl.BlockSpec((r, c), lambda i: (0, 0))
    stacked = lambda r: pl.BlockSpec((1, r, d), lambda i: (layer, 0, 0))
    acts = [row(d, 0), row(d, 1), row(d, 2), full(1, 3 * d), row(dss), row(da), row(dc)]
    weights = [stacked(dss), stacked(da), stacked(dc), stacked(d)]
    return row, full, acts, weights


def merge_fwd(x, proj, bg, so, ao, co, wbs, wba, wbc, wout, layer):
    n, d = x.shape
    tm = _tile(n, 256, 8)
    row, full, acts, weights = _merge_specs(tm, d, so.shape[1], ao.shape[1], co.shape[1], layer)

    def body(x_ref, l0_ref, l1_ref, l2_ref, bg_ref, so_ref, ao_ref, co_ref,
             wbs_ref, wba_ref, wbc_ref, wo_ref, o_ref):
        (ys, ya, yc), (gs, ga, gc) = _merge_common(
            l0_ref[...], l1_ref[...], l2_ref[...], bg_ref[...], so_ref[...], ao_ref[...], co_ref[...],
            wbs_ref[0], wba_ref[0], wbc_ref[0], d)
        merged = gs * ys + ga * ya + gc * yc
        o_ref[...] = x_ref[...] + _dot(merged.astype(BF16), wo_ref[0])

    return _call(
        body,
        name="merge_fwd",
        grid=(n // tm,),
        in_specs=[row(d)] + acts + weights,
        out_specs=row(d),
        out_shape=jax.ShapeDtypeStruct((n, d), F32),
        compiler_params=_params(),
    )(x, proj, proj, proj, bg, so, ao, co, wbs, wba, wbc, wout)


def merge_bwd(dx, proj, bg, so, ao, co, wbs, wba, wbc, wout, layer):
    n, d = dx.shape
    dss, da, dc = so.shape[1], ao.shape[1], co.shape[1]
    tm = _tile(n, 256, 8)
    row, full, acts, weights = _merge_specs(tm, d, dss, da, dc, layer)

    def body(dx_ref, l0_ref, l1_ref, l2_ref, bg_ref, so_ref, ao_ref, co_ref, wbs_ref, wba_ref, wbc_ref, wo_ref,
             dl_ref, dso_ref, dao_ref, dco_ref, dbg_ref, mg_ref, dxb_ref, dys_ref, dya_ref, dyc_ref):
        wbs, wba, wbc = wbs_ref[0], wba_ref[0], wbc_ref[0]
        (ys, ya, yc), (gs, ga, gc) = _merge_common(
            l0_ref[...], l1_ref[...], l2_ref[...], bg_ref[...], so_ref[...], ao_ref[...], co_ref[...],
            wbs, wba, wbc, d)
        mg_ref[...] = (gs * ys + ga * ya + gc * yc).astype(BF16)
        dxb = dx_ref[...].astype(BF16)
        dxb_ref[...] = dxb
        dm = _dot_nt(dxb, wo_ref[0])
        first = pl.program_id(0) == 0
        for k, (y, g, w, dy_ref, db_ref) in enumerate((
                (ys, gs, wbs, dys_ref, dso_ref), (ya, ga, wba, dya_ref, dao_ref), (yc, gc, wbc, dyc_ref, dco_ref))):
            dl = dm * y * g * (1.0 - g)
            dl_ref[:, k * d:(k + 1) * d] = dl.astype(BF16)
            _acc_out(dbg_ref.at[:, k * d:(k + 1) * d], jnp.sum(dl, axis=0, keepdims=True), first)
            dy = (dm * g).astype(BF16)
            dy_ref[...] = dy
            db_ref[...] = _dot_nt(dy, w)

    bf = lambda w: jax.ShapeDtypeStruct((n, w), BF16)
    return _call(
        body,
        name="merge_bwd",
        grid=(n // tm,),
        in_specs=[row(d)] + acts + weights,
        out_specs=[row(3 * d), row(dss), row(da), row(dc), full(1, 3 * d),
                   row(d), row(d), row(d), row(d), row(d)],
        out_shape=[bf(3 * d), jax.ShapeDtypeStruct((n, dss), F32), jax.ShapeDtypeStruct((n, da), F32),
                   jax.ShapeDtypeStruct((n, dc), F32), jax.ShapeDtypeStruct((1, 3 * d), F32),
                   bf(d), bf(d), bf(d), bf(d), bf(d)],
        compiler_params=_params(),
    )(dx, proj, proj, proj, bg, so, ao, co, wbs, wba, wbc, wout)


def loss_head(y, target):
    n, d = y.shape
    tm = _tile(n, 512, 8)

    def body(y_ref, t_ref, dy_ref, l_ref):
        e = y_ref[...] - t_ref[...]
        dy_ref[...] = e * (1.0 / d)
        part = 0.5 * jnp.sum(jnp.sum(e * e, axis=-1, keepdims=True) * (1.0 / d), axis=0, keepdims=True)
        _acc_out(l_ref, part, pl.program_id(0) == 0)

    return _call(
        body,
        name="loss_head",
        grid=(n // tm,),
        in_specs=[pl.BlockSpec((tm, d), lambda i: (i, 0)), pl.BlockSpec((tm, d), lambda i: (i, 0))],
        out_specs=[pl.BlockSpec((tm, d), lambda i: (i, 0)), pl.BlockSpec((1, 1), lambda i: (0, 0))],
        out_shape=[jax.ShapeDtypeStruct((n, d), F32), jax.ShapeDtypeStruct((1, 1), F32)],
        compiler_params=_params(),
    )(y, target)


def _mesh_pos():
    return lax.axis_index("x"), lax.axis_index("y"), lax.axis_index("c")


ANY = pl.BlockSpec(memory_space=pl.ANY)


def _comm_sems(na):
    return [pltpu.SemaphoreType.DMA((na, 7)), pltpu.SemaphoreType.DMA((na, 7)), pltpu.SemaphoreType.DMA((na,))]


def _gather_plan(x_refs, out_refs, sems):
    na = len(x_refs)
    send_sems, recv_sems, local_sems = sems
    x, y, c = _mesh_pos()
    me, sibling = (x, y, c), (x, y, 1 - c)
    chips = [(1 - x, y), (x, 1 - y), (1 - x, 1 - y)]

    def slot(a, px, py, pc):
        return out_refs[a].at[4 * px + 2 * py + pc]

    def copy(a, k, block, to, src=None):
        return pltpu.make_async_remote_copy(
            src_ref=slot(a, *block) if src is None else src, dst_ref=slot(a, *block),
            send_sem=send_sems.at[a, k], recv_sem=recv_sems.at[a, k],
            device_id=to, device_id_type=pl.DeviceIdType.MESH)

    mine = [pltpu.make_async_copy(x_refs[a], slot(a, *me), local_sems.at[a]) for a in range(na)]
    first = []
    for a in range(na):
        first.append(copy(a, 0, me, sibling, src=x_refs[a]))
        first += [copy(a, 1 + j, me, (*chip, c), src=x_refs[a]) for j, chip in enumerate(chips)]

    def start():
        for cp in mine + first:
            cp.start()

    def finish():
        passed = []
        for j, chip in enumerate(chips):
            for a in range(na):
                copy(a, 1 + j, (*chip, c), me).wait_recv()
                fwd = copy(a, 4 + j, (*chip, c), sibling)
                fwd.start()
                passed.append(fwd)
        for a in range(na):
            copy(a, 0, sibling, me).wait_recv()
            for j, chip in enumerate(chips):
                copy(a, 4 + j, (*chip, 1 - c), me).wait_recv()
        for cp in first + passed:
            cp.wait_send()
        for cp in mine:
            cp.wait()

    return start, finish


def _gather_out(shards):
    return [jax.ShapeDtypeStruct((N_DEV,) + s.shape, s.dtype) for s in shards]


def all_gather(shards):
    na = len(shards)

    def body(*refs):
        start, finish = _gather_plan(refs[:na], refs[na:2 * na], refs[2 * na:])
        start()
        finish()

    return _call(
        body,
        name="all_gather",
        out_shape=_gather_out(shards),
        in_specs=[ANY] * na,
        out_specs=[ANY] * na,
        scratch_shapes=_comm_sems(na),
    )(*shards)


def _hosted_call(body, comm, *, name, grid, in_specs, out_specs, out_shape, scratch_shapes, args):
    if comm is None:
        res = _call(body, name=name, grid=grid, in_specs=in_specs, out_specs=out_specs, out_shape=out_shape,
                    scratch_shapes=scratch_shapes, compiler_params=_params())(*args)
        return res, []
    plan, arrays, c_out = comm
    n_in, n_out, n_scr, ci, co = len(in_specs), len(out_specs), len(scratch_shapes), len(arrays), len(c_out)

    def hosted(*refs):
        ins, cins = refs[:n_in], refs[n_in:n_in + ci]
        o0 = n_in + ci
        outs, couts = refs[o0:o0 + n_out], refs[o0 + n_out:o0 + n_out + co]
        s0 = o0 + n_out + co
        scr, sems = refs[s0:s0 + n_scr], refs[s0 + n_scr:]
        ids = [pl.program_id(ax) for ax in range(len(grid))]
        first = functools.reduce(jnp.logical_and, [i == 0 for i in ids])
        last = functools.reduce(jnp.logical_and, [i == g - 1 for i, g in zip(ids, grid)])
        start, finish = plan(cins, couts, sems)
        pl.when(first)(start)
        body(*ins, *outs, *scr)
        pl.when(last)(finish)

    res = _call(hosted, name=name + "_comm", grid=grid, in_specs=list(in_specs) + [ANY] * ci,
                out_specs=list(out_specs) + [ANY] * co, out_shape=list(out_shape) + list(c_out),
                scratch_shapes=list(scratch_shapes) + _comm_sems(max(ci, co)),
                compiler_params=_params())(*args, *arrays)
    return res[:n_out], res[n_out:]


def _exchange_plan(bcast=()):
    def plan(s_refs, r_refs, sems):
        na = len(s_refs)
        send_sems, recv_sems, local_sems = sems
        x, y, c = _mesh_pos()
        me = 4 * x + 2 * y + c

        def peer(k):
            px = (1 - x) if k & 4 else x
            py = (1 - y) if k & 2 else y
            pc = (1 - c) if k & 1 else c
            return (px, py, pc), 4 * px + 2 * py + pc

        def src_dst(a, pid, slot):
            return (s_refs[a] if a in bcast else s_refs[a].at[pid]), r_refs[a].at[slot]

        def copy(a, k):
            to, pid = peer(k)
            src, dst = src_dst(a, pid, me)
            return pltpu.make_async_remote_copy(
                src_ref=src, dst_ref=dst, send_sem=send_sems.at[a, k - 1], recv_sem=recv_sems.at[a, k - 1],
                device_id=to, device_id_type=pl.DeviceIdType.MESH)

        def arrival(a, k):
            _, pid = peer(k)
            src, dst = src_dst(a, pid, pid)
            return pltpu.make_async_remote_copy(
                src_ref=src, dst_ref=dst, send_sem=send_sems.at[a, k - 1], recv_sem=recv_sems.at[a, k - 1],
                device_id=(x, y, c), device_id_type=pl.DeviceIdType.MESH)

        mine = [pltpu.make_async_copy(*src_dst(a, me, me), local_sems.at[a]) for a in range(na)]
        sends = [copy(a, k) for k in range(1, N_DEV) for a in range(na)]

        def start():
            for cp in mine + sends:
                cp.start()

        def finish():
            for k in range(1, N_DEV):
                for a in range(na):
                    arrival(a, k).wait_recv()
            for cp in sends:
                cp.wait_send()
            for cp in mine:
                cp.wait()

        return start, finish

    return plan


def _exchange_out(slabs, bcast=()):
    return [jax.ShapeDtypeStruct(((N_DEV,) + s.shape) if a in bcast else s.shape, s.dtype)
            for a, s in enumerate(slabs)]


def grad_exchange(slabs, small):
    arrays = list(slabs) + [small]
    na = len(arrays)
    bcast = (na - 1,)

    def body(*refs):
        start, finish = _exchange_plan(bcast)(refs[:na], refs[na:2 * na], refs[2 * na:])
        start()
        finish()

    return _call(
        body,
        name="grad_exchange",
        out_shape=_exchange_out(arrays, bcast),
        in_specs=[ANY] * na,
        out_specs=[ANY] * na,
        scratch_shapes=_comm_sems(na),
    )(*arrays)


ADAM_BLOCK = 256 * 1024


def sum_adamw(recv, w, m, v, layer, prev, name):
    depth, rows, cols = w.shape
    tr = _tile(rows, max(8, ADAM_BLOCK // cols // 8 * 8), 8)
    c1 = 1.0 / (1.0 - ADAM_B1 ** ADAM_STEP)
    c2 = 1.0 / (1.0 - ADAM_B2 ** ADAM_STEP)

    def body(r_ref, w_ref, m_ref, v_ref, *rest):
        g_ref, d_ref, mo_ref, vo_ref = rest[-4:]
        g = r_ref[0].astype(F32)
        for s in range(1, N_DEV):
            g = g + r_ref[s].astype(F32)
        mn = ADAM_B1 * m_ref[0] + (1.0 - ADAM_B1) * g
        vn = ADAM_B2 * v_ref[0] + (1.0 - ADAM_B2) * (g * g)
        g_ref[0] = g
        mo_ref[0] = mn
        vo_ref[0] = vn
        d_ref[0] = -ADAM_LR * ((mn * c1) / (jnp.sqrt(vn * c2) + ADAM_EPS) + ADAM_WD * w_ref[0])

    blk = pl.BlockSpec((1, tr, cols), lambda i: (layer, i, 0))
    out = jax.ShapeDtypeStruct((depth, rows, cols), F32)
    in_specs = [pl.BlockSpec((N_DEV, tr, cols), lambda i: (0, i, 0)), blk, blk, blk]
    args = [recv, w, m, v]
    extra = {}
    if prev is not None:
        in_specs += [ANY] * 4
        args += list(prev)
        extra["input_output_aliases"] = {4 + j: j for j in range(4)}
    return _call(
        body,
        name=name,
        grid=(rows // tr,),
        in_specs=in_specs,
        out_specs=[blk, blk, blk, blk],
        out_shape=[out, out, out, out],
        compiler_params=_params(),
        **extra,
    )(*args)


def _attn_bias_vector(rel_bias):
    h = rel_bias.shape[0]
    n_far = BAND - MAX_REL
    n_near = BAND + CHUNK - 1 - n_far
    far = jnp.broadcast_to(rel_bias[:, 2 * MAX_REL:], (h, n_far))
    near = rel_bias[:, 2 * MAX_REL - n_near:2 * MAX_REL][:, ::-1]
    pad = jnp.zeros((h, 2 * QBLK - n_far - n_near), F32)
    return jnp.concatenate([far, near, pad], axis=1)[:, None, :]


def _s5_prepare(lre, lim, ldt, bre, bim, cre, cim):
    g, p = lre.shape
    lr = jnp.minimum(lre, -1e-4)
    dt = jnp.exp(ldt)[:, None]
    mag = jnp.exp(lr * dt)
    ar = mag * jnp.cos(lim * dt)
    ai = mag * jnp.sin(lim * dt)
    den = lr * lr + lim * lim
    coef_r = ((ar - 1.0) * lr + ai * lim) / den
    coef_i = (ai * lr - (ar - 1.0) * lim) / den
    bbar_r = coef_r[..., None] * bre - coef_i[..., None] * bim
    bbar_i = coef_r[..., None] * bim + coef_i[..., None] * bre
    eye = jnp.eye(g, dtype=F32)
    bd_in = lambda b: jnp.einsum("gpc,gh->gchp", b, eye).reshape(g * S5_GROUP, g * p)
    bd_out = lambda c: jnp.einsum("gcp,gh->gphc", c, eye).reshape(g * p, g * S5_GROUP)
    return (ar.reshape(1, g * p), ai.reshape(1, g * p), bd_in(bbar_r), bd_in(bbar_i), bd_out(cre), bd_out(cim))


SHARDED = ("ffn1_w_up", "ffn1_w_down", "w_in", "s5_w_glu", "w_br_s5", "w_br_attn", "conv_w_dw", "w_br_conv",
           "w_out", "ffn2_w_up", "ffn2_w_down")
WEIGHTS = ("ffn1_norm", "ffn1_w_up", "ffn1_w_down", "mix_norm", "w_in", "b_gate", "s5_lambda_re", "s5_lambda_im",
           "s5_log_dt", "s5_b_re", "s5_b_im", "s5_c_re", "s5_c_im", "s5_d", "s5_w_glu", "w_br_s5", "attn_q_gain",
           "attn_k_gain", "attn_rel_bias", "w_br_attn", "conv_w_dw", "conv_b_dw", "conv_ln_g", "conv_ln_b",
           "w_br_conv", "w_out", "ffn2_norm", "ffn2_w_up", "ffn2_w_down")
SMALL = tuple(nm for nm in WEIGHTS if nm not in SHARDED)
SMALL_LANES = 1024


def _cols_full(g):
    _, depth, k, nn = g.shape
    return g.transpose(1, 2, 0, 3).reshape(depth, k, N_DEV * nn)


def _rows_full(g):
    _, depth, r, cc = g.shape
    return g.transpose(1, 0, 2, 3).reshape(depth, N_DEV * r, cc)


def _cols_slabs(gfull):
    depth, k, c8 = gfull.shape
    return gfull.reshape(depth, k, N_DEV, c8 // N_DEV).transpose(0, 2, 1, 3)


def _heads(a, h):
    n = a.shape[0]
    return a.reshape(n, h, HEAD).transpose(1, 0, 2)


def _unheads(a):
    h, n, _ = a.shape
    return a.transpose(1, 0, 2).reshape(n, h * HEAD)


def _small_pack(t):
    flat = jnp.concatenate([t[nm].reshape(-1) for nm in SMALL])
    rows = -(-flat.shape[0] // SMALL_LANES)
    rows = -(-rows // 8) * 8
    return jnp.pad(flat, (0, rows * SMALL_LANES - flat.shape[0])).reshape(rows, SMALL_LANES)


def _small_unpack(flat, like):
    flat = flat.reshape(-1)
    out, off = {}, 0
    for nm in SMALL:
        out[nm] = flat[off:off + like[nm].size].reshape(like[nm].shape)
        off += like[nm].size
    return out


def _step(x, target, w, m, v):
    bsz, seq, d = x.shape
    n = bsz * seq
    depth = w["ffn1_norm"].shape[0]
    da, dss, dc = d // 2, d // 4, d // 4
    heads = da // HEAD
    gp = dss // S5_GROUP * S5_STATE
    mid = 3 * da + 2 * dc
    q0, k0, v0, z0, u0 = 3 * d, 3 * d + da, 3 * d + 2 * da, 3 * d + 3 * da, 3 * d + mid
    to_kernel_cols = lambda a: jnp.concatenate([a[..., dss + mid:], a[..., dss:dss + mid], a[..., :dss]], axis=-1)
    to_ref_cols = lambda a: jnp.concatenate([a[..., 3 * d + mid:], a[..., 3 * d:3 * d + mid], a[..., :3 * d]], axis=-1)

    shard = lambda nm: w[nm] if nm == "conv_w_dw" else w[nm].astype(BF16)
    mixer_names = ("w_in", "s5_w_glu", "w_br_s5", "w_br_attn", "w_br_conv", "conv_w_dw", "w_out")
    gather_of = lambda names: (_gather_plan, [shard(nm) for nm in names], _gather_out([shard(nm) for nm in names]))
    full = {}

    def take(names, arrays):
        for nm, g in zip(names, arrays):
            if nm in ("ffn1_w_up", "ffn2_w_up"):
                full[nm] = g
            elif nm in ("ffn1_w_down", "ffn2_w_down", "w_out"):
                full[nm] = _rows_full(g)
            elif nm == "w_in":
                full[nm] = to_kernel_cols(_cols_full(g))
            else:
                full[nm] = _cols_full(g)

    take(("ffn1_w_up", "ffn1_w_down"), all_gather([shard("ffn1_w_up"), shard("ffn1_w_down")]))
    nff = full["ffn1_w_up"].shape[3]

    row = lambda a: a.reshape(1, -1)
    saved = []
    xin = x.reshape(n, d)
    for l in range(depth):
        s = {"x0": xin}
        (s["x1"], s["a1"], s["b1"]), got = ffn_fwd(xin, row(w["ffn1_norm"][l]), full["ffn1_w_up"], full["ffn1_w_down"], l,
                                                   comm=gather_of(mixer_names) if l == 0 else None)
        if l == 0:
            take(mixer_names, got)
            conv_w = jnp.pad(full["conv_w_dw"], ((0, 0), (0, HALO - CONV_W), (0, 0)))
        proj = proj_fwd(s["x1"], row(w["mix_norm"][l]), full["w_in"], l)
        s["proj"] = proj
        prep_in = (w["s5_lambda_re"][l], w["s5_lambda_im"][l], w["s5_log_dt"][l], w["s5_b_re"][l], w["s5_b_im"][l],
                   w["s5_c_re"][l], w["s5_c_im"][l])
        (ar, ai, bm_r, bm_i, cm_r, cm_i), s["prep_vjp"] = jax.vjp(_s5_prepare, *prep_in)
        s["s5p"] = (ar, ai, bm_r.astype(BF16), bm_i.astype(BF16), cm_r.astype(BF16), cm_i.astype(BF16),
                    row(w["s5_d"][l]), full["s5_w_glu"], l)
        s["so"], s["xr"], s["xi"], s["ypre"] = s5_fwd(proj, u0 // dss, seq, *s["s5p"])
        tv, s["tv_vjp"] = jax.vjp(_attn_bias_vector, w["attn_rel_bias"][l])
        s["qkv"] = tuple(_heads(proj[:, c0:c0 + da], heads) for c0 in (q0, k0, v0))
        s["attnp"] = (tv, row(w["attn_q_gain"][l]), row(w["attn_k_gain"][l]))
        (ao,), got = attn_fwd(*s["qkv"], *s["attnp"], seq,
                              comm=gather_of(("ffn2_w_up", "ffn2_w_down")) if l == 0 else None)
        if l == 0:
            take(("ffn2_w_up", "ffn2_w_down"), got)
        s["ao"] = _unheads(ao)
        s["convp"] = (conv_w[l], row(w["conv_b_dw"][l]), row(w["conv_ln_g"][l]), row(w["conv_ln_b"][l]))
        s["h1"], s["co"] = conv_fwd(proj, z0 // (2 * dc), seq, *s["convp"])
        s["mergep"] = (row(w["b_gate"][l]), s["so"], s["ao"], s["co"], full["w_br_s5"], full["w_br_attn"],
                       full["w_br_conv"], full["w_out"], l)
        s["x2"] = merge_fwd(s["x1"], proj, *s["mergep"])
        (xin, s["a2"], s["b2"]), _ = ffn_fwd(s["x2"], row(w["ffn2_norm"][l]), full["ffn2_w_up"], full["ffn2_w_down"], l)
        saved.append(s)

    dx, loss = loss_head(xin, target.reshape(n, d))
    loss = lax.psum(loss[0, 0], ("x", "y", "c"))

    small_g = {nm: [None] * depth for nm in SMALL}
    slab = {nm: None for nm in SHARDED}
    conv_g = [None] * depth

    recv = {}

    def exchange_of(names, l):
        arrays = [slab.pop(nm) for nm in names]
        return [(nm, l) for nm in names], (_exchange_plan(), arrays, _exchange_out(arrays))

    def ffn_grads(which, hn, dyb, dab_a, dab_b, act):
        up, down = which + "_w_up", which + "_w_down"
        up4, down4 = (1, N_DEV, d, nff), (1, FF_CHUNKS, nff, d)
        half = wgrad(hn, dab_a, "wg_ffn_up", up4, "col")
        slab[up] = wgrad(hn, dab_b, "wg_ffn_up", up4, "col", g0=FF_CHUNKS, prev=half).reshape(up4[1:])
        slab[down] = wgrad(act, dyb, "wg_ffn_down", down4, "row").reshape(N_DEV, nff // 2, d)

    pending = None
    for l in reversed(range(depth)):
        s = saved[l]
        outs, got = ffn_bwd(dx, s["x2"], row(w["ffn2_norm"][l]), s["a2"], s["b2"],
                            full["ffn2_w_up"], full["ffn2_w_down"], l, comm=pending[1] if pending else None)
        if pending:
            recv.update(zip(pending[0], got))
        dx, dg, hn, dyb, dab_a, dab_b, act = outs
        small_g["ffn2_norm"][l] = dg
        ffn_grads("ffn2", hn, dyb, dab_a, dab_b, act)

        dlog, dso, dao, dco, dbg, mg, dxb, dys, dya, dyc = merge_bwd(dx, s["proj"], *s["mergep"])
        small_g["b_gate"][l] = dbg
        slab["w_out"] = wgrad(mg, dxb, "wg_out", (1, N_DEV, d // N_DEV, d), "row").reshape(N_DEV, d // N_DEV, d)
        for nm, act_in, dy_br in (("w_br_s5", s["so"], dys), ("w_br_attn", s["ao"], dya), ("w_br_conv", s["co"], dyc)):
            k_in = act_in.shape[1]
            slab[nm] = wgrad(act_in, dy_br, "wg_" + nm, (1, N_DEV, k_in, d // N_DEV), "col").reshape(
                N_DEV, k_in, d // N_DEV)

        dz, dwdw, dbdw, dlng, dlnb = conv_bwd(dco, s["h1"], s["proj"], z0 // (2 * dc), seq, *s["convp"])
        slab["conv_w_dw"] = _cols_slabs(dwdw[None, :CONV_W])[0].astype(BF16)
        small_g["conv_b_dw"][l], small_g["conv_ln_g"][l], small_g["conv_ln_b"][l] = dbdw, dlng, dlnb

        keys, comm = exchange_of(("ffn2_w_up", "ffn2_w_down"), l)
        outs, got = attn_bwd(_heads(dao, heads), *s["qkv"], *s["attnp"], seq, comm=comm)
        recv.update(zip(keys, got))
        dq, dkp, dkc, dvp, dvc, dtv, dgq, dgk = outs
        small_g["attn_q_gain"][l], small_g["attn_k_gain"][l] = dgq, dgk
        small_g["attn_rel_bias"][l] = s["tv_vjp"](dtv)[0]

        def from_prev(cur, prev):
            prev = prev.reshape(heads, bsz, seq, HEAD)
            prev = jnp.concatenate([prev[:, :, QBLK:], jnp.zeros_like(prev[:, :, :QBLK])], axis=2)
            return cur + prev.reshape(heads, n, HEAD)

        dk, dv = from_prev(dkc, dkp), from_prev(dvc, dvp)

        du, gr, gi, dyb5, glb, dzb, dar, dai, dd = s5_bwd(dso, s["ypre"], s["proj"], u0 // dss, s["xr"], s["xi"],
                                                          seq, *s["s5p"])
        small_g["s5_d"][l] = dd
        one = lambda k1, k2: (1, 1, k1, k2)
        slab["s5_w_glu"] = _cols_slabs(wgrad(glb, dzb, "wg_s5_glu", one(dss, 2 * dss), "col")[0])[0]
        dcm_r = wgrad(s["xr"], dyb5, "wg_s5_c", one(gp, dss), "col", dtype=F32)[0, 0]
        dcm_i = -wgrad(s["xi"], dyb5, "wg_s5_c", one(gp, dss), "col", dtype=F32)[0, 0]
        dbm_r = wgrad(s["proj"], gr, "wg_s5_b", one(dss, gp), "col", a_cols=(u0, dss), dtype=F32)[0, 0]
        dbm_i = wgrad(s["proj"], gi, "wg_s5_b", one(dss, gp), "col", a_cols=(u0, dss), dtype=F32)[0, 0]
        pg = s["prep_vjp"]((dar, dai, dbm_r, dbm_i, dcm_r, dcm_i))
        for nm, gval in zip(("s5_lambda_re", "s5_lambda_im", "s5_log_dt", "s5_b_re", "s5_b_im", "s5_c_re", "s5_c_im"), pg):
            small_g[nm][l] = gval

        dproj = jnp.concatenate([dlog, _unheads(dq).astype(BF16), _unheads(dk).astype(BF16),
                                 _unheads(dv).astype(BF16), dz.astype(BF16), du.astype(BF16)], axis=1)
        dx, dgm, hn = proj_bwd(dproj, dx, s["x1"], row(w["mix_norm"][l]), full["w_in"], l)
        small_g["mix_norm"][l] = dgm
        slab["w_in"] = _cols_slabs(to_ref_cols(wgrad(hn, dproj, "wg_in", one(d, 3 * d + mid + dss), "col")[0]))[0]

        keys, comm = exchange_of(mixer_names, l)
        outs, got = ffn_bwd(dx, s["x0"], row(w["ffn1_norm"][l]), s["a1"], s["b1"],
                            full["ffn1_w_up"], full["ffn1_w_down"], l, comm=comm)
        recv.update(zip(keys, got))
        dx, dg, hn, dyb, dab_a, dab_b, act = outs
        small_g["ffn1_norm"][l] = dg
        ffn_grads("ffn1", hn, dyb, dab_a, dab_b, act)
        pending = exchange_of(("ffn1_w_up", "ffn1_w_down"), l)

    small_flat = _small_pack({nm: jnp.stack([g.reshape(w[nm].shape[1:]) for g in small_g[nm]]) for nm in SMALL})
    *got, recv_small = grad_exchange(pending[1][1], small_flat)
    recv.update(zip(pending[0], got))

    outs = {}
    for nm in SHARDED:
        shp = w[nm].shape
        as3 = lambda t: t.reshape(shp[0], -1, shp[-1])
        bufs = None
        for l in reversed(range(depth)):
            bufs = sum_adamw(recv[(nm, l)], as3(w[nm]), as3(m[nm]), as3(v[nm]), l, bufs, "adamw_" + nm)
        outs[nm] = [b.reshape(shp) for b in bufs]
    packed = sum_adamw(recv_small, _small_pack(w)[None], _small_pack(m)[None], _small_pack(v)[None], 0, None,
                       "adamw_small")
    unpacked = [_small_unpack(p, w) for p in packed]
    for nm in SMALL:
        outs[nm] = [u[nm] for u in unpacked]
    return loss, dx.reshape(x.shape), outs


def kernel(x, ffn1_norm, ffn1_w_up, ffn1_w_down, mix_norm, w_in, b_gate, s5_lambda_re, s5_lambda_im, s5_log_dt, s5_b_re, s5_b_im, s5_c_re, s5_c_im, s5_d, s5_w_glu, w_br_s5, attn_q_gain, attn_k_gain, attn_rel_bias, w_br_attn, conv_w_dw, conv_b_dw, conv_ln_g, conv_ln_b, w_br_conv, w_out, ffn2_norm, ffn2_w_up, ffn2_w_down, loss_target, m_ffn1_norm, m_ffn1_w_up, m_ffn1_w_down, m_mix_norm, m_w_in, m_b_gate, m_s5_lambda_re, m_s5_lambda_im, m_s5_log_dt, m_s5_b_re, m_s5_b_im, m_s5_c_re, m_s5_c_im, m_s5_d, m_s5_w_glu, m_w_br_s5, m_attn_q_gain, m_attn_k_gain, m_attn_rel_bias, m_w_br_attn, m_conv_w_dw, m_conv_b_dw, m_conv_ln_g, m_conv_ln_b, m_w_br_conv, m_w_out, m_ffn2_norm, m_ffn2_w_up, m_ffn2_w_down, v_ffn1_norm, v_ffn1_w_up, v_ffn1_w_down, v_mix_norm, v_w_in, v_b_gate, v_s5_lambda_re, v_s5_lambda_im, v_s5_log_dt, v_s5_b_re, v_s5_b_im, v_s5_c_re, v_s5_c_im, v_s5_d, v_s5_w_glu, v_w_br_s5, v_attn_q_gain, v_attn_k_gain, v_attn_rel_bias, v_w_br_attn, v_conv_w_dw, v_conv_b_dw, v_conv_ln_g, v_conv_ln_b, v_w_br_conv, v_w_out, v_ffn2_norm, v_ffn2_w_up, v_ffn2_w_down):
    args = locals()
    w = {nm: args[nm] for nm in WEIGHTS}
    m = {nm: args["m_" + nm] for nm in WEIGHTS}
    v = {nm: args["v_" + nm] for nm in WEIGHTS}
    loss, gx, outs = _step(x, loss_target, w, m, v)
    return (loss, gx, *[outs[nm][0] for nm in WEIGHTS], *[outs[nm][1] for nm in WEIGHTS],
            *[outs[nm][2] for nm in WEIGHTS], *[outs[nm][3] for nm in WEIGHTS])
```

```python
import functools
import math

import numpy as np
import jax
import jax.numpy as jnp
from jax import lax
from jax.experimental import pallas as pl
from jax.experimental.pallas import tpu as pltpu

F32 = jnp.float32
BF16 = jnp.bfloat16

CHUNK = 64
N_LEFT = 8
QBLK = CHUNK * N_LEFT
HEAD = 64
MAX_REL = 128
S5_GROUP = 16
S5_STATE = 64
CONV_W = 31
HALO = 32
EPS = 1e-6
NEG = -1e30
ADAM_LR, ADAM_B1, ADAM_B2, ADAM_EPS, ADAM_WD, ADAM_STEP = 0.001, 0.9, 0.999, 1e-08, 0.01, 10
N_DEV = 8
VMEM_LIMIT = 56 * 1024 * 1024


def _call(body, **kw):
    return pl.pallas_call(body, **kw)


def _params(**kw):
    return pltpu.CompilerParams(vmem_limit_bytes=VMEM_LIMIT, **kw)


def _tile(n, cap, unit=128):
    if n <= cap:
        return n
    d = (cap // unit) * unit
    while d >= unit:
        if n % d == 0:
            return d
        d -= unit
    raise ValueError(f"no tile for {n} under {cap}")


def _dot(a, b):
    return jnp.dot(a, b, preferred_element_type=F32)


def _dot_nt(a, b):
    return lax.dot_general(a, b, (((1,), (1,)), ((), ())), preferred_element_type=F32)


def _dot_tn(a, b):
    return lax.dot_general(a, b, (((0,), (0,)), ((), ())), preferred_element_type=F32)


def _sig(x):
    return 1.0 / (1.0 + jnp.exp(-x))


def _rms_fwd(x, g):
    rs = lax.rsqrt(jnp.mean(x * x, axis=-1, keepdims=True) + EPS)
    xhat = x * rs
    return xhat * g, xhat, rs


def _rms_bwd(dh, xhat, rs, g):
    dxh = dh * g
    dx = rs * (dxh - xhat * jnp.mean(dxh * xhat, axis=-1, keepdims=True))
    return dx, dh * xhat


_GELU_C = math.sqrt(2.0 / math.pi)


def _gelu(x):
    return 0.5 * x * (1.0 + jnp.tanh(_GELU_C * (x + 0.044715 * x * x * x)))


def _gelu_grad(x):
    t = jnp.tanh(_GELU_C * (x + 0.044715 * x * x * x))
    return 0.5 * (1.0 + t) + 0.5 * x * (1.0 - t * t) * _GELU_C * (1.0 + 3.0 * 0.044715 * x * x)


def _acc_out(ref, val, first):
    @pl.when(first)
    def _():
        ref[...] = val

    @pl.when(jnp.logical_not(first))
    def _():
        ref[...] += val


FF_CHUNKS = N_DEV // 2


def ffn_fwd(x, g, w_up, w_down, layer, comm=None):
    n, d = x.shape
    nn = w_up.shape[3]
    tm = _tile(n, 1024, 8)

    def body(x_ref, g_ref, wa_ref, wb_ref, wd_ref, xo_ref, a_ref, b_ref, hn_ref, acc_ref):
        j = pl.program_id(1)

        @pl.when(j == 0)
        def _():
            h, _, _ = _rms_fwd(x_ref[...], g_ref[...])
            hn_ref[...] = h.astype(BF16)
            acc_ref[...] = jnp.zeros_like(acc_ref)

        hn = hn_ref[...]
        a = _dot(hn, wa_ref[0, 0])
        b = _dot(hn, wb_ref[0, 0])
        a_ref[0] = a.astype(BF16)
        b_ref[0] = b.astype(BF16)
        act = a * _sig(a) * b
        acc_ref[...] += _dot(act.astype(BF16), wd_ref[0])

        @pl.when(j == FF_CHUNKS - 1)
        def _():
            xo_ref[...] = x_ref[...] + 0.5 * acc_ref[...]

    return _hosted_call(
        body, comm,
        name="ffn_fwd",
        grid=(n // tm, FF_CHUNKS),
        in_specs=[
            pl.BlockSpec((tm, d), lambda i, j: (i, 0)),
            pl.BlockSpec((1, d), lambda i, j: (0, 0)),
            pl.BlockSpec((1, 1, d, nn), lambda i, j: (j, layer, 0, 0)),
            pl.BlockSpec((1, 1, d, nn), lambda i, j: (j + FF_CHUNKS, layer, 0, 0)),
            pl.BlockSpec((1, nn, d), lambda i, j: (layer, j, 0)),
        ],
        out_specs=[
            pl.BlockSpec((tm, d), lambda i, j: (i, 0)),
            pl.BlockSpec((1, tm, nn), lambda i, j: (j, i, 0)),
            pl.BlockSpec((1, tm, nn), lambda i, j: (j, i, 0)),
        ],
        out_shape=[
            jax.ShapeDtypeStruct((n, d), F32),
            jax.ShapeDtypeStruct((FF_CHUNKS, n, nn), BF16),
            jax.ShapeDtypeStruct((FF_CHUNKS, n, nn), BF16),
        ],
        scratch_shapes=[pltpu.VMEM((tm, d), BF16), pltpu.VMEM((tm, d), F32)],
        args=(x, g, w_up, w_up, w_down),
    )


def ffn_bwd(dy, x, g, a, b, w_up, w_down, layer, comm=None):
    n, d = x.shape
    nn = w_up.shape[3]
    tm = _tile(n, 512, 8)

    def body(dy_ref, x_ref, g_ref, a_ref, b_ref, wa_ref, wb_ref, wd_ref,
             dx_ref, dg_ref, hn_ref, dyb_ref, da_ref, db_ref, act_ref, dyb_s, dh_ref):
        i, j = pl.program_id(0), pl.program_id(1)

        @pl.when(j == 0)
        def _():
            h, _, _ = _rms_fwd(x_ref[...], g_ref[...])
            hn_ref[...] = h.astype(BF16)
            dyb = (0.5 * dy_ref[...]).astype(BF16)
            dyb_ref[...] = dyb
            dyb_s[...] = dyb
            dh_ref[...] = jnp.zeros_like(dh_ref)

        dact = _dot_nt(dyb_s[...], wd_ref[0])
        a32 = a_ref[0].astype(F32)
        b32 = b_ref[0].astype(F32)
        s = _sig(a32)
        sil = a32 * s
        da = (dact * b32 * (s * (1.0 + a32 * (1.0 - s)))).astype(BF16)
        db = (dact * sil).astype(BF16)
        da_ref[0] = da
        db_ref[0] = db
        act_ref[0] = (sil * b32).astype(BF16)
        dh_ref[...] += _dot_nt(da, wa_ref[0, 0]) + _dot_nt(db, wb_ref[0, 0])

        @pl.when(j == FF_CHUNKS - 1)
        def _():
            gg = g_ref[...]
            _, xhat, rs = _rms_fwd(x_ref[...], gg)
            dxn, dgr = _rms_bwd(dh_ref[...], xhat, rs, gg)
            dx_ref[...] = dy_ref[...] + dxn
            _acc_out(dg_ref, jnp.sum(dgr, axis=0, keepdims=True), i == 0)

    tok = pl.BlockSpec((tm, d), lambda i, j: (i, 0))
    chunk = pl.BlockSpec((1, tm, nn), lambda i, j: (j, i, 0))
    vec = pl.BlockSpec((1, d), lambda i, j: (0, 0))
    chunks = jax.ShapeDtypeStruct((FF_CHUNKS, n, nn), BF16)
    return _hosted_call(
        body, comm,
        name="ffn_bwd",
        grid=(n // tm, FF_CHUNKS),
        in_specs=[
            tok, tok, vec, chunk, chunk,
            pl.BlockSpec((1, 1, d, nn), lambda i, j: (j, layer, 0, 0)),
            pl.BlockSpec((1, 1, d, nn), lambda i, j: (j + FF_CHUNKS, layer, 0, 0)),
            pl.BlockSpec((1, nn, d), lambda i, j: (layer, j, 0)),
        ],
        out_specs=[tok, vec, tok, tok, chunk, chunk, chunk],
        out_shape=[
            jax.ShapeDtypeStruct((n, d), F32),
            jax.ShapeDtypeStruct((1, d), F32),
            jax.ShapeDtypeStruct((n, d), BF16),
            jax.ShapeDtypeStruct((n, d), BF16),
            chunks, chunks, chunks,
        ],
        scratch_shapes=[pltpu.VMEM((tm, d), BF16), pltpu.VMEM((tm, d), F32)],
        args=(dy, x, g, a, b, w_up, w_up, w_down),
    )


def wgrad(a, b, name, out4, mode, *, g0=0, layer=0, prev=None, a_cols=None, dtype=BF16):
    a3 = a if a.ndim == 3 else a[None]
    b3 = b if b.ndim == 3 else b[None]
    sa, n, ka = a3.shape
    sb, _, kb = b3.shape
    a0 = 0
    if a_cols is not None:
        a0, ka = a_cols
    k1, k2 = sa * ka, sb * kb
    depth, groups, rr, cc = out4
    t1 = _tile(math.gcd(ka, rr), 1024)
    t2 = _tile(math.gcd(kb, cc), 1024)
    tn = _tile(n, 1024, 8)
    gpb = 1
    if mode == "col":
        assert rr == k1 and k2 % cc == 0 and g0 + k2 // cc <= groups
        if kb % cc == 0 and cc < kb <= 1024 and g0 % (kb // cc) == 0:
            t2, gpb = kb, kb // cc
        per = max(cc // t2, 1)
        oblock = (1, gpb, t1, min(t2, cc))
        omap = lambda i, j, k: (layer, (g0 + j // per) // gpb, i, j % per)
    else:
        assert cc == k2 and k1 % rr == 0 and g0 + k1 // rr <= groups
        if ka % rr == 0 and rr < ka <= 1024 and g0 % (ka // rr) == 0:
            t1, gpb = ka, ka // rr
        per = max(rr // t1, 1)
        oblock = (1, gpb, min(t1, rr), t2)
        omap = lambda i, j, k: (layer, (g0 + i // per) // gpb, i % per, j)
    na, nb = ka // t1, kb // t2
    nk = n // tn

    def body(a_ref, b_ref, *rest):
        o_ref, acc_ref = rest[-2], rest[-1]
        k = pl.program_id(2)

        @pl.when(k == 0)
        def _():
            acc_ref[...] = jnp.zeros_like(acc_ref)

        acc_ref[...] += _dot_tn(a_ref[0].astype(BF16), b_ref[0].astype(BF16))

        @pl.when(k == nk - 1)
        def _():
            for g in range(gpb):
                if gpb == 1:
                    o_ref[0, 0] = acc_ref[...].astype(dtype)
                elif mode == "col":
                    o_ref[0, g] = acc_ref[:, g * cc:(g + 1) * cc].astype(dtype)
                else:
                    o_ref[0, g] = acc_ref[g * rr:(g + 1) * rr, :].astype(dtype)

    in_specs = [
        pl.BlockSpec((1, tn, t1), lambda i, j, k: (i // na, k, a0 // t1 + i % na)),
        pl.BlockSpec((1, tn, t2), lambda i, j, k: (j // nb, k, j % nb)),
    ]
    args = [a3, b3]
    extra = {}
    if prev is not None:
        in_specs.append(pl.BlockSpec(memory_space=pl.ANY))
        args.append(prev)
        extra["input_output_aliases"] = {2: 0}
    return _call(
        body,
        name=name,
        grid=(k1 // t1, k2 // t2, nk),
        in_specs=in_specs,
        out_specs=pl.BlockSpec(oblock, omap),
        out_shape=jax.ShapeDtypeStruct(out4, dtype),
        scratch_shapes=[pltpu.VMEM((t1, t2), F32)],
        compiler_params=_params(),
        **extra,
    )(*args)


def proj_fwd(x, g, w, layer):
    n, d = x.shape
    c = w.shape[2]
    tm, tc = _tile(n, 1024, 8), _tile(c, 768)

    def body(x_ref, g_ref, w_ref, o_ref, hn_ref):
        @pl.when(pl.program_id(1) == 0)
        def _():
            h, _, _ = _rms_fwd(x_ref[...], g_ref[...])
            hn_ref[...] = h.astype(BF16)

        o_ref[...] = _dot(hn_ref[...], w_ref[0])

    return _call(
        body,
        name="proj_fwd",
        grid=(n // tm, c // tc),
        in_specs=[
            pl.BlockSpec((tm, d), lambda i, j: (i, 0)),
            pl.BlockSpec((1, d), lambda i, j: (0, 0)),
            pl.BlockSpec((1, d, tc), lambda i, j: (layer, 0, j)),
        ],
        out_specs=pl.BlockSpec((tm, tc), lambda i, j: (i, j)),
        out_shape=jax.ShapeDtypeStruct((n, c), F32),
        scratch_shapes=[pltpu.VMEM((tm, d), BF16)],
        compiler_params=_params(),
    )(x, g, w)


def proj_bwd(dproj, dres, x, g, w, layer):
    n, d = x.shape
    c = w.shape[2]
    tm, tc = _tile(n, 512, 8), _tile(c, 768)
    nc = c // tc

    def body(dp_ref, dr_ref, x_ref, g_ref, w_ref, dx_ref, dg_ref, hn_ref, dh_ref):
        i, j = pl.program_id(0), pl.program_id(1)

        @pl.when(j == 0)
        def _():
            dh_ref[...] = jnp.zeros_like(dh_ref)

        dh_ref[...] += _dot_nt(dp_ref[...], w_ref[0])

        @pl.when(j == nc - 1)
        def _():
            gg = g_ref[...]
            h, xhat, rs = _rms_fwd(x_ref[...], gg)
            hn_ref[...] = h.astype(BF16)
            dxn, dgr = _rms_bwd(dh_ref[...], xhat, rs, gg)
            dx_ref[...] = dr_ref[...] + dxn
            _acc_out(dg_ref, jnp.sum(dgr, axis=0, keepdims=True), i == 0)

    return _call(
        body,
        name="proj_bwd",
        grid=(n // tm, nc),
        in_specs=[
            pl.BlockSpec((tm, tc), lambda i, j: (i, j)),
            pl.BlockSpec((tm, d), lambda i, j: (i, 0)),
            pl.BlockSpec((tm, d), lambda i, j: (i, 0)),
            pl.BlockSpec((1, d), lambda i, j: (0, 0)),
            pl.BlockSpec((1, d, tc), lambda i, j: (layer, 0, j)),
        ],
        out_specs=[
            pl.BlockSpec((tm, d), lambda i, j: (i, 0)),
            pl.BlockSpec((1, d), lambda i, j: (0, 0)),
            pl.BlockSpec((tm, d), lambda i, j: (i, 0)),
        ],
        out_shape=[
            jax.ShapeDtypeStruct((n, d), F32),
            jax.ShapeDtypeStruct((1, d), F32),
            jax.ShapeDtypeStruct((n, d), BF16),
        ],
        scratch_shapes=[pltpu.VMEM((tm, d), F32)],
        compiler_params=_params(),
    )(dproj, dres, x, g, w)


S5_TS = 512


def _cmul(ar, ai, br, bi):
    return ar * br - ai * bi, ar * bi + ai * br


def _s5_tables(ar, ai, reverse):
    gp = ar.shape[1]
    if reverse:
        ai = -ai
    a1r, a1i = jnp.broadcast_to(ar, (8, gp)), jnp.broadcast_to(ai, (8, gp))
    a2r, a2i = _cmul(a1r, a1i, a1r, a1i)
    a4r, a4i = _cmul(a2r, a2i, a2r, a2i)
    a8r, a8i = _cmul(a4r, a4i, a4r, a4i)
    row = lax.broadcasted_iota(jnp.int32, (8, gp), 0)
    e = (8 - row) if reverse else (row + 1)
    pr, pi = jnp.ones((8, gp), F32), jnp.zeros((8, gp), F32)
    for bit, (fr, fi) in ((1, (a1r, a1i)), (2, (a2r, a2i)), (4, (a4r, a4i)), (8, (a8r, a8i))):
        nr, ni = _cmul(pr, pi, fr, fi)
        on = (e & bit) != 0
        pr, pi = jnp.where(on, nr, pr), jnp.where(on, ni, pi)
    return (a1r, a1i, a2r, a2i, a4r, a4i, pr, pi)


def _s5_scan(xr_ref, xi_ref, tab_ref, cr_ref, ci_ref, ts, reverse):
    gp = xr_ref.shape[1]
    nt = ts // 8
    row = lax.broadcasted_iota(jnp.int32, (8, gp), 0)

    def shifted(v, s):
        if reverse:
            return jnp.where(row < 8 - s, pltpu.roll(v, 8 - s, 0), 0.0)
        return jnp.where(row >= s, pltpu.roll(v, s, 0), 0.0)

    def step(k, carry):
        cr, ci = carry
        t = (nt - 1 - k) if reverse else k
        r0 = pl.multiple_of(t * 8, 8)
        br = xr_ref[pl.ds(r0, 8), :]
        bi = xi_ref[pl.ds(r0, 8), :]
        for q, s in enumerate((1, 2, 4)):
            fr, fi = tab_ref[2 * q], tab_ref[2 * q + 1]
            sr, si = shifted(br, s), shifted(bi, s)
            mr, mi = _cmul(fr, fi, sr, si)
            br, bi = br + mr, bi + mi
        mr, mi = _cmul(tab_ref[6], tab_ref[7], cr, ci)
        br, bi = br + mr, bi + mi
        xr_ref[pl.ds(r0, 8), :] = br
        xi_ref[pl.ds(r0, 8), :] = bi
        edge = 0 if reverse else 7
        return (jnp.broadcast_to(br[edge:edge + 1, :], (8, gp)),
                jnp.broadcast_to(bi[edge:edge + 1, :], (8, gp)))

    cr, ci = lax.fori_loop(0, nt, step, (cr_ref[...], ci_ref[...]))
    cr_ref[...] = cr
    ci_ref[...] = ci


def s5_fwd(proj, ucol, seq, ar, ai, bm_r, bm_i, cm_r, cm_i, dskip, w_glu, layer):
    n = proj.shape[0]
    ds, gp = bm_r.shape
    ts = min(S5_TS, seq)
    nt = seq // ts

    def body(u_ref, ar_ref, ai_ref, bmr_ref, bmi_ref, cmr_ref, cmi_ref, d_ref, wg_ref,
             out_ref, xr_ref, xi_ref, yp_ref, tab_ref, cr_ref, ci_ref):
        @pl.when(pl.program_id(1) == 0)
        def _():
            for q, v in enumerate(_s5_tables(ar_ref[...], ai_ref[...], False)):
                tab_ref[q] = v
            cr_ref[...] = jnp.zeros_like(cr_ref)
            ci_ref[...] = jnp.zeros_like(ci_ref)

        u = u_ref[...]
        ub = u.astype(BF16)
        xr_ref[...] = _dot(ub, bmr_ref[...])
        xi_ref[...] = _dot(ub, bmi_ref[...])
        _s5_scan(xr_ref, xi_ref, tab_ref, cr_ref, ci_ref, ts, False)
        y = (_dot(xr_ref[...].astype(BF16), cmr_ref[...]) - _dot(xi_ref[...].astype(BF16), cmi_ref[...])
             + d_ref[...] * u)
        yp_ref[...] = y
        z = _dot(_gelu(y).astype(BF16), wg_ref[0])
        out_ref[...] = z[:, :ds] * _sig(z[:, ds:])

    full = lambda shape: pl.BlockSpec(shape, lambda b, t: (0, 0))
    return _call(
        body,
        name="s5_fwd",
        grid=(n // seq, nt),
        in_specs=[
            pl.BlockSpec((ts, ds), lambda b, t: (b * nt + t, ucol)),
            full((1, gp)), full((1, gp)), full((ds, gp)), full((ds, gp)), full((gp, ds)), full((gp, ds)),
            full((1, ds)), pl.BlockSpec((1, ds, 2 * ds), lambda b, t: (layer, 0, 0)),
        ],
        out_specs=[
            pl.BlockSpec((ts, ds), lambda b, t: (b * nt + t, 0)),
            pl.BlockSpec((ts, gp), lambda b, t: (b * nt + t, 0)),
            pl.BlockSpec((ts, gp), lambda b, t: (b * nt + t, 0)),
            pl.BlockSpec((ts, ds), lambda b, t: (b * nt + t, 0)),
        ],
        out_shape=[
            jax.ShapeDtypeStruct((n, ds), F32),
            jax.ShapeDtypeStruct((n, gp), F32),
            jax.ShapeDtypeStruct((n, gp), F32),
            jax.ShapeDtypeStruct((n, ds), F32),
        ],
        scratch_shapes=[pltpu.VMEM((8, 8, gp), F32), pltpu.VMEM((8, gp), F32), pltpu.VMEM((8, gp), F32)],
        compiler_params=_params(),
    )(proj, ar, ai, bm_r, bm_i, cm_r, cm_i, dskip, w_glu)


def s5_bwd(dout, ypre, proj, ucol, xr, xi, seq, ar, ai, bm_r, bm_i, cm_r, cm_i, dskip, w_glu, layer):
    n = proj.shape[0]
    ds, gp = bm_r.shape
    ts = min(S5_TS, seq)
    nt = seq // ts

    def body(do_ref, yp_ref, u_ref, xr_ref, xi_ref, hr_ref, hi_ref, ar_ref, ai_ref, bmr_ref, bmi_ref,
             cmr_ref, cmi_ref, d_ref, wg_ref,
             du_ref, gr_ref, gi_ref, dyb_ref, glb_ref, dzb_ref, dar_ref, dai_ref, dd_ref,
             tab_ref, cr_ref, ci_ref):
        b, t = pl.program_id(0), pl.program_id(1)
        first = jnp.logical_and(b == 0, t == 0)

        @pl.when(t == 0)
        def _():
            for q, v in enumerate(_s5_tables(ar_ref[...], ai_ref[...], True)):
                tab_ref[q] = v
            cr_ref[...] = jnp.zeros_like(cr_ref)
            ci_ref[...] = jnp.zeros_like(ci_ref)

        yp = yp_ref[...]
        u = u_ref[...]
        gl = _gelu(yp).astype(BF16)
        glb_ref[...] = gl
        z = _dot(gl, wg_ref[0])
        za, sg = z[:, :ds], _sig(z[:, ds:])
        do = do_ref[...]
        da = (do * sg).astype(BF16)
        dg = (do * za * sg * (1.0 - sg)).astype(BF16)
        dzb_ref[:, :ds] = da
        dzb_ref[:, ds:] = dg
        dgl = _dot_nt(da, wg_ref[0, :, :ds]) + _dot_nt(dg, wg_ref[0, :, ds:])
        dyp = dgl * _gelu_grad(yp)
        dypb = dyp.astype(BF16)
        dyb_ref[...] = dypb
        _acc_out(dd_ref, jnp.sum(dyp * u, axis=0, keepdims=True), first)

        gr_ref[...] = _dot_nt(dypb, cmr_ref[...])
        gi_ref[...] = -_dot_nt(dypb, cmi_ref[...])
        _s5_scan(gr_ref, gi_ref, tab_ref, cr_ref, ci_ref, ts, True)
        gr, gi = gr_ref[...], gi_ref[...]
        du_ref[...] = d_ref[...] * dyp + _dot_nt(gr.astype(BF16), bmr_ref[...]) + _dot_nt(gi.astype(BF16), bmi_ref[...])

        row = lax.broadcasted_iota(jnp.int32, (ts, gp), 0)
        live = jnp.where(t == nt - 1, 0.0, 1.0)
        pr = jnp.broadcast_to(hr_ref[7:8, :] * live, (ts, gp))
        pi = jnp.broadcast_to(hi_ref[7:8, :] * live, (ts, gp))
        sr = jnp.where(row == 0, pr, pltpu.roll(xr_ref[...], 1, 0))
        si = jnp.where(row == 0, pi, pltpu.roll(xi_ref[...], 1, 0))
        _acc_out(dar_ref, jnp.sum(gr * sr + gi * si, axis=0, keepdims=True), first)
        _acc_out(dai_ref, jnp.sum(gi * sr - gr * si, axis=0, keepdims=True), first)

    full = lambda shape: pl.BlockSpec(shape, lambda b, t: (0, 0))
    blk = lambda w, col=0: pl.BlockSpec((ts, w), lambda b, t: (b * nt + nt - 1 - t, col))
    halo = pl.BlockSpec((8, gp), lambda b, t: (jnp.maximum((b * seq + (nt - 1 - t) * ts) // 8 - 1, 0), 0))
    return _call(
        body,
        name="s5_bwd",
        grid=(n // seq, nt),
        in_specs=[
            blk(ds), blk(ds), blk(ds, ucol), blk(gp), blk(gp), halo, halo,
            full((1, gp)), full((1, gp)), full((ds, gp)), full((ds, gp)), full((gp, ds)), full((gp, ds)),
            full((1, ds)), pl.BlockSpec((1, ds, 2 * ds), lambda b, t: (layer, 0, 0)),
        ],
        out_specs=[
            blk(ds), blk(gp), blk(gp), blk(ds), blk(ds), blk(2 * ds),
            full((1, gp)), full((1, gp)), full((1, ds)),
        ],
        out_shape=[
            jax.ShapeDtypeStruct((n, ds), F32),
            jax.ShapeDtypeStruct((n, gp), F32),
            jax.ShapeDtypeStruct((n, gp), F32),
            jax.ShapeDtypeStruct((n, ds), BF16),
            jax.ShapeDtypeStruct((n, ds), BF16),
            jax.ShapeDtypeStruct((n, 2 * ds), BF16),
            jax.ShapeDtypeStruct((1, gp), F32),
            jax.ShapeDtypeStruct((1, gp), F32),
            jax.ShapeDtypeStruct((1, ds), F32),
        ],
        scratch_shapes=[pltpu.VMEM((8, 8, gp), F32), pltpu.VMEM((8, gp), F32), pltpu.VMEM((8, gp), F32)],
        compiler_params=_params(),
    )(dout, ypre, proj, xr, xi, xr, xi, ar, ai, bm_r, bm_i, cm_r, cm_i, dskip, w_glu)


BAND = QBLK + CHUNK
NCH = QBLK // CHUNK


def _attn_specs(nq):
    cur = pl.BlockSpec((1, QBLK, HEAD), lambda h, b, i: (h, b * nq + i, 0))
    prev = pl.BlockSpec((1, QBLK, HEAD), lambda h, b, i: (h, b * nq + jnp.maximum(i - 1, 0), 0))
    vec = pl.BlockSpec((1, 1, 2 * QBLK), lambda h, b, i: (h, 0, 0))
    gain = pl.BlockSpec((1, HEAD), lambda h, b, i: (0, 0))
    return cur, prev, vec, gain


def _attn_build_table(tv, bias_ref, tab_ref):
    w = 2 * QBLK
    for qi in range(CHUNK):
        bias_ref[qi:qi + 1, :] = pltpu.roll(tv, (qi - (CHUNK - 1)) % w, 1)
    bias = bias_ref[...]
    lane = lax.broadcasted_iota(jnp.int32, (CHUNK, w), 1)
    for c in range(NCH):
        rolled = bias if c == 0 else pltpu.roll(bias, CHUNK * c, 1)
        ok = jnp.logical_and(lane >= CHUNK * c, lane < CHUNK * c + BAND)
        tab_ref[CHUNK * c:CHUNK * (c + 1), :] = jnp.where(ok, rolled, NEG)


def _attn_reduce_table(dt_ref, bias_ref):
    w = 2 * QBLK
    acc = dt_ref[0:CHUNK, :]
    for c in range(1, NCH):
        acc = acc + pltpu.roll(dt_ref[CHUNK * c:CHUNK * (c + 1), :], w - CHUNK * c, 1)
    bias_ref[...] = acc
    out = jnp.zeros((1, w), F32)
    for qi in range(CHUNK):
        out = out + pltpu.roll(bias_ref[qi:qi + 1, :], ((CHUNK - 1) - qi) % w, 1)
    return out


def _attn_probs(q, kp, kc, gq, gk, table, first_block):
    qn, qhat, qrs = _rms_fwd(q, gq)
    k = jnp.concatenate([kp, kc], axis=0)
    kn, khat, krs = _rms_fwd(k, gk)
    s = _dot_nt(qn.astype(BF16), kn.astype(BF16)) * (HEAD ** -0.5) + table
    col = lax.broadcasted_iota(jnp.int32, s.shape, 1)
    s = jnp.where(jnp.logical_and(first_block, col < QBLK), NEG, s)
    e = jnp.exp(s - jnp.max(s, axis=-1, keepdims=True))
    p = e * (1.0 / jnp.sum(e, axis=-1, keepdims=True))
    return p, (qn, qhat, qrs), (kn, khat, krs)


def attn_fwd(q, k, v, tv, gq, gk, seq, comm=None):
    h, n, _ = q.shape
    nq = seq // QBLK
    cur, prev, vec, gain = _attn_specs(nq)

    def body(q_ref, kp_ref, kc_ref, vp_ref, vc_ref, tv_ref, gq_ref, gk_ref, o_ref, bias_ref, tab_ref):
        @pl.when(jnp.logical_and(pl.program_id(1) == 0, pl.program_id(2) == 0))
        def _():
            _attn_build_table(tv_ref[0], bias_ref, tab_ref)

        p, _, _ = _attn_probs(q_ref[0], kp_ref[0], kc_ref[0], gq_ref[...], gk_ref[...], tab_ref[...],
                              pl.program_id(2) == 0)
        vv = jnp.concatenate([vp_ref[0], vc_ref[0]], axis=0).astype(BF16)
        o_ref[0] = _dot(p.astype(BF16), vv)

    return _hosted_call(
        body, comm,
        name="attn_fwd",
        grid=(h, n // seq, nq),
        in_specs=[cur, prev, cur, prev, cur, vec, gain, gain],
        out_specs=[cur],
        out_shape=[jax.ShapeDtypeStruct((h, n, HEAD), F32)],
        scratch_shapes=[pltpu.VMEM((CHUNK, 2 * QBLK), F32), pltpu.VMEM((QBLK, 2 * QBLK), F32)],
        args=(q, k, k, v, v, tv, gq, gk),
    )


def attn_bwd(do, q, k, v, tv, gq, gk, seq, comm=None):
    h, n, _ = q.shape
    nb = n // seq
    nq = seq // QBLK
    cur, prev, vec, gain = _attn_specs(nq)

    def body(do_ref, q_ref, kp_ref, kc_ref, vp_ref, vc_ref, tv_ref, gq_ref, gk_ref,
             dq_ref, dkp_ref, dkc_ref, dvp_ref, dvc_ref, dtv_ref, dgq_ref, dgk_ref, bias_ref, tab_ref, dt_ref):
        hh, b, i = pl.program_id(0), pl.program_id(1), pl.program_id(2)
        head_start = jnp.logical_and(b == 0, i == 0)

        @pl.when(head_start)
        def _():
            _attn_build_table(tv_ref[0], bias_ref, tab_ref)

        gq_, gk_ = gq_ref[...], gk_ref[...]
        p, (qn, qhat, qrs), (kn, khat, krs) = _attn_probs(
            q_ref[0], kp_ref[0], kc_ref[0], gq_, gk_, tab_ref[...], i == 0)
        dob = do_ref[0].astype(BF16)
        vv = jnp.concatenate([vp_ref[0], vc_ref[0]], axis=0).astype(BF16)
        dv = _dot_tn(p.astype(BF16), dob)
        dp = _dot_nt(dob, vv)
        ds = p * (dp - jnp.sum(p * dp, axis=-1, keepdims=True))
        _acc_out(dt_ref, ds, head_start)
        dsb = (ds * (HEAD ** -0.5)).astype(BF16)
        dqn = _dot(dsb, kn.astype(BF16))
        dkn = _dot_tn(dsb, qn.astype(BF16))
        dq, dgq_rows = _rms_bwd(dqn, qhat, qrs, gq_)
        dk, dgk_rows = _rms_bwd(dkn, khat, krs, gk_)
        dq_ref[0] = dq
        dkp_ref[0] = dk[:QBLK]
        dkc_ref[0] = dk[QBLK:]
        dvp_ref[0] = dv[:QBLK]
        dvc_ref[0] = dv[QBLK:]
        first = jnp.logical_and(hh == 0, head_start)
        _acc_out(dgq_ref, jnp.sum(dgq_rows, axis=0, keepdims=True), first)
        _acc_out(dgk_ref, jnp.sum(dgk_rows, axis=0, keepdims=True), first)

        @pl.when(jnp.logical_and(b == nb - 1, i == nq - 1))
        def _():
            dtv_ref[0] = _attn_reduce_table(dt_ref, bias_ref)

    hm = jax.ShapeDtypeStruct((h, n, HEAD), F32)
    return _hosted_call(
        body, comm,
        name="attn_bwd",
        grid=(h, nb, nq),
        in_specs=[cur, cur, prev, cur, prev, cur, vec, gain, gain],
        out_specs=[cur, cur, cur, cur, cur, vec, gain, gain],
        out_shape=[hm, hm, hm, hm, hm, jax.ShapeDtypeStruct(tv.shape, F32),
                   jax.ShapeDtypeStruct((1, HEAD), F32), jax.ShapeDtypeStruct((1, HEAD), F32)],
        scratch_shapes=[pltpu.VMEM((CHUNK, 2 * QBLK), F32), pltpu.VMEM((QBLK, 2 * QBLK), F32),
                        pltpu.VMEM((QBLK, 2 * QBLK), F32)],
        args=(do, q, k, k, v, v, tv, gq, gk),
    )


CONV_TC = 512


def _ln_fwd(h1, g, b):
    mu = jnp.mean(h1, axis=-1, keepdims=True)
    xc = h1 - mu
    rs = lax.rsqrt(jnp.mean(xc * xc, axis=-1, keepdims=True) + EPS)
    yhat = xc * rs
    return yhat * g + b, yhat, rs


def _glu(z, dc):
    return z[:, :dc] * _sig(z[:, dc:])


def conv_fwd(proj, zcol, seq, w, bdw, lng, lnb):
    n = proj.shape[0]
    dc = w.shape[1]
    tc = min(CONV_TC, seq)
    nt = seq // tc

    def body(z_ref, zp_ref, w_ref, b_ref, g_ref, lb_ref, h1_ref, o_ref, ext_ref):
        live = jnp.where(pl.program_id(1) == 0, 0.0, 1.0)
        ext_ref[pl.ds(0, HALO), :] = _glu(zp_ref[...], dc) * live
        ext_ref[pl.ds(HALO, tc), :] = _glu(z_ref[...], dc)
        acc = jnp.zeros((tc, dc), F32) + b_ref[...]
        for j in range(CONV_W):
            acc = acc + w_ref[j:j + 1, :] * ext_ref[pl.ds(HALO - (CONV_W - 1) + j, tc), :]
        h1_ref[...] = acc
        ln, _, _ = _ln_fwd(acc, g_ref[...], lb_ref[...])
        o_ref[...] = ln * _sig(ln)

    full = lambda shape: pl.BlockSpec(shape, lambda b, t: (0, 0))
    return _call(
        body,
        name="conv_fwd",
        grid=(n // seq, nt),
        in_specs=[
            pl.BlockSpec((tc, 2 * dc), lambda b, t: (b * nt + t, zcol)),
            pl.BlockSpec((HALO, 2 * dc), lambda b, t: (jnp.maximum((b * seq + t * tc) // HALO - 1, 0), zcol)),
            full((HALO, dc)), full((1, dc)), full((1, dc)), full((1, dc)),
        ],
        out_specs=[
            pl.BlockSpec((tc, dc), lambda b, t: (b * nt + t, 0)),
            pl.BlockSpec((tc, dc), lambda b, t: (b * nt + t, 0)),
        ],
        out_shape=[jax.ShapeDtypeStruct((n, dc), F32), jax.ShapeDtypeStruct((n, dc), F32)],
        scratch_shapes=[pltpu.VMEM((tc + HALO, dc), F32)],
        compiler_params=_params(),
    )(proj, proj, w, bdw, lng, lnb)


def conv_bwd(dco, h1, proj, zcol, seq, w, bdw, lng, lnb):
    n = proj.shape[0]
    dc = w.shape[1]
    tc = min(CONV_TC, seq)
    nt = seq // tc
    nrow = n // HALO

    def body(do_ref, don_ref, h1_ref, h1n_ref, z_ref, zp_ref, w_ref, g_ref, lb_ref,
             dz_ref, dw_ref, db_ref, dg_ref, dlb_ref, ext_ref, dext_ref):
        b, t = pl.program_id(0), pl.program_id(1)
        first = jnp.logical_and(b == 0, t == 0)
        g, lb = g_ref[...], lb_ref[...]

        def dh1_of(do, h1):
            ln, yhat, rs = _ln_fwd(h1, g, lb)
            s = _sig(ln)
            dln = do * (s * (1.0 + ln * (1.0 - s)))
            dyh = dln * g
            dh1 = rs * (dyh - jnp.mean(dyh, axis=-1, keepdims=True)
                        - yhat * jnp.mean(dyh * yhat, axis=-1, keepdims=True))
            return dh1, dln, yhat

        dh1, dln, yhat = dh1_of(do_ref[...], h1_ref[...])
        dh1n, _, _ = dh1_of(don_ref[...], h1n_ref[...])
        _acc_out(dg_ref, jnp.sum(dln * yhat, axis=0, keepdims=True), first)
        _acc_out(dlb_ref, jnp.sum(dln, axis=0, keepdims=True), first)
        _acc_out(db_ref, jnp.sum(dh1, axis=0, keepdims=True), first)

        dext_ref[pl.ds(0, tc), :] = dh1
        dext_ref[pl.ds(tc, HALO), :] = dh1n * jnp.where(t == nt - 1, 0.0, 1.0)
        z = z_ref[...]
        ext_ref[pl.ds(0, HALO), :] = _glu(zp_ref[...], dc) * jnp.where(t == 0, 0.0, 1.0)
        ext_ref[pl.ds(HALO, tc), :] = _glu(z, dc)

        @pl.when(first)
        def _():
            dw_ref[...] = jnp.zeros_like(dw_ref)

        dh0 = jnp.zeros((tc, dc), F32)
        for j in range(CONV_W):
            dh0 = dh0 + w_ref[j:j + 1, :] * dext_ref[pl.ds(CONV_W - 1 - j, tc), :]
            dw_ref[j:j + 1, :] += jnp.sum(dh1 * ext_ref[pl.ds(HALO - (CONV_W - 1) + j, tc), :],
                                          axis=0, keepdims=True)
        za, sg = z[:, :dc], _sig(z[:, dc:])
        dz_ref[:, :dc] = dh0 * sg
        dz_ref[:, dc:] = dh0 * za * sg * (1.0 - sg)

    full = lambda shape: pl.BlockSpec(shape, lambda b, t: (0, 0))
    cur = lambda wd, col=0: pl.BlockSpec((tc, wd), lambda b, t: (b * nt + t, col))
    nxt = pl.BlockSpec((HALO, dc), lambda b, t: (jnp.minimum((b * seq + (t + 1) * tc) // HALO, nrow - 1), 0))
    return _call(
        body,
        name="conv_bwd",
        grid=(n // seq, nt),
        in_specs=[
            cur(dc), nxt, cur(dc), nxt, cur(2 * dc, zcol),
            pl.BlockSpec((HALO, 2 * dc), lambda b, t: (jnp.maximum((b * seq + t * tc) // HALO - 1, 0), zcol)),
            full((HALO, dc)), full((1, dc)), full((1, dc)),
        ],
        out_specs=[cur(2 * dc), full((HALO, dc)), full((1, dc)), full((1, dc)), full((1, dc))],
        out_shape=[
            jax.ShapeDtypeStruct((n, 2 * dc), F32),
            jax.ShapeDtypeStruct((HALO, dc), F32),
            jax.ShapeDtypeStruct((1, dc), F32),
            jax.ShapeDtypeStruct((1, dc), F32),
            jax.ShapeDtypeStruct((1, dc), F32),
        ],
        scratch_shapes=[pltpu.VMEM((tc + HALO, dc), F32), pltpu.VMEM((tc + HALO, dc), F32)],
        compiler_params=_params(),
    )(dco, dco, h1, h1, proj, proj, w, lng, lnb)


def _merge_common(l0, l1, l2, bg, so, ao, co, wbs, wba, wbc, d):
    ys = _dot(so.astype(BF16), wbs)
    ya = _dot(ao.astype(BF16), wba)
    yc = _dot(co.astype(BF16), wbc)
    gs = _sig(l0 + bg[:, :d])
    ga = _sig(l1 + bg[:, d:2 * d])
    gc = _sig(l2 + bg[:, 2 * d:])
    return (ys, ya, yc), (gs, ga, gc)


def _merge_specs(tm, d, dss, da, dc, layer):
    row = lambda w, col=0: pl.BlockSpec((tm, w), lambda i: (i, col))
    full = lambda r, c: pl.BlockSpec((r, c), lambda i: (0, 0))
    stacked = lambda r: pl.BlockSpec((1, r, d), lambda i: (layer, 0, 0))
    acts = [row(d, 0), row(d, 1), row(d, 2), full(1, 3 * d), row(dss), row(da), row(dc)]
    weights = [stacked(dss), stacked(da), stacked(dc), stacked(d)]
    return row, full, acts, weights


def merge_fwd(x, proj, bg, so, ao, co, wbs, wba, wbc, wout, layer):
    n, d = x.shape
    tm = _tile(n, 256, 8)
    row, full, acts, weights = _merge_specs(tm, d, so.shape[1], ao.shape[1], co.shape[1], layer)

    def body(x_ref, l0_ref, l1_ref, l2_ref, bg_ref, so_ref, ao_ref, co_ref,
             wbs_ref, wba_ref, wbc_ref, wo_ref, o_ref):
        (ys, ya, yc), (gs, ga, gc) = _merge_common(
            l0_ref[...], l1_ref[...], l2_ref[...], bg_ref[...], so_ref[...], ao_ref[...], co_ref[...],
            wbs_ref[0], wba_ref[0], wbc_ref[0], d)
        merged = gs * ys + ga * ya + gc * yc
        o_ref[...] = x_ref[...] + _dot(merged.astype(BF16), wo_ref[0])

    return _call(
        body,
        name="merge_fwd",
        grid=(n // tm,),
        in_specs=[row(d)] + acts + weights,
        out_specs=row(d),
        out_shape=jax.ShapeDtypeStruct((n, d), F32),
        compiler_params=_params(),
    )(x, proj, proj, proj, bg, so, ao, co, wbs, wba, wbc, wout)


def merge_bwd(dx, proj, bg, so, ao, co, wbs, wba, wbc, wout, layer):
    n, d = dx.shape
    dss, da, dc = so.shape[1], ao.shape[1], co.shape[1]
    tm = _tile(n, 256, 8)
    row, full, acts, weights = _merge_specs(tm, d, dss, da, dc, layer)

    def body(dx_ref, l0_ref, l1_ref, l2_ref, bg_ref, so_ref, ao_ref, co_ref, wbs_ref, wba_ref, wbc_ref, wo_ref,
             dl_ref, dso_ref, dao_ref, dco_ref, dbg_ref, mg_ref, dxb_ref, dys_ref, dya_ref, dyc_ref):
        wbs, wba, wbc = wbs_ref[0], wba_ref[0], wbc_ref[0]
        (ys, ya, yc), (gs, ga, gc) = _merge_common(
            l0_ref[...], l1_ref[...], l2_ref[...], bg_ref[...], so_ref[...], ao_ref[...], co_ref[...],
            wbs, wba, wbc, d)
        mg_ref[...] = (gs * ys + ga * ya + gc * yc).astype(BF16)
        dxb = dx_ref[...].astype(BF16)
        dxb_ref[...] = dxb
        dm = _dot_nt(dxb, wo_ref[0])
        first = pl.program_id(0) == 0
        for k, (y, g, w, dy_ref, db_ref) in enumerate((
                (ys, gs, wbs, dys_ref, dso_ref), (ya, ga, wba, dya_ref, dao_ref), (yc, gc, wbc, dyc_ref, dco_ref))):
            dl = dm * y * g * (1.0 - g)
            dl_ref[:, k * d:(k + 1) * d] = dl.astype(BF16)
            _acc_out(dbg_ref.at[:, k * d:(k + 1) * d], jnp.sum(dl, axis=0, keepdims=True), first)
            dy = (dm * g).astype(BF16)
            dy_ref[...] = dy
            db_ref[...] = _dot_nt(dy, w)

    bf = lambda w: jax.ShapeDtypeStruct((n, w), BF16)
    return _call(
        body,
        name="merge_bwd",
        grid=(n // tm,),
        in_specs=[row(d)] + acts + weights,
        out_specs=[row(3 * d), row(dss), row(da), row(dc), full(1, 3 * d),
                   row(d), row(d), row(d), row(d), row(d)],
        out_shape=[bf(3 * d), jax.ShapeDtypeStruct((n, dss), F32), jax.ShapeDtypeStruct((n, da), F32),
                   jax.ShapeDtypeStruct((n, dc), F32), jax.ShapeDtypeStruct((1, 3 * d), F32),
                   bf(d), bf(d), bf(d), bf(d), bf(d)],
        compiler_params=_params(),
    )(dx, proj, proj, proj, bg, so, ao, co, wbs, wba, wbc, wout)


def loss_head(y, target):
    n, d = y.shape
    tm = _tile(n, 512, 8)

    def body(y_ref, t_ref, dy_ref, l_ref):
        e = y_ref[...] - t_ref[...]
        dy_ref[...] = e * (1.0 / d)
        part = 0.5 * jnp.sum(jnp.sum(e * e, axis=-1, keepdims=True) * (1.0 / d), axis=0, keepdims=True)
        _acc_out(l_ref, part, pl.program_id(0) == 0)

    return _call(
        body,
        name="loss_head",
        grid=(n // tm,),
        in_specs=[pl.BlockSpec((tm, d), lambda i: (i, 0)), pl.BlockSpec((tm, d), lambda i: (i, 0))],
        out_specs=[pl.BlockSpec((tm, d), lambda i: (i, 0)), pl.BlockSpec((1, 1), lambda i: (0, 0))],
        out_shape=[jax.ShapeDtypeStruct((n, d), F32), jax.ShapeDtypeStruct((1, 1), F32)],
        compiler_params=_params(),
    )(y, target)


def _mesh_pos():
    return lax.axis_index("x"), lax.axis_index("y"), lax.axis_index("c")


ANY = pl.BlockSpec(memory_space=pl.ANY)


def _comm_sems(na):
    return [pltpu.SemaphoreType.DMA((na, 7)), pltpu.SemaphoreType.DMA((na, 7)), pltpu.SemaphoreType.DMA((na,))]


def _gather_plan(x_refs, out_refs, sems):
    na = len(x_refs)
    send_sems, recv_sems, local_sems = sems
    x, y, c = _mesh_pos()
    me, sibling = (x, y, c), (x, y, 1 - c)
    chips = [(1 - x, y), (x, 1 - y), (1 - x, 1 - y)]

    def slot(a, px, py, pc):
        return out_refs[a].at[4 * px + 2 * py + pc]

    def copy(a, k, block, to, src=None):
        return pltpu.make_async_remote_copy(
            src_ref=slot(a, *block) if src is None else src, dst_ref=slot(a, *block),
            send_sem=send_sems.at[a, k], recv_sem=recv_sems.at[a, k],
            device_id=to, device_id_type=pl.DeviceIdType.MESH)

    mine = [pltpu.make_async_copy(x_refs[a], slot(a, *me), local_sems.at[a]) for a in range(na)]
    first = []
    for a in range(na):
        first.append(copy(a, 0, me, sibling, src=x_refs[a]))
        first += [copy(a, 1 + j, me, (*chip, c), src=x_refs[a]) for j, chip in enumerate(chips)]

    def start():
        for cp in mine + first:
            cp.start()

    def finish():
        passed = []
        for j, chip in enumerate(chips):
            for a in range(na):
                copy(a, 1 + j, (*chip, c), me).wait_recv()
                fwd = copy(a, 4 + j, (*chip, c), sibling)
                fwd.start()
                passed.append(fwd)
        for a in range(na):
            copy(a, 0, sibling, me).wait_recv()
            for j, chip in enumerate(chips):
                copy(a, 4 + j, (*chip, 1 - c), me).wait_recv()
        for cp in first + passed:
            cp.wait_send()
        for cp in mine:
            cp.wait()

    return start, finish


def _gather_out(shards):
    return [jax.ShapeDtypeStruct((N_DEV,) + s.shape, s.dtype) for s in shards]


def all_gather(shards):
    na = len(shards)

    def body(*refs):
        start, finish = _gather_plan(refs[:na], refs[na:2 * na], refs[2 * na:])
        start()
        finish()

    return _call(
        body,
        name="all_gather",
        out_shape=_gather_out(shards),
        in_specs=[ANY] * na,
        out_specs=[ANY] * na,
        scratch_shapes=_comm_sems(na),
    )(*shards)


def _hosted_call(body, comm, *, name, grid, in_specs, out_specs, out_shape, scratch_shapes, args):
    if comm is None:
        res = _call(body, name=name, grid=grid, in_specs=in_specs, out_specs=out_specs, out_shape=out_shape,
                    scratch_shapes=scratch_shapes, compiler_params=_params())(*args)
        return res, []
    plan, arrays, c_out = comm
    n_in, n_out, n_scr, ci, co = len(in_specs), len(out_specs), len(scratch_shapes), len(arrays), len(c_out)

    def hosted(*refs):
        ins, cins = refs[:n_in], refs[n_in:n_in + ci]
        o0 = n_in + ci
        outs, couts = refs[o0:o0 + n_out], refs[o0 + n_out:o0 + n_out + co]
        s0 = o0 + n_out + co
        scr, sems = refs[s0:s0 + n_scr], refs[s0 + n_scr:]
        ids = [pl.program_id(ax) for ax in range(len(grid))]
        first = functools.reduce(jnp.logical_and, [i == 0 for i in ids])
        last = functools.reduce(jnp.logical_and, [i == g - 1 for i, g in zip(ids, grid)])
        start, finish = plan(cins, couts, sems)
        pl.when(first)(start)
        body(*ins, *outs, *scr)
        pl.when(last)(finish)

    res = _call(hosted, name=name + "_comm", grid=grid, in_specs=list(in_specs) + [ANY] * ci,
                out_specs=list(out_specs) + [ANY] * co, out_shape=list(out_shape) + list(c_out),
                scratch_shapes=list(scratch_shapes) + _comm_sems(max(ci, co)),
                compiler_params=_params())(*args, *arrays)
    return res[:n_out], res[n_out:]


def _exchange_plan(bcast=()):
    def plan(s_refs, r_refs, sems):
        na = len(s_refs)
        send_sems, recv_sems, local_sems = sems
        x, y, c = _mesh_pos()
        me = 4 * x + 2 * y + c

        def peer(k):
            px = (1 - x) if k & 4 else x
            py = (1 - y) if k & 2 else y
            pc = (1 - c) if k & 1 else c
            return (px, py, pc), 4 * px + 2 * py + pc

        def src_dst(a, pid, slot):
            return (s_refs[a] if a in bcast else s_refs[a].at[pid]), r_refs[a].at[slot]

        def copy(a, k):
            to, pid = peer(k)
            src, dst = src_dst(a, pid, me)
            return pltpu.make_async_remote_copy(
                src_ref=src, dst_ref=dst, send_sem=send_sems.at[a, k - 1], recv_sem=recv_sems.at[a, k - 1],
                device_id=to, device_id_type=pl.DeviceIdType.MESH)

        def arrival(a, k):
            _, pid = peer(k)
            src, dst = src_dst(a, pid, pid)
            return pltpu.make_async_remote_copy(
                src_ref=src, dst_ref=dst, send_sem=send_sems.at[a, k - 1], recv_sem=recv_sems.at[a, k - 1],
                device_id=(x, y, c), device_id_type=pl.DeviceIdType.MESH)

        mine = [pltpu.make_async_copy(*src_dst(a, me, me), local_sems.at[a]) for a in range(na)]
        sends = [copy(a, k) for k in range(1, N_DEV) for a in range(na)]

        def start():
            for cp in mine + sends:
                cp.start()

        def finish():
            for k in range(1, N_DEV):
                for a in range(na):
                    arrival(a, k).wait_recv()
            for cp in sends:
                cp.wait_send()
            for cp in mine:
                cp.wait()

        return start, finish

    return plan


def _exchange_out(slabs, bcast=()):
    return [jax.ShapeDtypeStruct(((N_DEV,) + s.shape) if a in bcast else s.shape, s.dtype)
            for a, s in enumerate(slabs)]


def grad_exchange(slabs, small):
    arrays = list(slabs) + [small]
    na = len(arrays)
    bcast = (na - 1,)

    def body(*refs):
        start, finish = _exchange_plan(bcast)(refs[:na], refs[na:2 * na], refs[2 * na:])
        start()
        finish()

    return _call(
        body,
        name="grad_exchange",
        out_shape=_exchange_out(arrays, bcast),
        in_specs=[ANY] * na,
        out_specs=[ANY] * na,
        scratch_shapes=_comm_sems(na),
    )(*arrays)


ADAM_BLOCK = 256 * 1024


def sum_adamw(recv, w, m, v, layer, prev, name):
    depth, rows, cols = w.shape
    tr = _tile(rows, max(8, ADAM_BLOCK // cols // 8 * 8), 8)
    c1 = 1.0 / (1.0 - ADAM_B1 ** ADAM_STEP)
    c2 = 1.0 / (1.0 - ADAM_B2 ** ADAM_STEP)

    def body(r_ref, w_ref, m_ref, v_ref, *rest):
        g_ref, d_ref, mo_ref, vo_ref = rest[-4:]
        g = r_ref[0].astype(F32)
        for s in range(1, N_DEV):
            g = g + r_ref[s].astype(F32)
        mn = ADAM_B1 * m_ref[0] + (1.0 - ADAM_B1) * g
        vn = ADAM_B2 * v_ref[0] + (1.0 - ADAM_B2) * (g * g)
        g_ref[0] = g
        mo_ref[0] = mn
        vo_ref[0] = vn
        d_ref[0] = -ADAM_LR * ((mn * c1) / (jnp.sqrt(vn * c2) + ADAM_EPS) + ADAM_WD * w_ref[0])

    blk = pl.BlockSpec((1, tr, cols), lambda i: (layer, i, 0))
    out = jax.ShapeDtypeStruct((depth, rows, cols), F32)
    in_specs = [pl.BlockSpec((N_DEV, tr, cols), lambda i: (0, i, 0)), blk, blk, blk]
    args = [recv, w, m, v]
    extra = {}
    if prev is not None:
        in_specs += [ANY] * 4
        args += list(prev)
        extra["input_output_aliases"] = {4 + j: j for j in range(4)}
    return _call(
        body,
        name=name,
        grid=(rows // tr,),
        in_specs=in_specs,
        out_specs=[blk, blk, blk, blk],
        out_shape=[out, out, out, out],
        compiler_params=_params(),
        **extra,
    )(*args)


def _attn_bias_vector(rel_bias):
    h = rel_bias.shape[0]
    n_far = BAND - MAX_REL
    n_near = BAND + CHUNK - 1 - n_far
    far = jnp.broadcast_to(rel_bias[:, 2 * MAX_REL:], (h, n_far))
    near = rel_bias[:, 2 * MAX_REL - n_near:2 * MAX_REL][:, ::-1]
    pad = jnp.zeros((h, 2 * QBLK - n_far - n_near), F32)
    return jnp.concatenate([far, near, pad], axis=1)[:, None, :]


def _s5_prepare(lre, lim, ldt, bre, bim, cre, cim):
    g, p = lre.shape
    lr = jnp.minimum(lre, -1e-4)
    dt = jnp.exp(ldt)[:, None]
    mag = jnp.exp(lr * dt)
    ar = mag * jnp.cos(lim * dt)
    ai = mag * jnp.sin(lim * dt)
    den = lr * lr + lim * lim
    coef_r = ((ar - 1.0) * lr + ai * lim) / den
    coef_i = (ai * lr - (ar - 1.0) * lim) / den
    bbar_r = coef_r[..., None] * bre - coef_i[..., None] * bim
    bbar_i = coef_r[..., None] * bim + coef_i[..., None] * bre
    eye = jnp.eye(g, dtype=F32)
    bd_in = lambda b: jnp.einsum("gpc,gh->gchp", b, eye).reshape(g * S5_GROUP, g * p)
    bd_out = lambda c: jnp.einsum("gcp,gh->gphc", c, eye).reshape(g * p, g * S5_GROUP)
    return (ar.reshape(1, g * p), ai.reshape(1, g * p), bd_in(bbar_r), bd_in(bbar_i), bd_out(cre), bd_out(cim))


SHARDED = ("ffn1_w_up", "ffn1_w_down", "w_in", "s5_w_glu", "w_br_s5", "w_br_attn", "conv_w_dw", "w_br_conv",
           "w_out", "ffn2_w_up", "ffn2_w_down")
WEIGHTS = ("ffn1_norm", "ffn1_w_up", "ffn1_w_down", "mix_norm", "w_in", "b_gate", "s5_lambda_re", "s5_lambda_im",
           "s5_log_dt", "s5_b_re", "s5_b_im", "s5_c_re", "s5_c_im", "s5_d", "s5_w_glu", "w_br_s5", "attn_q_gain",
           "attn_k_gain", "attn_rel_bias", "w_br_attn", "conv_w_dw", "conv_b_dw", "conv_ln_g", "conv_ln_b",
           "w_br_conv", "w_out", "ffn2_norm", "ffn2_w_up", "ffn2_w_down")
SMALL = tuple(nm for nm in WEIGHTS if nm not in SHARDED)
SMALL_LANES = 1024


def _cols_full(g):
    _, depth, k, nn = g.shape
    return g.transpose(1, 2, 0, 3).reshape(depth, k, N_DEV * nn)


def _rows_full(g):
    _, depth, r, cc = g.shape
    return g.transpose(1, 0, 2, 3).reshape(depth, N_DEV * r, cc)


def _cols_slabs(gfull):
    depth, k, c8 = gfull.shape
    return gfull.reshape(depth, k, N_DEV, c8 // N_DEV).transpose(0, 2, 1, 3)


def _heads(a, h):
    n = a.shape[0]
    return a.reshape(n, h, HEAD).transpose(1, 0, 2)


def _unheads(a):
    h, n, _ = a.shape
    return a.transpose(1, 0, 2).reshape(n, h * HEAD)


def _small_pack(t):
    flat = jnp.concatenate([t[nm].reshape(-1) for nm in SMALL])
    rows = -(-flat.shape[0] // SMALL_LANES)
    rows = -(-rows // 8) * 8
    return jnp.pad(flat, (0, rows * SMALL_LANES - flat.shape[0])).reshape(rows, SMALL_LANES)


def _small_unpack(flat, like):
    flat = flat.reshape(-1)
    out, off = {}, 0
    for nm in SMALL:
        out[nm] = flat[off:off + like[nm].size].reshape(like[nm].shape)
        off += like[nm].size
    return out


def _step(x, target, w, m, v):
    bsz, seq, d = x.shape
    n = bsz * seq
    depth = w["ffn1_norm"].shape[0]
    da, dss, dc = d // 2, d // 4, d // 4
    heads = da // HEAD
    gp = dss // S5_GROUP * S5_STATE
    mid = 3 * da + 2 * dc
    q0, k0, v0, z0, u0 = 3 * d, 3 * d + da, 3 * d + 2 * da, 3 * d + 3 * da, 3 * d + mid
    to_kernel_cols = lambda a: jnp.concatenate([a[..., dss + mid:], a[..., dss:dss + mid], a[..., :dss]], axis=-1)
    to_ref_cols = lambda a: jnp.concatenate([a[..., 3 * d + mid:], a[..., 3 * d:3 * d + mid], a[..., :3 * d]], axis=-1)

    shard = lambda nm: w[nm] if nm == "conv_w_dw" else w[nm].astype(BF16)
    mixer_names = ("w_in", "s5_w_glu", "w_br_s5", "w_br_attn", "w_br_conv", "conv_w_dw", "w_out")
    gather_of = lambda names: (_gather_plan, [shard(nm) for nm in names], _gather_out([shard(nm) for nm in names]))
    full = {}

    def take(names, arrays):
        for nm, g in zip(names, arrays):
            if nm in ("ffn1_w_up", "ffn2_w_up"):
                full[nm] = g
            elif nm in ("ffn1_w_down", "ffn2_w_down", "w_out"):
                full[nm] = _rows_full(g)
            elif nm == "w_in":
                full[nm] = to_kernel_cols(_cols_full(g))
            else:
                full[nm] = _cols_full(g)

    take(("ffn1_w_up", "ffn1_w_down"), all_gather([shard("ffn1_w_up"), shard("ffn1_w_down")]))
    nff = full["ffn1_w_up"].shape[3]

    row = lambda a: a.reshape(1, -1)
    saved = []
    xin = x.reshape(n, d)
    for l in range(depth):
        s = {"x0": xin}
        (s["x1"], s["a1"], s["b1"]), got = ffn_fwd(xin, row(w["ffn1_norm"][l]), full["ffn1_w_up"], full["ffn1_w_down"], l,
                                                   comm=gather_of(mixer_names) if l == 0 else None)
        if l == 0:
            take(mixer_names, got)
            conv_w = jnp.pad(full["conv_w_dw"], ((0, 0), (0, HALO - CONV_W), (0, 0)))
        proj = proj_fwd(s["x1"], row(w["mix_norm"][l]), full["w_in"], l)
        s["proj"] = proj
        prep_in = (w["s5_lambda_re"][l], w["s5_lambda_im"][l], w["s5_log_dt"][l], w["s5_b_re"][l], w["s5_b_im"][l],
                   w["s5_c_re"][l], w["s5_c_im"][l])
        (ar, ai, bm_r, bm_i, cm_r, cm_i), s["prep_vjp"] = jax.vjp(_s5_prepare, *prep_in)
        s["s5p"] = (ar, ai, bm_r.astype(BF16), bm_i.astype(BF16), cm_r.astype(BF16), cm_i.astype(BF16),
                    row(w["s5_d"][l]), full["s5_w_glu"], l)
        s["so"], s["xr"], s["xi"], s["ypre"] = s5_fwd(proj, u0 // dss, seq, *s["s5p"])
        tv, s["tv_vjp"] = jax.vjp(_attn_bias_vector, w["attn_rel_bias"][l])
        s["qkv"] = tuple(_heads(proj[:, c0:c0 + da], heads) for c0 in (q0, k0, v0))
        s["attnp"] = (tv, row(w["attn_q_gain"][l]), row(w["attn_k_gain"][l]))
        (ao,), got = attn_fwd(*s["qkv"], *s["attnp"], seq,
                              comm=gather_of(("ffn2_w_up", "ffn2_w_down")) if l == 0 else None)
        if l == 0:
            take(("ffn2_w_up", "ffn2_w_down"), got)
        s["ao"] = _unheads(ao)
        s["convp"] = (conv_w[l], row(w["conv_b_dw"][l]), row(w["conv_ln_g"][l]), row(w["conv_ln_b"][l]))
        s["h1"], s["co"] = conv_fwd(proj, z0 // (2 * dc), seq, *s["convp"])
        s["mergep"] = (row(w["b_gate"][l]), s["so"], s["ao"], s["co"], full["w_br_s5"], full["w_br_attn"],
                       full["w_br_conv"], full["w_out"], l)
        s["x2"] = merge_fwd(s["x1"], proj, *s["mergep"])
        (xin, s["a2"], s["b2"]), _ = ffn_fwd(s["x2"], row(w["ffn2_norm"][l]), full["ffn2_w_up"], full["ffn2_w_down"], l)
        saved.append(s)

    dx, loss = loss_head(xin, target.reshape(n, d))
    loss = lax.psum(loss[0, 0], ("x", "y", "c"))

    small_g = {nm: [None] * depth for nm in SMALL}
    slab = {nm: None for nm in SHARDED}
    conv_g = [None] * depth

    recv = {}

    def exchange_of(names, l):
        arrays = [slab.pop(nm) for nm in names]
        return [(nm, l) for nm in names], (_exchange_plan(), arrays, _exchange_out(arrays))

    def ffn_grads(which, hn, dyb, dab_a, dab_b, act):
        up, down = which + "_w_up", which + "_w_down"
        up4, down4 = (1, N_DEV, d, nff), (1, FF_CHUNKS, nff, d)
        half = wgrad(hn, dab_a, "wg_ffn_up", up4, "col")
        slab[up] = wgrad(hn, dab_b, "wg_ffn_up", up4, "col", g0=FF_CHUNKS, prev=half).reshape(up4[1:])
        slab[down] = wgrad(act, dyb, "wg_ffn_down", down4, "row").reshape(N_DEV, nff // 2, d)

    pending = None
    for l in reversed(range(depth)):
        s = saved[l]
        outs, got = ffn_bwd(dx, s["x2"], row(w["ffn2_norm"][l]), s["a2"], s["b2"],
                            full["ffn2_w_up"], full["ffn2_w_down"], l, comm=pending[1] if pending else None)
        if pending:
            recv.update(zip(pending[0], got))
        dx, dg, hn, dyb, dab_a, dab_b, act = outs
        small_g["ffn2_norm"][l] = dg
        ffn_grads("ffn2", hn, dyb, dab_a, dab_b, act)

        dlog, dso, dao, dco, dbg, mg, dxb, dys, dya, dyc = merge_bwd(dx, s["proj"], *s["mergep"])
        small_g["b_gate"][l] = dbg
        slab["w_out"] = wgrad(mg, dxb, "wg_out", (1, N_DEV, d // N_DEV, d), "row").reshape(N_DEV, d // N_DEV, d)
        for nm, act_in, dy_br in (("w_br_s5", s["so"], dys), ("w_br_attn", s["ao"], dya), ("w_br_conv", s["co"], dyc)):
            k_in = act_in.shape[1]
            slab[nm] = wgrad(act_in, dy_br, "wg_" + nm, (1, N_DEV, k_in, d // N_DEV), "col").reshape(
                N_DEV, k_in, d // N_DEV)

        dz, dwdw, dbdw, dlng, dlnb = conv_bwd(dco, s["h1"], s["proj"], z0 // (2 * dc), seq, *s["convp"])
        slab["conv_w_dw"] = _cols_slabs(dwdw[None, :CONV_W])[0].astype(BF16)
        small_g["conv_b_dw"][l], small_g["conv_ln_g"][l], small_g["conv_ln_b"][l] = dbdw, dlng, dlnb

        keys, comm = exchange_of(("ffn2_w_up", "ffn2_w_down"), l)
        outs, got = attn_bwd(_heads(dao, heads), *s["qkv"], *s["attnp"], seq, comm=comm)
        recv.update(zip(keys, got))
        dq, dkp, dkc, dvp, dvc, dtv, dgq, dgk = outs
        small_g["attn_q_gain"][l], small_g["attn_k_gain"][l] = dgq, dgk
        small_g["attn_rel_bias"][l] = s["tv_vjp"](dtv)[0]

        def from_prev(cur, prev):
            prev = prev.reshape(heads, bsz, seq, HEAD)
            prev = jnp.concatenate([prev[:, :, QBLK:], jnp.zeros_like(prev[:, :, :QBLK])], axis=2)
            return cur + prev.reshape(heads, n, HEAD)

        dk, dv = from_prev(dkc, dkp), from_prev(dvc, dvp)

        du, gr, gi, dyb5, glb, dzb, dar, dai, dd = s5_bwd(dso, s["ypre"], s["proj"], u0 // dss, s["xr"], s["xi"],
                                                          seq, *s["s5p"])
        small_g["s5_d"][l] = dd
        one = lambda k1, k2: (1, 1, k1, k2)
        slab["s5_w_glu"] = _cols_slabs(wgrad(glb, dzb, "wg_s5_glu", one(dss, 2 * dss), "col")[0])[0]
        dcm_r = wgrad(s["xr"], dyb5, "wg_s5_c", one(gp, dss), "col", dtype=F32)[0, 0]
        dcm_i = -wgrad(s["xi"], dyb5, "wg_s5_c", one(gp, dss), "col", dtype=F32)[0, 0]
        dbm_r = wgrad(s["proj"], gr, "wg_s5_b", one(dss, gp), "col", a_cols=(u0, dss), dtype=F32)[0, 0]
        dbm_i = wgrad(s["proj"], gi, "wg_s5_b", one(dss, gp), "col", a_cols=(u0, dss), dtype=F32)[0, 0]
        pg = s["prep_vjp"]((dar, dai, dbm_r, dbm_i, dcm_r, dcm_i))
        for nm, gval in zip(("s5_lambda_re", "s5_lambda_im", "s5_log_dt", "s5_b_re", "s5_b_im", "s5_c_re", "s5_c_im"), pg):
            small_g[nm][l] = gval

        dproj = jnp.concatenate([dlog, _unheads(dq).astype(BF16), _unheads(dk).astype(BF16),
                                 _unheads(dv).astype(BF16), dz.astype(BF16), du.astype(BF16)], axis=1)
        dx, dgm, hn = proj_bwd(dproj, dx, s["x1"], row(w["mix_norm"][l]), full["w_in"], l)
        small_g["mix_norm"][l] = dgm
        slab["w_in"] = _cols_slabs(to_ref_cols(wgrad(hn, dproj, "wg_in", one(d, 3 * d + mid + dss), "col")[0]))[0]

        keys, comm = exchange_of(mixer_names, l)
        outs, got = ffn_bwd(dx, s["x0"], row(w["ffn1_norm"][l]), s["a1"], s["b1"],
                            full["ffn1_w_up"], full["ffn1_w_down"], l, comm=comm)
        recv.update(zip(keys, got))
        dx, dg, hn, dyb, dab_a, dab_b, act = outs
        small_g["ffn1_norm"][l] = dg
        ffn_grads("ffn1", hn, dyb, dab_a, dab_b, act)
        pending = exchange_of(("ffn1_w_up", "ffn1_w_down"), l)

    small_flat = _small_pack({nm: jnp.stack([g.reshape(w[nm].shape[1:]) for g in small_g[nm]]) for nm in SMALL})
    *got, recv_small = grad_exchange(pending[1][1], small_flat)
    recv.update(zip(pending[0], got))

    outs = {}
    for nm in SHARDED:
        shp = w[nm].shape
        as3 = lambda t: t.reshape(shp[0], -1, shp[-1])
        bufs = None
        for l in reversed(range(depth)):
            bufs = sum_adamw(recv[(nm, l)], as3(w[nm]), as3(m[nm]), as3(v[nm]), l, bufs, "adamw_" + nm)
        outs[nm] = [b.reshape(shp) for b in bufs]
    packed = sum_adamw(recv_small, _small_pack(w)[None], _small_pack(m)[None], _small_pack(v)[None], 0, None,
                       "adamw_small")
    unpacked = [_small_unpack(p, w) for p in packed]
    for nm in SMALL:
        outs[nm] = [u[nm] for u in unpacked]
    return loss, dx.reshape(x.shape), outs


def kernel(x, ffn1_norm, ffn1_w_up, ffn1_w_down, mix_norm, w_in, b_gate, s5_lambda_re, s5_lambda_im, s5_log_dt, s5_b_re, s5_b_im, s5_c_re, s5_c_im, s5_d, s5_w_glu, w_br_s5, attn_q_gain, attn_k_gain, attn_rel_bias, w_br_attn, conv_w_dw, conv_b_dw, conv_ln_g, conv_ln_b, w_br_conv, w_out, ffn2_norm, ffn2_w_up, ffn2_w_down, loss_target, m_ffn1_norm, m_ffn1_w_up, m_ffn1_w_down, m_mix_norm, m_w_in, m_b_gate, m_s5_lambda_re, m_s5_lambda_im, m_s5_log_dt, m_s5_b_re, m_s5_b_im, m_s5_c_re, m_s5_c_im, m_s5_d, m_s5_w_glu, m_w_br_s5, m_attn_q_gain, m_attn_k_gain, m_attn_rel_bias, m_w_br_attn, m_conv_w_dw, m_conv_b_dw, m_conv_ln_g, m_conv_ln_b, m_w_br_conv, m_w_out, m_ffn2_norm, m_ffn2_w_up, m_ffn2_w_down, v_ffn1_norm, v_ffn1_w_up, v_ffn1_w_down, v_mix_norm, v_w_in, v_b_gate, v_s5_lambda_re, v_s5_lambda_im, v_s5_log_dt, v_s5_b_re, v_s5_b_im, v_s5_c_re, v_s5_c_im, v_s5_d, v_s5_w_glu, v_w_br_s5, v_attn_q_gain, v_attn_k_gain, v_attn_rel_bias, v_w_br_attn, v_conv_w_dw, v_conv_b_dw, v_conv_ln_g, v_conv_ln_b, v_w_br_conv, v_w_out, v_ffn2_norm, v_ffn2_w_up, v_ffn2_w_down):
    args = locals()
    w = {nm: args[nm] for nm in WEIGHTS}
    m = {nm: args["m_" + nm] for nm in WEIGHTS}
    v = {nm: args["v_" + nm] for nm in WEIGHTS}
    loss, gx, outs = _step(x, loss_target, w, m, v)
    return (loss, gx, *[outs[nm][0] for nm in WEIGHTS], *[outs[nm][1] for nm in WEIGHTS],
            *[outs[nm][2] for nm in WEIGHTS], *[outs[nm][3] for nm in WEIGHTS])
```

```python
import functools
import math

import numpy as np
import jax
import jax.numpy as jnp
from jax import lax
from jax.experimental import pallas as pl
from jax.experimental.pallas import tpu as pltpu

F32 = jnp.float32
BF16 = jnp.bfloat16

CHUNK = 64
N_LEFT = 8
QBLK = CHUNK * N_LEFT
HEAD = 64
MAX_REL = 128
S5_GROUP = 16
S5_STATE = 64
CONV_W = 31
HALO = 32
EPS = 1e-6
NEG = -1e30
ADAM_LR, ADAM_B1, ADAM_B2, ADAM_EPS, ADAM_WD, ADAM_STEP = 0.001, 0.9, 0.999, 1e-08, 0.01, 10
N_DEV = 8
VMEM_LIMIT = 56 * 1024 * 1024


def _call(body, **kw):
    return pl.pallas_call(body, **kw)


def _params(**kw):
    return pltpu.CompilerParams(vmem_limit_bytes=VMEM_LIMIT, **kw)


def _tile(n, cap, unit=128):
    if n <= cap:
        return n
    d = (cap // unit) * unit
    while d >= unit:
        if n % d == 0:
            return d
        d -= unit
    raise ValueError(f"no tile for {n} under {cap}")


def _dot(a, b):
    return jnp.dot(a, b, preferred_element_type=F32)


def _dot_nt(a, b):
    return lax.dot_general(a, b, (((1,), (1,)), ((), ())), preferred_element_type=F32)


def _dot_tn(a, b):
    return lax.dot_general(a, b, (((0,), (0,)), ((), ())), preferred_element_type=F32)


def _sig(x):
    return 1.0 / (1.0 + jnp.exp(-x))


def _rms_fwd(x, g):
    rs = lax.rsqrt(jnp.mean(x * x, axis=-1, keepdims=True) + EPS)
    xhat = x * rs
    return xhat * g, xhat, rs


def _rms_bwd(dh, xhat, rs, g):
    dxh = dh * g
    dx = rs * (dxh - xhat * jnp.mean(dxh * xhat, axis=-1, keepdims=True))
    return dx, dh * xhat


_GELU_C = math.sqrt(2.0 / math.pi)


def _gelu(x):
    return 0.5 * x * (1.0 + jnp.tanh(_GELU_C * (x + 0.044715 * x * x * x)))


def _gelu_grad(x):
    t = jnp.tanh(_GELU_C * (x + 0.044715 * x * x * x))
    return 0.5 * (1.0 + t) + 0.5 * x * (1.0 - t * t) * _GELU_C * (1.0 + 3.0 * 0.044715 * x * x)


def _acc_out(ref, val, first):
    @pl.when(first)
    def _():
        ref[...] = val

    @pl.when(jnp.logical_not(first))
    def _():
        ref[...] += val


FF_CHUNKS = N_DEV // 2


def ffn_fwd(x, g, w_up, w_down, layer, comm=None):
    n, d = x.shape
    nn = w_up.shape[3]
    tm = _tile(n, 1024, 8)

    def body(x_ref, g_ref, wa_ref, wb_ref, wd_ref, xo_ref, a_ref, b_ref, hn_ref, acc_ref):
        j = pl.program_id(1)

        @pl.when(j == 0)
        def _():
            h, _, _ = _rms_fwd(x_ref[...], g_ref[...])
            hn_ref[...] = h.astype(BF16)
            acc_ref[...] = jnp.zeros_like(acc_ref)

        hn = hn_ref[...]
        a = _dot(hn, wa_ref[0, 0])
        b = _dot(hn, wb_ref[0, 0])
        a_ref[0] = a.astype(BF16)
        b_ref[0] = b.astype(BF16)
        act = a * _sig(a) * b
        acc_ref[...] += _dot(act.astype(BF16), wd_ref[0])

        @pl.when(j == FF_CHUNKS - 1)
        def _():
            xo_ref[...] = x_ref[...] + 0.5 * acc_ref[...]

    return _hosted_call(
        body, comm,
        name="ffn_fwd",
        grid=(n // tm, FF_CHUNKS),
        in_specs=[
            pl.BlockSpec((tm, d), lambda i, j: (i, 0)),
            pl.BlockSpec((1, d), lambda i, j: (0, 0)),
            pl.BlockSpec((1, 1, d, nn), lambda i, j: (j, layer, 0, 0)),
            pl.BlockSpec((1, 1, d, nn), lambda i, j: (j + FF_CHUNKS, layer, 0, 0)),
            pl.BlockSpec((1, nn, d), lambda i, j: (layer, j, 0)),
        ],
        out_specs=[
            pl.BlockSpec((tm, d), lambda i, j: (i, 0)),
            pl.BlockSpec((1, tm, nn), lambda i, j: (j, i, 0)),
            pl.BlockSpec((1, tm, nn), lambda i, j: (j, i, 0)),
        ],
        out_shape=[
            jax.ShapeDtypeStruct((n, d), F32),
            jax.ShapeDtypeStruct((FF_CHUNKS, n, nn), BF16),
            jax.ShapeDtypeStruct((FF_CHUNKS, n, nn), BF16),
        ],
        scratch_shapes=[pltpu.VMEM((tm, d), BF16), pltpu.VMEM((tm, d), F32)],
        args=(x, g, w_up, w_up, w_down),
    )


def ffn_bwd(dy, x, g, a, b, w_up, w_down, layer, comm=None):
    n, d = x.shape
    nn = w_up.shape[3]
    tm = _tile(n, 512, 8)

    def body(dy_ref, x_ref, g_ref, a_ref, b_ref, wa_ref, wb_ref, wd_ref,
             dx_ref, dg_ref, hn_ref, dyb_ref, da_ref, db_ref, act_ref, dyb_s, dh_ref):
        i, j = pl.program_id(0), pl.program_id(1)

        @pl.when(j == 0)
        def _():
            h, _, _ = _rms_fwd(x_ref[...], g_ref[...])
            hn_ref[...] = h.astype(BF16)
            dyb = (0.5 * dy_ref[...]).astype(BF16)
            dyb_ref[...] = dyb
            dyb_s[...] = dyb
            dh_ref[...] = jnp.zeros_like(dh_ref)

        dact = _dot_nt(dyb_s[...], wd_ref[0])
        a32 = a_ref[0].astype(F32)
        b32 = b_ref[0].astype(F32)
        s = _sig(a32)
        sil = a32 * s
        da = (dact * b32 * (s * (1.0 + a32 * (1.0 - s)))).astype(BF16)
        db = (dact * sil).astype(BF16)
        da_ref[0] = da
        db_ref[0] = db
        act_ref[0] = (sil * b32).astype(BF16)
        dh_ref[...] += _dot_nt(da, wa_ref[0, 0]) + _dot_nt(db, wb_ref[0, 0])

        @pl.when(j == FF_CHUNKS - 1)
        def _():
            gg = g_ref[...]
            _, xhat, rs = _rms_fwd(x_ref[...], gg)
            dxn, dgr = _rms_bwd(dh_ref[...], xhat, rs, gg)
            dx_ref[...] = dy_ref[...] + dxn
            _acc_out(dg_ref, jnp.sum(dgr, axis=0, keepdims=True), i == 0)

    tok = pl.BlockSpec((tm, d), lambda i, j: (i, 0))
    chunk = pl.BlockSpec((1, tm, nn), lambda i, j: (j, i, 0))
    vec = pl.BlockSpec((1, d), lambda i, j: (0, 0))
    chunks = jax.ShapeDtypeStruct((FF_CHUNKS, n, nn), BF16)
    return _hosted_call(
        body, comm,
        name="ffn_bwd",
        grid=(n // tm, FF_CHUNKS),
        in_specs=[
            tok, tok, vec, chunk, chunk,
            pl.BlockSpec((1, 1, d, nn), lambda i, j: (j, layer, 0, 0)),
            pl.BlockSpec((1, 1, d, nn), lambda i, j: (j + FF_CHUNKS, layer, 0, 0)),
            pl.BlockSpec((1, nn, d), lambda i, j: (layer, j, 0)),
        ],
        out_specs=[tok, vec, tok, tok, chunk, chunk, chunk],
        out_shape=[
            jax.ShapeDtypeStruct((n, d), F32),
            jax.ShapeDtypeStruct((1, d), F32),
            jax.ShapeDtypeStruct((n, d), BF16),
            jax.ShapeDtypeStruct((n, d), BF16),
            chunks, chunks, chunks,
        ],
        scratch_shapes=[pltpu.VMEM((tm, d), BF16), pltpu.VMEM((tm, d), F32)],
        args=(dy, x, g, a, b, w_up, w_up, w_down),
    )


def wgrad(a, b, name, out4, mode, *, g0=0, layer=0, prev=None, a_cols=None, dtype=BF16, comm=None):
    a3 = a if a.ndim == 3 else a[None]
    b3 = b if b.ndim == 3 else b[None]
    sa, n, ka = a3.shape
    sb, _, kb = b3.shape
    a0 = 0
    if a_cols is not None:
        a0, ka = a_cols
    k1, k2 = sa * ka, sb * kb
    depth, groups, rr, cc = out4
    t1 = _tile(math.gcd(ka, rr), 1024)
    t2 = _tile(math.gcd(kb, cc), 1024)
    tn = _tile(n, 1024, 8)
    gpb = 1
    if mode == "col":
        assert rr == k1 and k2 % cc == 0 and g0 + k2 // cc <= groups
        if kb % cc == 0 and cc < kb <= 1024 and g0 % (kb // cc) == 0:
            t2, gpb = kb, kb // cc
        per = max(cc // t2, 1)
        oblock = (1, gpb, t1, min(t2, cc))
        omap = lambda i, j, k: (layer, (g0 + j // per) // gpb, i, j % per)
    else:
        assert cc == k2 and k1 % rr == 0 and g0 + k1 // rr <= groups
        if ka % rr == 0 and rr < ka <= 1024 and g0 % (ka // rr) == 0:
            t1, gpb = ka, ka // rr
        per = max(rr // t1, 1)
        oblock = (1, gpb, min(t1, rr), t2)
        omap = lambda i, j, k: (layer, (g0 + i // per) // gpb, i % per, j)
    na, nb = ka // t1, kb // t2
    nk = n // tn

    def body(a_ref, b_ref, *rest):
        o_ref, acc_ref = rest[-2], rest[-1]
        k = pl.program_id(2)

        @pl.when(k == 0)
        def _():
            acc_ref[...] = jnp.zeros_like(acc_ref)

        acc_ref[...] += _dot_tn(a_ref[0].astype(BF16), b_ref[0].astype(BF16))

        @pl.when(k == nk - 1)
        def _():
            for g in range(gpb):
                if gpb == 1:
                    o_ref[0, 0] = acc_ref[...].astype(dtype)
                elif mode == "col":
                    o_ref[0, g] = acc_ref[:, g * cc:(g + 1) * cc].astype(dtype)
                else:
                    o_ref[0, g] = acc_ref[g * rr:(g + 1) * rr, :].astype(dtype)

    in_specs = [
        pl.BlockSpec((1, tn, t1), lambda i, j, k: (i // na, k, a0 // t1 + i % na)),
        pl.BlockSpec((1, tn, t2), lambda i, j, k: (j // nb, k, j % nb)),
    ]
    args = [a3, b3]
    if comm is not None:
        assert prev is None
        return _hosted_call(body, comm, name=name, grid=(k1 // t1, k2 // t2, nk), in_specs=in_specs,
                            out_specs=[pl.BlockSpec(oblock, omap)], out_shape=[jax.ShapeDtypeStruct(out4, dtype)],
                            scratch_shapes=[pltpu.VMEM((t1, t2), F32)], args=args)
    extra = {}
    if prev is not None:
        in_specs.append(pl.BlockSpec(memory_space=pl.ANY))
        args.append(prev)
        extra["input_output_aliases"] = {2: 0}
    return _call(
        body,
        name=name,
        grid=(k1 // t1, k2 // t2, nk),
        in_specs=in_specs,
        out_specs=pl.BlockSpec(oblock, omap),
        out_shape=jax.ShapeDtypeStruct(out4, dtype),
        scratch_shapes=[pltpu.VMEM((t1, t2), F32)],
        compiler_params=_params(),
        **extra,
    )(*args)


def proj_fwd(x, g, w, layer):
    n, d = x.shape
    c = w.shape[2]
    tm, tc = _tile(n, 1024, 8), _tile(c, 768)

    def body(x_ref, g_ref, w_ref, o_ref, hn_ref):
        @pl.when(pl.program_id(1) == 0)
        def _():
            h, _, _ = _rms_fwd(x_ref[...], g_ref[...])
            hn_ref[...] = h.astype(BF16)

        o_ref[...] = _dot(hn_ref[...], w_ref[0])

    return _call(
        body,
        name="proj_fwd",
        grid=(n // tm, c // tc),
        in_specs=[
            pl.BlockSpec((tm, d), lambda i, j: (i, 0)),
            pl.BlockSpec((1, d), lambda i, j: (0, 0)),
            pl.BlockSpec((1, d, tc), lambda i, j: (layer, 0, j)),
        ],
        out_specs=pl.BlockSpec((tm, tc), lambda i, j: (i, j)),
        out_shape=jax.ShapeDtypeStruct((n, c), F32),
        scratch_shapes=[pltpu.VMEM((tm, d), BF16)],
        compiler_params=_params(),
    )(x, g, w)


def proj_bwd(dproj, dres, x, g, w, layer):
    n, d = x.shape
    c = w.shape[2]
    tm = _tile(n, 512, 8)

    def body(dp_ref, dr_ref, x_ref, g_ref, w_ref, dx_ref, dg_ref, hn_ref):
        dh = _dot_nt(dp_ref[...], w_ref[0])
        gg = g_ref[...]
        h, xhat, rs = _rms_fwd(x_ref[...], gg)
        hn_ref[...] = h.astype(BF16)
        dxn, dgr = _rms_bwd(dh, xhat, rs, gg)
        dx_ref[...] = dr_ref[...] + dxn
        _acc_out(dg_ref, jnp.sum(dgr, axis=0, keepdims=True), pl.program_id(0) == 0)

    tok = pl.BlockSpec((tm, d), lambda i: (i, 0))
    vec = pl.BlockSpec((1, d), lambda i: (0, 0))
    return _call(
        body,
        name="proj_bwd",
        grid=(n // tm,),
        in_specs=[
            pl.BlockSpec((tm, c), lambda i: (i, 0)), tok, tok, vec,
            pl.BlockSpec((1, d, c), lambda i: (layer, 0, 0), pipeline_mode=pl.Buffered(1)),
        ],
        out_specs=[tok, vec, tok],
        out_shape=[
            jax.ShapeDtypeStruct((n, d), F32),
            jax.ShapeDtypeStruct((1, d), F32),
            jax.ShapeDtypeStruct((n, d), BF16),
        ],
        compiler_params=_params(),
    )(dproj, dres, x, g, w)


S5_TS = 512


def _cmul(ar, ai, br, bi):
    return ar * br - ai * bi, ar * bi + ai * br


def _s5_tables(ar, ai, reverse):
    gp = ar.shape[1]
    if reverse:
        ai = -ai
    a1r, a1i = jnp.broadcast_to(ar, (8, gp)), jnp.broadcast_to(ai, (8, gp))
    a2r, a2i = _cmul(a1r, a1i, a1r, a1i)
    a4r, a4i = _cmul(a2r, a2i, a2r, a2i)
    a8r, a8i = _cmul(a4r, a4i, a4r, a4i)
    row = lax.broadcasted_iota(jnp.int32, (8, gp), 0)
    e = (8 - row) if reverse else (row + 1)
    pr, pi = jnp.ones((8, gp), F32), jnp.zeros((8, gp), F32)
    for bit, (fr, fi) in ((1, (a1r, a1i)), (2, (a2r, a2i)), (4, (a4r, a4i)), (8, (a8r, a8i))):
        nr, ni = _cmul(pr, pi, fr, fi)
        on = (e & bit) != 0
        pr, pi = jnp.where(on, nr, pr), jnp.where(on, ni, pi)
    return (a1r, a1i, a2r, a2i, a4r, a4i, pr, pi)


def _s5_scan(xr_ref, xi_ref, tab_ref, cr_ref, ci_ref, ts, reverse):
    gp = xr_ref.shape[1]
    nt = ts // 8
    row = lax.broadcasted_iota(jnp.int32, (8, gp), 0)

    def shifted(v, s):
        if reverse:
            return jnp.where(row < 8 - s, pltpu.roll(v, 8 - s, 0), 0.0)
        return jnp.where(row >= s, pltpu.roll(v, s, 0), 0.0)

    def step(k, carry):
        cr, ci = carry
        t = (nt - 1 - k) if reverse else k
        r0 = pl.multiple_of(t * 8, 8)
        br = xr_ref[pl.ds(r0, 8), :]
        bi = xi_ref[pl.ds(r0, 8), :]
        for q, s in enumerate((1, 2, 4)):
            fr, fi = tab_ref[2 * q], tab_ref[2 * q + 1]
            sr, si = shifted(br, s), shifted(bi, s)
            mr, mi = _cmul(fr, fi, sr, si)
            br, bi = br + mr, bi + mi
        mr, mi = _cmul(tab_ref[6], tab_ref[7], cr, ci)
        br, bi = br + mr, bi + mi
        xr_ref[pl.ds(r0, 8), :] = br
        xi_ref[pl.ds(r0, 8), :] = bi
        edge = 0 if reverse else 7
        return (jnp.broadcast_to(br[edge:edge + 1, :], (8, gp)),
                jnp.broadcast_to(bi[edge:edge + 1, :], (8, gp)))

    cr, ci = lax.fori_loop(0, nt, step, (cr_ref[...], ci_ref[...]), unroll=2)
    cr_ref[...] = cr
    ci_ref[...] = ci


def s5_fwd(proj, ucol, seq, ar, ai, bm_r, bm_i, cm_r, cm_i, dskip, w_glu, layer):
    n = proj.shape[0]
    ds, gp = bm_r.shape
    ts = min(S5_TS, seq)
    nt = seq // ts

    def body(u_ref, ar_ref, ai_ref, bmr_ref, bmi_ref, cmr_ref, cmi_ref, d_ref, wg_ref,
             out_ref, xr_ref, xi_ref, yp_ref, tab_ref, cr_ref, ci_ref):
        @pl.when(pl.program_id(1) == 0)
        def _():
            for q, v in enumerate(_s5_tables(ar_ref[...], ai_ref[...], False)):
                tab_ref[q] = v
            cr_ref[...] = jnp.zeros_like(cr_ref)
            ci_ref[...] = jnp.zeros_like(ci_ref)

        u = u_ref[...]
        ub = u.astype(BF16)
        xr_ref[...] = _dot(ub, bmr_ref[...])
        xi_ref[...] = _dot(ub, bmi_ref[...])
        _s5_scan(xr_ref, xi_ref, tab_ref, cr_ref, ci_ref, ts, False)
        y = (_dot(xr_ref[...].astype(BF16), cmr_ref[...]) - _dot(xi_ref[...].astype(BF16), cmi_ref[...])
             + d_ref[...] * u)
        yp_ref[...] = y
        z = _dot(_gelu(y).astype(BF16), wg_ref[0])
        out_ref[...] = z[:, :ds] * _sig(z[:, ds:])

    full = lambda shape: pl.BlockSpec(shape, lambda b, t: (0, 0))
    return _call(
        body,
        name="s5_fwd",
        grid=(n // seq, nt),
        in_specs=[
            pl.BlockSpec((ts, ds), lambda b, t: (b * nt + t, ucol)),
            full((1, gp)), full((1, gp)), full((ds, gp)), full((ds, gp)), full((gp, ds)), full((gp, ds)),
            full((1, ds)), pl.BlockSpec((1, ds, 2 * ds), lambda b, t: (layer, 0, 0)),
        ],
        out_specs=[
            pl.BlockSpec((ts, ds), lambda b, t: (b * nt + t, 0)),
            pl.BlockSpec((ts, gp), lambda b, t: (b * nt + t, 0)),
            pl.BlockSpec((ts, gp), lambda b, t: (b * nt + t, 0)),
            pl.BlockSpec((ts, ds), lambda b, t: (b * nt + t, 0)),
        ],
        out_shape=[
            jax.ShapeDtypeStruct((n, ds), F32),
            jax.ShapeDtypeStruct((n, gp), F32),
            jax.ShapeDtypeStruct((n, gp), F32),
            jax.ShapeDtypeStruct((n, ds), F32),
        ],
        scratch_shapes=[pltpu.VMEM((8, 8, gp), F32), pltpu.VMEM((8, gp), F32), pltpu.VMEM((8, gp), F32)],
        compiler_params=_params(),
    )(proj, ar, ai, bm_r, bm_i, cm_r, cm_i, dskip, w_glu)


def s5_bwd(dout, ypre, proj, ucol, xr, xi, seq, ar, ai, bm_r, bm_i, cm_r, cm_i, dskip, w_glu, layer):
    n = proj.shape[0]
    ds, gp = bm_r.shape
    ts = min(S5_TS, seq)
    nt = seq // ts

    def body(do_ref, yp_ref, u_ref, xr_ref, xi_ref, hr_ref, hi_ref, ar_ref, ai_ref, bmr_ref, bmi_ref,
             cmr_ref, cmi_ref, d_ref, wg_ref,
             du_ref, gr_ref, gi_ref, dyb_ref, glb_ref, dzb_ref, dar_ref, dai_ref, dd_ref,
             tab_ref, cr_ref, ci_ref):
        b, t = pl.program_id(0), pl.program_id(1)
        first = jnp.logical_and(b == 0, t == 0)

        @pl.when(t == 0)
        def _():
            for q, v in enumerate(_s5_tables(ar_ref[...], ai_ref[...], True)):
                tab_ref[q] = v
            cr_ref[...] = jnp.zeros_like(cr_ref)
            ci_ref[...] = jnp.zeros_like(ci_ref)

        yp = yp_ref[...]
        u = u_ref[...]
        gl = _gelu(yp).astype(BF16)
        glb_ref[...] = gl
        z = _dot(gl, wg_ref[0])
        za, sg = z[:, :ds], _sig(z[:, ds:])
        do = do_ref[...]
        da = (do * sg).astype(BF16)
        dg = (do * za * sg * (1.0 - sg)).astype(BF16)
        dzb_ref[:, :ds] = da
        dzb_ref[:, ds:] = dg
        dgl = _dot_nt(da, wg_ref[0, :, :ds]) + _dot_nt(dg, wg_ref[0, :, ds:])
        dyp = dgl * _gelu_grad(yp)
        dypb = dyp.astype(BF16)
        dyb_ref[...] = dypb
        _acc_out(dd_ref, jnp.sum(dyp * u, axis=0, keepdims=True), first)

        gr_ref[...] = _dot_nt(dypb, cmr_ref[...])
        gi_ref[...] = -_dot_nt(dypb, cmi_ref[...])
        _s5_scan(gr_ref, gi_ref, tab_ref, cr_ref, ci_ref, ts, True)
        gr, gi = gr_ref[...], gi_ref[...]
        du_ref[...] = d_ref[...] * dyp + _dot_nt(gr.astype(BF16), bmr_ref[...]) + _dot_nt(gi.astype(BF16), bmi_ref[...])

        row = lax.broadcasted_iota(jnp.int32, (ts, gp), 0)
        live = jnp.where(t == nt - 1, 0.0, 1.0)
        pr = jnp.broadcast_to(hr_ref[7:8, :] * live, (ts, gp))
        pi = jnp.broadcast_to(hi_ref[7:8, :] * live, (ts, gp))
        sr = jnp.where(row == 0, pr, pltpu.roll(xr_ref[...], 1, 0))
        si = jnp.where(row == 0, pi, pltpu.roll(xi_ref[...], 1, 0))
        _acc_out(dar_ref, jnp.sum(gr * sr + gi * si, axis=0, keepdims=True), first)
        _acc_out(dai_ref, jnp.sum(gi * sr - gr * si, axis=0, keepdims=True), first)

    full = lambda shape: pl.BlockSpec(shape, lambda b, t: (0, 0))
    blk = lambda w, col=0: pl.BlockSpec((ts, w), lambda b, t: (b * nt + nt - 1 - t, col))
    halo = pl.BlockSpec((8, gp), lambda b, t: (jnp.maximum((b * seq + (nt - 1 - t) * ts) // 8 - 1, 0), 0))
    return _call(
        body,
        name="s5_bwd",
        grid=(n // seq, nt),
        in_specs=[
            blk(ds), blk(ds), blk(ds, ucol), blk(gp), blk(gp), halo, halo,
            full((1, gp)), full((1, gp)), full((ds, gp)), full((ds, gp)), full((gp, ds)), full((gp, ds)),
            full((1, ds)), pl.BlockSpec((1, ds, 2 * ds), lambda b, t: (layer, 0, 0)),
        ],
        out_specs=[
            blk(ds), blk(gp), blk(gp), blk(ds), blk(ds), blk(2 * ds),
            full((1, gp)), full((1, gp)), full((1, ds)),
        ],
        out_shape=[
            jax.ShapeDtypeStruct((n, ds), F32),
            jax.ShapeDtypeStruct((n, gp), F32),
            jax.ShapeDtypeStruct((n, gp), F32),
            jax.ShapeDtypeStruct((n, ds), BF16),
            jax.ShapeDtypeStruct((n, ds), BF16),
            jax.ShapeDtypeStruct((n, 2 * ds), BF16),
            jax.ShapeDtypeStruct((1, gp), F32),
            jax.ShapeDtypeStruct((1, gp), F32),
            jax.ShapeDtypeStruct((1, ds), F32),
        ],
        scratch_shapes=[pltpu.VMEM((8, 8, gp), F32), pltpu.VMEM((8, gp), F32), pltpu.VMEM((8, gp), F32)],
        compiler_params=_params(),
    )(dout, ypre, proj, xr, xi, xr, xi, ar, ai, bm_r, bm_i, cm_r, cm_i, dskip, w_glu)


BAND = QBLK + CHUNK
NCH = QBLK // CHUNK


def _attn_specs(nq):
    cur = pl.BlockSpec((1, QBLK, HEAD), lambda h, b, i: (h, b * nq + i, 0))
    prev = pl.BlockSpec((1, QBLK, HEAD), lambda h, b, i: (h, b * nq + jnp.maximum(i - 1, 0), 0))
    vec = pl.BlockSpec((1, 1, 2 * QBLK), lambda h, b, i: (h, 0, 0))
    gain = pl.BlockSpec((1, HEAD), lambda h, b, i: (0, 0))
    return cur, prev, vec, gain


def _attn_build_table(tv, bias_ref, tab_ref):
    w = 2 * QBLK
    for qi in range(CHUNK):
        bias_ref[qi:qi + 1, :] = pltpu.roll(tv, (qi - (CHUNK - 1)) % w, 1)
    bias = bias_ref[...]
    lane = lax.broadcasted_iota(jnp.int32, (CHUNK, w), 1)
    for c in range(NCH):
        rolled = bias if c == 0 else pltpu.roll(bias, CHUNK * c, 1)
        ok = jnp.logical_and(lane >= CHUNK * c, lane < CHUNK * c + BAND)
        tab_ref[CHUNK * c:CHUNK * (c + 1), :] = jnp.where(ok, rolled, NEG)


def _attn_reduce_table(dt_ref, bias_ref):
    w = 2 * QBLK
    acc = dt_ref[0:CHUNK, :]
    for c in range(1, NCH):
        acc = acc + pltpu.roll(dt_ref[CHUNK * c:CHUNK * (c + 1), :], w - CHUNK * c, 1)
    bias_ref[...] = acc
    out = jnp.zeros((1, w), F32)
    for qi in range(CHUNK):
        out = out + pltpu.roll(bias_ref[qi:qi + 1, :], ((CHUNK - 1) - qi) % w, 1)
    return out


def _attn_probs(q, kp, kc, gq, gk, table, first_block):
    qn, qhat, qrs = _rms_fwd(q, gq)
    k = jnp.concatenate([kp, kc], axis=0)
    kn, khat, krs = _rms_fwd(k, gk)
    s = _dot_nt(qn.astype(BF16), kn.astype(BF16)) * (HEAD ** -0.5) + table
    col = lax.broadcasted_iota(jnp.int32, s.shape, 1)
    s = jnp.where(jnp.logical_and(first_block, col < QBLK), NEG, s)
    e = jnp.exp(s - jnp.max(s, axis=-1, keepdims=True))
    p = e * (1.0 / jnp.sum(e, axis=-1, keepdims=True))
    return p, (qn, qhat, qrs), (kn, khat, krs)


def attn_fwd(q, k, v, tv, gq, gk, seq, comm=None):
    h, n, _ = q.shape
    nq = seq // QBLK
    cur, prev, vec, gain = _attn_specs(nq)

    def body(q_ref, kp_ref, kc_ref, vp_ref, vc_ref, tv_ref, gq_ref, gk_ref, o_ref, bias_ref, tab_ref):
        @pl.when(jnp.logical_and(pl.program_id(1) == 0, pl.program_id(2) == 0))
        def _():
            _attn_build_table(tv_ref[0], bias_ref, tab_ref)

        p, _, _ = _attn_probs(q_ref[0], kp_ref[0], kc_ref[0], gq_ref[...], gk_ref[...], tab_ref[...],
                              pl.program_id(2) == 0)
        vv = jnp.concatenate([vp_ref[0], vc_ref[0]], axis=0).astype(BF16)
        o_ref[0] = _dot(p.astype(BF16), vv)

    return _hosted_call(
        body, comm,
        name="attn_fwd",
        grid=(h, n // seq, nq),
        in_specs=[cur, prev, cur, prev, cur, vec, gain, gain],
        out_specs=[cur],
        out_shape=[jax.ShapeDtypeStruct((h, n, HEAD), F32)],
        scratch_shapes=[pltpu.VMEM((CHUNK, 2 * QBLK), F32), pltpu.VMEM((QBLK, 2 * QBLK), F32)],
        args=(q, k, k, v, v, tv, gq, gk),
    )


def attn_bwd(do, q, k, v, tv, gq, gk, seq, comm=None):
    h, n, _ = q.shape
    nb = n // seq
    nq = seq // QBLK
    cur, prev, vec, gain = _attn_specs(nq)

    def body(do_ref, q_ref, kp_ref, kc_ref, vp_ref, vc_ref, tv_ref, gq_ref, gk_ref,
             dq_ref, dkp_ref, dkc_ref, dvp_ref, dvc_ref, dtv_ref, dgq_ref, dgk_ref, bias_ref, tab_ref, dt_ref):
        hh, b, i = pl.program_id(0), pl.program_id(1), pl.program_id(2)
        head_start = jnp.logical_and(b == 0, i == 0)

        @pl.when(head_start)
        def _():
            _attn_build_table(tv_ref[0], bias_ref, tab_ref)

        gq_, gk_ = gq_ref[...], gk_ref[...]
        p, (qn, qhat, qrs), (kn, khat, krs) = _attn_probs(
            q_ref[0], kp_ref[0], kc_ref[0], gq_, gk_, tab_ref[...], i == 0)
        dob = do_ref[0].astype(BF16)
        vv = jnp.concatenate([vp_ref[0], vc_ref[0]], axis=0).astype(BF16)
        dv = _dot_tn(p.astype(BF16), dob)
        dp = _dot_nt(dob, vv)
        ds = p * (dp - jnp.sum(p * dp, axis=-1, keepdims=True))
        _acc_out(dt_ref, ds, head_start)
        dsb = (ds * (HEAD ** -0.5)).astype(BF16)
        dqn = _dot(dsb, kn.astype(BF16))
        dkn = _dot_tn(dsb, qn.astype(BF16))
        dq, dgq_rows = _rms_bwd(dqn, qhat, qrs, gq_)
        dk, dgk_rows = _rms_bwd(dkn, khat, krs, gk_)
        dq_ref[0] = dq
        dkp_ref[0] = dk[:QBLK]
        dkc_ref[0] = dk[QBLK:]
        dvp_ref[0] = dv[:QBLK]
        dvc_ref[0] = dv[QBLK:]
        first = jnp.logical_and(hh == 0, head_start)
        _acc_out(dgq_ref, jnp.sum(dgq_rows, axis=0, keepdims=True), first)
        _acc_out(dgk_ref, jnp.sum(dgk_rows, axis=0, keepdims=True), first)

        @pl.when(jnp.logical_and(b == nb - 1, i == nq - 1))
        def _():
            dtv_ref[0] = _attn_reduce_table(dt_ref, bias_ref)

    hm = jax.ShapeDtypeStruct((h, n, HEAD), F32)
    return _hosted_call(
        body, comm,
        name="attn_bwd",
        grid=(h, nb, nq),
        in_specs=[cur, cur, prev, cur, prev, cur, vec, gain, gain],
        out_specs=[cur, cur, cur, cur, cur, vec, gain, gain],
        out_shape=[hm, hm, hm, hm, hm, jax.ShapeDtypeStruct(tv.shape, F32),
                   jax.ShapeDtypeStruct((1, HEAD), F32), jax.ShapeDtypeStruct((1, HEAD), F32)],
        scratch_shapes=[pltpu.VMEM((CHUNK, 2 * QBLK), F32), pltpu.VMEM((QBLK, 2 * QBLK), F32),
                        pltpu.VMEM((QBLK, 2 * QBLK), F32)],
        args=(do, q, k, k, v, v, tv, gq, gk),
    )


CONV_TC = 512


def _ln_fwd(h1, g, b):
    mu = jnp.mean(h1, axis=-1, keepdims=True)
    xc = h1 - mu
    rs = lax.rsqrt(jnp.mean(xc * xc, axis=-1, keepdims=True) + EPS)
    yhat = xc * rs
    return yhat * g + b, yhat, rs


def _glu(z, dc):
    return z[:, :dc] * _sig(z[:, dc:])


def conv_fwd(proj, zcol, seq, w, bdw, lng, lnb):
    n = proj.shape[0]
    dc = w.shape[1]
    tc = min(CONV_TC, seq)
    nt = seq // tc

    def body(z_ref, zp_ref, w_ref, b_ref, g_ref, lb_ref, h1_ref, o_ref, ext_ref):
        live = jnp.where(pl.program_id(1) == 0, 0.0, 1.0)
        ext_ref[pl.ds(0, HALO), :] = _glu(zp_ref[...], dc) * live
        ext_ref[pl.ds(HALO, tc), :] = _glu(z_ref[...], dc)
        acc = jnp.zeros((tc, dc), F32) + b_ref[...]
        for j in range(CONV_W):
            acc = acc + w_ref[j:j + 1, :] * ext_ref[pl.ds(HALO - (CONV_W - 1) + j, tc), :]
        h1_ref[...] = acc
        ln, _, _ = _ln_fwd(acc, g_ref[...], lb_ref[...])
        o_ref[...] = ln * _sig(ln)

    full = lambda shape: pl.BlockSpec(shape, lambda b, t: (0, 0))
    return _call(
        body,
        name="conv_fwd",
        grid=(n // seq, nt),
        in_specs=[
            pl.BlockSpec((tc, 2 * dc), lambda b, t: (b * nt + t, zcol)),
            pl.BlockSpec((HALO, 2 * dc), lambda b, t: (jnp.maximum((b * seq + t * tc) // HALO - 1, 0), zcol)),
            full((HALO, dc)), full((1, dc)), full((1, dc)), full((1, dc)),
        ],
        out_specs=[
            pl.BlockSpec((tc, dc), lambda b, t: (b * nt + t, 0)),
            pl.BlockSpec((tc, dc), lambda b, t: (b * nt + t, 0)),
        ],
        out_shape=[jax.ShapeDtypeStruct((n, dc), F32), jax.ShapeDtypeStruct((n, dc), F32)],
        scratch_shapes=[pltpu.VMEM((tc + HALO, dc), F32)],
        compiler_params=_params(),
    )(proj, proj, w, bdw, lng, lnb)


def conv_bwd(dco, h1, proj, zcol, seq, w, bdw, lng, lnb):
    n = proj.shape[0]
    dc = w.shape[1]
    tc = min(CONV_TC, seq)
    nt = seq // tc
    nrow = n // HALO

    def body(do_ref, don_ref, h1_ref, h1n_ref, z_ref, zp_ref, w_ref, g_ref, lb_ref,
             dz_ref, dw_ref, db_ref, dg_ref, dlb_ref, ext_ref, dext_ref):
        b, t = pl.program_id(0), pl.program_id(1)
        first = jnp.logical_and(b == 0, t == 0)
        g, lb = g_ref[...], lb_ref[...]

        def dh1_of(do, h1):
            ln, yhat, rs = _ln_fwd(h1, g, lb)
            s = _sig(ln)
            dln = do * (s * (1.0 + ln * (1.0 - s)))
            dyh = dln * g
            dh1 = rs * (dyh - jnp.mean(dyh, axis=-1, keepdims=True)
                        - yhat * jnp.mean(dyh * yhat, axis=-1, keepdims=True))
            return dh1, dln, yhat

        dh1, dln, yhat = dh1_of(do_ref[...], h1_ref[...])
        dh1n, _, _ = dh1_of(don_ref[...], h1n_ref[...])
        _acc_out(dg_ref, jnp.sum(dln * yhat, axis=0, keepdims=True), first)
        _acc_out(dlb_ref, jnp.sum(dln, axis=0, keepdims=True), first)
        _acc_out(db_ref, jnp.sum(dh1, axis=0, keepdims=True), first)

        dext_ref[pl.ds(0, tc), :] = dh1
        dext_ref[pl.ds(tc, HALO), :] = dh1n * jnp.where(t == nt - 1, 0.0, 1.0)
        z = z_ref[...]
        ext_ref[pl.ds(0, HALO), :] = _glu(zp_ref[...], dc) * jnp.where(t == 0, 0.0, 1.0)
        ext_ref[pl.ds(HALO, tc), :] = _glu(z, dc)

        @pl.when(first)
        def _():
            dw_ref[...] = jnp.zeros_like(dw_ref)

        dh0 = jnp.zeros((tc, dc), F32)
        for j in range(CONV_W):
            dh0 = dh0 + w_ref[j:j + 1, :] * dext_ref[pl.ds(CONV_W - 1 - j, tc), :]
            dw_ref[j:j + 1, :] += jnp.sum(dh1 * ext_ref[pl.ds(HALO - (CONV_W - 1) + j, tc), :],
                                          axis=0, keepdims=True)
        za, sg = z[:, :dc], _sig(z[:, dc:])
        dz_ref[:, :dc] = dh0 * sg
        dz_ref[:, dc:] = dh0 * za * sg * (1.0 - sg)

    full = lambda shape: pl.BlockSpec(shape, lambda b, t: (0, 0))
    cur = lambda wd, col=0: pl.BlockSpec((tc, wd), lambda b, t: (b * nt + t, col))
    nxt = pl.BlockSpec((HALO, dc), lambda b, t: (jnp.minimum((b * seq + (t + 1) * tc) // HALO, nrow - 1), 0))
    return _call(
        body,
        name="conv_bwd",
        grid=(n // seq, nt),
        in_specs=[
            cur(dc), nxt, cur(dc), nxt, cur(2 * dc, zcol),
            pl.BlockSpec((HALO, 2 * dc), lambda b, t: (jnp.maximum((b * seq + t * tc) // HALO - 1, 0), zcol)),
            full((HALO, dc)), full((1, dc)), full((1, dc)),
        ],
        out_specs=[cur(2 * dc), full((HALO, dc)), full((1, dc)), full((1, dc)), full((1, dc))],
        out_shape=[
            jax.ShapeDtypeStruct((n, 2 * dc), F32),
            jax.ShapeDtypeStruct((HALO, dc), F32),
            jax.ShapeDtypeStruct((1, dc), F32),
            jax.ShapeDtypeStruct((1, dc), F32),
            jax.ShapeDtypeStruct((1, dc), F32),
        ],
        scratch_shapes=[pltpu.VMEM((tc + HALO, dc), F32), pltpu.VMEM((tc + HALO, dc), F32)],
        compiler_params=_params(),
    )(dco, dco, h1, h1, proj, proj, w, lng, lnb)


def _merge_common(l0, l1, l2, bg, so, ao, co, wbs, wba, wbc, d):
    ys = _dot(so.astype(BF16), wbs)
    ya = _dot(ao.astype(BF16), wba)
    yc = _dot(co.astype(BF16), wbc)
    gs = _sig(l0 + bg[:, :d])
    ga = _sig(l1 + bg[:, d:2 * d])
    gc = _sig(l2 + bg[:, 2 * d:])
    return (ys, ya, yc), (gs, ga, gc)


def _merge_specs(tm, d, dss, da, dc, layer):
    row = lambda w, col=0: pl.BlockSpec((tm, w), lambda i: (i, col))
    full = lambda r, c: pl.BlockSpec((r, c), lambda i: (0, 0))
    stacked = lambda r: pl.BlockSpec((1, r, d), lambda i: (layer, 0, 0))
    acts = [row(d, 0), row(d, 1), row(d, 2), full(1, 3 * d), row(dss), row(da), row(dc)]
    weights = [stacked(dss), stacked(da), stacked(dc), stacked(d)]
    return row, full, acts, weights


def merge_fwd(x, proj, bg, so, ao, co, wbs, wba, wbc, wout, layer):
    n, d = x.shape
    tm = _tile(n, 256, 8)
    row, full, acts, weights = _merge_specs(tm, d, so.shape[1], ao.shape[1], co.shape[1], layer)

    def body(x_ref, l0_ref, l1_ref, l2_ref, bg_ref, so_ref, ao_ref, co_ref,
             wbs_ref, wba_ref, wbc_ref, wo_ref, o_ref):
        (ys, ya, yc), (gs, ga, gc) = _merge_common(
            l0_ref[...], l1_ref[...], l2_ref[...], bg_ref[...], so_ref[...], ao_ref[...], co_ref[...],
            wbs_ref[0], wba_ref[0], wbc_ref[0], d)
        merged = gs * ys + ga * ya + gc * yc
        o_ref[...] = x_ref[...] + _dot(merged.astype(BF16), wo_ref[0])

    return _call(
        body,
        name="merge_fwd",
        grid=(n // tm,),
        in_specs=[row(d)] + acts + weights,
        out_specs=row(d),
        out_shape=jax.ShapeDtypeStruct((n, d), F32),
        compiler_params=_params(),
    )(x, proj, proj, proj, bg, so, ao, co, wbs, wba, wbc, wout)


def merge_bwd(dx, proj, bg, so, ao, co, wbs, wba, wbc, wout, layer):
    n, d = dx.shape
    dss, da, dc = so.shape[1], ao.shape[1], co.shape[1]
    tm = _tile(n, 256, 8)
    row, full, acts, weights = _merge_specs(tm, d, dss, da, dc, layer)

    def body(dx_ref, l0_ref, l1_ref, l2_ref, bg_ref, so_ref, ao_ref, co_ref, wbs_ref, wba_ref, wbc_ref, wo_ref,
             dl_ref, dso_ref, dao_ref, dco_ref, dbg_ref, mg_ref, dxb_ref, dys_ref, dya_ref, dyc_ref):
        wbs, wba, wbc = wbs_ref[0], wba_ref[0], wbc_ref[0]
        (ys, ya, yc), (gs, ga, gc) = _merge_common(
            l0_ref[...], l1_ref[...], l2_ref[...], bg_ref[...], so_ref[...], ao_ref[...], co_ref[...],
            wbs, wba, wbc, d)
        mg_ref[...] = (gs * ys + ga * ya + gc * yc).astype(BF16)
        dxb = dx_ref[...].astype(BF16)
        dxb_ref[...] = dxb
        dm = _dot_nt(dxb, wo_ref[0])
        first = pl.program_id(0) == 0
        for k, (y, g, w, dy_ref, db_ref) in enumerate((
                (ys, gs, wbs, dys_ref, dso_ref), (ya, ga, wba, dya_ref, dao_ref), (yc, gc, wbc, dyc_ref, dco_ref))):
            dl = dm * y * g * (1.0 - g)
            dl_ref[:, k * d:(k + 1) * d] = dl.astype(BF16)
            _acc_out(dbg_ref.at[:, k * d:(k + 1) * d], jnp.sum(dl, axis=0, keepdims=True), first)
            dy = (dm * g).astype(BF16)
            dy_ref[...] = dy
            db_ref[...] = _dot_nt(dy, w)

    bf = lambda w: jax.ShapeDtypeStruct((n, w), BF16)
    return _call(
        body,
        name="merge_bwd",
        grid=(n // tm,),
        in_specs=[row(d)] + acts + weights,
        out_specs=[row(3 * d), row(dss), row(da), row(dc), full(1, 3 * d),
                   row(d), row(d), row(d), row(d), row(d)],
        out_shape=[bf(3 * d), jax.ShapeDtypeStruct((n, dss), F32), jax.ShapeDtypeStruct((n, da), F32),
                   jax.ShapeDtypeStruct((n, dc), F32), jax.ShapeDtypeStruct((1, 3 * d), F32),
                   bf(d), bf(d), bf(d), bf(d), bf(d)],
        compiler_params=_params(),
    )(dx, proj, proj, proj, bg, so, ao, co, wbs, wba, wbc, wout)


def loss_head(y, target):
    n, d = y.shape
    tm = _tile(n, 512, 8)

    def body(y_ref, t_ref, dy_ref, l_ref):
        e = y_ref[...] - t_ref[...]
        dy_ref[...] = e * (1.0 / d)
        part = 0.5 * jnp.sum(jnp.sum(e * e, axis=-1, keepdims=True) * (1.0 / d), axis=0, keepdims=True)
        _acc_out(l_ref, part, pl.program_id(0) == 0)

    return _call(
        body,
        name="loss_head",
        grid=(n // tm,),
        in_specs=[pl.BlockSpec((tm, d), lambda i: (i, 0)), pl.BlockSpec((tm, d), lambda i: (i, 0))],
        out_specs=[pl.BlockSpec((tm, d), lambda i: (i, 0)), pl.BlockSpec((1, 1), lambda i: (0, 0))],
        out_shape=[jax.ShapeDtypeStruct((n, d), F32), jax.ShapeDtypeStruct((1, 1), F32)],
        compiler_params=_params(),
    )(y, target)


def _mesh_pos():
    return lax.axis_index("x"), lax.axis_index("y"), lax.axis_index("c")


ANY = pl.BlockSpec(memory_space=pl.ANY)


def _comm_sems(na):
    return [pltpu.SemaphoreType.DMA((na, 7)), pltpu.SemaphoreType.DMA((na, 7)), pltpu.SemaphoreType.DMA((na,))]


def _gather_plan(x_refs, out_refs, sems):
    na = len(x_refs)
    send_sems, recv_sems, local_sems = sems
    x, y, c = _mesh_pos()
    me, sibling = (x, y, c), (x, y, 1 - c)
    chips = [(1 - x, y), (x, 1 - y), (1 - x, 1 - y)]

    def slot(a, px, py, pc):
        return out_refs[a].at[4 * px + 2 * py + pc]

    def copy(a, k, block, to, src=None):
        return pltpu.make_async_remote_copy(
            src_ref=slot(a, *block) if src is None else src, dst_ref=slot(a, *block),
            send_sem=send_sems.at[a, k], recv_sem=recv_sems.at[a, k],
            device_id=to, device_id_type=pl.DeviceIdType.MESH)

    mine = [pltpu.make_async_copy(x_refs[a], slot(a, *me), local_sems.at[a]) for a in range(na)]
    first = []
    for a in range(na):
        first.append(copy(a, 0, me, sibling, src=x_refs[a]))
        first += [copy(a, 1 + j, me, (*chip, c), src=x_refs[a]) for j, chip in enumerate(chips)]

    def start():
        for cp in mine + first:
            cp.start()

    def finish():
        passed = []
        for j, chip in enumerate(chips):
            for a in range(na):
                copy(a, 1 + j, (*chip, c), me).wait_recv()
                fwd = copy(a, 4 + j, (*chip, c), sibling)
                fwd.start()
                passed.append(fwd)
        for a in range(na):
            copy(a, 0, sibling, me).wait_recv()
            for j, chip in enumerate(chips):
                copy(a, 4 + j, (*chip, 1 - c), me).wait_recv()
        for cp in first + passed:
            cp.wait_send()
        for cp in mine:
            cp.wait()

    return start, finish


def _gather_out(shards):
    return [jax.ShapeDtypeStruct((N_DEV,) + s.shape, s.dtype) for s in shards]


def all_gather(shards):
    na = len(shards)

    def body(*refs):
        start, finish = _gather_plan(refs[:na], refs[na:2 * na], refs[2 * na:])
        start()
        finish()

    return _call(
        body,
        name="all_gather",
        out_shape=_gather_out(shards),
        in_specs=[ANY] * na,
        out_specs=[ANY] * na,
        scratch_shapes=_comm_sems(na),
    )(*shards)


def _hosted_call(body, comm, *, name, grid, in_specs, out_specs, out_shape, scratch_shapes, args):
    if comm is None:
        res = _call(body, name=name, grid=grid, in_specs=in_specs, out_specs=out_specs, out_shape=out_shape,
                    scratch_shapes=scratch_shapes, compiler_params=_params())(*args)
        return res, []
    plan, arrays, c_out = comm
    n_in, n_out, n_scr, ci, co = len(in_specs), len(out_specs), len(scratch_shapes), len(arrays), len(c_out)

    def hosted(*refs):
        ins, cins = refs[:n_in], refs[n_in:n_in + ci]
        o0 = n_in + ci
        outs, couts = refs[o0:o0 + n_out], refs[o0 + n_out:o0 + n_out + co]
        s0 = o0 + n_out + co
        scr, sems = refs[s0:s0 + n_scr], refs[s0 + n_scr:]
        ids = [pl.program_id(ax) for ax in range(len(grid))]
        first = functools.reduce(jnp.logical_and, [i == 0 for i in ids])
        last = functools.reduce(jnp.logical_and, [i == g - 1 for i, g in zip(ids, grid)])
        start, finish = plan(cins, couts, sems)
        pl.when(first)(start)
        body(*ins, *outs, *scr)
        pl.when(last)(finish)

    res = _call(hosted, name=name + "_comm", grid=grid, in_specs=list(in_specs) + [ANY] * ci,
                out_specs=list(out_specs) + [ANY] * co, out_shape=list(out_shape) + list(c_out),
                scratch_shapes=list(scratch_shapes) + _comm_sems(max(ci, co)),
                compiler_params=_params())(*args, *arrays)
    return res[:n_out], res[n_out:]


def _exchange_plan(bcast=()):
    def plan(s_refs, r_refs, sems):
        na = len(s_refs)
        send_sems, recv_sems, local_sems = sems
        x, y, c = _mesh_pos()
        me = 4 * x + 2 * y + c

        def peer(k):
            px = (1 - x) if k & 4 else x
            py = (1 - y) if k & 2 else y
            pc = (1 - c) if k & 1 else c
            return (px, py, pc), 4 * px + 2 * py + pc

        def src_dst(a, pid, slot):
            return (s_refs[a] if a in bcast else s_refs[a].at[pid]), r_refs[a].at[slot]

        def copy(a, k):
            to, pid = peer(k)
            src, dst = src_dst(a, pid, me)
            return pltpu.make_async_remote_copy(
                src_ref=src, dst_ref=dst, send_sem=send_sems.at[a, k - 1], recv_sem=recv_sems.at[a, k - 1],
                device_id=to, device_id_type=pl.DeviceIdType.MESH)

        def arrival(a, k):
            _, pid = peer(k)
            src, dst = src_dst(a, pid, pid)
            return pltpu.make_async_remote_copy(
                src_ref=src, dst_ref=dst, send_sem=send_sems.at[a, k - 1], recv_sem=recv_sems.at[a, k - 1],
                device_id=(x, y, c), device_id_type=pl.DeviceIdType.MESH)

        mine = [pltpu.make_async_copy(*src_dst(a, me, me), local_sems.at[a]) for a in range(na)]
        sends = [copy(a, k) for k in range(1, N_DEV) for a in range(na)]

        def start():
            for cp in mine + sends:
                cp.start()

        def finish():
            for k in range(1, N_DEV):
                for a in range(na):
                    arrival(a, k).wait_recv()
            for cp in sends:
                cp.wait_send()
            for cp in mine:
                cp.wait()

        return start, finish

    return plan


def _exchange_out(slabs, bcast=()):
    return [jax.ShapeDtypeStruct(((N_DEV,) + s.shape) if a in bcast else s.shape, s.dtype)
            for a, s in enumerate(slabs)]


def grad_exchange(slabs, small):
    arrays = list(slabs) + [small]
    na = len(arrays)
    bcast = (na - 1,)

    def body(*refs):
        start, finish = _exchange_plan(bcast)(refs[:na], refs[na:2 * na], refs[2 * na:])
        start()
        finish()

    return _call(
        body,
        name="grad_exchange",
        out_shape=_exchange_out(arrays, bcast),
        in_specs=[ANY] * na,
        out_specs=[ANY] * na,
        scratch_shapes=_comm_sems(na),
    )(*arrays)


ADAM_BLOCK = 256 * 1024


def sum_adamw(recv, w, m, v, layer, prev, name):
    depth, rows, cols = w.shape
    tr = _tile(rows, max(8, ADAM_BLOCK // cols // 8 * 8), 8)
    c1 = 1.0 / (1.0 - ADAM_B1 ** ADAM_STEP)
    c2 = 1.0 / (1.0 - ADAM_B2 ** ADAM_STEP)

    def body(r_ref, w_ref, m_ref, v_ref, *rest):
        g_ref, d_ref, mo_ref, vo_ref = rest[-4:]
        g = r_ref[0].astype(F32)
        for s in range(1, N_DEV):
            g = g + r_ref[s].astype(F32)
        mn = ADAM_B1 * m_ref[0] + (1.0 - ADAM_B1) * g
        vn = ADAM_B2 * v_ref[0] + (1.0 - ADAM_B2) * (g * g)
        g_ref[0] = g
        mo_ref[0] = mn
        vo_ref[0] = vn
        d_ref[0] = -ADAM_LR * ((mn * c1) / (jnp.sqrt(vn * c2) + ADAM_EPS) + ADAM_WD * w_ref[0])

    blk = pl.BlockSpec((1, tr, cols), lambda i: (layer, i, 0))
    out = jax.ShapeDtypeStruct((depth, rows, cols), F32)
    in_specs = [pl.BlockSpec((N_DEV, tr, cols), lambda i: (0, i, 0)), blk, blk, blk]
    args = [recv, w, m, v]
    extra = {}
    if prev is not None:
        in_specs += [ANY] * 4
        args += list(prev)
        extra["input_output_aliases"] = {4 + j: j for j in range(4)}
    return _call(
        body,
        name=name,
        grid=(rows // tr,),
        in_specs=in_specs,
        out_specs=[blk, blk, blk, blk],
        out_shape=[out, out, out, out],
        compiler_params=_params(),
        **extra,
    )(*args)


def _attn_bias_vector(rel_bias):
    h = rel_bias.shape[0]
    n_far = BAND - MAX_REL
    n_near = BAND + CHUNK - 1 - n_far
    far = jnp.broadcast_to(rel_bias[:, 2 * MAX_REL:], (h, n_far))
    near = rel_bias[:, 2 * MAX_REL - n_near:2 * MAX_REL][:, ::-1]
    pad = jnp.zeros((h, 2 * QBLK - n_far - n_near), F32)
    return jnp.concatenate([far, near, pad], axis=1)[:, None, :]


def _s5_prepare(lre, lim, ldt, bre, bim, cre, cim):
    g, p = lre.shape
    lr = jnp.minimum(lre, -1e-4)
    dt = jnp.exp(ldt)[:, None]
    mag = jnp.exp(lr * dt)
    ar = mag * jnp.cos(lim * dt)
    ai = mag * jnp.sin(lim * dt)
    den = lr * lr + lim * lim
    coef_r = ((ar - 1.0) * lr + ai * lim) / den
    coef_i = (ai * lr - (ar - 1.0) * lim) / den
    bbar_r = coef_r[..., None] * bre - coef_i[..., None] * bim
    bbar_i = coef_r[..., None] * bim + coef_i[..., None] * bre
    eye = jnp.eye(g, dtype=F32)
    bd_in = lambda b: jnp.einsum("gpc,gh->gchp", b, eye).reshape(g * S5_GROUP, g * p)
    bd_out = lambda c: jnp.einsum("gcp,gh->gphc", c, eye).reshape(g * p, g * S5_GROUP)
    return (ar.reshape(1, g * p), ai.reshape(1, g * p), bd_in(bbar_r), bd_in(bbar_i), bd_out(cre), bd_out(cim))


SHARDED = ("ffn1_w_up", "ffn1_w_down", "w_in", "s5_w_glu", "w_br_s5", "w_br_attn", "conv_w_dw", "w_br_conv",
           "w_out", "ffn2_w_up", "ffn2_w_down")
WEIGHTS = ("ffn1_norm", "ffn1_w_up", "ffn1_w_down", "mix_norm", "w_in", "b_gate", "s5_lambda_re", "s5_lambda_im",
           "s5_log_dt", "s5_b_re", "s5_b_im", "s5_c_re", "s5_c_im", "s5_d", "s5_w_glu", "w_br_s5", "attn_q_gain",
           "attn_k_gain", "attn_rel_bias", "w_br_attn", "conv_w_dw", "conv_b_dw", "conv_ln_g", "conv_ln_b",
           "w_br_conv", "w_out", "ffn2_norm", "ffn2_w_up", "ffn2_w_down")
SMALL = tuple(nm for nm in WEIGHTS if nm not in SHARDED)
SMALL_LANES = 1024


def _cols_full(g):
    _, depth, k, nn = g.shape
    return g.transpose(1, 2, 0, 3).reshape(depth, k, N_DEV * nn)


def _rows_full(g):
    _, depth, r, cc = g.shape
    return g.transpose(1, 0, 2, 3).reshape(depth, N_DEV * r, cc)


def _cols_slabs(gfull):
    depth, k, c8 = gfull.shape
    return gfull.reshape(depth, k, N_DEV, c8 // N_DEV).transpose(0, 2, 1, 3)


def _heads(a, h):
    n = a.shape[0]
    return a.reshape(n, h, HEAD).transpose(1, 0, 2)


def _unheads(a):
    h, n, _ = a.shape
    return a.transpose(1, 0, 2).reshape(n, h * HEAD)


def _small_pack(t):
    flat = jnp.concatenate([t[nm].reshape(-1) for nm in SMALL])
    rows = -(-flat.shape[0] // SMALL_LANES)
    rows = -(-rows // 8) * 8
    return jnp.pad(flat, (0, rows * SMALL_LANES - flat.shape[0])).reshape(rows, SMALL_LANES)


def _small_unpack(flat, like):
    flat = flat.reshape(-1)
    out, off = {}, 0
    for nm in SMALL:
        out[nm] = flat[off:off + like[nm].size].reshape(like[nm].shape)
        off += like[nm].size
    return out


def _step(x, target, w, m, v):
    bsz, seq, d = x.shape
    n = bsz * seq
    depth = w["ffn1_norm"].shape[0]
    da, dss, dc = d // 2, d // 4, d // 4
    heads = da // HEAD
    gp = dss // S5_GROUP * S5_STATE
    mid = 3 * da + 2 * dc
    q0, k0, v0, z0, u0 = 3 * d, 3 * d + da, 3 * d + 2 * da, 3 * d + 3 * da, 3 * d + mid
    to_kernel_cols = lambda a: jnp.concatenate([a[..., dss + mid:], a[..., dss:dss + mid], a[..., :dss]], axis=-1)
    to_ref_cols = lambda a: jnp.concatenate([a[..., 3 * d + mid:], a[..., 3 * d:3 * d + mid], a[..., :3 * d]], axis=-1)

    def shard(key):
        nm, sl = (key[0], slice(key[1], key[1] + 1)) if isinstance(key, tuple) else (key, slice(None))
        return w[nm][sl] if nm == "conv_w_dw" else w[nm][sl].astype(BF16)

    mixer_names = ("w_in", "s5_w_glu", "w_br_s5", "w_br_attn", "w_br_conv", "conv_w_dw", "w_out")
    ffn1_of = lambda l: (("ffn1_w_up", l), ("ffn1_w_down", l))
    later_ffn1 = tuple(k for l in range(1, depth) for k in ffn1_of(l))
    gather_of = lambda keys: (_gather_plan, [shard(k) for k in keys], _gather_out([shard(k) for k in keys]))
    full = {}

    def take(keys, arrays):
        for key, g in zip(keys, arrays):
            nm = key[0] if isinstance(key, tuple) else key
            if nm in ("ffn1_w_up", "ffn2_w_up"):
                full[key] = g
            elif nm in ("ffn1_w_down", "ffn2_w_down", "w_out"):
                full[key] = _rows_full(g)
            elif nm == "w_in":
                full[key] = to_kernel_cols(_cols_full(g))
            else:
                full[key] = _cols_full(g)

    take(ffn1_of(0), all_gather([shard(k) for k in ffn1_of(0)]))
    nff = full[("ffn1_w_up", 0)].shape[3]

    row = lambda a: a.reshape(1, -1)
    saved = []
    xin = x.reshape(n, d)
    for l in range(depth):
        s = {"x0": xin}
        f1_up, f1_down = ffn1_of(l)
        (s["x1"], s["a1"], s["b1"]), got = ffn_fwd(xin, row(w["ffn1_norm"][l]), full[f1_up], full[f1_down], 0,
                                                   comm=gather_of(mixer_names + later_ffn1) if l == 0 else None)
        if l == 0:
            take(mixer_names + later_ffn1, got)
            conv_w = jnp.pad(full["conv_w_dw"], ((0, 0), (0, HALO - CONV_W), (0, 0)))
        proj = proj_fwd(s["x1"], row(w["mix_norm"][l]), full["w_in"], l)
        s["proj"] = proj
        prep_in = (w["s5_lambda_re"][l], w["s5_lambda_im"][l], w["s5_log_dt"][l], w["s5_b_re"][l], w["s5_b_im"][l],
                   w["s5_c_re"][l], w["s5_c_im"][l])
        (ar, ai, bm_r, bm_i, cm_r, cm_i), s["prep_vjp"] = jax.vjp(_s5_prepare, *prep_in)
        s["s5p"] = (ar, ai, bm_r.astype(BF16), bm_i.astype(BF16), cm_r.astype(BF16), cm_i.astype(BF16),
                    row(w["s5_d"][l]), full["s5_w_glu"], l)
        s["so"], s["xr"], s["xi"], s["ypre"] = s5_fwd(proj, u0 // dss, seq, *s["s5p"])
        tv, s["tv_vjp"] = jax.vjp(_attn_bias_vector, w["attn_rel_bias"][l])
        s["qkv"] = tuple(_heads(proj[:, c0:c0 + da], heads) for c0 in (q0, k0, v0))
        s["attnp"] = (tv, row(w["attn_q_gain"][l]), row(w["attn_k_gain"][l]))
        (ao,), got = attn_fwd(*s["qkv"], *s["attnp"], seq,
                              comm=gather_of(("ffn2_w_up", "ffn2_w_down")) if l == 0 else None)
        if l == 0:
            take(("ffn2_w_up", "ffn2_w_down"), got)
        s["ao"] = _unheads(ao)
        s["convp"] = (conv_w[l], row(w["conv_b_dw"][l]), row(w["conv_ln_g"][l]), row(w["conv_ln_b"][l]))
        s["h1"], s["co"] = conv_fwd(proj, z0 // (2 * dc), seq, *s["convp"])
        s["mergep"] = (row(w["b_gate"][l]), s["so"], s["ao"], s["co"], full["w_br_s5"], full["w_br_attn"],
                       full["w_br_conv"], full["w_out"], l)
        s["x2"] = merge_fwd(s["x1"], proj, *s["mergep"])
        (xin, s["a2"], s["b2"]), _ = ffn_fwd(s["x2"], row(w["ffn2_norm"][l]), full["ffn2_w_up"], full["ffn2_w_down"], l)
        saved.append(s)

    dx, loss = loss_head(xin, target.reshape(n, d))
    loss = lax.psum(loss[0, 0], ("x", "y", "c"))

    small_g = {nm: [None] * depth for nm in SMALL}
    slab = {}

    recv = {}

    def exchange_of(names, l):
        arrays = [slab.pop(nm) for nm in names]
        return [(nm, l) for nm in names], (_exchange_plan(), arrays, _exchange_out(arrays))

    def ffn_grads(which, l, hn, dyb, dab_a, dab_b, act, send_up_now=False):
        up, down = which + "_w_up", which + "_w_down"
        up4, down4 = (1, N_DEV, d, nff), (1, FF_CHUNKS, nff, d)
        half = wgrad(hn, dab_a, "wg_ffn_up", up4, "col")
        slab[up] = wgrad(hn, dab_b, "wg_ffn_up", up4, "col", g0=FF_CHUNKS, prev=half).reshape(up4[1:])
        if send_up_now:
            keys, comm = exchange_of((up,), l)
            (dn,), got = wgrad(act, dyb, "wg_ffn_down", down4, "row", comm=comm)
            recv.update(zip(keys, got))
        else:
            dn = wgrad(act, dyb, "wg_ffn_down", down4, "row")
        slab[down] = dn.reshape(N_DEV, nff // 2, d)

    pending = None
    for l in reversed(range(depth)):
        s = saved[l]
        outs, got = ffn_bwd(dx, s["x2"], row(w["ffn2_norm"][l]), s["a2"], s["b2"],
                            full["ffn2_w_up"], full["ffn2_w_down"], l, comm=pending[1] if pending else None)
        if pending:
            recv.update(zip(pending[0], got))
        dx, dg, hn, dyb, dab_a, dab_b, act = outs
        small_g["ffn2_norm"][l] = dg
        ffn_grads("ffn2", l, hn, dyb, dab_a, dab_b, act)

        dlog, dso, dao, dco, dbg, mg, dxb, dys, dya, dyc = merge_bwd(dx, s["proj"], *s["mergep"])
        small_g["b_gate"][l] = dbg
        slab["w_out"] = wgrad(mg, dxb, "wg_out", (1, N_DEV, d // N_DEV, d), "row").reshape(N_DEV, d // N_DEV, d)
        for nm, act_in, dy_br in (("w_br_s5", s["so"], dys), ("w_br_attn", s["ao"], dya), ("w_br_conv", s["co"], dyc)):
            k_in = act_in.shape[1]
            slab[nm] = wgrad(act_in, dy_br, "wg_" + nm, (1, N_DEV, k_in, d // N_DEV), "col").reshape(
                N_DEV, k_in, d // N_DEV)

        dz, dwdw, dbdw, dlng, dlnb = conv_bwd(dco, s["h1"], s["proj"], z0 // (2 * dc), seq, *s["convp"])
        slab["conv_w_dw"] = _cols_slabs(dwdw[None, :CONV_W])[0].astype(BF16)
        small_g["conv_b_dw"][l], small_g["conv_ln_g"][l], small_g["conv_ln_b"][l] = dbdw, dlng, dlnb

        keys, comm = exchange_of(("ffn2_w_up", "ffn2_w_down"), l)
        outs, got = attn_bwd(_heads(dao, heads), *s["qkv"], *s["attnp"], seq, comm=comm)
        recv.update(zip(keys, got))
        dq, dkp, dkc, dvp, dvc, dtv, dgq, dgk = outs
        small_g["attn_q_gain"][l], small_g["attn_k_gain"][l] = dgq, dgk
        small_g["attn_rel_bias"][l] = s["tv_vjp"](dtv)[0]

        def from_prev(cur, prev):
            prev = prev.reshape(heads, bsz, seq, HEAD)
            prev = jnp.concatenate([prev[:, :, QBLK:], jnp.zeros_like(prev[:, :, :QBLK])], axis=2)
            return cur + prev.reshape(heads, n, HEAD)

        dk, dv = from_prev(dkc, dkp), from_prev(dvc, dvp)

        du, gr, gi, dyb5, glb, dzb, dar, dai, dd = s5_bwd(dso, s["ypre"], s["proj"], u0 // dss, s["xr"], s["xi"],
                                                          seq, *s["s5p"])
        small_g["s5_d"][l] = dd
        one = lambda k1, k2: (1, 1, k1, k2)
        slab["s5_w_glu"] = _cols_slabs(wgrad(glb, dzb, "wg_s5_glu", one(dss, 2 * dss), "col")[0])[0]
        dcm_r = wgrad(s["xr"], dyb5, "wg_s5_c", one(gp, dss), "col", dtype=F32)[0, 0]
        dcm_i = -wgrad(s["xi"], dyb5, "wg_s5_c", one(gp, dss), "col", dtype=F32)[0, 0]
        dbm_r = wgrad(s["proj"], gr, "wg_s5_b", one(dss, gp), "col", a_cols=(u0, dss), dtype=F32)[0, 0]
        dbm_i = wgrad(s["proj"], gi, "wg_s5_b", one(dss, gp), "col", a_cols=(u0, dss), dtype=F32)[0, 0]
        pg = s["prep_vjp"]((dar, dai, dbm_r, dbm_i, dcm_r, dcm_i))
        for nm, gval in zip(("s5_lambda_re", "s5_lambda_im", "s5_log_dt", "s5_b_re", "s5_b_im", "s5_c_re", "s5_c_im"), pg):
            small_g[nm][l] = gval

        dproj = jnp.concatenate([dlog, _unheads(dq).astype(BF16), _unheads(dk).astype(BF16),
                                 _unheads(dv).astype(BF16), dz.astype(BF16), du.astype(BF16)], axis=1)
        dx, dgm, hn = proj_bwd(dproj, dx, s["x1"], row(w["mix_norm"][l]), full["w_in"], l)
        small_g["mix_norm"][l] = dgm
        slab["w_in"] = _cols_slabs(to_ref_cols(wgrad(hn, dproj, "wg_in", one(d, 3 * d + mid + dss), "col")[0]))[0]

        keys, comm = exchange_of(mixer_names, l)
        f1_up, f1_down = ffn1_of(l)
        outs, got = ffn_bwd(dx, s["x0"], row(w["ffn1_norm"][l]), s["a1"], s["b1"],
                            full[f1_up], full[f1_down], 0, comm=comm)
        recv.update(zip(keys, got))
        dx, dg, hn, dyb, dab_a, dab_b, act = outs
        small_g["ffn1_norm"][l] = dg
        ffn_grads("ffn1", l, hn, dyb, dab_a, dab_b, act, send_up_now=(l == 0))
        pending = exchange_of(("ffn1_w_down",) if l == 0 else ("ffn1_w_up", "ffn1_w_down"), l)

    small_flat = _small_pack({nm: jnp.stack([g.reshape(w[nm].shape[1:]) for g in small_g[nm]]) for nm in SMALL})
    *got, recv_small = grad_exchange(pending[1][1], small_flat)
    recv.update(zip(pending[0], got))

    outs = {}
    for nm in SHARDED:
        shp = w[nm].shape
        as3 = lambda t: t.reshape(shp[0], -1, shp[-1])
        bufs = None
        for l in reversed(range(depth)):
            bufs = sum_adamw(recv[(nm, l)], as3(w[nm]), as3(m[nm]), as3(v[nm]), l, bufs, "adamw_" + nm)
        outs[nm] = [b.reshape(shp) for b in bufs]
    packed = sum_adamw(recv_small, _small_pack(w)[None], _small_pack(m)[None], _small_pack(v)[None], 0, None,
                       "adamw_small")
    unpacked = [_small_unpack(p, w) for p in packed]
    for nm in SMALL:
        outs[nm] = [u[nm] for u in unpacked]
    return loss, dx.reshape(x.shape), outs


def kernel(x, ffn1_norm, ffn1_w_up, ffn1_w_down, mix_norm, w_in, b_gate, s5_lambda_re, s5_lambda_im, s5_log_dt, s5_b_re, s5_b_im, s5_c_re, s5_c_im, s5_d, s5_w_glu, w_br_s5, attn_q_gain, attn_k_gain, attn_rel_bias, w_br_attn, conv_w_dw, conv_b_dw, conv_ln_g, conv_ln_b, w_br_conv, w_out, ffn2_norm, ffn2_w_up, ffn2_w_down, loss_target, m_ffn1_norm, m_ffn1_w_up, m_ffn1_w_down, m_mix_norm, m_w_in, m_b_gate, m_s5_lambda_re, m_s5_lambda_im, m_s5_log_dt, m_s5_b_re, m_s5_b_im, m_s5_c_re, m_s5_c_im, m_s5_d, m_s5_w_glu, m_w_br_s5, m_attn_q_gain, m_attn_k_gain, m_attn_rel_bias, m_w_br_attn, m_conv_w_dw, m_conv_b_dw, m_conv_ln_g, m_conv_ln_b, m_w_br_conv, m_w_out, m_ffn2_norm, m_ffn2_w_up, m_ffn2_w_down, v_ffn1_norm, v_ffn1_w_up, v_ffn1_w_down, v_mix_norm, v_w_in, v_b_gate, v_s5_lambda_re, v_s5_lambda_im, v_s5_log_dt, v_s5_b_re, v_s5_b_im, v_s5_c_re, v_s5_c_im, v_s5_d, v_s5_w_glu, v_w_br_s5, v_attn_q_gain, v_attn_k_gain, v_attn_rel_bias, v_w_br_attn, v_conv_w_dw, v_conv_b_dw, v_conv_ln_g, v_conv_ln_b, v_w_br_conv, v_w_out, v_ffn2_norm, v_ffn2_w_up, v_ffn2_w_down):
    args = locals()
    w = {nm: args[nm] for nm in WEIGHTS}
    m = {nm: args["m_" + nm] for nm in WEIGHTS}
    v = {nm: args["v_" + nm] for nm in WEIGHTS}
    loss, gx, outs = _step(x, loss_target, w, m, v)
    return (loss, gx, *[outs[nm][0] for nm in WEIGHTS], *[outs[nm][1] for nm in WEIGHTS],
            *[outs[nm][2] for nm in WEIGHTS], *[outs[nm][3] for nm in WEIGHTS])
```

```python
import functools
import math

import numpy as np
import jax
import jax.numpy as jnp
from jax import lax
from jax.experimental import pallas as pl
from jax.experimental.pallas import tpu as pltpu

F32 = jnp.float32
BF16 = jnp.bfloat16

CHUNK = 64
N_LEFT = 8
QBLK = CHUNK * N_LEFT
HEAD = 64
MAX_REL = 128
S5_GROUP = 16
S5_STATE = 64
CONV_W = 31
HALO = 32
EPS = 1e-6
NEG = -1e30
ADAM_LR, ADAM_B1, ADAM_B2, ADAM_EPS, ADAM_WD, ADAM_STEP = 0.001, 0.9, 0.999, 1e-08, 0.01, 10
N_DEV = 8
VMEM_LIMIT = 56 * 1024 * 1024


def _call(body, **kw):
    return pl.pallas_call(body, **kw)


def _params(**kw):
    return pltpu.CompilerParams(vmem_limit_bytes=VMEM_LIMIT, **kw)


def _tile(n, cap, unit=128):
    if n <= cap:
        return n
    d = (cap // unit) * unit
    while d >= unit:
        if n % d == 0:
            return d
        d -= unit
    raise ValueError(f"no tile for {n} under {cap}")


def _dot(a, b):
    return jnp.dot(a, b, preferred_element_type=F32)


def _dot_nt(a, b):
    return lax.dot_general(a, b, (((1,), (1,)), ((), ())), preferred_element_type=F32)


def _dot_tn(a, b):
    return lax.dot_general(a, b, (((0,), (0,)), ((), ())), preferred_element_type=F32)


def _sig(x):
    return 1.0 / (1.0 + jnp.exp(-x))


def _rms_fwd(x, g):
    rs = lax.rsqrt(jnp.mean(x * x, axis=-1, keepdims=True) + EPS)
    xhat = x * rs
    return xhat * g, xhat, rs


def _rms_bwd(dh, xhat, rs, g):
    dxh = dh * g
    dx = rs * (dxh - xhat * jnp.mean(dxh * xhat, axis=-1, keepdims=True))
    return dx, dh * xhat


_GELU_C = math.sqrt(2.0 / math.pi)


def _gelu(x):
    return 0.5 * x * (1.0 + jnp.tanh(_GELU_C * (x + 0.044715 * x * x * x)))


def _gelu_grad(x):
    t = jnp.tanh(_GELU_C * (x + 0.044715 * x * x * x))
    return 0.5 * (1.0 + t) + 0.5 * x * (1.0 - t * t) * _GELU_C * (1.0 + 3.0 * 0.044715 * x * x)


def _acc_out(ref, val, first):
    @pl.when(first)
    def _():
        ref[...] = val

    @pl.when(jnp.logical_not(first))
    def _():
        ref[...] += val


FF_CHUNKS = N_DEV // 2


def ffn_fwd(x, g, w_up, w_down, layer, comm=None):
    n, d = x.shape
    nn = w_up.shape[3]
    tm = _tile(n, 1024, 8)

    def body(x_ref, g_ref, wa_ref, wb_ref, wd_ref, xo_ref, a_ref, b_ref, hn_ref, acc_ref):
        j = pl.program_id(1)

        @pl.when(j == 0)
        def _():
            h, _, _ = _rms_fwd(x_ref[...], g_ref[...])
            hn_ref[...] = h.astype(BF16)
            acc_ref[...] = jnp.zeros_like(acc_ref)

        hn = hn_ref[...]
        a = _dot(hn, wa_ref[0, 0])
        b = _dot(hn, wb_ref[0, 0])
        a_ref[0] = a.astype(BF16)
        b_ref[0] = b.astype(BF16)
        act = a * _sig(a) * b
        acc_ref[...] += _dot(act.astype(BF16), wd_ref[0])

        @pl.when(j == FF_CHUNKS - 1)
        def _():
            xo_ref[...] = x_ref[...] + 0.5 * acc_ref[...]

    return _hosted_call(
        body, comm,
        name="ffn_fwd",
        grid=(n // tm, FF_CHUNKS),
        in_specs=[
            pl.BlockSpec((tm, d), lambda i, j: (i, 0)),
            pl.BlockSpec((1, d), lambda i, j: (0, 0)),
            pl.BlockSpec((1, 1, d, nn), lambda i, j: (j, layer, 0, 0)),
            pl.BlockSpec((1, 1, d, nn), lambda i, j: (j + FF_CHUNKS, layer, 0, 0)),
            pl.BlockSpec((1, nn, d), lambda i, j: (layer, j, 0)),
        ],
        out_specs=[
            pl.BlockSpec((tm, d), lambda i, j: (i, 0)),
            pl.BlockSpec((1, tm, nn), lambda i, j: (j, i, 0)),
            pl.BlockSpec((1, tm, nn), lambda i, j: (j, i, 0)),
        ],
        out_shape=[
            jax.ShapeDtypeStruct((n, d), F32),
            jax.ShapeDtypeStruct((FF_CHUNKS, n, nn), BF16),
            jax.ShapeDtypeStruct((FF_CHUNKS, n, nn), BF16),
        ],
        scratch_shapes=[pltpu.VMEM((tm, d), BF16), pltpu.VMEM((tm, d), F32)],
        args=(x, g, w_up, w_up, w_down),
    )


def ffn_bwd(dy, x, g, a, b, w_up, w_down, layer, comm=None):
    n, d = x.shape
    nn = w_up.shape[3]
    tm = _tile(n, 512, 8)

    def body(dy_ref, x_ref, g_ref, a_ref, b_ref, wa_ref, wb_ref, wd_ref,
             dx_ref, dg_ref, hn_ref, dyb_ref, da_ref, db_ref, act_ref, dyb_s, dh_ref):
        i, j = pl.program_id(0), pl.program_id(1)

        @pl.when(j == 0)
        def _():
            h, _, _ = _rms_fwd(x_ref[...], g_ref[...])
            hn_ref[...] = h.astype(BF16)
            dyb = (0.5 * dy_ref[...]).astype(BF16)
            dyb_ref[...] = dyb
            dyb_s[...] = dyb
            dh_ref[...] = jnp.zeros_like(dh_ref)

        dact = _dot_nt(dyb_s[...], wd_ref[0])
        a32 = a_ref[0].astype(F32)
        b32 = b_ref[0].astype(F32)
        s = _sig(a32)
        sil = a32 * s
        da = (dact * b32 * (s * (1.0 + a32 * (1.0 - s)))).astype(BF16)
        db = (dact * sil).astype(BF16)
        da_ref[0] = da
        db_ref[0] = db
        act_ref[0] = (sil * b32).astype(BF16)
        dh_ref[...] += _dot_nt(da, wa_ref[0, 0]) + _dot_nt(db, wb_ref[0, 0])

        @pl.when(j == FF_CHUNKS - 1)
        def _():
            gg = g_ref[...]
            _, xhat, rs = _rms_fwd(x_ref[...], gg)
            dxn, dgr = _rms_bwd(dh_ref[...], xhat, rs, gg)
            dx_ref[...] = dy_ref[...] + dxn
            _acc_out(dg_ref, jnp.sum(dgr, axis=0, keepdims=True), i == 0)

    tok = pl.BlockSpec((tm, d), lambda i, j: (i, 0))
    chunk = pl.BlockSpec((1, tm, nn), lambda i, j: (j, i, 0))
    vec = pl.BlockSpec((1, d), lambda i, j: (0, 0))
    chunks = jax.ShapeDtypeStruct((FF_CHUNKS, n, nn), BF16)
    return _hosted_call(
        body, comm,
        name="ffn_bwd",
        grid=(n // tm, FF_CHUNKS),
        in_specs=[
            tok, tok, vec, chunk, chunk,
            pl.BlockSpec((1, 1, d, nn), lambda i, j: (j, layer, 0, 0)),
            pl.BlockSpec((1, 1, d, nn), lambda i, j: (j + FF_CHUNKS, layer, 0, 0)),
            pl.BlockSpec((1, nn, d), lambda i, j: (layer, j, 0)),
        ],
        out_specs=[tok, vec, tok, tok, chunk, chunk, chunk],
        out_shape=[
            jax.ShapeDtypeStruct((n, d), F32),
            jax.ShapeDtypeStruct((1, d), F32),
            jax.ShapeDtypeStruct((n, d), BF16),
            jax.ShapeDtypeStruct((n, d), BF16),
            chunks, chunks, chunks,
        ],
        scratch_shapes=[pltpu.VMEM((tm, d), BF16), pltpu.VMEM((tm, d), F32)],
        args=(dy, x, g, a, b, w_up, w_up, w_down),
    )


def wgrad(a, b, name, out4, mode, *, g0=0, layer=0, prev=None, a_cols=None, dtype=BF16, comm=None):
    a3 = a if a.ndim == 3 else a[None]
    b3 = b if b.ndim == 3 else b[None]
    sa, n, ka = a3.shape
    sb, _, kb = b3.shape
    a0 = 0
    if a_cols is not None:
        a0, ka = a_cols
    k1, k2 = sa * ka, sb * kb
    depth, groups, rr, cc = out4
    t1 = _tile(math.gcd(ka, rr), 1024)
    t2 = _tile(math.gcd(kb, cc), 1024)
    tn = _tile(n, 1024, 8)
    gpb = 1
    if mode == "col":
        assert rr == k1 and k2 % cc == 0 and g0 + k2 // cc <= groups
        if kb % cc == 0 and cc < kb <= 1024 and g0 % (kb // cc) == 0:
            t2, gpb = kb, kb // cc
        per = max(cc // t2, 1)
        oblock = (1, gpb, t1, min(t2, cc))
        omap = lambda i, j, k: (layer, (g0 + j // per) // gpb, i, j % per)
    else:
        assert cc == k2 and k1 % rr == 0 and g0 + k1 // rr <= groups
        if ka % rr == 0 and rr < ka <= 1024 and g0 % (ka // rr) == 0:
            t1, gpb = ka, ka // rr
        per = max(rr // t1, 1)
        oblock = (1, gpb, min(t1, rr), t2)
        omap = lambda i, j, k: (layer, (g0 + i // per) // gpb, i % per, j)
    na, nb = ka // t1, kb // t2
    nk = n // tn

    def body(a_ref, b_ref, *rest):
        o_ref, acc_ref = rest[-2], rest[-1]
        k = pl.program_id(2)

        @pl.when(k == 0)
        def _():
            acc_ref[...] = jnp.zeros_like(acc_ref)

        acc_ref[...] += _dot_tn(a_ref[0].astype(BF16), b_ref[0].astype(BF16))

        @pl.when(k == nk - 1)
        def _():
            for g in range(gpb):
                if gpb == 1:
                    o_ref[0, 0] = acc_ref[...].astype(dtype)
                elif mode == "col":
                    o_ref[0, g] = acc_ref[:, g * cc:(g + 1) * cc].astype(dtype)
                else:
                    o_ref[0, g] = acc_ref[g * rr:(g + 1) * rr, :].astype(dtype)

    in_specs = [
        pl.BlockSpec((1, tn, t1), lambda i, j, k: (i // na, k, a0 // t1 + i % na)),
        pl.BlockSpec((1, tn, t2), lambda i, j, k: (j // nb, k, j % nb)),
    ]
    args = [a3, b3]
    if comm is not None:
        assert prev is None
        return _hosted_call(body, comm, name=name, grid=(k1 // t1, k2 // t2, nk), in_specs=in_specs,
                            out_specs=[pl.BlockSpec(oblock, omap)], out_shape=[jax.ShapeDtypeStruct(out4, dtype)],
                            scratch_shapes=[pltpu.VMEM((t1, t2), F32)], args=args)
    extra = {}
    if prev is not None:
        in_specs.append(pl.BlockSpec(memory_space=pl.ANY))
        args.append(prev)
        extra["input_output_aliases"] = {2: 0}
    return _call(
        body,
        name=name,
        grid=(k1 // t1, k2 // t2, nk),
        in_specs=in_specs,
        out_specs=pl.BlockSpec(oblock, omap),
        out_shape=jax.ShapeDtypeStruct(out4, dtype),
        scratch_shapes=[pltpu.VMEM((t1, t2), F32)],
        compiler_params=_params(),
        **extra,
    )(*args)


def proj_fwd(x, g, w, layer, comm=None):
    n, d = x.shape
    c = w.shape[2]
    tm, tc = _tile(n, 1024, 8), _tile(c, 768)

    def body(x_ref, g_ref, w_ref, o_ref, hn_ref):
        @pl.when(pl.program_id(1) == 0)
        def _():
            h, _, _ = _rms_fwd(x_ref[...], g_ref[...])
            hn_ref[...] = h.astype(BF16)

        o_ref[...] = _dot(hn_ref[...], w_ref[0])

    return _hosted_call(
        body, comm,
        name="proj_fwd",
        grid=(n // tm, c // tc),
        in_specs=[
            pl.BlockSpec((tm, d), lambda i, j: (i, 0)),
            pl.BlockSpec((1, d), lambda i, j: (0, 0)),
            pl.BlockSpec((1, d, tc), lambda i, j: (layer, 0, j)),
        ],
        out_specs=[pl.BlockSpec((tm, tc), lambda i, j: (i, j))],
        out_shape=[jax.ShapeDtypeStruct((n, c), F32)],
        scratch_shapes=[pltpu.VMEM((tm, d), BF16)],
        args=(x, g, w),
    )


def proj_bwd(dproj, dres, x, g, w, layer):
    n, d = x.shape
    c = w.shape[2]
    tm = _tile(n, 512, 8)

    def body(dp_ref, dr_ref, x_ref, g_ref, w_ref, dx_ref, dg_ref, hn_ref):
        dh = _dot_nt(dp_ref[...], w_ref[0])
        gg = g_ref[...]
        h, xhat, rs = _rms_fwd(x_ref[...], gg)
        hn_ref[...] = h.astype(BF16)
        dxn, dgr = _rms_bwd(dh, xhat, rs, gg)
        dx_ref[...] = dr_ref[...] + dxn
        _acc_out(dg_ref, jnp.sum(dgr, axis=0, keepdims=True), pl.program_id(0) == 0)

    tok = pl.BlockSpec((tm, d), lambda i: (i, 0))
    vec = pl.BlockSpec((1, d), lambda i: (0, 0))
    return _call(
        body,
        name="proj_bwd",
        grid=(n // tm,),
        in_specs=[
            pl.BlockSpec((tm, c), lambda i: (i, 0)), tok, tok, vec,
            pl.BlockSpec((1, d, c), lambda i: (layer, 0, 0), pipeline_mode=pl.Buffered(1)),
        ],
        out_specs=[tok, vec, tok],
        out_shape=[
            jax.ShapeDtypeStruct((n, d), F32),
            jax.ShapeDtypeStruct((1, d), F32),
            jax.ShapeDtypeStruct((n, d), BF16),
        ],
        compiler_params=_params(),
    )(dproj, dres, x, g, w)


S5_TS = 512


def _cmul(ar, ai, br, bi):
    return ar * br - ai * bi, ar * bi + ai * br


def _s5_tables(ar, ai, reverse):
    gp = ar.shape[1]
    if reverse:
        ai = -ai
    a1r, a1i = jnp.broadcast_to(ar, (8, gp)), jnp.broadcast_to(ai, (8, gp))
    a2r, a2i = _cmul(a1r, a1i, a1r, a1i)
    a4r, a4i = _cmul(a2r, a2i, a2r, a2i)
    a8r, a8i = _cmul(a4r, a4i, a4r, a4i)
    row = lax.broadcasted_iota(jnp.int32, (8, gp), 0)
    e = (8 - row) if reverse else (row + 1)
    pr, pi = jnp.ones((8, gp), F32), jnp.zeros((8, gp), F32)
    for bit, (fr, fi) in ((1, (a1r, a1i)), (2, (a2r, a2i)), (4, (a4r, a4i)), (8, (a8r, a8i))):
        nr, ni = _cmul(pr, pi, fr, fi)
        on = (e & bit) != 0
        pr, pi = jnp.where(on, nr, pr), jnp.where(on, ni, pi)
    return (a1r, a1i, a2r, a2i, a4r, a4i, pr, pi)


def _s5_scan(xr_ref, xi_ref, tab_ref, cr_ref, ci_ref, ts, reverse):
    gp = xr_ref.shape[1]
    nt = ts // 8
    row = lax.broadcasted_iota(jnp.int32, (8, gp), 0)

    def shifted(v, s):
        if reverse:
            return jnp.where(row < 8 - s, pltpu.roll(v, 8 - s, 0), 0.0)
        return jnp.where(row >= s, pltpu.roll(v, s, 0), 0.0)

    def step(k, carry):
        cr, ci = carry
        t = (nt - 1 - k) if reverse else k
        r0 = pl.multiple_of(t * 8, 8)
        br = xr_ref[pl.ds(r0, 8), :]
        bi = xi_ref[pl.ds(r0, 8), :]
        for q, s in enumerate((1, 2, 4)):
            fr, fi = tab_ref[2 * q], tab_ref[2 * q + 1]
            sr, si = shifted(br, s), shifted(bi, s)
            mr, mi = _cmul(fr, fi, sr, si)
            br, bi = br + mr, bi + mi
        mr, mi = _cmul(tab_ref[6], tab_ref[7], cr, ci)
        br, bi = br + mr, bi + mi
        xr_ref[pl.ds(r0, 8), :] = br
        xi_ref[pl.ds(r0, 8), :] = bi
        edge = 0 if reverse else 7
        return (jnp.broadcast_to(br[edge:edge + 1, :], (8, gp)),
                jnp.broadcast_to(bi[edge:edge + 1, :], (8, gp)))

    cr, ci = lax.fori_loop(0, nt, step, (cr_ref[...], ci_ref[...]), unroll=2)
    cr_ref[...] = cr
    ci_ref[...] = ci


def s5_fwd(proj, ucol, seq, ar, ai, bm_r, bm_i, cm_r, cm_i, dskip, w_glu, layer):
    n = proj.shape[0]
    ds, gp = bm_r.shape
    ts = min(S5_TS, seq)
    nt = seq // ts

    def body(u_ref, ar_ref, ai_ref, bmr_ref, bmi_ref, cmr_ref, cmi_ref, d_ref, wg_ref,
             out_ref, xr_ref, xi_ref, yp_ref, tab_ref, cr_ref, ci_ref):
        @pl.when(pl.program_id(1) == 0)
        def _():
            for q, v in enumerate(_s5_tables(ar_ref[...], ai_ref[...], False)):
                tab_ref[q] = v
            cr_ref[...] = jnp.zeros_like(cr_ref)
            ci_ref[...] = jnp.zeros_like(ci_ref)

        u = u_ref[...]
        ub = u.astype(BF16)
        xr_ref[...] = _dot(ub, bmr_ref[...])
        xi_ref[...] = _dot(ub, bmi_ref[...])
        _s5_scan(xr_ref, xi_ref, tab_ref, cr_ref, ci_ref, ts, False)
        y = (_dot(xr_ref[...].astype(BF16), cmr_ref[...]) - _dot(xi_ref[...].astype(BF16), cmi_ref[...])
             + d_ref[...] * u)
        yp_ref[...] = y
        z = _dot(_gelu(y).astype(BF16), wg_ref[0])
        out_ref[...] = z[:, :ds] * _sig(z[:, ds:])

    full = lambda shape: pl.BlockSpec(shape, lambda b, t: (0, 0))
    return _call(
        body,
        name="s5_fwd",
        grid=(n // seq, nt),
        in_specs=[
            pl.BlockSpec((ts, ds), lambda b, t: (b * nt + t, ucol)),
            full((1, gp)), full((1, gp)), full((ds, gp)), full((ds, gp)), full((gp, ds)), full((gp, ds)),
            full((1, ds)), pl.BlockSpec((1, ds, 2 * ds), lambda b, t: (layer, 0, 0)),
        ],
        out_specs=[
            pl.BlockSpec((ts, ds), lambda b, t: (b * nt + t, 0)),
            pl.BlockSpec((ts, gp), lambda b, t: (b * nt + t, 0)),
            pl.BlockSpec((ts, gp), lambda b, t: (b * nt + t, 0)),
            pl.BlockSpec((ts, ds), lambda b, t: (b * nt + t, 0)),
        ],
        out_shape=[
            jax.ShapeDtypeStruct((n, ds), F32),
            jax.ShapeDtypeStruct((n, gp), F32),
            jax.ShapeDtypeStruct((n, gp), F32),
            jax.ShapeDtypeStruct((n, ds), F32),
        ],
        scratch_shapes=[pltpu.VMEM((8, 8, gp), F32), pltpu.VMEM((8, gp), F32), pltpu.VMEM((8, gp), F32)],
        compiler_params=_params(),
    )(proj, ar, ai, bm_r, bm_i, cm_r, cm_i, dskip, w_glu)


def s5_bwd(dout, ypre, proj, ucol, xr, xi, seq, ar, ai, bm_r, bm_i, cm_r, cm_i, dskip, w_glu, layer):
    n = proj.shape[0]
    ds, gp = bm_r.shape
    ts = min(S5_TS, seq)
    nt = seq // ts

    def body(do_ref, yp_ref, u_ref, xr_ref, xi_ref, hr_ref, hi_ref, ar_ref, ai_ref, bmr_ref, bmi_ref,
             cmr_ref, cmi_ref, d_ref, wg_ref,
             du_ref, gr_ref, gi_ref, dyb_ref, glb_ref, dzb_ref, dar_ref, dai_ref, dd_ref,
             tab_ref, cr_ref, ci_ref):
        b, t = pl.program_id(0), pl.program_id(1)
        first = jnp.logical_and(b == 0, t == 0)

        @pl.when(t == 0)
        def _():
            for q, v in enumerate(_s5_tables(ar_ref[...], ai_ref[...], True)):
                tab_ref[q] = v
            cr_ref[...] = jnp.zeros_like(cr_ref)
            ci_ref[...] = jnp.zeros_like(ci_ref)

        yp = yp_ref[...]
        u = u_ref[...]
        gl = _gelu(yp).astype(BF16)
        glb_ref[...] = gl
        z = _dot(gl, wg_ref[0])
        za, sg = z[:, :ds], _sig(z[:, ds:])
        do = do_ref[...]
        da = (do * sg).astype(BF16)
        dg = (do * za * sg * (1.0 - sg)).astype(BF16)
        dzb_ref[:, :ds] = da
        dzb_ref[:, ds:] = dg
        dgl = _dot_nt(da, wg_ref[0, :, :ds]) + _dot_nt(dg, wg_ref[0, :, ds:])
        dyp = dgl * _gelu_grad(yp)
        dypb = dyp.astype(BF16)
        dyb_ref[...] = dypb
        _acc_out(dd_ref, jnp.sum(dyp * u, axis=0, keepdims=True), first)

        gr_ref[...] = _dot_nt(dypb, cmr_ref[...])
        gi_ref[...] = -_dot_nt(dypb, cmi_ref[...])
        _s5_scan(gr_ref, gi_ref, tab_ref, cr_ref, ci_ref, ts, True)
        gr, gi = gr_ref[...], gi_ref[...]
        du_ref[...] = d_ref[...] * dyp + _dot_nt(gr.astype(BF16), bmr_ref[...]) + _dot_nt(gi.astype(BF16), bmi_ref[...])

        row = lax.broadcasted_iota(jnp.int32, (ts, gp), 0)
        live = jnp.where(t == nt - 1, 0.0, 1.0)
        pr = jnp.broadcast_to(hr_ref[7:8, :] * live, (ts, gp))
        pi = jnp.broadcast_to(hi_ref[7:8, :] * live, (ts, gp))
        sr = jnp.where(row == 0, pr, pltpu.roll(xr_ref[...], 1, 0))
        si = jnp.where(row == 0, pi, pltpu.roll(xi_ref[...], 1, 0))
        _acc_out(dar_ref, jnp.sum(gr * sr + gi * si, axis=0, keepdims=True), first)
        _acc_out(dai_ref, jnp.sum(gi * sr - gr * si, axis=0, keepdims=True), first)

    full = lambda shape: pl.BlockSpec(shape, lambda b, t: (0, 0))
    blk = lambda w, col=0: pl.BlockSpec((ts, w), lambda b, t: (b * nt + nt - 1 - t, col))
    halo = pl.BlockSpec((8, gp), lambda b, t: (jnp.maximum((b * seq + (nt - 1 - t) * ts) // 8 - 1, 0), 0))
    return _call(
        body,
        name="s5_bwd",
        grid=(n // seq, nt),
        in_specs=[
            blk(ds), blk(ds), blk(ds, ucol), blk(gp), blk(gp), halo, halo,
            full((1, gp)), full((1, gp)), full((ds, gp)), full((ds, gp)), full((gp, ds)), full((gp, ds)),
            full((1, ds)), pl.BlockSpec((1, ds, 2 * ds), lambda b, t: (layer, 0, 0)),
        ],
        out_specs=[
            blk(ds), blk(gp), blk(gp), blk(ds), blk(ds), blk(2 * ds),
            full((1, gp)), full((1, gp)), full((1, ds)),
        ],
        out_shape=[
            jax.ShapeDtypeStruct((n, ds), F32),
            jax.ShapeDtypeStruct((n, gp), F32),
            jax.ShapeDtypeStruct((n, gp), F32),
            jax.ShapeDtypeStruct((n, ds), BF16),
            jax.ShapeDtypeStruct((n, ds), BF16),
            jax.ShapeDtypeStruct((n, 2 * ds), BF16),
            jax.ShapeDtypeStruct((1, gp), F32),
            jax.ShapeDtypeStruct((1, gp), F32),
            jax.ShapeDtypeStruct((1, ds), F32),
        ],
        scratch_shapes=[pltpu.VMEM((8, 8, gp), F32), pltpu.VMEM((8, gp), F32), pltpu.VMEM((8, gp), F32)],
        compiler_params=_params(),
    )(dout, ypre, proj, xr, xi, xr, xi, ar, ai, bm_r, bm_i, cm_r, cm_i, dskip, w_glu)


BAND = QBLK + CHUNK
NCH = QBLK // CHUNK


PAIR = 2 * HEAD


def _attn_specs(nq):
    last = nq - 1
    cur = lambda col0=0: pl.BlockSpec(
        (QBLK, PAIR), lambda p, b, i: (b * nq + jnp.minimum(i, last), col0 // PAIR + p))
    prev = lambda col0=0: pl.BlockSpec(
        (QBLK, PAIR), lambda p, b, i: (b * nq + jnp.maximum(jnp.minimum(i, last) - 1, 0), col0 // PAIR + p))
    vec = pl.BlockSpec((2, 1, 2 * QBLK), lambda p, b, i: (p, 0, 0))
    gain = pl.BlockSpec((1, PAIR), lambda p, b, i: (0, 0))
    return cur, prev, vec, gain


def _pair_masks():
    lane = lax.broadcasted_iota(jnp.int32, (1, PAIR), 1)
    return lane < HEAD, [(lane < HEAD).astype(F32), (lane >= HEAD).astype(F32)]


def _pair_mean(t, low, m0):
    s0 = jnp.sum(t * m0, axis=-1, keepdims=True)
    s1 = jnp.sum(t, axis=-1, keepdims=True) - s0
    return jnp.where(low, s0, s1) * (1.0 / HEAD)


def _pair_rms_fwd(x, g, low, m0):
    rs = lax.rsqrt(_pair_mean(x * x, low, m0) + EPS)
    xhat = x * rs
    return xhat * g, xhat, rs


def _pair_rms_bwd(dh, xhat, rs, g, low, m0):
    dxh = dh * g
    dx = rs * (dxh - xhat * _pair_mean(dxh * xhat, low, m0))
    return dx, dh * xhat


def _attn_build_table(tv, bias_ref, tab_ref):
    w = 2 * QBLK
    for qi in range(CHUNK):
        bias_ref[qi:qi + 1, :] = pltpu.roll(tv, (qi - (CHUNK - 1)) % w, 1)
    bias = bias_ref[...]
    lane = lax.broadcasted_iota(jnp.int32, (CHUNK, w), 1)
    for c in range(NCH):
        rolled = bias if c == 0 else pltpu.roll(bias, CHUNK * c, 1)
        ok = jnp.logical_and(lane >= CHUNK * c, lane < CHUNK * c + BAND)
        tab_ref[CHUNK * c:CHUNK * (c + 1), :] = jnp.where(ok, rolled, NEG)


def _attn_reduce_table(dt_ref, bias_ref):
    w = 2 * QBLK
    acc = dt_ref[0:CHUNK, :]
    for c in range(1, NCH):
        acc = acc + pltpu.roll(dt_ref[CHUNK * c:CHUNK * (c + 1), :], w - CHUNK * c, 1)
    bias_ref[...] = acc
    out = jnp.zeros((1, w), F32)
    for qi in range(CHUNK):
        out = out + pltpu.roll(bias_ref[qi:qi + 1, :], ((CHUNK - 1) - qi) % w, 1)
    return out


def _attn_softmax(s, table, first_block):
    s = s * (HEAD ** -0.5) + table
    col = lax.broadcasted_iota(jnp.int32, s.shape, 1)
    s = jnp.where(jnp.logical_and(first_block, col < QBLK), NEG, s)
    e = jnp.exp(s - jnp.max(s, axis=-1, keepdims=True))
    return e * (1.0 / jnp.sum(e, axis=-1, keepdims=True))


def attn_fwd(proj, q0, k0, v0, tv, gq, gk, seq, comm=None):
    n = proj.shape[0]
    heads = tv.shape[0]
    nq = seq // QBLK
    cur, prev, vec, gain = _attn_specs(nq)

    def body(q_ref, kp_ref, kc_ref, vp_ref, vc_ref, tv_ref, gq_ref, gk_ref, o_ref, bias_ref, tab_ref):
        @pl.when(jnp.logical_and(pl.program_id(1) == 0, pl.program_id(2) == 0))
        def _():
            for h in range(2):
                _attn_build_table(tv_ref[h], bias_ref, tab_ref.at[h])

        low, m = _pair_masks()
        qn, _, _ = _pair_rms_fwd(q_ref[...], gq_ref[...], low, m[0])
        kn, _, _ = _pair_rms_fwd(jnp.concatenate([kp_ref[...], kc_ref[...]], axis=0), gk_ref[...], low, m[0])
        knb = kn.astype(BF16)
        v = jnp.concatenate([vp_ref[...], vc_ref[...]], axis=0)
        o = jnp.zeros((QBLK, PAIR), F32)
        for h in range(2):
            p = _attn_softmax(_dot_nt((qn * m[h]).astype(BF16), knb), tab_ref[h], pl.program_id(2) == 0)
            o = o + _dot(p.astype(BF16), (v * m[h]).astype(BF16))
        o_ref[...] = o

    return _hosted_call(
        body, comm,
        name="attn_fwd",
        grid=(heads // 2, n // seq, nq),
        in_specs=[cur(q0), prev(k0), cur(k0), prev(v0), cur(v0), vec, gain, gain],
        out_specs=[cur()],
        out_shape=[jax.ShapeDtypeStruct((n, heads * HEAD), F32)],
        scratch_shapes=[pltpu.VMEM((CHUNK, 2 * QBLK), F32), pltpu.VMEM((2, QBLK, 2 * QBLK), F32)],
        args=(proj, proj, proj, proj, proj, tv, gq, gk),
    )


def attn_bwd(do, proj, q0, k0, v0, tv, gq, gk, seq, comm=None):
    n = proj.shape[0]
    heads = tv.shape[0]
    nb = n // seq
    nq = seq // QBLK
    cur, prev, vec, gain = _attn_specs(nq)
    keyblk = pl.BlockSpec((QBLK, PAIR), lambda p, b, i: (b * nq + jnp.maximum(i - 1, 0), p))

    def body(do_ref, q_ref, kp_ref, kc_ref, vp_ref, vc_ref, tv_ref, gq_ref, gk_ref,
             dq_ref, dk_ref, dv_ref, dtv_ref, dgq_ref, dgk_ref, bias_ref, tab_ref, dt_ref, ck_ref, cv_ref):
        pp, b, i = pl.program_id(0), pl.program_id(1), pl.program_id(2)
        head_start = jnp.logical_and(b == 0, i == 0)

        @pl.when(head_start)
        def _():
            for h in range(2):
                _attn_build_table(tv_ref[h], bias_ref, tab_ref.at[h])

        @pl.when(i < nq)
        def _():
            low, m = _pair_masks()
            gq_, gk_ = gq_ref[...], gk_ref[...]
            qn, qhat, qrs = _pair_rms_fwd(q_ref[...], gq_, low, m[0])
            kn, khat, krs = _pair_rms_fwd(jnp.concatenate([kp_ref[...], kc_ref[...]], axis=0), gk_, low, m[0])
            knb = kn.astype(BF16)
            vb = jnp.concatenate([vp_ref[...], vc_ref[...]], axis=0).astype(BF16)
            do_ = do_ref[...]
            dv = jnp.zeros((2 * QBLK, PAIR), F32)
            dqn = jnp.zeros((QBLK, PAIR), F32)
            dkn = jnp.zeros((2 * QBLK, PAIR), F32)
            for h in range(2):
                qh = (qn * m[h]).astype(BF16)
                doh = (do_ * m[h]).astype(BF16)
                p = _attn_softmax(_dot_nt(qh, knb), tab_ref[h], i == 0)
                dv = dv + _dot_tn(p.astype(BF16), doh)
                dp = _dot_nt(doh, vb)
                ds = p * (dp - jnp.sum(p * dp, axis=-1, keepdims=True))
                _acc_out(dt_ref.at[h], ds, head_start)
                dsb = (ds * (HEAD ** -0.5)).astype(BF16)
                dqn = dqn + _dot(dsb, (kn * m[h]).astype(BF16))
                dkn = dkn + _dot_tn(dsb, qh)
            dq, dgq_rows = _pair_rms_bwd(dqn, qhat, qrs, gq_, low, m[0])
            dk, dgk_rows = _pair_rms_bwd(dkn, khat, krs, gk_, low, m[0])
            dq_ref[...] = dq

            @pl.when(i == 0)
            def _():
                dk_ref[...] = dk[:QBLK]
                dv_ref[...] = dv[:QBLK]

            @pl.when(i > 0)
            def _():
                dk_ref[...] = ck_ref[...] + dk[:QBLK]
                dv_ref[...] = cv_ref[...] + dv[:QBLK]

            ck_ref[...] = dk[QBLK:]
            cv_ref[...] = dv[QBLK:]
            first = jnp.logical_and(pp == 0, head_start)
            _acc_out(dgq_ref, jnp.sum(dgq_rows, axis=0, keepdims=True), first)
            _acc_out(dgk_ref, jnp.sum(dgk_rows, axis=0, keepdims=True), first)

        @pl.when(i == nq)
        def _():
            dk_ref[...] = ck_ref[...]
            dv_ref[...] = cv_ref[...]

        @pl.when(jnp.logical_and(b == nb - 1, i == nq))
        def _():
            for h in range(2):
                dtv_ref[h] = _attn_reduce_table(dt_ref.at[h], bias_ref)

    tok = jax.ShapeDtypeStruct((n, heads * HEAD), F32)
    return _hosted_call(
        body, comm,
        name="attn_bwd",
        grid=(heads // 2, nb, nq + 1),
        in_specs=[cur(), cur(q0), prev(k0), cur(k0), prev(v0), cur(v0), vec, gain, gain],
        out_specs=[cur(), keyblk, keyblk, vec, gain, gain],
        out_shape=[tok, tok, tok, jax.ShapeDtypeStruct(tv.shape, F32),
                   jax.ShapeDtypeStruct((1, PAIR), F32), jax.ShapeDtypeStruct((1, PAIR), F32)],
        scratch_shapes=[pltpu.VMEM((CHUNK, 2 * QBLK), F32), pltpu.VMEM((2, QBLK, 2 * QBLK), F32),
                        pltpu.VMEM((2, QBLK, 2 * QBLK), F32), pltpu.VMEM((QBLK, PAIR), F32),
                        pltpu.VMEM((QBLK, PAIR), F32)],
        args=(do, proj, proj, proj, proj, proj, tv, gq, gk),
    )


CONV_TC = 512


def _ln_fwd(h1, g, b):
    mu = jnp.mean(h1, axis=-1, keepdims=True)
    xc = h1 - mu
    rs = lax.rsqrt(jnp.mean(xc * xc, axis=-1, keepdims=True) + EPS)
    yhat = xc * rs
    return yhat * g + b, yhat, rs


def _glu(z, dc):
    return z[:, :dc] * _sig(z[:, dc:])


def conv_fwd(proj, zcol, seq, w, bdw, lng, lnb):
    n = proj.shape[0]
    dc = w.shape[1]
    tc = min(CONV_TC, seq)
    nt = seq // tc

    def body(z_ref, zp_ref, w_ref, b_ref, g_ref, lb_ref, h1_ref, o_ref, ext_ref):
        live = jnp.where(pl.program_id(1) == 0, 0.0, 1.0)
        ext_ref[pl.ds(0, HALO), :] = _glu(zp_ref[...], dc) * live
        ext_ref[pl.ds(HALO, tc), :] = _glu(z_ref[...], dc)
        acc = jnp.zeros((tc, dc), F32) + b_ref[...]
        for j in range(CONV_W):
            acc = acc + w_ref[j:j + 1, :] * ext_ref[pl.ds(HALO - (CONV_W - 1) + j, tc), :]
        h1_ref[...] = acc
        ln, _, _ = _ln_fwd(acc, g_ref[...], lb_ref[...])
        o_ref[...] = ln * _sig(ln)

    full = lambda shape: pl.BlockSpec(shape, lambda b, t: (0, 0))
    return _call(
        body,
        name="conv_fwd",
        grid=(n // seq, nt),
        in_specs=[
            pl.BlockSpec((tc, 2 * dc), lambda b, t: (b * nt + t, zcol)),
            pl.BlockSpec((HALO, 2 * dc), lambda b, t: (jnp.maximum((b * seq + t * tc) // HALO - 1, 0), zcol)),
            full((HALO, dc)), full((1, dc)), full((1, dc)), full((1, dc)),
        ],
        out_specs=[
            pl.BlockSpec((tc, dc), lambda b, t: (b * nt + t, 0)),
            pl.BlockSpec((tc, dc), lambda b, t: (b * nt + t, 0)),
        ],
        out_shape=[jax.ShapeDtypeStruct((n, dc), F32), jax.ShapeDtypeStruct((n, dc), F32)],
        scratch_shapes=[pltpu.VMEM((tc + HALO, dc), F32)],
        compiler_params=_params(),
    )(proj, proj, w, bdw, lng, lnb)


def conv_bwd(dco, h1, proj, zcol, seq, w, bdw, lng, lnb):
    n = proj.shape[0]
    dc = w.shape[1]
    tc = min(CONV_TC, seq)
    nt = seq // tc
    nrow = n // HALO

    def body(do_ref, don_ref, h1_ref, h1n_ref, z_ref, zp_ref, w_ref, g_ref, lb_ref,
             dz_ref, dw_ref, db_ref, dg_ref, dlb_ref, ext_ref, dext_ref):
        b, t = pl.program_id(0), pl.program_id(1)
        first = jnp.logical_and(b == 0, t == 0)
        g, lb = g_ref[...], lb_ref[...]

        def dh1_of(do, h1):
            ln, yhat, rs = _ln_fwd(h1, g, lb)
            s = _sig(ln)
            dln = do * (s * (1.0 + ln * (1.0 - s)))
            dyh = dln * g
            dh1 = rs * (dyh - jnp.mean(dyh, axis=-1, keepdims=True)
                        - yhat * jnp.mean(dyh * yhat, axis=-1, keepdims=True))
            return dh1, dln, yhat

        dh1, dln, yhat = dh1_of(do_ref[...], h1_ref[...])
        dh1n, _, _ = dh1_of(don_ref[...], h1n_ref[...])
        _acc_out(dg_ref, jnp.sum(dln * yhat, axis=0, keepdims=True), first)
        _acc_out(dlb_ref, jnp.sum(dln, axis=0, keepdims=True), first)
        _acc_out(db_ref, jnp.sum(dh1, axis=0, keepdims=True), first)

        dext_ref[pl.ds(0, tc), :] = dh1
        dext_ref[pl.ds(tc, HALO), :] = dh1n * jnp.where(t == nt - 1, 0.0, 1.0)
        z = z_ref[...]
        ext_ref[pl.ds(0, HALO), :] = _glu(zp_ref[...], dc) * jnp.where(t == 0, 0.0, 1.0)
        ext_ref[pl.ds(HALO, tc), :] = _glu(z, dc)

        @pl.when(first)
        def _():
            dw_ref[...] = jnp.zeros_like(dw_ref)

        dh0 = jnp.zeros((tc, dc), F32)
        for j in range(CONV_W):
            dh0 = dh0 + w_ref[j:j + 1, :] * dext_ref[pl.ds(CONV_W - 1 - j, tc), :]
            dw_ref[j:j + 1, :] += jnp.sum(dh1 * ext_ref[pl.ds(HALO - (CONV_W - 1) + j, tc), :],
                                          axis=0, keepdims=True)
        za, sg = z[:, :dc], _sig(z[:, dc:])
        dz_ref[:, :dc] = dh0 * sg
        dz_ref[:, dc:] = dh0 * za * sg * (1.0 - sg)

    full = lambda shape: pl.BlockSpec(shape, lambda b, t: (0, 0))
    cur = lambda wd, col=0: pl.BlockSpec((tc, wd), lambda b, t: (b * nt + t, col))
    nxt = pl.BlockSpec((HALO, dc), lambda b, t: (jnp.minimum((b * seq + (t + 1) * tc) // HALO, nrow - 1), 0))
    return _call(
        body,
        name="conv_bwd",
        grid=(n // seq, nt),
        in_specs=[
            cur(dc), nxt, cur(dc), nxt, cur(2 * dc, zcol),
            pl.BlockSpec((HALO, 2 * dc), lambda b, t: (jnp.maximum((b * seq + t * tc) // HALO - 1, 0), zcol)),
            full((HALO, dc)), full((1, dc)), full((1, dc)),
        ],
        out_specs=[cur(2 * dc), full((HALO, dc)), full((1, dc)), full((1, dc)), full((1, dc))],
        out_shape=[
            jax.ShapeDtypeStruct((n, 2 * dc), F32),
            jax.ShapeDtypeStruct((HALO, dc), F32),
            jax.ShapeDtypeStruct((1, dc), F32),
            jax.ShapeDtypeStruct((1, dc), F32),
            jax.ShapeDtypeStruct((1, dc), F32),
        ],
        scratch_shapes=[pltpu.VMEM((tc + HALO, dc), F32), pltpu.VMEM((tc + HALO, dc), F32)],
        compiler_params=_params(),
    )(dco, dco, h1, h1, proj, proj, w, lng, lnb)


def _merge_common(l0, l1, l2, bg, so, ao, co, wbs, wba, wbc, d):
    ys = _dot(so.astype(BF16), wbs)
    ya = _dot(ao.astype(BF16), wba)
    yc = _dot(co.astype(BF16), wbc)
    gs = _sig(l0 + bg[:, :d])
    ga = _sig(l1 + bg[:, d:2 * d])
    gc = _sig(l2 + bg[:, 2 * d:])
    return (ys, ya, yc), (gs, ga, gc)


def _merge_specs(tm, d, dss, da, dc, layer):
    row = lambda w, col=0: pl.BlockSpec((tm, w), lambda i: (i, col))
    full = lambda r, c: pl.BlockSpec((r, c), lambda i: (0, 0))
    stacked = lambda r: pl.BlockSpec((1, r, d), lambda i: (layer, 0, 0))
    acts = [row(d, 0), row(d, 1), row(d, 2), full(1, 3 * d), row(dss), row(da), row(dc)]
    weights = [stacked(dss), stacked(da), stacked(dc), stacked(d)]
    return row, full, acts, weights


def merge_fwd(x, proj, bg, so, ao, co, wbs, wba, wbc, wout, layer):
    n, d = x.shape
    tm = _tile(n, 256, 8)
    row, full, acts, weights = _merge_specs(tm, d, so.shape[1], ao.shape[1], co.shape[1], layer)

    def body(x_ref, l0_ref, l1_ref, l2_ref, bg_ref, so_ref, ao_ref, co_ref,
             wbs_ref, wba_ref, wbc_ref, wo_ref, o_ref):
        (ys, ya, yc), (gs, ga, gc) = _merge_common(
            l0_ref[...], l1_ref[...], l2_ref[...], bg_ref[...], so_ref[...], ao_ref[...], co_ref[...],
            wbs_ref[0], wba_ref[0], wbc_ref[0], d)
        merged = gs * ys + ga * ya + gc * yc
        o_ref[...] = x_ref[...] + _dot(merged.astype(BF16), wo_ref[0])

    return _call(
        body,
        name="merge_fwd",
        grid=(n // tm,),
        in_specs=[row(d)] + acts + weights,
        out_specs=row(d),
        out_shape=jax.ShapeDtypeStruct((n, d), F32),
        compiler_params=_params(),
    )(x, proj, proj, proj, bg, so, ao, co, wbs, wba, wbc, wout)


def merge_bwd(dx, proj, bg, so, ao, co, wbs, wba, wbc, wout, layer):
    n, d = dx.shape
    dss, da, dc = so.shape[1], ao.shape[1], co.shape[1]
    tm = _tile(n, 256, 8)
    row, full, acts, weights = _merge_specs(tm, d, dss, da, dc, layer)

    def body(dx_ref, l0_ref, l1_ref, l2_ref, bg_ref, so_ref, ao_ref, co_ref, wbs_ref, wba_ref, wbc_ref, wo_ref,
             dl_ref, dso_ref, dao_ref, dco_ref, dbg_ref, mg_ref, dxb_ref, dys_ref, dya_ref, dyc_ref):
        wbs, wba, wbc = wbs_ref[0], wba_ref[0], wbc_ref[0]
        (ys, ya, yc), (gs, ga, gc) = _merge_common(
            l0_ref[...], l1_ref[...], l2_ref[...], bg_ref[...], so_ref[...], ao_ref[...], co_ref[...],
            wbs, wba, wbc, d)
        mg_ref[...] = (gs * ys + ga * ya + gc * yc).astype(BF16)
        dxb = dx_ref[...].astype(BF16)
        dxb_ref[...] = dxb
        dm = _dot_nt(dxb, wo_ref[0])
        first = pl.program_id(0) == 0
        for k, (y, g, w, dy_ref, db_ref) in enumerate((
                (ys, gs, wbs, dys_ref, dso_ref), (ya, ga, wba, dya_ref, dao_ref), (yc, gc, wbc, dyc_ref, dco_ref))):
            dl = dm * y * g * (1.0 - g)
            dl_ref[:, k * d:(k + 1) * d] = dl.astype(BF16)
            _acc_out(dbg_ref.at[:, k * d:(k + 1) * d], jnp.sum(dl, axis=0, keepdims=True), first)
            dy = (dm * g).astype(BF16)
            dy_ref[...] = dy
            db_ref[...] = _dot_nt(dy, w)

    bf = lambda w: jax.ShapeDtypeStruct((n, w), BF16)
    return _call(
        body,
        name="merge_bwd",
        grid=(n // tm,),
        in_specs=[row(d)] + acts + weights,
        out_specs=[row(3 * d), row(dss), row(da), row(dc), full(1, 3 * d),
                   row(d), row(d), row(d), row(d), row(d)],
        out_shape=[bf(3 * d), jax.ShapeDtypeStruct((n, dss), F32), jax.ShapeDtypeStruct((n, da), F32),
                   jax.ShapeDtypeStruct((n, dc), F32), jax.ShapeDtypeStruct((1, 3 * d), F32),
                   bf(d), bf(d), bf(d), bf(d), bf(d)],
        compiler_params=_params(),
    )(dx, proj, proj, proj, bg, so, ao, co, wbs, wba, wbc, wout)


def loss_head(y, target):
    n, d = y.shape
    tm = _tile(n, 512, 8)

    def body(y_ref, t_ref, dy_ref, l_ref):
        e = y_ref[...] - t_ref[...]
        dy_ref[...] = e * (1.0 / d)
        part = 0.5 * jnp.sum(jnp.sum(e * e, axis=-1, keepdims=True) * (1.0 / d), axis=0, keepdims=True)
        _acc_out(l_ref, part, pl.program_id(0) == 0)

    return _call(
        body,
        name="loss_head",
        grid=(n // tm,),
        in_specs=[pl.BlockSpec((tm, d), lambda i: (i, 0)), pl.BlockSpec((tm, d), lambda i: (i, 0))],
        out_specs=[pl.BlockSpec((tm, d), lambda i: (i, 0)), pl.BlockSpec((1, 1), lambda i: (0, 0))],
        out_shape=[jax.ShapeDtypeStruct((n, d), F32), jax.ShapeDtypeStruct((1, 1), F32)],
        compiler_params=_params(),
    )(y, target)


def _mesh_pos():
    return lax.axis_index("x"), lax.axis_index("y"), lax.axis_index("c")


ANY = pl.BlockSpec(memory_space=pl.ANY)


def _comm_sems(na):
    return [pltpu.SemaphoreType.DMA((na, 7)), pltpu.SemaphoreType.DMA((na, 7)), pltpu.SemaphoreType.DMA((na,))]


def _gather_plan(x_refs, out_refs, sems):
    na = len(x_refs)
    send_sems, recv_sems, local_sems = sems
    x, y, c = _mesh_pos()
    me, sibling = (x, y, c), (x, y, 1 - c)
    chips = [(1 - x, y), (x, 1 - y), (1 - x, 1 - y)]

    def slot(a, px, py, pc):
        return out_refs[a].at[4 * px + 2 * py + pc]

    def copy(a, k, block, to, src=None):
        return pltpu.make_async_remote_copy(
            src_ref=slot(a, *block) if src is None else src, dst_ref=slot(a, *block),
            send_sem=send_sems.at[a, k], recv_sem=recv_sems.at[a, k],
            device_id=to, device_id_type=pl.DeviceIdType.MESH)

    mine = [pltpu.make_async_copy(x_refs[a], slot(a, *me), local_sems.at[a]) for a in range(na)]
    first = []
    for a in range(na):
        first.append(copy(a, 0, me, sibling, src=x_refs[a]))
        first += [copy(a, 1 + j, me, (*chip, c), src=x_refs[a]) for j, chip in enumerate(chips)]

    def start():
        for cp in mine + first:
            cp.start()

    def finish():
        passed = []
        for j, chip in enumerate(chips):
            for a in range(na):
                copy(a, 1 + j, (*chip, c), me).wait_recv()
                fwd = copy(a, 4 + j, (*chip, c), sibling)
                fwd.start()
                passed.append(fwd)
        for a in range(na):
            copy(a, 0, sibling, me).wait_recv()
            for j, chip in enumerate(chips):
                copy(a, 4 + j, (*chip, 1 - c), me).wait_recv()
        for cp in first + passed:
            cp.wait_send()
        for cp in mine:
            cp.wait()

    return start, finish


def _gather_out(shards):
    return [jax.ShapeDtypeStruct((N_DEV,) + s.shape, s.dtype) for s in shards]


def all_gather(shards):
    na = len(shards)

    def body(*refs):
        start, finish = _gather_plan(refs[:na], refs[na:2 * na], refs[2 * na:])
        start()
        finish()

    return _call(
        body,
        name="all_gather",
        out_shape=_gather_out(shards),
        in_specs=[ANY] * na,
        out_specs=[ANY] * na,
        scratch_shapes=_comm_sems(na),
    )(*shards)


def _hosted_call(body, comm, *, name, grid, in_specs, out_specs, out_shape, scratch_shapes, args):
    if comm is None:
        res = _call(body, name=name, grid=grid, in_specs=in_specs, out_specs=out_specs, out_shape=out_shape,
                    scratch_shapes=scratch_shapes, compiler_params=_params())(*args)
        return res, []
    plan, arrays, c_out = comm
    n_in, n_out, n_scr, ci, co = len(in_specs), len(out_specs), len(scratch_shapes), len(arrays), len(c_out)

    def hosted(*refs):
        ins, cins = refs[:n_in], refs[n_in:n_in + ci]
        o0 = n_in + ci
        outs, couts = refs[o0:o0 + n_out], refs[o0 + n_out:o0 + n_out + co]
        s0 = o0 + n_out + co
        scr, sems = refs[s0:s0 + n_scr], refs[s0 + n_scr:]
        ids = [pl.program_id(ax) for ax in range(len(grid))]
        first = functools.reduce(jnp.logical_and, [i == 0 for i in ids])
        last = functools.reduce(jnp.logical_and, [i == g - 1 for i, g in zip(ids, grid)])
        start, finish = plan(cins, couts, sems)
        pl.when(first)(start)
        body(*ins, *outs, *scr)
        pl.when(last)(finish)

    res = _call(hosted, name=name + "_comm", grid=grid, in_specs=list(in_specs) + [ANY] * ci,
                out_specs=list(out_specs) + [ANY] * co, out_shape=list(out_shape) + list(c_out),
                scratch_shapes=list(scratch_shapes) + _comm_sems(max(ci, co)),
                compiler_params=_params())(*args, *arrays)
    return res[:n_out], res[n_out:]


def _exchange_plan(bcast=()):
    def plan(s_refs, r_refs, sems):
        na = len(s_refs)
        send_sems, recv_sems, local_sems = sems
        x, y, c = _mesh_pos()
        me = 4 * x + 2 * y + c

        def peer(k):
            px = (1 - x) if k & 4 else x
            py = (1 - y) if k & 2 else y
            pc = (1 - c) if k & 1 else c
            return (px, py, pc), 4 * px + 2 * py + pc

        def src_dst(a, pid, slot):
            return (s_refs[a] if a in bcast else s_refs[a].at[pid]), r_refs[a].at[slot]

        def copy(a, k):
            to, pid = peer(k)
            src, dst = src_dst(a, pid, me)
            return pltpu.make_async_remote_copy(
                src_ref=src, dst_ref=dst, send_sem=send_sems.at[a, k - 1], recv_sem=recv_sems.at[a, k - 1],
                device_id=to, device_id_type=pl.DeviceIdType.MESH)

        def arrival(a, k):
            _, pid = peer(k)
            src, dst = src_dst(a, pid, pid)
            return pltpu.make_async_remote_copy(
                src_ref=src, dst_ref=dst, send_sem=send_sems.at[a, k - 1], recv_sem=recv_sems.at[a, k - 1],
                device_id=(x, y, c), device_id_type=pl.DeviceIdType.MESH)

        mine = [pltpu.make_async_copy(*src_dst(a, me, me), local_sems.at[a]) for a in range(na)]
        sends = [copy(a, k) for k in range(1, N_DEV) for a in range(na)]

        def start():
            for cp in mine + sends:
                cp.start()

        def finish():
            for k in range(1, N_DEV):
                for a in range(na):
                    arrival(a, k).wait_recv()
            for cp in sends:
                cp.wait_send()
            for cp in mine:
                cp.wait()

        return start, finish

    return plan


def _exchange_out(slabs, bcast=()):
    return [jax.ShapeDtypeStruct(((N_DEV,) + s.shape) if a in bcast else s.shape, s.dtype)
            for a, s in enumerate(slabs)]


def grad_exchange(slabs, small):
    arrays = list(slabs) + [small]
    na = len(arrays)
    bcast = (na - 1,)

    def body(*refs):
        start, finish = _exchange_plan(bcast)(refs[:na], refs[na:2 * na], refs[2 * na:])
        start()
        finish()

    return _call(
        body,
        name="grad_exchange",
        out_shape=_exchange_out(arrays, bcast),
        in_specs=[ANY] * na,
        out_specs=[ANY] * na,
        scratch_shapes=_comm_sems(na),
    )(*arrays)


ADAM_BLOCK = 256 * 1024


def sum_adamw(recv, w, m, v, layer, prev, name):
    depth, rows, cols = w.shape
    tr = _tile(rows, max(8, ADAM_BLOCK // cols // 8 * 8), 8)
    c1 = 1.0 / (1.0 - ADAM_B1 ** ADAM_STEP)
    c2 = 1.0 / (1.0 - ADAM_B2 ** ADAM_STEP)

    def body(r_ref, w_ref, m_ref, v_ref, *rest):
        g_ref, d_ref, mo_ref, vo_ref = rest[-4:]
        g = r_ref[0].astype(F32)
        for s in range(1, N_DEV):
            g = g + r_ref[s].astype(F32)
        mn = ADAM_B1 * m_ref[0] + (1.0 - ADAM_B1) * g
        vn = ADAM_B2 * v_ref[0] + (1.0 - ADAM_B2) * (g * g)
        g_ref[0] = g
        mo_ref[0] = mn
        vo_ref[0] = vn
        d_ref[0] = -ADAM_LR * ((mn * c1) / (jnp.sqrt(vn * c2) + ADAM_EPS) + ADAM_WD * w_ref[0])

    blk = pl.BlockSpec((1, tr, cols), lambda i: (layer, i, 0))
    out = jax.ShapeDtypeStruct((depth, rows, cols), F32)
    in_specs = [pl.BlockSpec((N_DEV, tr, cols), lambda i: (0, i, 0)), blk, blk, blk]
    args = [recv, w, m, v]
    extra = {}
    if prev is not None:
        in_specs += [ANY] * 4
        args += list(prev)
        extra["input_output_aliases"] = {4 + j: j for j in range(4)}
    return _call(
        body,
        name=name,
        grid=(rows // tr,),
        in_specs=in_specs,
        out_specs=[blk, blk, blk, blk],
        out_shape=[out, out, out, out],
        compiler_params=_params(),
        **extra,
    )(*args)


def _attn_bias_vector(rel_bias):
    h = rel_bias.shape[0]
    n_far = BAND - MAX_REL
    n_near = BAND + CHUNK - 1 - n_far
    far = jnp.broadcast_to(rel_bias[:, 2 * MAX_REL:], (h, n_far))
    near = rel_bias[:, 2 * MAX_REL - n_near:2 * MAX_REL][:, ::-1]
    pad = jnp.zeros((h, 2 * QBLK - n_far - n_near), F32)
    return jnp.concatenate([far, near, pad], axis=1)[:, None, :]


def _s5_prepare(lre, lim, ldt, bre, bim, cre, cim):
    g, p = lre.shape
    lr = jnp.minimum(lre, -1e-4)
    dt = jnp.exp(ldt)[:, None]
    mag = jnp.exp(lr * dt)
    ar = mag * jnp.cos(lim * dt)
    ai = mag * jnp.sin(lim * dt)
    den = lr * lr + lim * lim
    coef_r = ((ar - 1.0) * lr + ai * lim) / den
    coef_i = (ai * lr - (ar - 1.0) * lim) / den
    bbar_r = coef_r[..., None] * bre - coef_i[..., None] * bim
    bbar_i = coef_r[..., None] * bim + coef_i[..., None] * bre
    eye = jnp.eye(g, dtype=F32)
    bd_in = lambda b: jnp.einsum("gpc,gh->gchp", b, eye).reshape(g * S5_GROUP, g * p)
    bd_out = lambda c: jnp.einsum("gcp,gh->gphc", c, eye).reshape(g * p, g * S5_GROUP)
    return (ar.reshape(1, g * p), ai.reshape(1, g * p), bd_in(bbar_r), bd_in(bbar_i), bd_out(cre), bd_out(cim))


SHARDED = ("ffn1_w_up", "ffn1_w_down", "w_in", "s5_w_glu", "w_br_s5", "w_br_attn", "conv_w_dw", "w_br_conv",
           "w_out", "ffn2_w_up", "ffn2_w_down")
WEIGHTS = ("ffn1_norm", "ffn1_w_up", "ffn1_w_down", "mix_norm", "w_in", "b_gate", "s5_lambda_re", "s5_lambda_im",
           "s5_log_dt", "s5_b_re", "s5_b_im", "s5_c_re", "s5_c_im", "s5_d", "s5_w_glu", "w_br_s5", "attn_q_gain",
           "attn_k_gain", "attn_rel_bias", "w_br_attn", "conv_w_dw", "conv_b_dw", "conv_ln_g", "conv_ln_b",
           "w_br_conv", "w_out", "ffn2_norm", "ffn2_w_up", "ffn2_w_down")
SMALL = tuple(nm for nm in WEIGHTS if nm not in SHARDED)
SMALL_LANES = 1024


def _cols_full(g):
    _, depth, k, nn = g.shape
    return g.transpose(1, 2, 0, 3).reshape(depth, k, N_DEV * nn)


def _rows_full(g):
    _, depth, r, cc = g.shape
    return g.transpose(1, 0, 2, 3).reshape(depth, N_DEV * r, cc)


def _cols_slabs(gfull):
    depth, k, c8 = gfull.shape
    return gfull.reshape(depth, k, N_DEV, c8 // N_DEV).transpose(0, 2, 1, 3)


def _small_pack(t):
    flat = jnp.concatenate([t[nm].reshape(-1) for nm in SMALL])
    rows = -(-flat.shape[0] // SMALL_LANES)
    rows = -(-rows // 8) * 8
    return jnp.pad(flat, (0, rows * SMALL_LANES - flat.shape[0])).reshape(rows, SMALL_LANES)


def _small_unpack(flat, like):
    flat = flat.reshape(-1)
    out, off = {}, 0
    for nm in SMALL:
        out[nm] = flat[off:off + like[nm].size].reshape(like[nm].shape)
        off += like[nm].size
    return out


def _step(x, target, w, m, v):
    bsz, seq, d = x.shape
    n = bsz * seq
    depth = w["ffn1_norm"].shape[0]
    da, dss, dc = d // 2, d // 4, d // 4
    gp =dss // S5_GROUP * S5_STATE
    mid = 3 * da + 2 * dc
    q0, k0, v0, z0, u0 = 3 * d, 3 * d + da, 3 * d + 2 * da, 3 * d + 3 * da, 3 * d + mid
    to_kernel_cols = lambda a: jnp.concatenate([a[..., dss + mid:], a[..., dss:dss + mid], a[..., :dss]], axis=-1)
    to_ref_cols = lambda a: jnp.concatenate([a[..., 3 * d + mid:], a[..., 3 * d:3 * d + mid], a[..., :3 * d]], axis=-1)

    def shard(key):
        nm, sl = (key[0], slice(key[1], key[1] + 1)) if isinstance(key, tuple) else (key, slice(None))
        return w[nm][sl] if nm == "conv_w_dw" else w[nm][sl].astype(BF16)

    mixer_names = ("w_in", "s5_w_glu", "w_br_s5", "w_br_attn", "w_br_conv", "conv_w_dw", "w_out")
    ffn1_of = lambda l: (("ffn1_w_up", l), ("ffn1_w_down", l))
    later_ffn1 = tuple(k for l in range(1, depth) for k in ffn1_of(l))
    gather_of = lambda keys: (_gather_plan, [shard(k) for k in keys], _gather_out([shard(k) for k in keys]))
    full = {}

    def take(keys, arrays):
        for key, g in zip(keys, arrays):
            nm = key[0] if isinstance(key, tuple) else key
            if nm in ("ffn1_w_up", "ffn2_w_up"):
                full[key] = g
            elif nm in ("ffn1_w_down", "ffn2_w_down", "w_out"):
                full[key] = _rows_full(g)
            elif nm == "w_in":
                full[key] = to_kernel_cols(_cols_full(g))
            else:
                full[key] = _cols_full(g)

    take(ffn1_of(0), all_gather([shard(k) for k in ffn1_of(0)]))
    nff = full[("ffn1_w_up", 0)].shape[3]

    row = lambda a: a.reshape(1, -1)
    saved = []
    xin = x.reshape(n, d)
    for l in range(depth):
        s = {"x0": xin}
        f1_up, f1_down = ffn1_of(l)
        (s["x1"], s["a1"], s["b1"]), got = ffn_fwd(xin, row(w["ffn1_norm"][l]), full[f1_up], full[f1_down], 0,
                                                   comm=gather_of(mixer_names) if l == 0 else None)
        if l == 0:
            take(mixer_names, got)
            conv_w = jnp.pad(full["conv_w_dw"], ((0, 0), (0, HALO - CONV_W), (0, 0)))
        (proj,), got = proj_fwd(s["x1"], row(w["mix_norm"][l]), full["w_in"], l,
                                comm=gather_of(later_ffn1) if l == 0 and later_ffn1 else None)
        if l == 0:
            take(later_ffn1, got)
        s["proj"] = proj
        prep_in = (w["s5_lambda_re"][l], w["s5_lambda_im"][l], w["s5_log_dt"][l], w["s5_b_re"][l], w["s5_b_im"][l],
                   w["s5_c_re"][l], w["s5_c_im"][l])
        (ar, ai, bm_r, bm_i, cm_r, cm_i), s["prep_vjp"] = jax.vjp(_s5_prepare, *prep_in)
        s["s5p"] = (ar, ai, bm_r.astype(BF16), bm_i.astype(BF16), cm_r.astype(BF16), cm_i.astype(BF16),
                    row(w["s5_d"][l]), full["s5_w_glu"], l)
        s["so"], s["xr"], s["xi"], s["ypre"] = s5_fwd(proj, u0 // dss, seq, *s["s5p"])
        tv, s["tv_vjp"] = jax.vjp(_attn_bias_vector, w["attn_rel_bias"][l])
        pair_gain = lambda g: jnp.tile(row(g), (1, 2))
        s["attnp"] = (q0, k0, v0, tv, pair_gain(w["attn_q_gain"][l]), pair_gain(w["attn_k_gain"][l]))
        (s["ao"],), got = attn_fwd(proj, *s["attnp"], seq,
                                   comm=gather_of(("ffn2_w_up", "ffn2_w_down")) if l == 0 else None)
        if l == 0:
            take(("ffn2_w_up", "ffn2_w_down"), got)
        s["convp"] = (conv_w[l], row(w["conv_b_dw"][l]), row(w["conv_ln_g"][l]), row(w["conv_ln_b"][l]))
        s["h1"], s["co"] = conv_fwd(proj, z0 // (2 * dc), seq, *s["convp"])
        s["mergep"] = (row(w["b_gate"][l]), s["so"], s["ao"], s["co"], full["w_br_s5"], full["w_br_attn"],
                       full["w_br_conv"], full["w_out"], l)
        s["x2"] = merge_fwd(s["x1"], proj, *s["mergep"])
        (xin, s["a2"], s["b2"]), _ = ffn_fwd(s["x2"], row(w["ffn2_norm"][l]), full["ffn2_w_up"], full["ffn2_w_down"], l)
        saved.append(s)

    dx, loss = loss_head(xin, target.reshape(n, d))
    loss = lax.psum(loss[0, 0], ("x", "y", "c"))

    small_g = {nm: [None] * depth for nm in SMALL}
    slab = {}

    recv = {}

    def exchange_of(names, l):
        arrays = [slab.pop(nm) for nm in names]
        return [(nm, l) for nm in names], (_exchange_plan(), arrays, _exchange_out(arrays))

    def ffn_grads(which, l, hn, dyb, dab_a, dab_b, act, send_up_now=False):
        up, down = which + "_w_up", which + "_w_down"
        up4, down4 = (1, N_DEV, d, nff), (1, FF_CHUNKS, nff, d)
        half = wgrad(hn, dab_a, "wg_ffn_up", up4, "col")
        slab[up] = wgrad(hn, dab_b, "wg_ffn_up", up4, "col", g0=FF_CHUNKS, prev=half).reshape(up4[1:])
        if send_up_now:
            keys, comm = exchange_of((up,), l)
            (dn,), got = wgrad(act, dyb, "wg_ffn_down", down4, "row", comm=comm)
            recv.update(zip(keys, got))
        else:
            dn = wgrad(act, dyb, "wg_ffn_down", down4, "row")
        slab[down] = dn.reshape(N_DEV, nff // 2, d)

    pending = None
    for l in reversed(range(depth)):
        s = saved[l]
        outs, got = ffn_bwd(dx, s["x2"], row(w["ffn2_norm"][l]), s["a2"], s["b2"],
                            full["ffn2_w_up"], full["ffn2_w_down"], l, comm=pending[1] if pending else None)
        if pending:
            recv.update(zip(pending[0], got))
        dx, dg, hn, dyb, dab_a, dab_b, act = outs
        small_g["ffn2_norm"][l] = dg
        ffn_grads("ffn2", l, hn, dyb, dab_a, dab_b, act)

        dlog, dso, dao, dco, dbg, mg, dxb, dys, dya, dyc = merge_bwd(dx, s["proj"], *s["mergep"])
        small_g["b_gate"][l] = dbg
        slab["w_out"] = wgrad(mg, dxb, "wg_out", (1, N_DEV, d // N_DEV, d), "row").reshape(N_DEV, d // N_DEV, d)
        for nm, act_in, dy_br in (("w_br_s5", s["so"], dys), ("w_br_attn", s["ao"], dya), ("w_br_conv", s["co"], dyc)):
            k_in = act_in.shape[1]
            slab[nm] = wgrad(act_in, dy_br, "wg_" + nm, (1, N_DEV, k_in, d // N_DEV), "col").reshape(
                N_DEV, k_in, d // N_DEV)

        dz, dwdw, dbdw, dlng, dlnb = conv_bwd(dco, s["h1"], s["proj"], z0 // (2 * dc), seq, *s["convp"])
        slab["conv_w_dw"] = _cols_slabs(dwdw[None, :CONV_W])[0].astype(BF16)
        small_g["conv_b_dw"][l], small_g["conv_ln_g"][l], small_g["conv_ln_b"][l] = dbdw, dlng, dlnb

        keys, comm = exchange_of(("ffn2_w_up", "ffn2_w_down"), l)
        outs, got = attn_bwd(dao, s["proj"], *s["attnp"], seq, comm=comm)
        recv.update(zip(keys, got))
        dq, dk, dv, dtv, dgq, dgk = outs
        small_g["attn_q_gain"][l] = dgq[:, :HEAD] + dgq[:, HEAD:]
        small_g["attn_k_gain"][l] = dgk[:, :HEAD] + dgk[:, HEAD:]
        small_g["attn_rel_bias"][l] = s["tv_vjp"](dtv)[0]

        du, gr, gi, dyb5, glb, dzb, dar, dai, dd = s5_bwd(dso, s["ypre"], s["proj"], u0 // dss, s["xr"], s["xi"],
                                                          seq, *s["s5p"])
        small_g["s5_d"][l] = dd
        one = lambda k1, k2: (1, 1, k1, k2)
        slab["s5_w_glu"] = _cols_slabs(wgrad(glb, dzb, "wg_s5_glu", one(dss, 2 * dss), "col")[0])[0]
        dcm_r = wgrad(s["xr"], dyb5, "wg_s5_c", one(gp, dss), "col", dtype=F32)[0, 0]
        dcm_i = -wgrad(s["xi"], dyb5, "wg_s5_c", one(gp, dss), "col", dtype=F32)[0, 0]
        dbm_r = wgrad(s["proj"], gr, "wg_s5_b", one(dss, gp), "col", a_cols=(u0, dss), dtype=F32)[0, 0]
        dbm_i = wgrad(s["proj"], gi, "wg_s5_b", one(dss, gp), "col", a_cols=(u0, dss), dtype=F32)[0, 0]
        pg = s["prep_vjp"]((dar, dai, dbm_r, dbm_i, dcm_r, dcm_i))
        for nm, gval in zip(("s5_lambda_re", "s5_lambda_im", "s5_log_dt", "s5_b_re", "s5_b_im", "s5_c_re", "s5_c_im"), pg):
            small_g[nm][l] = gval

        dproj = jnp.concatenate([dlog, dq.astype(BF16), dk.astype(BF16), dv.astype(BF16),
                                 dz.astype(BF16), du.astype(BF16)], axis=1)
        dx, dgm, hn = proj_bwd(dproj, dx, s["x1"], row(w["mix_norm"][l]), full["w_in"], l)
        small_g["mix_norm"][l] = dgm
        slab["w_in"] = _cols_slabs(to_ref_cols(wgrad(hn, dproj, "wg_in", one(d, 3 * d + mid + dss), "col")[0]))[0]

        keys, comm = exchange_of(mixer_names, l)
        f1_up, f1_down = ffn1_of(l)
        outs, got = ffn_bwd(dx, s["x0"], row(w["ffn1_norm"][l]), s["a1"], s["b1"],
                            full[f1_up], full[f1_down], 0, comm=comm)
        recv.update(zip(keys, got))
        dx, dg, hn, dyb, dab_a, dab_b, act = outs
        small_g["ffn1_norm"][l] = dg
        ffn_grads("ffn1", l, hn, dyb, dab_a, dab_b, act, send_up_now=(l == 0))
        pending = exchange_of(("ffn1_w_down",) if l == 0 else ("ffn1_w_up", "ffn1_w_down"), l)

    small_flat = _small_pack({nm: jnp.stack([g.reshape(w[nm].shape[1:]) for g in small_g[nm]]) for nm in SMALL})
    *got, recv_small = grad_exchange(pending[1][1], small_flat)
    recv.update(zip(pending[0], got))

    outs = {}
    for nm in SHARDED:
        shp = w[nm].shape
        as3 = lambda t: t.reshape(shp[0], -1, shp[-1])
        bufs = None
        for l in reversed(range(depth)):
            bufs = sum_adamw(recv[(nm, l)], as3(w[nm]), as3(m[nm]), as3(v[nm]), l, bufs, "adamw_" + nm)
        outs[nm] = [b.reshape(shp) for b in bufs]
    packed = sum_adamw(recv_small, _small_pack(w)[None], _small_pack(m)[None], _small_pack(v)[None], 0, None,
                       "adamw_small")
    unpacked = [_small_unpack(p, w) for p in packed]
    for nm in SMALL:
        outs[nm] = [u[nm] for u in unpacked]
    return loss, dx.reshape(x.shape), outs


def kernel(x, ffn1_norm, ffn1_w_up, ffn1_w_down, mix_norm, w_in, b_gate, s5_lambda_re, s5_lambda_im, s5_log_dt, s5_b_re, s5_b_im, s5_c_re, s5_c_im, s5_d, s5_w_glu, w_br_s5, attn_q_gain, attn_k_gain, attn_rel_bias, w_br_attn, conv_w_dw, conv_b_dw, conv_ln_g, conv_ln_b, w_br_conv, w_out, ffn2_norm, ffn2_w_up, ffn2_w_down, loss_target, m_ffn1_norm, m_ffn1_w_up, m_ffn1_w_down, m_mix_norm, m_w_in, m_b_gate, m_s5_lambda_re, m_s5_lambda_im, m_s5_log_dt, m_s5_b_re, m_s5_b_im, m_s5_c_re, m_s5_c_im, m_s5_d, m_s5_w_glu, m_w_br_s5, m_attn_q_gain, m_attn_k_gain, m_attn_rel_bias, m_w_br_attn, m_conv_w_dw, m_conv_b_dw, m_conv_ln_g, m_conv_ln_b, m_w_br_conv, m_w_out, m_ffn2_norm, m_ffn2_w_up, m_ffn2_w_down, v_ffn1_norm, v_ffn1_w_up, v_ffn1_w_down, v_mix_norm, v_w_in, v_b_gate, v_s5_lambda_re, v_s5_lambda_im, v_s5_log_dt, v_s5_b_re, v_s5_b_im, v_s5_c_re, v_s5_c_im, v_s5_d, v_s5_w_glu, v_w_br_s5, v_attn_q_gain, v_attn_k_gain, v_attn_rel_bias, v_w_br_attn, v_conv_w_dw, v_conv_b_dw, v_conv_ln_g, v_conv_ln_b, v_w_br_conv, v_w_out, v_ffn2_norm, v_ffn2_w_up, v_ffn2_w_down):
    args = locals()
    w = {nm: args[nm] for nm in WEIGHTS}
    m = {nm: args["m_" + nm] for nm in WEIGHTS}
    v = {nm: args["v_" + nm] for nm in WEIGHTS}
    loss, gx, outs = _step(x, loss_target, w, m, v)
    return (loss, gx, *[outs[nm][0] for nm in WEIGHTS], *[outs[nm][1] for nm in WEIGHTS],
            *[outs[nm][2] for nm in WEIGHTS], *[outs[nm][3] for nm in WEIGHTS])
```

```python
import functools
import math

import numpy as np
import jax
import jax.numpy as jnp
from jax import lax
from jax.experimental import pallas as pl
from jax.experimental.pallas import tpu as pltpu

F32 = jnp.float32
BF16 = jnp.bfloat16

CHUNK = 64
N_LEFT = 8
QBLK = CHUNK * N_LEFT
HEAD = 64
MAX_REL = 128
S5_GROUP = 16
S5_STATE = 64
CONV_W = 31
HALO = 32
EPS = 1e-6
NEG = -1e30
ADAM_LR, ADAM_B1, ADAM_B2, ADAM_EPS, ADAM_WD, ADAM_STEP = 0.001, 0.9, 0.999, 1e-08, 0.01, 10
N_DEV = 8
VMEM_LIMIT = 56 * 1024 * 1024


def _call(body, **kw):
    return pl.pallas_call(body, **kw)


def _params(**kw):
    return pltpu.CompilerParams(vmem_limit_bytes=VMEM_LIMIT, **kw)


def _tile(n, cap, unit=128):
    if n <= cap:
        return n
    d = (cap // unit) * unit
    while d >= unit:
        if n % d == 0:
            return d
        d -= unit
    raise ValueError(f"no tile for {n} under {cap}")


def _dot(a, b):
    return jnp.dot(a, b, preferred_element_type=F32)


def _dot_nt(a, b):
    return lax.dot_general(a, b, (((1,), (1,)), ((), ())), preferred_element_type=F32)


def _dot_tn(a, b):
    return lax.dot_general(a, b, (((0,), (0,)), ((), ())), preferred_element_type=F32)


def _sig(x):
    return 1.0 / (1.0 + jnp.exp(-x))


def _rms_fwd(x, g):
    rs = lax.rsqrt(jnp.mean(x * x, axis=-1, keepdims=True) + EPS)
    xhat = x * rs
    return xhat * g, xhat, rs


def _rms_bwd(dh, xhat, rs, g):
    dxh = dh * g
    dx = rs * (dxh - xhat * jnp.mean(dxh * xhat, axis=-1, keepdims=True))
    return dx, dh * xhat


_GELU_C = math.sqrt(2.0 / math.pi)


def _gelu(x):
    return 0.5 * x * (1.0 + jnp.tanh(_GELU_C * (x + 0.044715 * x * x * x)))


def _gelu_grad(x):
    t = jnp.tanh(_GELU_C * (x + 0.044715 * x * x * x))
    return 0.5 * (1.0 + t) + 0.5 * x * (1.0 - t * t) * _GELU_C * (1.0 + 3.0 * 0.044715 * x * x)


def _acc_out(ref, val, first):
    @pl.when(first)
    def _():
        ref[...] = val

    @pl.when(jnp.logical_not(first))
    def _():
        ref[...] += val


FF_CHUNKS = N_DEV // 2


def _ffn_resident(w_up, w_down, layer):
    _, _, d, nn = w_up.shape
    depth = w_down.shape[0]
    wu_spec = pl.BlockSpec((N_DEV, 1, d, nn), lambda i, j: (0, layer, 0, 0), pipeline_mode=pl.Buffered(1))
    wd_spec = pl.BlockSpec((1, FF_CHUNKS, nn, d), lambda i, j: (layer, 0, 0, 0), pipeline_mode=pl.Buffered(1))
    return wu_spec, wd_spec, w_down.reshape(depth, FF_CHUNKS, nn, d)


def ffn_fwd(x, g, w_up, w_down, layer, comm=None):
    n, d = x.shape
    nn = w_up.shape[3]
    tm = _tile(n, 512, 8)
    wu_spec, wd_spec, w_down = _ffn_resident(w_up, w_down, layer)

    def body(x_ref, g_ref, wu_ref, wd_ref, xo_ref, a_ref, b_ref, hn_ref, acc_ref):
        j = pl.program_id(1)

        @pl.when(j == 0)
        def _():
            h, _, _ = _rms_fwd(x_ref[...], g_ref[...])
            hn_ref[...] = h.astype(BF16)
            acc_ref[...] = jnp.zeros_like(acc_ref)

        hn = hn_ref[...]
        a = _dot(hn, wu_ref[j, 0])
        b = _dot(hn, wu_ref[j + FF_CHUNKS, 0])
        a_ref[0] = a.astype(BF16)
        b_ref[0] = b.astype(BF16)
        act = a * _sig(a) * b
        acc_ref[...] += _dot(act.astype(BF16), wd_ref[0, j])

        @pl.when(j == FF_CHUNKS - 1)
        def _():
            xo_ref[...] = x_ref[...] + 0.5 * acc_ref[...]

    return _hosted_call(
        body, comm,
        name="ffn_fwd",
        grid=(n // tm, FF_CHUNKS),
        in_specs=[
            pl.BlockSpec((tm, d), lambda i, j: (i, 0)),
            pl.BlockSpec((1, d), lambda i, j: (0, 0)),
            wu_spec, wd_spec,
        ],
        out_specs=[
            pl.BlockSpec((tm, d), lambda i, j: (i, 0)),
            pl.BlockSpec((1, tm, nn), lambda i, j: (j, i, 0)),
            pl.BlockSpec((1, tm, nn), lambda i, j: (j, i, 0)),
        ],
        out_shape=[
            jax.ShapeDtypeStruct((n, d), F32),
            jax.ShapeDtypeStruct((FF_CHUNKS, n, nn), BF16),
            jax.ShapeDtypeStruct((FF_CHUNKS, n, nn), BF16),
        ],
        scratch_shapes=[pltpu.VMEM((tm, d), BF16), pltpu.VMEM((tm, d), F32)],
        args=(x, g, w_up, w_down),
    )


def ffn_bwd(dy, x, g, a, b, w_up, w_down, layer, comm=None):
    n, d = x.shape
    nn = w_up.shape[3]
    tm = _tile(n, 512, 8)
    wu_spec, wd_spec, w_down = _ffn_resident(w_up, w_down, layer)

    def body(dy_ref, x_ref, g_ref, a_ref, b_ref, wu_ref, wd_ref,
             dx_ref, dg_ref, hn_ref, dyb_ref, da_ref, db_ref, act_ref, dyb_s, dh_ref):
        i, j = pl.program_id(0), pl.program_id(1)

        @pl.when(j == 0)
        def _():
            h, _, _ = _rms_fwd(x_ref[...], g_ref[...])
            hn_ref[...] = h.astype(BF16)
            dyb = (0.5 * dy_ref[...]).astype(BF16)
            dyb_ref[...] = dyb
            dyb_s[...] = dyb
            dh_ref[...] = jnp.zeros_like(dh_ref)

        dact = _dot_nt(dyb_s[...], wd_ref[0, j])
        a32 = a_ref[0].astype(F32)
        b32 = b_ref[0].astype(F32)
        s = _sig(a32)
        sil = a32 * s
        da = (dact * b32 * (s * (1.0 + a32 * (1.0 - s)))).astype(BF16)
        db = (dact * sil).astype(BF16)
        da_ref[0] = da
        db_ref[0] = db
        act_ref[0] = (sil * b32).astype(BF16)
        dh_ref[...] += _dot_nt(da, wu_ref[j, 0]) + _dot_nt(db, wu_ref[j + FF_CHUNKS, 0])

        @pl.when(j == FF_CHUNKS - 1)
        def _():
            gg = g_ref[...]
            _, xhat, rs = _rms_fwd(x_ref[...], gg)
            dxn, dgr = _rms_bwd(dh_ref[...], xhat, rs, gg)
            dx_ref[...] = dy_ref[...] + dxn
            _acc_out(dg_ref, jnp.sum(dgr, axis=0, keepdims=True), i == 0)

    tok = pl.BlockSpec((tm, d), lambda i, j: (i, 0))
    chunk = pl.BlockSpec((1, tm, nn), lambda i, j: (j, i, 0))
    vec = pl.BlockSpec((1, d), lambda i, j: (0, 0))
    chunks = jax.ShapeDtypeStruct((FF_CHUNKS, n, nn), BF16)
    return _hosted_call(
        body, comm,
        name="ffn_bwd",
        grid=(n // tm, FF_CHUNKS),
        in_specs=[
            tok, tok, vec, chunk, chunk, wu_spec, wd_spec,
        ],
        out_specs=[tok, vec, tok, tok, chunk, chunk, chunk],
        out_shape=[
            jax.ShapeDtypeStruct((n, d), F32),
            jax.ShapeDtypeStruct((1, d), F32),
            jax.ShapeDtypeStruct((n, d), BF16),
            jax.ShapeDtypeStruct((n, d), BF16),
            chunks, chunks, chunks,
        ],
        scratch_shapes=[pltpu.VMEM((tm, d), BF16), pltpu.VMEM((tm, d), F32)],
        args=(dy, x, g, a, b, w_up, w_down),
    )


def wgrad(a, b, name, out4, mode, *, g0=0, layer=0, prev=None, a_cols=None, dtype=BF16, comm=None):
    a3 = a if a.ndim == 3 else a[None]
    b3 = b if b.ndim == 3 else b[None]
    sa, n, ka = a3.shape
    sb, _, kb = b3.shape
    a0 = 0
    if a_cols is not None:
        a0, ka = a_cols
    k1, k2 = sa * ka, sb * kb
    depth, groups, rr, cc = out4
    t1 = _tile(math.gcd(ka, rr), 1024)
    t2 = _tile(math.gcd(kb, cc), 1024)
    tn = _tile(n, 1024, 8)
    gpb = 1
    if mode == "col":
        assert rr == k1 and k2 % cc == 0 and g0 + k2 // cc <= groups
        if kb % cc == 0 and cc < kb <= 1024 and g0 % (kb // cc) == 0:
            t2, gpb = kb, kb // cc
        per = max(cc // t2, 1)
        oblock = (1, gpb, t1, min(t2, cc))
        omap = lambda i, j, k: (layer, (g0 + j // per) // gpb, i, j % per)
    else:
        assert cc == k2 and k1 % rr == 0 and g0 + k1 // rr <= groups
        if ka % rr == 0 and rr < ka <= 1024 and g0 % (ka // rr) == 0:
            t1, gpb = ka, ka // rr
        per = max(rr // t1, 1)
        oblock = (1, gpb, min(t1, rr), t2)
        omap = lambda i, j, k: (layer, (g0 + i // per) // gpb, i % per, j)
    na, nb = ka // t1, kb // t2
    nk = n // tn

    def body(a_ref, b_ref, *rest):
        o_ref, acc_ref = rest[-2], rest[-1]
        k = pl.program_id(2)

        @pl.when(k == 0)
        def _():
            acc_ref[...] = jnp.zeros_like(acc_ref)

        acc_ref[...] += _dot_tn(a_ref[0].astype(BF16), b_ref[0].astype(BF16))

        @pl.when(k == nk - 1)
        def _():
            for g in range(gpb):
                if gpb == 1:
                    o_ref[0, 0] = acc_ref[...].astype(dtype)
                elif mode == "col":
                    o_ref[0, g] = acc_ref[:, g * cc:(g + 1) * cc].astype(dtype)
                else:
                    o_ref[0, g] = acc_ref[g * rr:(g + 1) * rr, :].astype(dtype)

    in_specs = [
        pl.BlockSpec((1, tn, t1), lambda i, j, k: (i // na, k, a0 // t1 + i % na)),
        pl.BlockSpec((1, tn, t2), lambda i, j, k: (j // nb, k, j % nb)),
    ]
    args = [a3, b3]
    if comm is not None:
        assert prev is None
        return _hosted_call(body, comm, name=name, grid=(k1 // t1, k2 // t2, nk), in_specs=in_specs,
                            out_specs=[pl.BlockSpec(oblock, omap)], out_shape=[jax.ShapeDtypeStruct(out4, dtype)],
                            scratch_shapes=[pltpu.VMEM((t1, t2), F32)], args=args)
    extra = {}
    if prev is not None:
        in_specs.append(pl.BlockSpec(memory_space=pl.ANY))
        args.append(prev)
        extra["input_output_aliases"] = {2: 0}
    return _call(
        body,
        name=name,
        grid=(k1 // t1, k2 // t2, nk),
        in_specs=in_specs,
        out_specs=pl.BlockSpec(oblock, omap),
        out_shape=jax.ShapeDtypeStruct(out4, dtype),
        scratch_shapes=[pltpu.VMEM((t1, t2), F32)],
        compiler_params=_params(),
        **extra,
    )(*args)


def proj_fwd(x, g, w, layer, comm=None):
    n, d = x.shape
    c = w.shape[2]
    tm, tc = _tile(n, 1024, 8), _tile(c, 768)

    def body(x_ref, g_ref, w_ref, o_ref, hn_ref):
        @pl.when(pl.program_id(1) == 0)
        def _():
            h, _, _ = _rms_fwd(x_ref[...], g_ref[...])
            hn_ref[...] = h.astype(BF16)

        o_ref[...] = _dot(hn_ref[...], w_ref[0])

    return _hosted_call(
        body, comm,
        name="proj_fwd",
        grid=(n // tm, c // tc),
        in_specs=[
            pl.BlockSpec((tm, d), lambda i, j: (i, 0)),
            pl.BlockSpec((1, d), lambda i, j: (0, 0)),
            pl.BlockSpec((1, d, tc), lambda i, j: (layer, 0, j)),
        ],
        out_specs=[pl.BlockSpec((tm, tc), lambda i, j: (i, j))],
        out_shape=[jax.ShapeDtypeStruct((n, c), F32)],
        scratch_shapes=[pltpu.VMEM((tm, d), BF16)],
        args=(x, g, w),
    )


def proj_bwd(dproj, dres, x, g, w, layer):
    n, d = x.shape
    c = w.shape[2]
    tm = _tile(n, 512, 8)

    def body(dp_ref, dr_ref, x_ref, g_ref, w_ref, dx_ref, dg_ref, hn_ref):
        dh = _dot_nt(dp_ref[...], w_ref[0])
        gg = g_ref[...]
        h, xhat, rs = _rms_fwd(x_ref[...], gg)
        hn_ref[...] = h.astype(BF16)
        dxn, dgr = _rms_bwd(dh, xhat, rs, gg)
        dx_ref[...] = dr_ref[...] + dxn
        _acc_out(dg_ref, jnp.sum(dgr, axis=0, keepdims=True), pl.program_id(0) == 0)

    tok = pl.BlockSpec((tm, d), lambda i: (i, 0))
    vec = pl.BlockSpec((1, d), lambda i: (0, 0))
    return _call(
        body,
        name="proj_bwd",
        grid=(n // tm,),
        in_specs=[
            pl.BlockSpec((tm, c), lambda i: (i, 0)), tok, tok, vec,
            pl.BlockSpec((1, d, c), lambda i: (layer, 0, 0), pipeline_mode=pl.Buffered(1)),
        ],
        out_specs=[tok, vec, tok],
        out_shape=[
            jax.ShapeDtypeStruct((n, d), F32),
            jax.ShapeDtypeStruct((1, d), F32),
            jax.ShapeDtypeStruct((n, d), BF16),
        ],
        compiler_params=_params(),
    )(dproj, dres, x, g, w)


S5_TS = 512


def _cmul(ar, ai, br, bi):
    return ar * br - ai * bi, ar * bi + ai * br


def _s5_tables(ar, ai, reverse):
    gp = ar.shape[1]
    if reverse:
        ai = -ai
    a1r, a1i = jnp.broadcast_to(ar, (8, gp)), jnp.broadcast_to(ai, (8, gp))
    a2r, a2i = _cmul(a1r, a1i, a1r, a1i)
    a4r, a4i = _cmul(a2r, a2i, a2r, a2i)
    a8r, a8i = _cmul(a4r, a4i, a4r, a4i)
    row = lax.broadcasted_iota(jnp.int32, (8, gp), 0)
    e = (8 - row) if reverse else (row + 1)
    pr, pi = jnp.ones((8, gp), F32), jnp.zeros((8, gp), F32)
    for bit, (fr, fi) in ((1, (a1r, a1i)), (2, (a2r, a2i)), (4, (a4r, a4i)), (8, (a8r, a8i))):
        nr, ni = _cmul(pr, pi, fr, fi)
        on = (e & bit) != 0
        pr, pi = jnp.where(on, nr, pr), jnp.where(on, ni, pi)
    return (a1r, a1i, a2r, a2i, a4r, a4i, pr, pi)


def _s5_scan(xr_ref, xi_ref, tab_ref, cr_ref, ci_ref, ts, reverse):
    gp = xr_ref.shape[1]
    nt = ts // 8
    row = lax.broadcasted_iota(jnp.int32, (8, gp), 0)

    def shifted(v, s):
        if reverse:
            return jnp.where(row < 8 - s, pltpu.roll(v, 8 - s, 0), 0.0)
        return jnp.where(row >= s, pltpu.roll(v, s, 0), 0.0)

    def step(k, carry):
        cr, ci = carry
        t = (nt - 1 - k) if reverse else k
        r0 = pl.multiple_of(t * 8, 8)
        br = xr_ref[pl.ds(r0, 8), :]
        bi = xi_ref[pl.ds(r0, 8), :]
        for q, s in enumerate((1, 2, 4)):
            fr, fi = tab_ref[2 * q], tab_ref[2 * q + 1]
            sr, si = shifted(br, s), shifted(bi, s)
            mr, mi = _cmul(fr, fi, sr, si)
            br, bi = br + mr, bi + mi
        mr, mi = _cmul(tab_ref[6], tab_ref[7], cr, ci)
        br, bi = br + mr, bi + mi
        xr_ref[pl.ds(r0, 8), :] = br
        xi_ref[pl.ds(r0, 8), :] = bi
        edge = 0 if reverse else 7
        return (jnp.broadcast_to(br[edge:edge + 1, :], (8, gp)),
                jnp.broadcast_to(bi[edge:edge + 1, :], (8, gp)))

    cr, ci = lax.fori_loop(0, nt, step, (cr_ref[...], ci_ref[...]), unroll=2)
    cr_ref[...] = cr
    ci_ref[...] = ci


def s5_fwd(proj, ucol, seq, ar, ai, bm_r, bm_i, cm_r, cm_i, dskip, w_glu, layer):
    n = proj.shape[0]
    ds, gp = bm_r.shape
    ts = min(S5_TS, seq)
    nt = seq // ts

    def body(u_ref, ar_ref, ai_ref, bmr_ref, bmi_ref, cmr_ref, cmi_ref, d_ref, wg_ref,
             out_ref, xr_ref, xi_ref, yp_ref, tab_ref, cr_ref, ci_ref):
        @pl.when(pl.program_id(1) == 0)
        def _():
            for q, v in enumerate(_s5_tables(ar_ref[...], ai_ref[...], False)):
                tab_ref[q] = v
            cr_ref[...] = jnp.zeros_like(cr_ref)
            ci_ref[...] = jnp.zeros_like(ci_ref)

        u = u_ref[...]
        ub = u.astype(BF16)
        xr_ref[...] = _dot(ub, bmr_ref[...])
        xi_ref[...] = _dot(ub, bmi_ref[...])
        _s5_scan(xr_ref, xi_ref, tab_ref, cr_ref, ci_ref, ts, False)
        y = (_dot(xr_ref[...].astype(BF16), cmr_ref[...]) - _dot(xi_ref[...].astype(BF16), cmi_ref[...])
             + d_ref[...] * u)
        yp_ref[...] = y
        z = _dot(_gelu(y).astype(BF16), wg_ref[0])
        out_ref[...] = z[:, :ds] * _sig(z[:, ds:])

    full = lambda shape: pl.BlockSpec(shape, lambda b, t: (0, 0))
    return _call(
        body,
        name="s5_fwd",
        grid=(n // seq, nt),
        in_specs=[
            pl.BlockSpec((ts, ds), lambda b, t: (b * nt + t, ucol)),
            full((1, gp)), full((1, gp)), full((ds, gp)), full((ds, gp)), full((gp, ds)), full((gp, ds)),
            full((1, ds)), pl.BlockSpec((1, ds, 2 * ds), lambda b, t: (layer, 0, 0)),
        ],
        out_specs=[
            pl.BlockSpec((ts, ds), lambda b, t: (b * nt + t, 0)),
            pl.BlockSpec((ts, gp), lambda b, t: (b * nt + t, 0)),
            pl.BlockSpec((ts, gp), lambda b, t: (b * nt + t, 0)),
            pl.BlockSpec((ts, ds), lambda b, t: (b * nt + t, 0)),
        ],
        out_shape=[
            jax.ShapeDtypeStruct((n, ds), F32),
            jax.ShapeDtypeStruct((n, gp), F32),
            jax.ShapeDtypeStruct((n, gp), F32),
            jax.ShapeDtypeStruct((n, ds), F32),
        ],
        scratch_shapes=[pltpu.VMEM((8, 8, gp), F32), pltpu.VMEM((8, gp), F32), pltpu.VMEM((8, gp), F32)],
        compiler_params=_params(),
    )(proj, ar, ai, bm_r, bm_i, cm_r, cm_i, dskip, w_glu)


def s5_bwd(dout, ypre, proj, ucol, xr, xi, seq, ar, ai, bm_r, bm_i, cm_r, cm_i, dskip, w_glu, layer):
    n = proj.shape[0]
    ds, gp = bm_r.shape
    ts = min(S5_TS, seq)
    nt = seq // ts

    def body(do_ref, yp_ref, u_ref, xr_ref, xi_ref, hr_ref, hi_ref, ar_ref, ai_ref, bmr_ref, bmi_ref,
             cmr_ref, cmi_ref, d_ref, wg_ref,
             du_ref, gr_ref, gi_ref, dyb_ref, glb_ref, dzb_ref, dar_ref, dai_ref, dd_ref,
             tab_ref, cr_ref, ci_ref):
        b, t = pl.program_id(0), pl.program_id(1)
        first = jnp.logical_and(b == 0, t == 0)

        @pl.when(t == 0)
        def _():
            for q, v in enumerate(_s5_tables(ar_ref[...], ai_ref[...], True)):
                tab_ref[q] = v
            cr_ref[...] = jnp.zeros_like(cr_ref)
            ci_ref[...] = jnp.zeros_like(ci_ref)

        yp = yp_ref[...]
        u = u_ref[...]
        gl = _gelu(yp).astype(BF16)
        glb_ref[...] = gl
        z = _dot(gl, wg_ref[0])
        za, sg = z[:, :ds], _sig(z[:, ds:])
        do = do_ref[...]
        da = (do * sg).astype(BF16)
        dg = (do * za * sg * (1.0 - sg)).astype(BF16)
        dzb_ref[:, :ds] = da
        dzb_ref[:, ds:] = dg
        dgl = _dot_nt(da, wg_ref[0, :, :ds]) + _dot_nt(dg, wg_ref[0, :, ds:])
        dyp = dgl * _gelu_grad(yp)
        dypb = dyp.astype(BF16)
        dyb_ref[...] = dypb
        _acc_out(dd_ref, jnp.sum(dyp * u, axis=0, keepdims=True), first)

        gr_ref[...] = _dot_nt(dypb, cmr_ref[...])
        gi_ref[...] = -_dot_nt(dypb, cmi_ref[...])
        _s5_scan(gr_ref, gi_ref, tab_ref, cr_ref, ci_ref, ts, True)
        gr, gi = gr_ref[...], gi_ref[...]
        du_ref[...] = d_ref[...] * dyp + _dot_nt(gr.astype(BF16), bmr_ref[...]) + _dot_nt(gi.astype(BF16), bmi_ref[...])

        row = lax.broadcasted_iota(jnp.int32, (ts, gp), 0)
        live = jnp.where(t == nt - 1, 0.0, 1.0)
        pr = jnp.broadcast_to(hr_ref[7:8, :] * live, (ts, gp))
        pi = jnp.broadcast_to(hi_ref[7:8, :] * live, (ts, gp))
        sr = jnp.where(row == 0, pr, pltpu.roll(xr_ref[...], 1, 0))
        si = jnp.where(row == 0, pi, pltpu.roll(xi_ref[...], 1, 0))
        _acc_out(dar_ref, jnp.sum(gr * sr + gi * si, axis=0, keepdims=True), first)
        _acc_out(dai_ref, jnp.sum(gi * sr - gr * si, axis=0, keepdims=True), first)

    full = lambda shape: pl.BlockSpec(shape, lambda b, t: (0, 0))
    blk = lambda w, col=0: pl.BlockSpec((ts, w), lambda b, t: (b * nt + nt - 1 - t, col))
    halo = pl.BlockSpec((8, gp), lambda b, t: (jnp.maximum((b * seq + (nt - 1 - t) * ts) // 8 - 1, 0), 0))
    return _call(
        body,
        name="s5_bwd",
        grid=(n // seq, nt),
        in_specs=[
            blk(ds), blk(ds), blk(ds, ucol), blk(gp), blk(gp), halo, halo,
            full((1, gp)), full((1, gp)), full((ds, gp)), full((ds, gp)), full((gp, ds)), full((gp, ds)),
            full((1, ds)), pl.BlockSpec((1, ds, 2 * ds), lambda b, t: (layer, 0, 0)),
        ],
        out_specs=[
            blk(ds), blk(gp), blk(gp), blk(ds), blk(ds), blk(2 * ds),
            full((1, gp)), full((1, gp)), full((1, ds)),
        ],
        out_shape=[
            jax.ShapeDtypeStruct((n, ds), F32),
            jax.ShapeDtypeStruct((n, gp), F32),
            jax.ShapeDtypeStruct((n, gp), F32),
            jax.ShapeDtypeStruct((n, ds), BF16),
            jax.ShapeDtypeStruct((n, ds), BF16),
            jax.ShapeDtypeStruct((n, 2 * ds), BF16),
            jax.ShapeDtypeStruct((1, gp), F32),
            jax.ShapeDtypeStruct((1, gp), F32),
            jax.ShapeDtypeStruct((1, ds), F32),
        ],
        scratch_shapes=[pltpu.VMEM((8, 8, gp), F32), pltpu.VMEM((8, gp), F32), pltpu.VMEM((8, gp), F32)],
        compiler_params=_params(),
    )(dout, ypre, proj, xr, xi, xr, xi, ar, ai, bm_r, bm_i, cm_r, cm_i, dskip, w_glu)


BAND = QBLK + CHUNK
NCH = QBLK // CHUNK


PAIR = 2 * HEAD


def _attn_specs(nq):
    last = nq - 1
    cur = lambda col0=0: pl.BlockSpec(
        (QBLK, PAIR), lambda p, b, i: (b * nq + jnp.minimum(i, last), col0 // PAIR + p))
    prev = lambda col0=0: pl.BlockSpec(
        (QBLK, PAIR), lambda p, b, i: (b * nq + jnp.maximum(jnp.minimum(i, last) - 1, 0), col0 // PAIR + p))
    vec = pl.BlockSpec((2, 1, 2 * QBLK), lambda p, b, i: (p, 0, 0))
    gain = pl.BlockSpec((1, PAIR), lambda p, b, i: (0, 0))
    return cur, prev, vec, gain


def _pair_masks():
    lane = lax.broadcasted_iota(jnp.int32, (1, PAIR), 1)
    return lane < HEAD, [(lane < HEAD).astype(F32), (lane >= HEAD).astype(F32)]


def _pair_mean(t, low, m0):
    s0 = jnp.sum(t * m0, axis=-1, keepdims=True)
    s1 = jnp.sum(t, axis=-1, keepdims=True) - s0
    return jnp.where(low, s0, s1) * (1.0 / HEAD)


def _pair_rms_fwd(x, g, low, m0):
    rs = lax.rsqrt(_pair_mean(x * x, low, m0) + EPS)
    xhat = x * rs
    return xhat * g, xhat, rs


def _pair_rms_bwd(dh, xhat, rs, g, low, m0):
    dxh = dh * g
    dx = rs * (dxh - xhat * _pair_mean(dxh * xhat, low, m0))
    return dx, dh * xhat


def _attn_build_table(tv, bias_ref, tab_ref):
    w = 2 * QBLK
    for qi in range(CHUNK):
        bias_ref[qi:qi + 1, :] = pltpu.roll(tv, (qi - (CHUNK - 1)) % w, 1)
    bias = bias_ref[...]
    lane = lax.broadcasted_iota(jnp.int32, (CHUNK, w), 1)
    for c in range(NCH):
        rolled = bias if c == 0 else pltpu.roll(bias, CHUNK * c, 1)
        ok = jnp.logical_and(lane >= CHUNK * c, lane < CHUNK * c + BAND)
        tab_ref[CHUNK * c:CHUNK * (c + 1), :] = jnp.where(ok, rolled, NEG)


def _attn_reduce_table(dt_ref, bias_ref):
    w = 2 * QBLK
    acc = dt_ref[0:CHUNK, :]
    for c in range(1, NCH):
        acc = acc + pltpu.roll(dt_ref[CHUNK * c:CHUNK * (c + 1), :], w - CHUNK * c, 1)
    bias_ref[...] = acc
    out = jnp.zeros((1, w), F32)
    for qi in range(CHUNK):
        out = out + pltpu.roll(bias_ref[qi:qi + 1, :], ((CHUNK - 1) - qi) % w, 1)
    return out


def _attn_softmax(s, table, first_block):
    s = s * (HEAD ** -0.5) + table
    col = lax.broadcasted_iota(jnp.int32, s.shape, 1)
    s = jnp.where(jnp.logical_and(first_block, col < QBLK), NEG, s)
    e = jnp.exp(s - jnp.max(s, axis=-1, keepdims=True))
    return e * (1.0 / jnp.sum(e, axis=-1, keepdims=True))


def attn_fwd(proj, q0, k0, v0, tv, gq, gk, seq, comm=None):
    n = proj.shape[0]
    heads = tv.shape[0]
    nq = seq // QBLK
    cur, prev, vec, gain = _attn_specs(nq)

    def body(q_ref, kp_ref, kc_ref, vp_ref, vc_ref, tv_ref, gq_ref, gk_ref, o_ref, bias_ref, tab_ref):
        @pl.when(jnp.logical_and(pl.program_id(1) == 0, pl.program_id(2) == 0))
        def _():
            for h in range(2):
                _attn_build_table(tv_ref[h], bias_ref, tab_ref.at[h])

        low, m = _pair_masks()
        qn, _, _ = _pair_rms_fwd(q_ref[...], gq_ref[...], low, m[0])
        kn, _, _ = _pair_rms_fwd(jnp.concatenate([kp_ref[...], kc_ref[...]], axis=0), gk_ref[...], low, m[0])
        knb = kn.astype(BF16)
        v = jnp.concatenate([vp_ref[...], vc_ref[...]], axis=0)
        o = jnp.zeros((QBLK, PAIR), F32)
        for h in range(2):
            p = _attn_softmax(_dot_nt((qn * m[h]).astype(BF16), knb), tab_ref[h], pl.program_id(2) == 0)
            o = o + _dot(p.astype(BF16), (v * m[h]).astype(BF16))
        o_ref[...] = o

    return _hosted_call(
        body, comm,
        name="attn_fwd",
        grid=(heads // 2, n // seq, nq),
        in_specs=[cur(q0), prev(k0), cur(k0), prev(v0), cur(v0), vec, gain, gain],
        out_specs=[cur()],
        out_shape=[jax.ShapeDtypeStruct((n, heads * HEAD), F32)],
        scratch_shapes=[pltpu.VMEM((CHUNK, 2 * QBLK), F32), pltpu.VMEM((2, QBLK, 2 * QBLK), F32)],
        args=(proj, proj, proj, proj, proj, tv, gq, gk),
    )


def attn_bwd(do, proj, q0, k0, v0, tv, gq, gk, seq, comm=None):
    n = proj.shape[0]
    heads = tv.shape[0]
    nb = n // seq
    nq = seq // QBLK
    cur, prev, vec, gain = _attn_specs(nq)
    keyblk = pl.BlockSpec((QBLK, PAIR), lambda p, b, i: (b * nq + jnp.maximum(i - 1, 0), p))

    def body(do_ref, q_ref, kp_ref, kc_ref, vp_ref, vc_ref, tv_ref, gq_ref, gk_ref,
             dq_ref, dk_ref, dv_ref, dtv_ref, dgq_ref, dgk_ref, bias_ref, tab_ref, dt_ref, ck_ref, cv_ref):
        pp, b, i = pl.program_id(0), pl.program_id(1), pl.program_id(2)
        head_start = jnp.logical_and(b == 0, i == 0)

        @pl.when(head_start)
        def _():
            for h in range(2):
                _attn_build_table(tv_ref[h], bias_ref, tab_ref.at[h])

        @pl.when(i < nq)
        def _():
            low, m = _pair_masks()
            gq_, gk_ = gq_ref[...], gk_ref[...]
            qn, qhat, qrs = _pair_rms_fwd(q_ref[...], gq_, low, m[0])
            kn, khat, krs = _pair_rms_fwd(jnp.concatenate([kp_ref[...], kc_ref[...]], axis=0), gk_, low, m[0])
            knb = kn.astype(BF16)
            vb = jnp.concatenate([vp_ref[...], vc_ref[...]], axis=0).astype(BF16)
            do_ = do_ref[...]
            dv = jnp.zeros((2 * QBLK, PAIR), F32)
            dqn = jnp.zeros((QBLK, PAIR), F32)
            dkn = jnp.zeros((2 * QBLK, PAIR), F32)
            for h in range(2):
                qh = (qn * m[h]).astype(BF16)
                doh = (do_ * m[h]).astype(BF16)
                p = _attn_softmax(_dot_nt(qh, knb), tab_ref[h], i == 0)
                dv = dv + _dot_tn(p.astype(BF16), doh)
                dp = _dot_nt(doh, vb)
                ds = p * (dp - jnp.sum(p * dp, axis=-1, keepdims=True))
                _acc_out(dt_ref.at[h], ds, head_start)
                dsb = (ds * (HEAD ** -0.5)).astype(BF16)
                dqn = dqn + _dot(dsb, (kn * m[h]).astype(BF16))
                dkn = dkn + _dot_tn(dsb, qh)
            dq, dgq_rows = _pair_rms_bwd(dqn, qhat, qrs, gq_, low, m[0])
            dk, dgk_rows = _pair_rms_bwd(dkn, khat, krs, gk_, low, m[0])
            dq_ref[...] = dq

            @pl.when(i == 0)
            def _():
                dk_ref[...] = dk[:QBLK]
                dv_ref[...] = dv[:QBLK]

            @pl.when(i > 0)
            def _():
                dk_ref[...] = ck_ref[...] + dk[:QBLK]
                dv_ref[...] = cv_ref[...] + dv[:QBLK]

            ck_ref[...] = dk[QBLK:]
            cv_ref[...] = dv[QBLK:]
            first = jnp.logical_and(pp == 0, head_start)
            _acc_out(dgq_ref, jnp.sum(dgq_rows, axis=0, keepdims=True), first)
            _acc_out(dgk_ref, jnp.sum(dgk_rows, axis=0, keepdims=True), first)

        @pl.when(i == nq)
        def _():
            dk_ref[...] = ck_ref[...]
            dv_ref[...] = cv_ref[...]

        @pl.when(jnp.logical_and(b == nb - 1, i == nq))
        def _():
            for h in range(2):
                dtv_ref[h] = _attn_reduce_table(dt_ref.at[h], bias_ref)

    tok = jax.ShapeDtypeStruct((n, heads * HEAD), F32)
    return _hosted_call(
        body, comm,
        name="attn_bwd",
        grid=(heads // 2, nb, nq + 1),
        in_specs=[cur(), cur(q0), prev(k0), cur(k0), prev(v0), cur(v0), vec, gain, gain],
        out_specs=[cur(), keyblk, keyblk, vec, gain, gain],
        out_shape=[tok, tok, tok, jax.ShapeDtypeStruct(tv.shape, F32),
                   jax.ShapeDtypeStruct((1, PAIR), F32), jax.ShapeDtypeStruct((1, PAIR), F32)],
        scratch_shapes=[pltpu.VMEM((CHUNK, 2 * QBLK), F32), pltpu.VMEM((2, QBLK, 2 * QBLK), F32),
                        pltpu.VMEM((2, QBLK, 2 * QBLK), F32), pltpu.VMEM((QBLK, PAIR), F32),
                        pltpu.VMEM((QBLK, PAIR), F32)],
        args=(do, proj, proj, proj, proj, proj, tv, gq, gk),
    )


CONV_TC = 512


def _ln_fwd(h1, g, b):
    mu = jnp.mean(h1, axis=-1, keepdims=True)
    xc = h1 - mu
    rs = lax.rsqrt(jnp.mean(xc * xc, axis=-1, keepdims=True) + EPS)
    yhat = xc * rs
    return yhat * g + b, yhat, rs


def _glu(z, dc):
    return z[:, :dc] * _sig(z[:, dc:])


def conv_fwd(proj, zcol, seq, w, bdw, lng, lnb):
    n = proj.shape[0]
    dc = w.shape[1]
    tc = min(CONV_TC, seq)
    nt = seq // tc

    def body(z_ref, zp_ref, w_ref, b_ref, g_ref, lb_ref, h1_ref, o_ref, ext_ref):
        live = jnp.where(pl.program_id(1) == 0, 0.0, 1.0)
        ext_ref[pl.ds(0, HALO), :] = _glu(zp_ref[...], dc) * live
        ext_ref[pl.ds(HALO, tc), :] = _glu(z_ref[...], dc)
        acc = jnp.zeros((tc, dc), F32) + b_ref[...]
        for j in range(CONV_W):
            acc = acc + w_ref[j:j + 1, :] * ext_ref[pl.ds(HALO - (CONV_W - 1) + j, tc), :]
        h1_ref[...] = acc
        ln, _, _ = _ln_fwd(acc, g_ref[...], lb_ref[...])
        o_ref[...] = ln * _sig(ln)

    full = lambda shape: pl.BlockSpec(shape, lambda b, t: (0, 0))
    return _call(
        body,
        name="conv_fwd",
        grid=(n // seq, nt),
        in_specs=[
            pl.BlockSpec((tc, 2 * dc), lambda b, t: (b * nt + t, zcol)),
            pl.BlockSpec((HALO, 2 * dc), lambda b, t: (jnp.maximum((b * seq + t * tc) // HALO - 1, 0), zcol)),
            full((HALO, dc)), full((1, dc)), full((1, dc)), full((1, dc)),
        ],
        out_specs=[
            pl.BlockSpec((tc, dc), lambda b, t: (b * nt + t, 0)),
            pl.BlockSpec((tc, dc), lambda b, t: (b * nt + t, 0)),
        ],
        out_shape=[jax.ShapeDtypeStruct((n, dc), F32), jax.ShapeDtypeStruct((n, dc), F32)],
        scratch_shapes=[pltpu.VMEM((tc + HALO, dc), F32)],
        compiler_params=_params(),
    )(proj, proj, w, bdw, lng, lnb)


def conv_bwd(dco, h1, proj, zcol, seq, w, bdw, lng, lnb):
    n = proj.shape[0]
    dc = w.shape[1]
    tc = min(CONV_TC, seq)
    nt = seq // tc
    nrow = n // HALO

    def body(do_ref, don_ref, h1_ref, h1n_ref, z_ref, zp_ref, w_ref, g_ref, lb_ref,
             dz_ref, dw_ref, db_ref, dg_ref, dlb_ref, ext_ref, dext_ref):
        b, t = pl.program_id(0), pl.program_id(1)
        first = jnp.logical_and(b == 0, t == 0)
        g, lb = g_ref[...], lb_ref[...]

        def dh1_of(do, h1):
            ln, yhat, rs = _ln_fwd(h1, g, lb)
            s = _sig(ln)
            dln = do * (s * (1.0 + ln * (1.0 - s)))
            dyh = dln * g
            dh1 = rs * (dyh - jnp.mean(dyh, axis=-1, keepdims=True)
                        - yhat * jnp.mean(dyh * yhat, axis=-1, keepdims=True))
            return dh1, dln, yhat

        dh1, dln, yhat = dh1_of(do_ref[...], h1_ref[...])
        dh1n, _, _ = dh1_of(don_ref[...], h1n_ref[...])
        _acc_out(dg_ref, jnp.sum(dln * yhat, axis=0, keepdims=True), first)
        _acc_out(dlb_ref, jnp.sum(dln, axis=0, keepdims=True), first)
        _acc_out(db_ref, jnp.sum(dh1, axis=0, keepdims=True), first)

        dext_ref[pl.ds(0, tc), :] = dh1
        dext_ref[pl.ds(tc, HALO), :] = dh1n * jnp.where(t == nt - 1, 0.0, 1.0)
        z = z_ref[...]
        ext_ref[pl.ds(0, HALO), :] = _glu(zp_ref[...], dc) * jnp.where(t == 0, 0.0, 1.0)
        ext_ref[pl.ds(HALO, tc), :] = _glu(z, dc)

        @pl.when(first)
        def _():
            dw_ref[...] = jnp.zeros_like(dw_ref)

        dh0 = jnp.zeros((tc, dc), F32)
        for j in range(CONV_W):
            dh0 = dh0 + w_ref[j:j + 1, :] * dext_ref[pl.ds(CONV_W - 1 - j, tc), :]
            dw_ref[j:j + 1, :] += jnp.sum(dh1 * ext_ref[pl.ds(HALO - (CONV_W - 1) + j, tc), :],
                                          axis=0, keepdims=True)
        za, sg = z[:, :dc], _sig(z[:, dc:])
        dz_ref[:, :dc] = dh0 * sg
        dz_ref[:, dc:] = dh0 * za * sg * (1.0 - sg)

    full = lambda shape: pl.BlockSpec(shape, lambda b, t: (0, 0))
    cur = lambda wd, col=0: pl.BlockSpec((tc, wd), lambda b, t: (b * nt + t, col))
    nxt = pl.BlockSpec((HALO, dc), lambda b, t: (jnp.minimum((b * seq + (t + 1) * tc) // HALO, nrow - 1), 0))
    return _call(
        body,
        name="conv_bwd",
        grid=(n // seq, nt),
        in_specs=[
            cur(dc), nxt, cur(dc), nxt, cur(2 * dc, zcol),
            pl.BlockSpec((HALO, 2 * dc), lambda b, t: (jnp.maximum((b * seq + t * tc) // HALO - 1, 0), zcol)),
            full((HALO, dc)), full((1, dc)), full((1, dc)),
        ],
        out_specs=[cur(2 * dc), full((HALO, dc)), full((1, dc)), full((1, dc)), full((1, dc))],
        out_shape=[
            jax.ShapeDtypeStruct((n, 2 * dc), F32),
            jax.ShapeDtypeStruct((HALO, dc), F32),
            jax.ShapeDtypeStruct((1, dc), F32),
            jax.ShapeDtypeStruct((1, dc), F32),
            jax.ShapeDtypeStruct((1, dc), F32),
        ],
        scratch_shapes=[pltpu.VMEM((tc + HALO, dc), F32), pltpu.VMEM((tc + HALO, dc), F32)],
        compiler_params=_params(),
    )(dco, dco, h1, h1, proj, proj, w, lng, lnb)


def _merge_common(l0, l1, l2, bg, so, ao, co, wbs, wba, wbc, d):
    ys = _dot(so.astype(BF16), wbs)
    ya = _dot(ao.astype(BF16), wba)
    yc = _dot(co.astype(BF16), wbc)
    gs = _sig(l0 + bg[:, :d])
    ga = _sig(l1 + bg[:, d:2 * d])
    gc = _sig(l2 + bg[:, 2 * d:])
    return (ys, ya, yc), (gs, ga, gc)


def _merge_specs(tm, d, dss, da, dc, layer):
    row = lambda w, col=0: pl.BlockSpec((tm, w), lambda i: (i, col))
    full = lambda r, c: pl.BlockSpec((r, c), lambda i: (0, 0))
    stacked = lambda r: pl.BlockSpec((1, r, d), lambda i: (layer, 0, 0))
    acts = [row(d, 0), row(d, 1), row(d, 2), full(1, 3 * d), row(dss), row(da), row(dc)]
    weights = [stacked(dss), stacked(da), stacked(dc), stacked(d)]
    return row, full, acts, weights


def merge_fwd(x, proj, bg, so, ao, co, wbs, wba, wbc, wout, layer):
    n, d = x.shape
    tm = _tile(n, 256, 8)
    row, full, acts, weights = _merge_specs(tm, d, so.shape[1], ao.shape[1], co.shape[1], layer)

    def body(x_ref, l0_ref, l1_ref, l2_ref, bg_ref, so_ref, ao_ref, co_ref,
             wbs_ref, wba_ref, wbc_ref, wo_ref, o_ref):
        (ys, ya, yc), (gs, ga, gc) = _merge_common(
            l0_ref[...], l1_ref[...], l2_ref[...], bg_ref[...], so_ref[...], ao_ref[...], co_ref[...],
            wbs_ref[0], wba_ref[0], wbc_ref[0], d)
        merged = gs * ys + ga * ya + gc * yc
        o_ref[...] = x_ref[...] + _dot(merged.astype(BF16), wo_ref[0])

    return _call(
        body,
        name="merge_fwd",
        grid=(n // tm,),
        in_specs=[row(d)] + acts + weights,
        out_specs=row(d),
        out_shape=jax.ShapeDtypeStruct((n, d), F32),
        compiler_params=_params(),
    )(x, proj, proj, proj, bg, so, ao, co, wbs, wba, wbc, wout)


def merge_bwd(dx, proj, bg, so, ao, co, wbs, wba, wbc, wout, layer):
    n, d = dx.shape
    dss, da, dc = so.shape[1], ao.shape[1], co.shape[1]
    tm = _tile(n, 256, 8)
    row, full, acts, weights = _merge_specs(tm, d, dss, da, dc, layer)

    def body(dx_ref, l0_ref, l1_ref, l2_ref, bg_ref, so_ref, ao_ref, co_ref, wbs_ref, wba_ref, wbc_ref, wo_ref,
             dl_ref, dso_ref, dao_ref, dco_ref, dbg_ref, mg_ref, dxb_ref, dys_ref, dya_ref, dyc_ref):
        wbs, wba, wbc = wbs_ref[0], wba_ref[0], wbc_ref[0]
        (ys, ya, yc), (gs, ga, gc) = _merge_common(
            l0_ref[...], l1_ref[...], l2_ref[...], bg_ref[...], so_ref[...], ao_ref[...], co_ref[...],
            wbs, wba, wbc, d)
        mg_ref[...] = (gs * ys + ga * ya + gc * yc).astype(BF16)
        dxb = dx_ref[...].astype(BF16)
        dxb_ref[...] = dxb
        dm = _dot_nt(dxb, wo_ref[0])
        first = pl.program_id(0) == 0
        for k, (y, g, w, dy_ref, db_ref) in enumerate((
                (ys, gs, wbs, dys_ref, dso_ref), (ya, ga, wba, dya_ref, dao_ref), (yc, gc, wbc, dyc_ref, dco_ref))):
            dl = dm * y * g * (1.0 - g)
            dl_ref[:, k * d:(k + 1) * d] = dl.astype(BF16)
            _acc_out(dbg_ref.at[:, k * d:(k + 1) * d], jnp.sum(dl, axis=0, keepdims=True), first)
            dy = (dm * g).astype(BF16)
            dy_ref[...] = dy
            db_ref[...] = _dot_nt(dy, w)

    bf = lambda w: jax.ShapeDtypeStruct((n, w), BF16)
    return _call(
        body,
        name="merge_bwd",
        grid=(n // tm,),
        in_specs=[row(d)] + acts + weights,
        out_specs=[row(3 * d), row(dss), row(da), row(dc), full(1, 3 * d),
                   row(d), row(d), row(d), row(d), row(d)],
        out_shape=[bf(3 * d), jax.ShapeDtypeStruct((n, dss), F32), jax.ShapeDtypeStruct((n, da), F32),
                   jax.ShapeDtypeStruct((n, dc), F32), jax.ShapeDtypeStruct((1, 3 * d), F32),
                   bf(d), bf(d), bf(d), bf(d), bf(d)],
        compiler_params=_params(),
    )(dx, proj, proj, proj, bg, so, ao, co, wbs, wba, wbc, wout)


def loss_head(y, target):
    n, d = y.shape
    tm = _tile(n, 512, 8)

    def body(y_ref, t_ref, dy_ref, l_ref):
        e = y_ref[...] - t_ref[...]
        dy_ref[...] = e * (1.0 / d)
        part = 0.5 * jnp.sum(jnp.sum(e * e, axis=-1, keepdims=True) * (1.0 / d), axis=0, keepdims=True)
        _acc_out(l_ref, part, pl.program_id(0) == 0)

    return _call(
        body,
        name="loss_head",
        grid=(n // tm,),
        in_specs=[pl.BlockSpec((tm, d), lambda i: (i, 0)), pl.BlockSpec((tm, d), lambda i: (i, 0))],
        out_specs=[pl.BlockSpec((tm, d), lambda i: (i, 0)), pl.BlockSpec((1, 1), lambda i: (0, 0))],
        out_shape=[jax.ShapeDtypeStruct((n, d), F32), jax.ShapeDtypeStruct((1, 1), F32)],
        compiler_params=_params(),
    )(y, target)


def _mesh_pos():
    return lax.axis_index("x"), lax.axis_index("y"), lax.axis_index("c")


ANY = pl.BlockSpec(memory_space=pl.ANY)


def _comm_sems(na):
    return [pltpu.SemaphoreType.DMA((na, 7)), pltpu.SemaphoreType.DMA((na, 7)), pltpu.SemaphoreType.DMA((na,))]


def _gather_plan(x_refs, out_refs, sems):
    na = len(x_refs)
    send_sems, recv_sems, local_sems = sems
    x, y, c = _mesh_pos()
    me, sibling = (x, y, c), (x, y, 1 - c)
    chips = [(1 - x, y), (x, 1 - y), (1 - x, 1 - y)]

    def slot(a, px, py, pc):
        return out_refs[a].at[4 * px + 2 * py + pc]

    def copy(a, k, block, to, src=None):
        return pltpu.make_async_remote_copy(
            src_ref=slot(a, *block) if src is None else src, dst_ref=slot(a, *block),
            send_sem=send_sems.at[a, k], recv_sem=recv_sems.at[a, k],
            device_id=to, device_id_type=pl.DeviceIdType.MESH)

    mine = [pltpu.make_async_copy(x_refs[a], slot(a, *me), local_sems.at[a]) for a in range(na)]
    first = []
    for a in range(na):
        first.append(copy(a, 0, me, sibling, src=x_refs[a]))
        first += [copy(a, 1 + j, me, (*chip, c), src=x_refs[a]) for j, chip in enumerate(chips)]

    def start():
        for cp in mine + first:
            cp.start()

    def finish():
        passed = []
        for j, chip in enumerate(chips):
            for a in range(na):
                copy(a, 1 + j, (*chip, c), me).wait_recv()
                fwd = copy(a, 4 + j, (*chip, c), sibling)
                fwd.start()
                passed.append(fwd)
        for a in range(na):
            copy(a, 0, sibling, me).wait_recv()
            for j, chip in enumerate(chips):
                copy(a, 4 + j, (*chip, 1 - c), me).wait_recv()
        for cp in first + passed:
            cp.wait_send()
        for cp in mine:
            cp.wait()

    return start, finish


def _gather_out(shards):
    return [jax.ShapeDtypeStruct((N_DEV,) + s.shape, s.dtype) for s in shards]


def all_gather(shards):
    na = len(shards)

    def body(*refs):
        start, finish = _gather_plan(refs[:na], refs[na:2 * na], refs[2 * na:])
        start()
        finish()

    return _call(
        body,
        name="all_gather",
        out_shape=_gather_out(shards),
        in_specs=[ANY] * na,
        out_specs=[ANY] * na,
        scratch_shapes=_comm_sems(na),
    )(*shards)


def _hosted_call(body, comm, *, name, grid, in_specs, out_specs, out_shape, scratch_shapes, args):
    if comm is None:
        res = _call(body, name=name, grid=grid, in_specs=in_specs, out_specs=out_specs, out_shape=out_shape,
                    scratch_shapes=scratch_shapes, compiler_params=_params())(*args)
        return res, []
    plan, arrays, c_out = comm
    n_in, n_out, n_scr, ci, co = len(in_specs), len(out_specs), len(scratch_shapes), len(arrays), len(c_out)

    def hosted(*refs):
        ins, cins = refs[:n_in], refs[n_in:n_in + ci]
        o0 = n_in + ci
        outs, couts = refs[o0:o0 + n_out], refs[o0 + n_out:o0 + n_out + co]
        s0 = o0 + n_out + co
        scr, sems = refs[s0:s0 + n_scr], refs[s0 + n_scr:]
        ids = [pl.program_id(ax) for ax in range(len(grid))]
        first = functools.reduce(jnp.logical_and, [i == 0 for i in ids])
        last = functools.reduce(jnp.logical_and, [i == g - 1 for i, g in zip(ids, grid)])
        start, finish = plan(cins, couts, sems)
        pl.when(first)(start)
        body(*ins, *outs, *scr)
        pl.when(last)(finish)

    res = _call(hosted, name=name + "_comm", grid=grid, in_specs=list(in_specs) + [ANY] * ci,
                out_specs=list(out_specs) + [ANY] * co, out_shape=list(out_shape) + list(c_out),
                scratch_shapes=list(scratch_shapes) + _comm_sems(max(ci, co)),
                compiler_params=_params())(*args, *arrays)
    return res[:n_out], res[n_out:]


def _exchange_plan(bcast=()):
    def plan(s_refs, r_refs, sems):
        na = len(s_refs)
        send_sems, recv_sems, local_sems = sems
        x, y, c = _mesh_pos()
        me = 4 * x + 2 * y + c

        def peer(k):
            px = (1 - x) if k & 4 else x
            py = (1 - y) if k & 2 else y
            pc = (1 - c) if k & 1 else c
            return (px, py, pc), 4 * px + 2 * py + pc

        def src_dst(a, pid, slot):
            return (s_refs[a] if a in bcast else s_refs[a].at[pid]), r_refs[a].at[slot]

        def copy(a, k):
            to, pid = peer(k)
            src, dst = src_dst(a, pid, me)
            return pltpu.make_async_remote_copy(
                src_ref=src, dst_ref=dst, send_sem=send_sems.at[a, k - 1], recv_sem=recv_sems.at[a, k - 1],
                device_id=to, device_id_type=pl.DeviceIdType.MESH)

        def arrival(a, k):
            _, pid = peer(k)
            src, dst = src_dst(a, pid, pid)
            return pltpu.make_async_remote_copy(
                src_ref=src, dst_ref=dst, send_sem=send_sems.at[a, k - 1], recv_sem=recv_sems.at[a, k - 1],
                device_id=(x, y, c), device_id_type=pl.DeviceIdType.MESH)

        mine = [pltpu.make_async_copy(*src_dst(a, me, me), local_sems.at[a]) for a in range(na)]
        sends = [copy(a, k) for k in range(1, N_DEV) for a in range(na)]

        def start():
            for cp in mine + sends:
                cp.start()

        def finish():
            for k in range(1, N_DEV):
                for a in range(na):
                    arrival(a, k).wait_recv()
            for cp in sends:
                cp.wait_send()
            for cp in mine:
                cp.wait()

        return start, finish

    return plan


def _exchange_out(slabs, bcast=()):
    return [jax.ShapeDtypeStruct(((N_DEV,) + s.shape) if a in bcast else s.shape, s.dtype)
            for a, s in enumerate(slabs)]


def grad_exchange(slabs, small):
    arrays = list(slabs) + [small]
    na = len(arrays)
    bcast = (na - 1,)

    def body(*refs):
        start, finish = _exchange_plan(bcast)(refs[:na], refs[na:2 * na], refs[2 * na:])
        start()
        finish()

    return _call(
        body,
        name="grad_exchange",
        out_shape=_exchange_out(arrays, bcast),
        in_specs=[ANY] * na,
        out_specs=[ANY] * na,
        scratch_shapes=_comm_sems(na),
    )(*arrays)


ADAM_BLOCK = 256 * 1024


def sum_adamw(recv, w, m, v, layer, prev, name):
    depth, rows, cols = w.shape
    tr = _tile(rows, max(8, ADAM_BLOCK // cols // 8 * 8), 8)
    c1 = 1.0 / (1.0 - ADAM_B1 ** ADAM_STEP)
    c2 = 1.0 / (1.0 - ADAM_B2 ** ADAM_STEP)

    def body(r_ref, w_ref, m_ref, v_ref, *rest):
        g_ref, d_ref, mo_ref, vo_ref = rest[-4:]
        g = r_ref[0].astype(F32)
        for s in range(1, N_DEV):
            g = g + r_ref[s].astype(F32)
        mn = ADAM_B1 * m_ref[0] + (1.0 - ADAM_B1) * g
        vn = ADAM_B2 * v_ref[0] + (1.0 - ADAM_B2) * (g * g)
        g_ref[0] = g
        mo_ref[0] = mn
        vo_ref[0] = vn
        d_ref[0] = -ADAM_LR * ((mn * c1) / (jnp.sqrt(vn * c2) + ADAM_EPS) + ADAM_WD * w_ref[0])

    blk = pl.BlockSpec((1, tr, cols), lambda i: (layer, i, 0))
    out = jax.ShapeDtypeStruct((depth, rows, cols), F32)
    in_specs = [pl.BlockSpec((N_DEV, tr, cols), lambda i: (0, i, 0)), blk, blk, blk]
    args = [recv, w, m, v]
    extra = {}
    if prev is not None:
        in_specs += [ANY] * 4
        args += list(prev)
        extra["input_output_aliases"] = {4 + j: j for j in range(4)}
    return _call(
        body,
        name=name,
        grid=(rows // tr,),
        in_specs=in_specs,
        out_specs=[blk, blk, blk, blk],
        out_shape=[out, out, out, out],
        compiler_params=_params(),
        **extra,
    )(*args)


def _attn_bias_vector(rel_bias):
    h = rel_bias.shape[0]
    n_far = BAND - MAX_REL
    n_near = BAND + CHUNK - 1 - n_far
    far = jnp.broadcast_to(rel_bias[:, 2 * MAX_REL:], (h, n_far))
    near = rel_bias[:, 2 * MAX_REL - n_near:2 * MAX_REL][:, ::-1]
    pad = jnp.zeros((h, 2 * QBLK - n_far - n_near), F32)
    return jnp.concatenate([far, near, pad], axis=1)[:, None, :]


def _s5_prepare(lre, lim, ldt, bre, bim, cre, cim):
    g, p = lre.shape
    lr = jnp.minimum(lre, -1e-4)
    dt = jnp.exp(ldt)[:, None]
    mag = jnp.exp(lr * dt)
    ar = mag * jnp.cos(lim * dt)
    ai = mag * jnp.sin(lim * dt)
    den = lr * lr + lim * lim
    coef_r = ((ar - 1.0) * lr + ai * lim) / den
    coef_i = (ai * lr - (ar - 1.0) * lim) / den
    bbar_r = coef_r[..., None] * bre - coef_i[..., None] * bim
    bbar_i = coef_r[..., None] * bim + coef_i[..., None] * bre
    eye = jnp.eye(g, dtype=F32)
    bd_in = lambda b: jnp.einsum("gpc,gh->gchp", b, eye).reshape(g * S5_GROUP, g * p)
    bd_out = lambda c: jnp.einsum("gcp,gh->gphc", c, eye).reshape(g * p, g * S5_GROUP)
    return (ar.reshape(1, g * p), ai.reshape(1, g * p), bd_in(bbar_r), bd_in(bbar_i), bd_out(cre), bd_out(cim))


SHARDED = ("ffn1_w_up", "ffn1_w_down", "w_in", "s5_w_glu", "w_br_s5", "w_br_attn", "conv_w_dw", "w_br_conv",
           "w_out", "ffn2_w_up", "ffn2_w_down")
WEIGHTS = ("ffn1_norm", "ffn1_w_up", "ffn1_w_down", "mix_norm", "w_in", "b_gate", "s5_lambda_re", "s5_lambda_im",
           "s5_log_dt", "s5_b_re", "s5_b_im", "s5_c_re", "s5_c_im", "s5_d", "s5_w_glu", "w_br_s5", "attn_q_gain",
           "attn_k_gain", "attn_rel_bias", "w_br_attn", "conv_w_dw", "conv_b_dw", "conv_ln_g", "conv_ln_b",
           "w_br_conv", "w_out", "ffn2_norm", "ffn2_w_up", "ffn2_w_down")
SMALL = tuple(nm for nm in WEIGHTS if nm not in SHARDED)
SMALL_LANES = 1024


def _cols_full(g):
    _, depth, k, nn = g.shape
    return g.transpose(1, 2, 0, 3).reshape(depth, k, N_DEV * nn)


def _rows_full(g):
    _, depth, r, cc = g.shape
    return g.transpose(1, 0, 2, 3).reshape(depth, N_DEV * r, cc)


def _cols_slabs(gfull):
    depth, k, c8 = gfull.shape
    return gfull.reshape(depth, k, N_DEV, c8 // N_DEV).transpose(0, 2, 1, 3)


def _small_pack(t):
    flat = jnp.concatenate([t[nm].reshape(-1) for nm in SMALL])
    rows = -(-flat.shape[0] // SMALL_LANES)
    rows = -(-rows // 8) * 8
    return jnp.pad(flat, (0, rows * SMALL_LANES - flat.shape[0])).reshape(rows, SMALL_LANES)


def _small_unpack(flat, like):
    flat = flat.reshape(-1)
    out, off = {}, 0
    for nm in SMALL:
        out[nm] = flat[off:off + like[nm].size].reshape(like[nm].shape)
        off += like[nm].size
    return out


def _step(x, target, w, m, v):
    bsz, seq, d = x.shape
    n = bsz * seq
    depth = w["ffn1_norm"].shape[0]
    da, dss, dc = d // 2, d // 4, d // 4
    gp =dss // S5_GROUP * S5_STATE
    mid = 3 * da + 2 * dc
    q0, k0, v0, z0, u0 = 3 * d, 3 * d + da, 3 * d + 2 * da, 3 * d + 3 * da, 3 * d + mid
    to_kernel_cols = lambda a: jnp.concatenate([a[..., dss + mid:], a[..., dss:dss + mid], a[..., :dss]], axis=-1)
    to_ref_cols = lambda a: jnp.concatenate([a[..., 3 * d + mid:], a[..., 3 * d:3 * d + mid], a[..., :3 * d]], axis=-1)

    def shard(key):
        nm, sl = (key[0], slice(key[1], key[1] + 1)) if isinstance(key, tuple) else (key, slice(None))
        return w[nm][sl] if nm == "conv_w_dw" else w[nm][sl].astype(BF16)

    mixer_names = ("w_in", "s5_w_glu", "w_br_s5", "w_br_attn", "w_br_conv", "conv_w_dw", "w_out")
    ffn1_of = lambda l: (("ffn1_w_up", l), ("ffn1_w_down", l))
    later_ffn1 = tuple(k for l in range(1, depth) for k in ffn1_of(l))
    gather_of = lambda keys: (_gather_plan, [shard(k) for k in keys], _gather_out([shard(k) for k in keys]))
    full = {}

    def take(keys, arrays):
        for key, g in zip(keys, arrays):
            nm = key[0] if isinstance(key, tuple) else key
            if nm in ("ffn1_w_up", "ffn2_w_up"):
                full[key] = g
            elif nm in ("ffn1_w_down", "ffn2_w_down", "w_out"):
                full[key] = _rows_full(g)
            elif nm == "w_in":
                full[key] = to_kernel_cols(_cols_full(g))
            else:
                full[key] = _cols_full(g)

    take(ffn1_of(0), all_gather([shard(k) for k in ffn1_of(0)]))
    nff = full[("ffn1_w_up", 0)].shape[3]

    row = lambda a: a.reshape(1, -1)
    saved = []
    xin = x.reshape(n, d)
    for l in range(depth):
        s = {"x0": xin}
        f1_up, f1_down = ffn1_of(l)
        (s["x1"], s["a1"], s["b1"]), got = ffn_fwd(xin, row(w["ffn1_norm"][l]), full[f1_up], full[f1_down], 0,
                                                   comm=gather_of(mixer_names) if l == 0 else None)
        if l == 0:
            take(mixer_names, got)
            conv_w = jnp.pad(full["conv_w_dw"], ((0, 0), (0, HALO - CONV_W), (0, 0)))
        (proj,), got = proj_fwd(s["x1"], row(w["mix_norm"][l]), full["w_in"], l,
                                comm=gather_of(later_ffn1) if l == 0 and later_ffn1 else None)
        if l == 0:
            take(later_ffn1, got)
        s["proj"] = proj
        prep_in = (w["s5_lambda_re"][l], w["s5_lambda_im"][l], w["s5_log_dt"][l], w["s5_b_re"][l], w["s5_b_im"][l],
                   w["s5_c_re"][l], w["s5_c_im"][l])
        (ar, ai, bm_r, bm_i, cm_r, cm_i), s["prep_vjp"] = jax.vjp(_s5_prepare, *prep_in)
        s["s5p"] = (ar, ai, bm_r.astype(BF16), bm_i.astype(BF16), cm_r.astype(BF16), cm_i.astype(BF16),
                    row(w["s5_d"][l]), full["s5_w_glu"], l)
        s["so"], s["xr"], s["xi"], s["ypre"] = s5_fwd(proj, u0 // dss, seq, *s["s5p"])
        tv, s["tv_vjp"] = jax.vjp(_attn_bias_vector, w["attn_rel_bias"][l])
        pair_gain = lambda g: jnp.tile(row(g), (1, 2))
        s["attnp"] = (q0, k0, v0, tv, pair_gain(w["attn_q_gain"][l]), pair_gain(w["attn_k_gain"][l]))
        (s["ao"],), got = attn_fwd(proj, *s["attnp"], seq,
                                   comm=gather_of(("ffn2_w_up", "ffn2_w_down")) if l == 0 else None)
        if l == 0:
            take(("ffn2_w_up", "ffn2_w_down"), got)
        s["convp"] = (conv_w[l], row(w["conv_b_dw"][l]), row(w["conv_ln_g"][l]), row(w["conv_ln_b"][l]))
        s["h1"], s["co"] = conv_fwd(proj, z0 // (2 * dc), seq, *s["convp"])
        s["mergep"] = (row(w["b_gate"][l]), s["so"], s["ao"], s["co"], full["w_br_s5"], full["w_br_attn"],
                       full["w_br_conv"], full["w_out"], l)
        s["x2"] = merge_fwd(s["x1"], proj, *s["mergep"])
        (xin, s["a2"], s["b2"]), _ = ffn_fwd(s["x2"], row(w["ffn2_norm"][l]), full["ffn2_w_up"], full["ffn2_w_down"], l)
        saved.append(s)

    dx, loss = loss_head(xin, target.reshape(n, d))
    loss = lax.psum(loss[0, 0], ("x", "y", "c"))

    small_g = {nm: [None] * depth for nm in SMALL}
    slab = {}

    recv = {}

    def exchange_of(names, l):
        arrays = [slab.pop(nm) for nm in names]
        return [(nm, l) for nm in names], (_exchange_plan(), arrays, _exchange_out(arrays))

    def ffn_grads(which, l, hn, dyb, dab_a, dab_b, act, send_up_now=False):
        up, down = which + "_w_up", which + "_w_down"
        up4, down4 = (1, N_DEV, d, nff), (1, FF_CHUNKS, nff, d)
        half = wgrad(hn, dab_a, "wg_ffn_up", up4, "col")
        slab[up] = wgrad(hn, dab_b, "wg_ffn_up", up4, "col", g0=FF_CHUNKS, prev=half).reshape(up4[1:])
        if send_up_now:
            keys, comm = exchange_of((up,), l)
            (dn,), got = wgrad(act, dyb, "wg_ffn_down", down4, "row", comm=comm)
            recv.update(zip(keys, got))
        else:
            dn = wgrad(act, dyb, "wg_ffn_down", down4, "row")
        slab[down] = dn.reshape(N_DEV, nff // 2, d)

    pending = None
    for l in reversed(range(depth)):
        s = saved[l]
        outs, got = ffn_bwd(dx, s["x2"], row(w["ffn2_norm"][l]), s["a2"], s["b2"],
                            full["ffn2_w_up"], full["ffn2_w_down"], l, comm=pending[1] if pending else None)
        if pending:
            recv.update(zip(pending[0], got))
        dx, dg, hn, dyb, dab_a, dab_b, act = outs
        small_g["ffn2_norm"][l] = dg
        ffn_grads("ffn2", l, hn, dyb, dab_a, dab_b, act)

        dlog, dso, dao, dco, dbg, mg, dxb, dys, dya, dyc = merge_bwd(dx, s["proj"], *s["mergep"])
        small_g["b_gate"][l] = dbg
        slab["w_out"] = wgrad(mg, dxb, "wg_out", (1, N_DEV, d // N_DEV, d), "row").reshape(N_DEV, d // N_DEV, d)
        for nm, act_in, dy_br in (("w_br_s5", s["so"], dys), ("w_br_attn", s["ao"], dya), ("w_br_conv", s["co"], dyc)):
            k_in = act_in.shape[1]
            slab[nm] = wgrad(act_in, dy_br, "wg_" + nm, (1, N_DEV, k_in, d // N_DEV), "col").reshape(
                N_DEV, k_in, d // N_DEV)

        dz, dwdw, dbdw, dlng, dlnb = conv_bwd(dco, s["h1"], s["proj"], z0 // (2 * dc), seq, *s["convp"])
        slab["conv_w_dw"] = _cols_slabs(dwdw[None, :CONV_W])[0].astype(BF16)
        small_g["conv_b_dw"][l], small_g["conv_ln_g"][l], small_g["conv_ln_b"][l] = dbdw, dlng, dlnb

        keys, comm = exchange_of(("ffn2_w_up", "ffn2_w_down"), l)
        outs, got = attn_bwd(dao, s["proj"], *s["attnp"], seq, comm=comm)
        recv.update(zip(keys, got))
        dq, dk, dv, dtv, dgq, dgk = outs
        small_g["attn_q_gain"][l] = dgq[:, :HEAD] + dgq[:, HEAD:]
        small_g["attn_k_gain"][l] = dgk[:, :HEAD] + dgk[:, HEAD:]
        small_g["attn_rel_bias"][l] = s["tv_vjp"](dtv)[0]

        du, gr, gi, dyb5, glb, dzb, dar, dai, dd = s5_bwd(dso, s["ypre"], s["proj"], u0 // dss, s["xr"], s["xi"],
                                                          seq, *s["s5p"])
        small_g["s5_d"][l] = dd
        one = lambda k1, k2: (1, 1, k1, k2)
        slab["s5_w_glu"] = _cols_slabs(wgrad(glb, dzb, "wg_s5_glu", one(dss, 2 * dss), "col")[0])[0]
        dcm_r = wgrad(s["xr"], dyb5, "wg_s5_c", one(gp, dss), "col", dtype=F32)[0, 0]
        dcm_i = -wgrad(s["xi"], dyb5, "wg_s5_c", one(gp, dss), "col", dtype=F32)[0, 0]
        dbm_r = wgrad(s["proj"], gr, "wg_s5_b", one(dss, gp), "col", a_cols=(u0, dss), dtype=F32)[0, 0]
        dbm_i = wgrad(s["proj"], gi, "wg_s5_b", one(dss, gp), "col", a_cols=(u0, dss), dtype=F32)[0, 0]
        pg = s["prep_vjp"]((dar, dai, dbm_r, dbm_i, dcm_r, dcm_i))
        for nm, gval in zip(("s5_lambda_re", "s5_lambda_im", "s5_log_dt", "s5_b_re", "s5_b_im", "s5_c_re", "s5_c_im"), pg):
            small_g[nm][l] = gval

        dproj = jnp.concatenate([dlog, dq.astype(BF16), dk.astype(BF16), dv.astype(BF16),
                                 dz.astype(BF16), du.astype(BF16)], axis=1)
        dx, dgm, hn = proj_bwd(dproj, dx, s["x1"], row(w["mix_norm"][l]), full["w_in"], l)
        small_g["mix_norm"][l] = dgm
        slab["w_in"] = _cols_slabs(to_ref_cols(wgrad(hn, dproj, "wg_in", one(d, 3 * d + mid + dss), "col")[0]))[0]

        keys, comm = exchange_of(mixer_names, l)
        f1_up, f1_down = ffn1_of(l)
        outs, got = ffn_bwd(dx, s["x0"], row(w["ffn1_norm"][l]), s["a1"], s["b1"],
                            full[f1_up], full[f1_down], 0, comm=comm)
        recv.update(zip(keys, got))
        dx, dg, hn, dyb, dab_a, dab_b, act = outs
        small_g["ffn1_norm"][l] = dg
        ffn_grads("ffn1", l, hn, dyb, dab_a, dab_b, act, send_up_now=(l == 0))
        pending = exchange_of(("ffn1_w_down",) if l == 0 else ("ffn1_w_up", "ffn1_w_down"), l)

    small_flat = _small_pack({nm: jnp.stack([g.reshape(w[nm].shape[1:]) for g in small_g[nm]]) for nm in SMALL})
    *got, recv_small = grad_exchange(pending[1][1], small_flat)
    recv.update(zip(pending[0], got))

    outs = {}
    for nm in SHARDED:
        shp = w[nm].shape
        as3 = lambda t: t.reshape(shp[0], -1, shp[-1])
        bufs = None
        for l in reversed(range(depth)):
            bufs = sum_adamw(recv[(nm, l)], as3(w[nm]), as3(m[nm]), as3(v[nm]), l, bufs, "adamw_" + nm)
        outs[nm] = [b.reshape(shp) for b in bufs]
    packed = sum_adamw(recv_small, _small_pack(w)[None], _small_pack(m)[None], _small_pack(v)[None], 0, None,
                       "adamw_small")
    unpacked = [_small_unpack(p, w) for p in packed]
    for nm in SMALL:
        outs[nm] = [u[nm] for u in unpacked]
    return loss, dx.reshape(x.shape), outs


def kernel(x, ffn1_norm, ffn1_w_up, ffn1_w_down, mix_norm, w_in, b_gate, s5_lambda_re, s5_lambda_im, s5_log_dt, s5_b_re, s5_b_im, s5_c_re, s5_c_im, s5_d, s5_w_glu, w_br_s5, attn_q_gain, attn_k_gain, attn_rel_bias, w_br_attn, conv_w_dw, conv_b_dw, conv_ln_g, conv_ln_b, w_br_conv, w_out, ffn2_norm, ffn2_w_up, ffn2_w_down, loss_target, m_ffn1_norm, m_ffn1_w_up, m_ffn1_w_down, m_mix_norm, m_w_in, m_b_gate, m_s5_lambda_re, m_s5_lambda_im, m_s5_log_dt, m_s5_b_re, m_s5_b_im, m_s5_c_re, m_s5_c_im, m_s5_d, m_s5_w_glu, m_w_br_s5, m_attn_q_gain, m_attn_k_gain, m_attn_rel_bias, m_w_br_attn, m_conv_w_dw, m_conv_b_dw, m_conv_ln_g, m_conv_ln_b, m_w_br_conv, m_w_out, m_ffn2_norm, m_ffn2_w_up, m_ffn2_w_down, v_ffn1_norm, v_ffn1_w_up, v_ffn1_w_down, v_mix_norm, v_w_in, v_b_gate, v_s5_lambda_re, v_s5_lambda_im, v_s5_log_dt, v_s5_b_re, v_s5_b_im, v_s5_c_re, v_s5_c_im, v_s5_d, v_s5_w_glu, v_w_br_s5, v_attn_q_gain, v_attn_k_gain, v_attn_rel_bias, v_w_br_attn, v_conv_w_dw, v_conv_b_dw, v_conv_ln_g, v_conv_ln_b, v_w_br_conv, v_w_out, v_ffn2_norm, v_ffn2_w_up, v_ffn2_w_down):
    args = locals()
    w = {nm: args[nm] for nm in WEIGHTS}
    m = {nm: args["m_" + nm] for nm in WEIGHTS}
    v = {nm: args["v_" + nm] for nm in WEIGHTS}
    loss, gx, outs = _step(x, loss_target, w, m, v)
    return (loss, gx, *[outs[nm][0] for nm in WEIGHTS], *[outs[nm][1] for nm in WEIGHTS],
            *[outs[nm][2] for nm in WEIGHTS], *[outs[nm][3] for nm in WEIGHTS])
```

```python
import functools
import math

import numpy as np
import jax
import jax.numpy as jnp
from jax import lax
from jax.experimental import pallas as pl
from jax.experimental.pallas import tpu as pltpu

F32 = jnp.float32
BF16 = jnp.bfloat16

CHUNK = 64
N_LEFT = 8
QBLK = CHUNK * N_LEFT
HEAD = 64
MAX_REL = 128
S5_GROUP = 16
S5_STATE = 64
CONV_W = 31
HALO = 32
EPS = 1e-6
NEG = -1e30
ADAM_LR, ADAM_B1, ADAM_B2, ADAM_EPS, ADAM_WD, ADAM_STEP = 0.001, 0.9, 0.999, 1e-08, 0.01, 10
N_DEV = 8
VMEM_LIMIT = 56 * 1024 * 1024


def _call(body, **kw):
    return pl.pallas_call(body, **kw)


def _params(**kw):
    return pltpu.CompilerParams(vmem_limit_bytes=VMEM_LIMIT, **kw)


def _tile(n, cap, unit=128):
    if n <= cap:
        return n
    d = (cap // unit) * unit
    while d >= unit:
        if n % d == 0:
            return d
        d -= unit
    raise ValueError(f"no tile for {n} under {cap}")


def _dot(a, b):
    return jnp.dot(a, b, preferred_element_type=F32)


def _dot_nt(a, b):
    return lax.dot_general(a, b, (((1,), (1,)), ((), ())), preferred_element_type=F32)


def _dot_tn(a, b):
    return lax.dot_general(a, b, (((0,), (0,)), ((), ())), preferred_element_type=F32)


def _sig(x):
    return 1.0 / (1.0 + jnp.exp(-x))


def _rms_fwd(x, g):
    rs = lax.rsqrt(jnp.mean(x * x, axis=-1, keepdims=True) + EPS)
    xhat = x * rs
    return xhat * g, xhat, rs


def _rms_bwd(dh, xhat, rs, g):
    dxh = dh * g
    dx = rs * (dxh - xhat * jnp.mean(dxh * xhat, axis=-1, keepdims=True))
    return dx, dh * xhat


_GELU_C = math.sqrt(2.0 / math.pi)


def _gelu(x):
    return 0.5 * x * (1.0 + jnp.tanh(_GELU_C * (x + 0.044715 * x * x * x)))


def _gelu_grad(x):
    t = jnp.tanh(_GELU_C * (x + 0.044715 * x * x * x))
    return 0.5 * (1.0 + t) + 0.5 * x * (1.0 - t * t) * _GELU_C * (1.0 + 3.0 * 0.044715 * x * x)


def _acc_out(ref, val, first):
    @pl.when(first)
    def _():
        ref[...] = val

    @pl.when(jnp.logical_not(first))
    def _():
        ref[...] += val


FF_CHUNKS = N_DEV // 2


def ffn_fwd(x, g, w_up, w_down, layer, comm=None):
    n, d = x.shape
    nn = w_up.shape[3]
    tm = _tile(n, 1024, 8)

    def body(x_ref, g_ref, wa_ref, wb_ref, wd_ref, xo_ref, a_ref, b_ref, hn_ref, acc_ref):
        j = pl.program_id(1)

        @pl.when(j == 0)
        def _():
            h, _, _ = _rms_fwd(x_ref[...], g_ref[...])
            hn_ref[...] = h.astype(BF16)
            acc_ref[...] = jnp.zeros_like(acc_ref)

        hn = hn_ref[...]
        a = _dot(hn, wa_ref[0, 0])
        b = _dot(hn, wb_ref[0, 0])
        a_ref[0] = a.astype(BF16)
        b_ref[0] = b.astype(BF16)
        act = a * _sig(a) * b
        acc_ref[...] += _dot(act.astype(BF16), wd_ref[0])

        @pl.when(j == FF_CHUNKS - 1)
        def _():
            xo_ref[...] = x_ref[...] + 0.5 * acc_ref[...]

    return _hosted_call(
        body, comm,
        name="ffn_fwd",
        grid=(n // tm, FF_CHUNKS),
        in_specs=[
            pl.BlockSpec((tm, d), lambda i, j: (i, 0)),
            pl.BlockSpec((1, d), lambda i, j: (0, 0)),
            pl.BlockSpec((1, 1, d, nn), lambda i, j: (j, layer, 0, 0)),
            pl.BlockSpec((1, 1, d, nn), lambda i, j: (j + FF_CHUNKS, layer, 0, 0)),
            pl.BlockSpec((1, nn, d), lambda i, j: (layer, j, 0)),
        ],
        out_specs=[
            pl.BlockSpec((tm, d), lambda i, j: (i, 0)),
            pl.BlockSpec((1, tm, nn), lambda i, j: (j, i, 0)),
            pl.BlockSpec((1, tm, nn), lambda i, j: (j, i, 0)),
        ],
        out_shape=[
            jax.ShapeDtypeStruct((n, d), F32),
            jax.ShapeDtypeStruct((FF_CHUNKS, n, nn), BF16),
            jax.ShapeDtypeStruct((FF_CHUNKS, n, nn), BF16),
        ],
        scratch_shapes=[pltpu.VMEM((tm, d), BF16), pltpu.VMEM((tm, d), F32)],
        args=(x, g, w_up, w_up, w_down),
    )


def ffn_bwd(dy, x, g, a, b, w_up, w_down, layer, comm=None):
    n, d = x.shape
    nn = w_up.shape[3]
    tm = _tile(n, 512, 8)

    def body(dy_ref, x_ref, g_ref, a_ref, b_ref, wa_ref, wb_ref, wd_ref,
             dx_ref, dg_ref, hn_ref, dyb_ref, da_ref, db_ref, act_ref, dyb_s, dh_ref):
        i, j = pl.program_id(0), pl.program_id(1)

        @pl.when(j == 0)
        def _():
            h, _, _ = _rms_fwd(x_ref[...], g_ref[...])
            hn_ref[...] = h.astype(BF16)
            dyb = (0.5 * dy_ref[...]).astype(BF16)
            dyb_ref[...] = dyb
            dyb_s[...] = dyb
            dh_ref[...] = jnp.zeros_like(dh_ref)

        dact = _dot_nt(dyb_s[...], wd_ref[0])
        a32 = a_ref[0].astype(F32)
        b32 = b_ref[0].astype(F32)
        s = _sig(a32)
        sil = a32 * s
        da = (dact * b32 * (s * (1.0 + a32 * (1.0 - s)))).astype(BF16)
        db = (dact * sil).astype(BF16)
        da_ref[0] = da
        db_ref[0] = db
        act_ref[0] = (sil * b32).astype(BF16)
        dh_ref[...] += _dot_nt(da, wa_ref[0, 0]) + _dot_nt(db, wb_ref[0, 0])

        @pl.when(j == FF_CHUNKS - 1)
        def _():
            gg = g_ref[...]
            _, xhat, rs = _rms_fwd(x_ref[...], gg)
            dxn, dgr = _rms_bwd(dh_ref[...], xhat, rs, gg)
            dx_ref[...] = dy_ref[...] + dxn
            _acc_out(dg_ref, jnp.sum(dgr, axis=0, keepdims=True), i == 0)

    tok = pl.BlockSpec((tm, d), lambda i, j: (i, 0))
    chunk = pl.BlockSpec((1, tm, nn), lambda i, j: (j, i, 0))
    vec = pl.BlockSpec((1, d), lambda i, j: (0, 0))
    chunks = jax.ShapeDtypeStruct((FF_CHUNKS, n, nn), BF16)
    return _hosted_call(
        body, comm,
        name="ffn_bwd",
        grid=(n // tm, FF_CHUNKS),
        in_specs=[
            tok, tok, vec, chunk, chunk,
            pl.BlockSpec((1, 1, d, nn), lambda i, j: (j, layer, 0, 0)),
            pl.BlockSpec((1, 1, d, nn), lambda i, j: (j + FF_CHUNKS, layer, 0, 0)),
            pl.BlockSpec((1, nn, d), lambda i, j: (layer, j, 0)),
        ],
        out_specs=[tok, vec, tok, tok, chunk, chunk, chunk],
        out_shape=[
            jax.ShapeDtypeStruct((n, d), F32),
            jax.ShapeDtypeStruct((1, d), F32),
            jax.ShapeDtypeStruct((n, d), BF16),
            jax.ShapeDtypeStruct((n, d), BF16),
            chunks, chunks, chunks,
        ],
        scratch_shapes=[pltpu.VMEM((tm, d), BF16), pltpu.VMEM((tm, d), F32)],
        args=(dy, x, g, a, b, w_up, w_up, w_down),
    )


def wgrad(a, b, name, out4, mode, *, g0=0, layer=0, prev=None, a_cols=None, dtype=BF16, comm=None):
    a3 = a if a.ndim == 3 else a[None]
    b3 = b if b.ndim == 3 else b[None]
    sa, n, ka = a3.shape
    sb, _, kb = b3.shape
    a0 = 0
    if a_cols is not None:
        a0, ka = a_cols
    k1, k2 = sa * ka, sb * kb
    depth, groups, rr, cc = out4
    t1 = _tile(math.gcd(ka, rr), 1024)
    t2 = _tile(math.gcd(kb, cc), 1024)
    tn = _tile(n, 1024, 8)
    gpb = 1
    if mode == "col":
        assert rr == k1 and k2 % cc == 0 and g0 + k2 // cc <= groups
        if kb % cc == 0 and cc < kb <= 1024 and g0 % (kb // cc) == 0:
            t2, gpb = kb, kb // cc
        per = max(cc // t2, 1)
        oblock = (1, gpb, t1, min(t2, cc))
        omap = lambda i, j, k: (layer, (g0 + j // per) // gpb, i, j % per)
    else:
        assert cc == k2 and k1 % rr == 0 and g0 + k1 // rr <= groups
        if ka % rr == 0 and rr < ka <= 1024 and g0 % (ka // rr) == 0:
            t1, gpb = ka, ka // rr
        per = max(rr // t1, 1)
        oblock = (1, gpb, min(t1, rr), t2)
        omap = lambda i, j, k: (layer, (g0 + i // per) // gpb, i % per, j)
    na, nb = ka // t1, kb // t2
    nk = n // tn

    def body(a_ref, b_ref, *rest):
        o_ref, acc_ref = rest[-2], rest[-1]
        k = pl.program_id(2)

        @pl.when(k == 0)
        def _():
            acc_ref[...] = jnp.zeros_like(acc_ref)

        acc_ref[...] += _dot_tn(a_ref[0].astype(BF16), b_ref[0].astype(BF16))

        @pl.when(k == nk - 1)
        def _():
            for g in range(gpb):
                if gpb == 1:
                    o_ref[0, 0] = acc_ref[...].astype(dtype)
                elif mode == "col":
                    o_ref[0, g] = acc_ref[:, g * cc:(g + 1) * cc].astype(dtype)
                else:
                    o_ref[0, g] = acc_ref[g * rr:(g + 1) * rr, :].astype(dtype)

    in_specs = [
        pl.BlockSpec((1, tn, t1), lambda i, j, k: (i // na, k, a0 // t1 + i % na)),
        pl.BlockSpec((1, tn, t2), lambda i, j, k: (j // nb, k, j % nb)),
    ]
    args = [a3, b3]
    if comm is not None:
        assert prev is None
        return _hosted_call(body, comm, name=name, grid=(k1 // t1, k2 // t2, nk), in_specs=in_specs,
                            out_specs=[pl.BlockSpec(oblock, omap)], out_shape=[jax.ShapeDtypeStruct(out4, dtype)],
                            scratch_shapes=[pltpu.VMEM((t1, t2), F32)], args=args)
    extra = {}
    if prev is not None:
        in_specs.append(pl.BlockSpec(memory_space=pl.ANY))
        args.append(prev)
        extra["input_output_aliases"] = {2: 0}
    return _call(
        body,
        name=name,
        grid=(k1 // t1, k2 // t2, nk),
        in_specs=in_specs,
        out_specs=pl.BlockSpec(oblock, omap),
        out_shape=jax.ShapeDtypeStruct(out4, dtype),
        scratch_shapes=[pltpu.VMEM((t1, t2), F32)],
        compiler_params=_params(),
        **extra,
    )(*args)


def proj_fwd(x, g, w, layer, comm=None):
    n, d = x.shape
    c = w.shape[2]
    tm, tc = _tile(n, 1024, 8), _tile(c, 768)

    def body(x_ref, g_ref, w_ref, o_ref, hn_ref):
        @pl.when(pl.program_id(1) == 0)
        def _():
            h, _, _ = _rms_fwd(x_ref[...], g_ref[...])
            hn_ref[...] = h.astype(BF16)

        o_ref[...] = _dot(hn_ref[...], w_ref[0])

    return _hosted_call(
        body, comm,
        name="proj_fwd",
        grid=(n // tm, c // tc),
        in_specs=[
            pl.BlockSpec((tm, d), lambda i, j: (i, 0)),
            pl.BlockSpec((1, d), lambda i, j: (0, 0)),
            pl.BlockSpec((1, d, tc), lambda i, j: (layer, 0, j)),
        ],
        out_specs=[pl.BlockSpec((tm, tc), lambda i, j: (i, j))],
        out_shape=[jax.ShapeDtypeStruct((n, c), F32)],
        scratch_shapes=[pltpu.VMEM((tm, d), BF16)],
        args=(x, g, w),
    )


def proj_bwd(dproj, dres, x, g, w, layer):
    n, d = x.shape
    c = w.shape[2]
    tm = _tile(n, 512, 8)

    def body(dp_ref, dr_ref, x_ref, g_ref, w_ref, dx_ref, dg_ref, hn_ref):
        dh = _dot_nt(dp_ref[...], w_ref[0])
        gg = g_ref[...]
        h, xhat, rs = _rms_fwd(x_ref[...], gg)
        hn_ref[...] = h.astype(BF16)
        dxn, dgr = _rms_bwd(dh, xhat, rs, gg)
        dx_ref[...] = dr_ref[...] + dxn
        _acc_out(dg_ref, jnp.sum(dgr, axis=0, keepdims=True), pl.program_id(0) == 0)

    tok = pl.BlockSpec((tm, d), lambda i: (i, 0))
    vec = pl.BlockSpec((1, d), lambda i: (0, 0))
    return _call(
        body,
        name="proj_bwd",
        grid=(n // tm,),
        in_specs=[
            pl.BlockSpec((tm, c), lambda i: (i, 0)), tok, tok, vec,
            pl.BlockSpec((1, d, c), lambda i: (layer, 0, 0), pipeline_mode=pl.Buffered(1)),
        ],
        out_specs=[tok, vec, tok],
        out_shape=[
            jax.ShapeDtypeStruct((n, d), F32),
            jax.ShapeDtypeStruct((1, d), F32),
            jax.ShapeDtypeStruct((n, d), BF16),
        ],
        compiler_params=_params(),
    )(dproj, dres, x, g, w)


S5_TS = 512


def _cmul(ar, ai, br, bi):
    return ar * br - ai * bi, ar * bi + ai * br


def _s5_tables(ar, ai, reverse):
    gp = ar.shape[1]
    if reverse:
        ai = -ai
    a1r, a1i = jnp.broadcast_to(ar, (8, gp)), jnp.broadcast_to(ai, (8, gp))
    a2r, a2i = _cmul(a1r, a1i, a1r, a1i)
    a4r, a4i = _cmul(a2r, a2i, a2r, a2i)
    a8r, a8i = _cmul(a4r, a4i, a4r, a4i)
    row = lax.broadcasted_iota(jnp.int32, (8, gp), 0)
    e = (8 - row) if reverse else (row + 1)
    pr, pi = jnp.ones((8, gp), F32), jnp.zeros((8, gp), F32)
    for bit, (fr, fi) in ((1, (a1r, a1i)), (2, (a2r, a2i)), (4, (a4r, a4i)), (8, (a8r, a8i))):
        nr, ni = _cmul(pr, pi, fr, fi)
        on = (e & bit) != 0
        pr, pi = jnp.where(on, nr, pr), jnp.where(on, ni, pi)
    return (a1r, a1i, a2r, a2i, a4r, a4i, pr, pi)


def _s5_scan(xr_ref, xi_ref, tab_ref, cr_ref, ci_ref, ts, reverse):
    gp = xr_ref.shape[1]
    nt = ts // 8
    row = lax.broadcasted_iota(jnp.int32, (8, gp), 0)

    def shifted(v, s):
        if reverse:
            return jnp.where(row < 8 - s, pltpu.roll(v, 8 - s, 0), 0.0)
        return jnp.where(row >= s, pltpu.roll(v, s, 0), 0.0)

    def step(k, carry):
        cr, ci = carry
        t = (nt - 1 - k) if reverse else k
        r0 = pl.multiple_of(t * 8, 8)
        br = xr_ref[pl.ds(r0, 8), :]
        bi = xi_ref[pl.ds(r0, 8), :]
        for q, s in enumerate((1, 2, 4)):
            fr, fi = tab_ref[2 * q], tab_ref[2 * q + 1]
            sr, si = shifted(br, s), shifted(bi, s)
            mr, mi = _cmul(fr, fi, sr, si)
            br, bi = br + mr, bi + mi
        mr, mi = _cmul(tab_ref[6], tab_ref[7], cr, ci)
        br, bi = br + mr, bi + mi
        xr_ref[pl.ds(r0, 8), :] = br
        xi_ref[pl.ds(r0, 8), :] = bi
        edge = 0 if reverse else 7
        return (jnp.broadcast_to(br[edge:edge + 1, :], (8, gp)),
                jnp.broadcast_to(bi[edge:edge + 1, :], (8, gp)))

    cr, ci = lax.fori_loop(0, nt, step, (cr_ref[...], ci_ref[...]), unroll=2)
    cr_ref[...] = cr
    ci_ref[...] = ci


def s5_fwd(proj, ucol, seq, ar, ai, bm_r, bm_i, cm_r, cm_i, dskip, w_glu, layer):
    n = proj.shape[0]
    ds, gp = bm_r.shape
    ts = min(S5_TS, seq)
    nt = seq // ts

    def body(u_ref, ar_ref, ai_ref, bmr_ref, bmi_ref, cmr_ref, cmi_ref, d_ref, wg_ref,
             out_ref, xr_ref, xi_ref, yp_ref, tab_ref, cr_ref, ci_ref):
        @pl.when(pl.program_id(1) == 0)
        def _():
            for q, v in enumerate(_s5_tables(ar_ref[...], ai_ref[...], False)):
                tab_ref[q] = v
            cr_ref[...] = jnp.zeros_like(cr_ref)
            ci_ref[...] = jnp.zeros_like(ci_ref)

        u = u_ref[...]
        ub = u.astype(BF16)
        xr_ref[...] = _dot(ub, bmr_ref[...])
        xi_ref[...] = _dot(ub, bmi_ref[...])
        _s5_scan(xr_ref, xi_ref, tab_ref, cr_ref, ci_ref, ts, False)
        y = (_dot(xr_ref[...].astype(BF16), cmr_ref[...]) - _dot(xi_ref[...].astype(BF16), cmi_ref[...])
             + d_ref[...] * u)
        yp_ref[...] = y
        z = _dot(_gelu(y).astype(BF16), wg_ref[0])
        out_ref[...] = z[:, :ds] * _sig(z[:, ds:])

    full = lambda shape: pl.BlockSpec(shape, lambda b, t: (0, 0))
    return _call(
        body,
        name="s5_fwd",
        grid=(n // seq, nt),
        in_specs=[
            pl.BlockSpec((ts, ds), lambda b, t: (b * nt + t, ucol)),
            full((1, gp)), full((1, gp)), full((ds, gp)), full((ds, gp)), full((gp, ds)), full((gp, ds)),
            full((1, ds)), pl.BlockSpec((1, ds, 2 * ds), lambda b, t: (layer, 0, 0)),
        ],
        out_specs=[
            pl.BlockSpec((ts, ds), lambda b, t: (b * nt + t, 0)),
            pl.BlockSpec((ts, gp), lambda b, t: (b * nt + t, 0)),
            pl.BlockSpec((ts, gp), lambda b, t: (b * nt + t, 0)),
            pl.BlockSpec((ts, ds), lambda b, t: (b * nt + t, 0)),
        ],
        out_shape=[
            jax.ShapeDtypeStruct((n, ds), F32),
            jax.ShapeDtypeStruct((n, gp), F32),
            jax.ShapeDtypeStruct((n, gp), F32),
            jax.ShapeDtypeStruct((n, ds), F32),
        ],
        scratch_shapes=[pltpu.VMEM((8, 8, gp), F32), pltpu.VMEM((8, gp), F32), pltpu.VMEM((8, gp), F32)],
        compiler_params=_params(),
    )(proj, ar, ai, bm_r, bm_i, cm_r, cm_i, dskip, w_glu)


def s5_bwd(dout, ypre, proj, ucol, xr, xi, seq, ar, ai, bm_r, bm_i, cm_r, cm_i, dskip, w_glu, layer):
    n = proj.shape[0]
    ds, gp = bm_r.shape
    ts = min(S5_TS, seq)
    nt = seq // ts

    def body(do_ref, yp_ref, u_ref, xr_ref, xi_ref, hr_ref, hi_ref, ar_ref, ai_ref, bmr_ref, bmi_ref,
             cmr_ref, cmi_ref, d_ref, wg_ref,
             du_ref, gr_ref, gi_ref, dyb_ref, glb_ref, dzb_ref, dar_ref, dai_ref, dd_ref,
             tab_ref, cr_ref, ci_ref):
        b, t = pl.program_id(0), pl.program_id(1)
        first = jnp.logical_and(b == 0, t == 0)

        @pl.when(t == 0)
        def _():
            for q, v in enumerate(_s5_tables(ar_ref[...], ai_ref[...], True)):
                tab_ref[q] = v
            cr_ref[...] = jnp.zeros_like(cr_ref)
            ci_ref[...] = jnp.zeros_like(ci_ref)

        yp = yp_ref[...]
        u = u_ref[...]
        gl = _gelu(yp).astype(BF16)
        glb_ref[...] = gl
        z = _dot(gl, wg_ref[0])
        za, sg = z[:, :ds], _sig(z[:, ds:])
        do = do_ref[...]
        da = (do * sg).astype(BF16)
        dg = (do * za * sg * (1.0 - sg)).astype(BF16)
        dzb_ref[:, :ds] = da
        dzb_ref[:, ds:] = dg
        dgl = _dot_nt(da, wg_ref[0, :, :ds]) + _dot_nt(dg, wg_ref[0, :, ds:])
        dyp = dgl * _gelu_grad(yp)
        dypb = dyp.astype(BF16)
        dyb_ref[...] = dypb
        _acc_out(dd_ref, jnp.sum(dyp * u, axis=0, keepdims=True), first)

        gr_ref[...] = _dot_nt(dypb, cmr_ref[...])
        gi_ref[...] = -_dot_nt(dypb, cmi_ref[...])
        _s5_scan(gr_ref, gi_ref, tab_ref, cr_ref, ci_ref, ts, True)
        gr, gi = gr_ref[...], gi_ref[...]
        du_ref[...] = d_ref[...] * dyp + _dot_nt(gr.astype(BF16), bmr_ref[...]) + _dot_nt(gi.astype(BF16), bmi_ref[...])

        row = lax.broadcasted_iota(jnp.int32, (ts, gp), 0)
        live = jnp.where(t == nt - 1, 0.0, 1.0)
        pr = jnp.broadcast_to(hr_ref[7:8, :] * live, (ts, gp))
        pi = jnp.broadcast_to(hi_ref[7:8, :] * live, (ts, gp))
        sr = jnp.where(row == 0, pr, pltpu.roll(xr_ref[...], 1, 0))
        si = jnp.where(row == 0, pi, pltpu.roll(xi_ref[...], 1, 0))
        _acc_out(dar_ref, jnp.sum(gr * sr + gi * si, axis=0, keepdims=True), first)
        _acc_out(dai_ref, jnp.sum(gi * sr - gr * si, axis=0, keepdims=True), first)

    full = lambda shape: pl.BlockSpec(shape, lambda b, t: (0, 0))
    blk = lambda w, col=0: pl.BlockSpec((ts, w), lambda b, t: (b * nt + nt - 1 - t, col))
    halo = pl.BlockSpec((8, gp), lambda b, t: (jnp.maximum((b * seq + (nt - 1 - t) * ts) // 8 - 1, 0), 0))
    return _call(
        body,
        name="s5_bwd",
        grid=(n // seq, nt),
        in_specs=[
            blk(ds), blk(ds), blk(ds, ucol), blk(gp), blk(gp), halo, halo,
            full((1, gp)), full((1, gp)), full((ds, gp)), full((ds, gp)), full((gp, ds)), full((gp, ds)),
            full((1, ds)), pl.BlockSpec((1, ds, 2 * ds), lambda b, t: (layer, 0, 0)),
        ],
        out_specs=[
            blk(ds), blk(gp), blk(gp), blk(ds), blk(ds), blk(2 * ds),
            full((1, gp)), full((1, gp)), full((1, ds)),
        ],
        out_shape=[
            jax.ShapeDtypeStruct((n, ds), F32),
            jax.ShapeDtypeStruct((n, gp), F32),
            jax.ShapeDtypeStruct((n, gp), F32),
            jax.ShapeDtypeStruct((n, ds), BF16),
            jax.ShapeDtypeStruct((n, ds), BF16),
            jax.ShapeDtypeStruct((n, 2 * ds), BF16),
            jax.ShapeDtypeStruct((1, gp), F32),
            jax.ShapeDtypeStruct((1, gp), F32),
            jax.ShapeDtypeStruct((1, ds), F32),
        ],
        scratch_shapes=[pltpu.VMEM((8, 8, gp), F32), pltpu.VMEM((8, gp), F32), pltpu.VMEM((8, gp), F32)],
        compiler_params=_params(),
    )(dout, ypre, proj, xr, xi, xr, xi, ar, ai, bm_r, bm_i, cm_r, cm_i, dskip, w_glu)


BAND = QBLK + CHUNK
NCH = QBLK // CHUNK


PAIR = 2 * HEAD


def _attn_specs(nq):
    last = nq - 1
    cur = lambda col0=0: pl.BlockSpec(
        (QBLK, PAIR), lambda p, b, i: (b * nq + jnp.minimum(i, last), col0 // PAIR + p))
    prev = lambda col0=0: pl.BlockSpec(
        (QBLK, PAIR), lambda p, b, i: (b * nq + jnp.maximum(jnp.minimum(i, last) - 1, 0), col0 // PAIR + p))
    vec = pl.BlockSpec((2, 1, 2 * QBLK), lambda p, b, i: (p, 0, 0))
    gain = pl.BlockSpec((1, PAIR), lambda p, b, i: (0, 0))
    return cur, prev, vec, gain


def _pair_masks():
    lane = lax.broadcasted_iota(jnp.int32, (1, PAIR), 1)
    return lane < HEAD, [(lane < HEAD).astype(F32), (lane >= HEAD).astype(F32)]


def _pair_mean(t, low, m0):
    s0 = jnp.sum(t * m0, axis=-1, keepdims=True)
    s1 = jnp.sum(t, axis=-1, keepdims=True) - s0
    return jnp.where(low, s0, s1) * (1.0 / HEAD)


def _pair_rms_fwd(x, g, low, m0):
    rs = lax.rsqrt(_pair_mean(x * x, low, m0) + EPS)
    xhat = x * rs
    return xhat * g, xhat, rs


def _pair_rms_bwd(dh, xhat, rs, g, low, m0):
    dxh = dh * g
    dx = rs * (dxh - xhat * _pair_mean(dxh * xhat, low, m0))
    return dx, dh * xhat


def _attn_build_table(tv, bias_ref, tab_ref):
    w = 2 * QBLK
    for qi in range(CHUNK):
        bias_ref[qi:qi + 1, :] = pltpu.roll(tv, (qi - (CHUNK - 1)) % w, 1)
    bias = bias_ref[...]
    lane = lax.broadcasted_iota(jnp.int32, (CHUNK, w), 1)
    for c in range(NCH):
        rolled = bias if c == 0 else pltpu.roll(bias, CHUNK * c, 1)
        ok = jnp.logical_and(lane >= CHUNK * c, lane < CHUNK * c + BAND)
        tab_ref[CHUNK * c:CHUNK * (c + 1), :] = jnp.where(ok, rolled, NEG)


def _attn_reduce_table(dt_ref, bias_ref):
    w = 2 * QBLK
    acc = dt_ref[0:CHUNK, :]
    for c in range(1, NCH):
        acc = acc + pltpu.roll(dt_ref[CHUNK * c:CHUNK * (c + 1), :], w - CHUNK * c, 1)
    bias_ref[...] = acc
    out = jnp.zeros((1, w), F32)
    for qi in range(CHUNK):
        out = out + pltpu.roll(bias_ref[qi:qi + 1, :], ((CHUNK - 1) - qi) % w, 1)
    return out


HALF = QBLK // 2
KSPAN = QBLK + HALF


def _attn_softmax(s, table, first_block, n_prev):
    s = s * (HEAD ** -0.5) + table
    col = lax.broadcasted_iota(jnp.int32, s.shape, 1)
    s = jnp.where(jnp.logical_and(first_block, col < n_prev), NEG, s)
    e = jnp.exp(s - jnp.max(s, axis=-1, keepdims=True))
    return e * (1.0 / jnp.sum(e, axis=-1, keepdims=True))


def attn_fwd(proj, q0, k0, v0, tv, gq, gk, seq, comm=None):
    n = proj.shape[0]
    heads = tv.shape[0]
    nq = seq // QBLK
    cur, prev, vec, gain = _attn_specs(nq)

    def body(q_ref, kp_ref, kc_ref, vp_ref, vc_ref, tv_ref, gq_ref, gk_ref, o_ref, bias_ref, tab_ref):
        @pl.when(jnp.logical_and(pl.program_id(1) == 0, pl.program_id(2) == 0))
        def _():
            for h in range(2):
                _attn_build_table(tv_ref[h], bias_ref, tab_ref.at[h])

        low, m = _pair_masks()
        qn, _, _ = _pair_rms_fwd(q_ref[...], gq_ref[...], low, m[0])
        kn, _, _ = _pair_rms_fwd(jnp.concatenate([kp_ref[...], kc_ref[...]], axis=0), gk_ref[...], low, m[0])
        knb = kn.astype(BF16)
        v = jnp.concatenate([vp_ref[...], vc_ref[...]], axis=0)
        first_block = pl.program_id(2) == 0
        for r in range(2):
            rows, cols = slice(r * HALF, (r + 1) * HALF), slice(r * HALF, r * HALF + KSPAN)
            o = jnp.zeros((HALF, PAIR), F32)
            for h in range(2):
                s = _dot_nt((qn[rows] * m[h]).astype(BF16), knb[cols])
                p = _attn_softmax(s, tab_ref[h, rows, cols], first_block, QBLK - r * HALF)
                o = o + _dot(p.astype(BF16), (v[cols] * m[h]).astype(BF16))
            o_ref[rows, :] = o

    return _hosted_call(
        body, comm,
        name="attn_fwd",
        grid=(heads // 2, n // seq, nq),
        in_specs=[cur(q0), prev(k0), cur(k0), prev(v0), cur(v0), vec, gain, gain],
        out_specs=[cur()],
        out_shape=[jax.ShapeDtypeStruct((n, heads * HEAD), F32)],
        scratch_shapes=[pltpu.VMEM((CHUNK, 2 * QBLK), F32), pltpu.VMEM((2, QBLK, 2 * QBLK), F32)],
        args=(proj, proj, proj, proj, proj, tv, gq, gk),
    )


def attn_bwd(do, proj, q0, k0, v0, tv, gq, gk, seq, comm=None):
    n = proj.shape[0]
    heads = tv.shape[0]
    nb = n // seq
    nq = seq // QBLK
    cur, prev, vec, gain = _attn_specs(nq)
    keyblk = pl.BlockSpec((QBLK, PAIR), lambda p, b, i: (b * nq + jnp.maximum(i - 1, 0), p))

    def body(do_ref, q_ref, kp_ref, kc_ref, vp_ref, vc_ref, tv_ref, gq_ref, gk_ref,
             dq_ref, dk_ref, dv_ref, dtv_ref, dgq_ref, dgk_ref, bias_ref, tab_ref, dt_ref, ck_ref, cv_ref,
             dqn_ref, dkn_ref, dvv_ref):
        pp, b, i = pl.program_id(0), pl.program_id(1), pl.program_id(2)
        head_start = jnp.logical_and(b == 0, i == 0)

        @pl.when(head_start)
        def _():
            for h in range(2):
                _attn_build_table(tv_ref[h], bias_ref, tab_ref.at[h])
            dt_ref[...] = jnp.zeros_like(dt_ref)

        @pl.when(i < nq)
        def _():
            low, m = _pair_masks()
            gq_, gk_ = gq_ref[...], gk_ref[...]
            qn, qhat, qrs = _pair_rms_fwd(q_ref[...], gq_, low, m[0])
            kn, khat, krs = _pair_rms_fwd(jnp.concatenate([kp_ref[...], kc_ref[...]], axis=0), gk_, low, m[0])
            knb = kn.astype(BF16)
            vb = jnp.concatenate([vp_ref[...], vc_ref[...]], axis=0).astype(BF16)
            do_ = do_ref[...]
            dkn_ref[...] = jnp.zeros_like(dkn_ref)
            dvv_ref[...] = jnp.zeros_like(dvv_ref)
            for r in range(2):
                rows, cols = slice(r * HALF, (r + 1) * HALF), slice(r * HALF, r * HALF + KSPAN)
                dqn = jnp.zeros((HALF, PAIR), F32)
                for h in range(2):
                    qh = (qn[rows] * m[h]).astype(BF16)
                    doh = (do_[rows] * m[h]).astype(BF16)
                    p = _attn_softmax(_dot_nt(qh, knb[cols]), tab_ref[h, rows, cols], i == 0, QBLK - r * HALF)
                    dvv_ref[cols, :] += _dot_tn(p.astype(BF16), doh)
                    dp = _dot_nt(doh, vb[cols])
                    ds = p * (dp - jnp.sum(p * dp, axis=-1, keepdims=True))
                    dt_ref[h, rows, cols] += ds
                    dsb = (ds * (HEAD ** -0.5)).astype(BF16)
                    dqn = dqn + _dot(dsb, (kn[cols] * m[h]).astype(BF16))
                    dkn_ref[cols, :] += _dot_tn(dsb, qh)
                dqn_ref[rows, :] = dqn
            dv = dvv_ref[...]
            dq, dgq_rows = _pair_rms_bwd(dqn_ref[...], qhat, qrs, gq_, low, m[0])
            dk, dgk_rows = _pair_rms_bwd(dkn_ref[...], khat, krs, gk_, low, m[0])
            dq_ref[...] = dq

            @pl.when(i == 0)
            def _():
                dk_ref[...] = dk[:QBLK]
                dv_ref[...] = dv[:QBLK]

            @pl.when(i > 0)
            def _():
                dk_ref[...] = ck_ref[...] + dk[:QBLK]
                dv_ref[...] = cv_ref[...] + dv[:QBLK]

            ck_ref[...] = dk[QBLK:]
            cv_ref[...] = dv[QBLK:]
            first = jnp.logical_and(pp == 0, head_start)
            _acc_out(dgq_ref, jnp.sum(dgq_rows, axis=0, keepdims=True), first)
            _acc_out(dgk_ref, jnp.sum(dgk_rows, axis=0, keepdims=True), first)

        @pl.when(i == nq)
        def _():
            dk_ref[...] = ck_ref[...]
            dv_ref[...] = cv_ref[...]

        @pl.when(jnp.logical_and(b == nb - 1, i == nq))
        def _():
            for h in range(2):
                dtv_ref[h] = _attn_reduce_table(dt_ref.at[h], bias_ref)

    tok = jax.ShapeDtypeStruct((n, heads * HEAD), F32)
    return _hosted_call(
        body, comm,
        name="attn_bwd",
        grid=(heads // 2, nb, nq + 1),
        in_specs=[cur(), cur(q0), prev(k0), cur(k0), prev(v0), cur(v0), vec, gain, gain],
        out_specs=[cur(), keyblk, keyblk, vec, gain, gain],
        out_shape=[tok, tok, tok, jax.ShapeDtypeStruct(tv.shape, F32),
                   jax.ShapeDtypeStruct((1, PAIR), F32), jax.ShapeDtypeStruct((1, PAIR), F32)],
        scratch_shapes=[pltpu.VMEM((CHUNK, 2 * QBLK), F32), pltpu.VMEM((2, QBLK, 2 * QBLK), F32),
                        pltpu.VMEM((2, QBLK, 2 * QBLK), F32), pltpu.VMEM((QBLK, PAIR), F32),
                        pltpu.VMEM((QBLK, PAIR), F32), pltpu.VMEM((QBLK, PAIR), F32),
                        pltpu.VMEM((2 * QBLK, PAIR), F32), pltpu.VMEM((2 * QBLK, PAIR), F32)],
        args=(do, proj, proj, proj, proj, proj, tv, gq, gk),
    )


CONV_TC = 512


def _ln_fwd(h1, g, b):
    mu = jnp.mean(h1, axis=-1, keepdims=True)
    xc = h1 - mu
    rs = lax.rsqrt(jnp.mean(xc * xc, axis=-1, keepdims=True) + EPS)
    yhat = xc * rs
    return yhat * g + b, yhat, rs


def _glu(z, dc):
    return z[:, :dc] * _sig(z[:, dc:])


def conv_fwd(proj, zcol, seq, w, bdw, lng, lnb):
    n = proj.shape[0]
    dc = w.shape[1]
    tc = min(CONV_TC, seq)
    nt = seq // tc

    def body(z_ref, zp_ref, w_ref, b_ref, g_ref, lb_ref, h1_ref, o_ref, ext_ref):
        live = jnp.where(pl.program_id(1) == 0, 0.0, 1.0)
        ext_ref[pl.ds(0, HALO), :] = _glu(zp_ref[...], dc) * live
        ext_ref[pl.ds(HALO, tc), :] = _glu(z_ref[...], dc)
        acc = jnp.zeros((tc, dc), F32) + b_ref[...]
        for j in range(CONV_W):
            acc = acc + w_ref[j:j + 1, :] * ext_ref[pl.ds(HALO - (CONV_W - 1) + j, tc), :]
        h1_ref[...] = acc
        ln, _, _ = _ln_fwd(acc, g_ref[...], lb_ref[...])
        o_ref[...] = ln * _sig(ln)

    full = lambda shape: pl.BlockSpec(shape, lambda b, t: (0, 0))
    return _call(
        body,
        name="conv_fwd",
        grid=(n // seq, nt),
        in_specs=[
            pl.BlockSpec((tc, 2 * dc), lambda b, t: (b * nt + t, zcol)),
            pl.BlockSpec((HALO, 2 * dc), lambda b, t: (jnp.maximum((b * seq + t * tc) // HALO - 1, 0), zcol)),
            full((HALO, dc)), full((1, dc)), full((1, dc)), full((1, dc)),
        ],
        out_specs=[
            pl.BlockSpec((tc, dc), lambda b, t: (b * nt + t, 0)),
            pl.BlockSpec((tc, dc), lambda b, t: (b * nt + t, 0)),
        ],
        out_shape=[jax.ShapeDtypeStruct((n, dc), F32), jax.ShapeDtypeStruct((n, dc), F32)],
        scratch_shapes=[pltpu.VMEM((tc + HALO, dc), F32)],
        compiler_params=_params(),
    )(proj, proj, w, bdw, lng, lnb)


def conv_bwd(dco, h1, proj, zcol, seq, w, bdw, lng, lnb):
    n = proj.shape[0]
    dc = w.shape[1]
    tc = min(CONV_TC, seq)
    nt = seq // tc
    nrow = n // HALO

    def body(do_ref, don_ref, h1_ref, h1n_ref, z_ref, zp_ref, w_ref, g_ref, lb_ref,
             dz_ref, dw_ref, db_ref, dg_ref, dlb_ref, ext_ref, dext_ref):
        b, t = pl.program_id(0), pl.program_id(1)
        first = jnp.logical_and(b == 0, t == 0)
        g, lb = g_ref[...], lb_ref[...]

        def dh1_of(do, h1):
            ln, yhat, rs = _ln_fwd(h1, g, lb)
            s = _sig(ln)
            dln = do * (s * (1.0 + ln * (1.0 - s)))
            dyh = dln * g
            dh1 = rs * (dyh - jnp.mean(dyh, axis=-1, keepdims=True)
                        - yhat * jnp.mean(dyh * yhat, axis=-1, keepdims=True))
            return dh1, dln, yhat

        dh1, dln, yhat = dh1_of(do_ref[...], h1_ref[...])
        dh1n, _, _ = dh1_of(don_ref[...], h1n_ref[...])
        _acc_out(dg_ref, jnp.sum(dln * yhat, axis=0, keepdims=True), first)
        _acc_out(dlb_ref, jnp.sum(dln, axis=0, keepdims=True), first)
        _acc_out(db_ref, jnp.sum(dh1, axis=0, keepdims=True), first)

        dext_ref[pl.ds(0, tc), :] = dh1
        dext_ref[pl.ds(tc, HALO), :] = dh1n * jnp.where(t == nt - 1, 0.0, 1.0)
        z = z_ref[...]
        ext_ref[pl.ds(0, HALO), :] = _glu(zp_ref[...], dc) * jnp.where(t == 0, 0.0, 1.0)
        ext_ref[pl.ds(HALO, tc), :] = _glu(z, dc)

        @pl.when(first)
        def _():
            dw_ref[...] = jnp.zeros_like(dw_ref)

        dh0 = jnp.zeros((tc, dc), F32)
        for j in range(CONV_W):
            dh0 = dh0 + w_ref[j:j + 1, :] * dext_ref[pl.ds(CONV_W - 1 - j, tc), :]
            dw_ref[j:j + 1, :] += jnp.sum(dh1 * ext_ref[pl.ds(HALO - (CONV_W - 1) + j, tc), :],
                                          axis=0, keepdims=True)
        za, sg = z[:, :dc], _sig(z[:, dc:])
        dz_ref[:, :dc] = dh0 * sg
        dz_ref[:, dc:] = dh0 * za * sg * (1.0 - sg)

    full = lambda shape: pl.BlockSpec(shape, lambda b, t: (0, 0))
    cur = lambda wd, col=0: pl.BlockSpec((tc, wd), lambda b, t: (b * nt + t, col))
    nxt = pl.BlockSpec((HALO, dc), lambda b, t: (jnp.minimum((b * seq + (t + 1) * tc) // HALO, nrow - 1), 0))
    return _call(
        body,
        name="conv_bwd",
        grid=(n // seq, nt),
        in_specs=[
            cur(dc), nxt, cur(dc), nxt, cur(2 * dc, zcol),
            pl.BlockSpec((HALO, 2 * dc), lambda b, t: (jnp.maximum((b * seq + t * tc) // HALO - 1, 0), zcol)),
            full((HALO, dc)), full((1, dc)), full((1, dc)),
        ],
        out_specs=[cur(2 * dc), full((HALO, dc)), full((1, dc)), full((1, dc)), full((1, dc))],
        out_shape=[
            jax.ShapeDtypeStruct((n, 2 * dc), F32),
            jax.ShapeDtypeStruct((HALO, dc), F32),
            jax.ShapeDtypeStruct((1, dc), F32),
            jax.ShapeDtypeStruct((1, dc), F32),
            jax.ShapeDtypeStruct((1, dc), F32),
        ],
        scratch_shapes=[pltpu.VMEM((tc + HALO, dc), F32), pltpu.VMEM((tc + HALO, dc), F32)],
        compiler_params=_params(),
    )(dco, dco, h1, h1, proj, proj, w, lng, lnb)


def _merge_common(l0, l1, l2, bg, so, ao, co, wbs, wba, wbc, d):
    ys = _dot(so.astype(BF16), wbs)
    ya = _dot(ao.astype(BF16), wba)
    yc = _dot(co.astype(BF16), wbc)
    gs = _sig(l0 + bg[:, :d])
    ga = _sig(l1 + bg[:, d:2 * d])
    gc = _sig(l2 + bg[:, 2 * d:])
    return (ys, ya, yc), (gs, ga, gc)


def _merge_specs(tm, d, dss, da, dc, layer):
    row = lambda w, col=0: pl.BlockSpec((tm, w), lambda i: (i, col))
    full = lambda r, c: pl.BlockSpec((r, c), lambda i: (0, 0))
    stacked = lambda r: pl.BlockSpec((1, r, d), lambda i: (layer, 0, 0))
    acts = [row(d, 0), row(d, 1), row(d, 2), full(1, 3 * d), row(dss), row(da), row(dc)]
    weights = [stacked(dss), stacked(da), stacked(dc), stacked(d)]
    return row, full, acts, weights


def merge_fwd(x, proj, bg, so, ao, co, wbs, wba, wbc, wout, layer):
    n, d = x.shape
    tm = _tile(n, 256, 8)
    row, full, acts, weights = _merge_specs(tm, d, so.shape[1], ao.shape[1], co.shape[1], layer)

    def body(x_ref, l0_ref, l1_ref, l2_ref, bg_ref, so_ref, ao_ref, co_ref,
             wbs_ref, wba_ref, wbc_ref, wo_ref, o_ref):
        (ys, ya, yc), (gs, ga, gc) = _merge_common(
            l0_ref[...], l1_ref[...], l2_ref[...], bg_ref[...], so_ref[...], ao_ref[...], co_ref[...],
            wbs_ref[0], wba_ref[0], wbc_ref[0], d)
        merged = gs * ys + ga * ya + gc * yc
        o_ref[...] = x_ref[...] + _dot(merged.astype(BF16), wo_ref[0])

    return _call(
        body,
        name="merge_fwd",
        grid=(n // tm,),
        in_specs=[row(d)] + acts + weights,
        out_specs=row(d),
        out_shape=jax.ShapeDtypeStruct((n, d), F32),
        compiler_params=_params(),
    )(x, proj, proj, proj, bg, so, ao, co, wbs, wba, wbc, wout)


def merge_bwd(dx, proj, bg, so, ao, co, wbs, wba, wbc, wout, layer):
    n, d = dx.shape
    dss, da, dc = so.shape[1], ao.shape[1], co.shape[1]
    tm = _tile(n, 256, 8)
    row, full, acts, weights = _merge_specs(tm, d, dss, da, dc, layer)

    def body(dx_ref, l0_ref, l1_ref, l2_ref, bg_ref, so_ref, ao_ref, co_ref, wbs_ref, wba_ref, wbc_ref, wo_ref,
             dl_ref, dso_ref, dao_ref, dco_ref, dbg_ref, mg_ref, dxb_ref, dys_ref, dya_ref, dyc_ref):
        wbs, wba, wbc = wbs_ref[0], wba_ref[0], wbc_ref[0]
        (ys, ya, yc), (gs, ga, gc) = _merge_common(
            l0_ref[...], l1_ref[...], l2_ref[...], bg_ref[...], so_ref[...], ao_ref[...], co_ref[...],
            wbs, wba, wbc, d)
        mg_ref[...] = (gs * ys + ga * ya + gc * yc).astype(BF16)
        dxb = dx_ref[...].astype(BF16)
        dxb_ref[...] = dxb
        dm = _dot_nt(dxb, wo_ref[0])
        first = pl.program_id(0) == 0
        for k, (y, g, w, dy_ref, db_ref) in enumerate((
                (ys, gs, wbs, dys_ref, dso_ref), (ya, ga, wba, dya_ref, dao_ref), (yc, gc, wbc, dyc_ref, dco_ref))):
            dl = dm * y * g * (1.0 - g)
            dl_ref[:, k * d:(k + 1) * d] = dl.astype(BF16)
            _acc_out(dbg_ref.at[:, k * d:(k + 1) * d], jnp.sum(dl, axis=0, keepdims=True), first)
            dy = (dm * g).astype(BF16)
            dy_ref[...] = dy
            db_ref[...] = _dot_nt(dy, w)

    bf = lambda w: jax.ShapeDtypeStruct((n, w), BF16)
    return _call(
        body,
        name="merge_bwd",
        grid=(n // tm,),
        in_specs=[row(d)] + acts + weights,
        out_specs=[row(3 * d), row(dss), row(da), row(dc), full(1, 3 * d),
                   row(d), row(d), row(d), row(d), row(d)],
        out_shape=[bf(3 * d), jax.ShapeDtypeStruct((n, dss), F32), jax.ShapeDtypeStruct((n, da), F32),
                   jax.ShapeDtypeStruct((n, dc), F32), jax.ShapeDtypeStruct((1, 3 * d), F32),
                   bf(d), bf(d), bf(d), bf(d), bf(d)],
        compiler_params=_params(),
    )(dx, proj, proj, proj, bg, so, ao, co, wbs, wba, wbc, wout)


def loss_head(y, target):
    n, d = y.shape
    tm = _tile(n, 512, 8)

    def body(y_ref, t_ref, dy_ref, l_ref):
        e = y_ref[...] - t_ref[...]
        dy_ref[...] = e * (1.0 / d)
        part = 0.5 * jnp.sum(jnp.sum(e * e, axis=-1, keepdims=True) * (1.0 / d), axis=0, keepdims=True)
        _acc_out(l_ref, part, pl.program_id(0) == 0)

    return _call(
        body,
        name="loss_head",
        grid=(n // tm,),
        in_specs=[pl.BlockSpec((tm, d), lambda i: (i, 0)), pl.BlockSpec((tm, d), lambda i: (i, 0))],
        out_specs=[pl.BlockSpec((tm, d), lambda i: (i, 0)), pl.BlockSpec((1, 1), lambda i: (0, 0))],
        out_shape=[jax.ShapeDtypeStruct((n, d), F32), jax.ShapeDtypeStruct((1, 1), F32)],
        compiler_params=_params(),
    )(y, target)


def _mesh_pos():
    return lax.axis_index("x"), lax.axis_index("y"), lax.axis_index("c")


ANY = pl.BlockSpec(memory_space=pl.ANY)


def _comm_sems(na):
    return [pltpu.SemaphoreType.DMA((na, 7)), pltpu.SemaphoreType.DMA((na, 7)), pltpu.SemaphoreType.DMA((na,))]


def _gather_plan(x_refs, out_refs, sems):
    na = len(x_refs)
    send_sems, recv_sems, local_sems = sems
    x, y, c = _mesh_pos()
    me, sibling = (x, y, c), (x, y, 1 - c)
    chips = [(1 - x, y), (x, 1 - y), (1 - x, 1 - y)]

    def slot(a, px, py, pc):
        return out_refs[a].at[4 * px + 2 * py + pc]

    def copy(a, k, block, to, src=None):
        return pltpu.make_async_remote_copy(
            src_ref=slot(a, *block) if src is None else src, dst_ref=slot(a, *block),
            send_sem=send_sems.at[a, k], recv_sem=recv_sems.at[a, k],
            device_id=to, device_id_type=pl.DeviceIdType.MESH)

    mine = [pltpu.make_async_copy(x_refs[a], slot(a, *me), local_sems.at[a]) for a in range(na)]
    first = []
    for a in range(na):
        first.append(copy(a, 0, me, sibling, src=x_refs[a]))
        first += [copy(a, 1 + j, me, (*chip, c), src=x_refs[a]) for j, chip in enumerate(chips)]

    def start():
        for cp in mine + first:
            cp.start()

    def finish():
        passed = []
        for j, chip in enumerate(chips):
            for a in range(na):
                copy(a, 1 + j, (*chip, c), me).wait_recv()
                fwd = copy(a, 4 + j, (*chip, c), sibling)
                fwd.start()
                passed.append(fwd)
        for a in range(na):
            copy(a, 0, sibling, me).wait_recv()
            for j, chip in enumerate(chips):
                copy(a, 4 + j, (*chip, 1 - c), me).wait_recv()
        for cp in first + passed:
            cp.wait_send()
        for cp in mine:
            cp.wait()

    return start, finish


def _gather_out(shards):
    return [jax.ShapeDtypeStruct((N_DEV,) + s.shape, s.dtype) for s in shards]


def all_gather(shards):
    na = len(shards)

    def body(*refs):
        start, finish = _gather_plan(refs[:na], refs[na:2 * na], refs[2 * na:])
        start()
        finish()

    return _call(
        body,
        name="all_gather",
        out_shape=_gather_out(shards),
        in_specs=[ANY] * na,
        out_specs=[ANY] * na,
        scratch_shapes=_comm_sems(na),
    )(*shards)


def _hosted_call(body, comm, *, name, grid, in_specs, out_specs, out_shape, scratch_shapes, args):
    if comm is None:
        res = _call(body, name=name, grid=grid, in_specs=in_specs, out_specs=out_specs, out_shape=out_shape,
                    scratch_shapes=scratch_shapes, compiler_params=_params())(*args)
        return res, []
    plan, arrays, c_out = comm
    n_in, n_out, n_scr, ci, co = len(in_specs), len(out_specs), len(scratch_shapes), len(arrays), len(c_out)

    def hosted(*refs):
        ins, cins = refs[:n_in], refs[n_in:n_in + ci]
        o0 = n_in + ci
        outs, couts = refs[o0:o0 + n_out], refs[o0 + n_out:o0 + n_out + co]
        s0 = o0 + n_out + co
        scr, sems = refs[s0:s0 + n_scr], refs[s0 + n_scr:]
        ids = [pl.program_id(ax) for ax in range(len(grid))]
        first = functools.reduce(jnp.logical_and, [i == 0 for i in ids])
        last = functools.reduce(jnp.logical_and, [i == g - 1 for i, g in zip(ids, grid)])
        start, finish = plan(cins, couts, sems)
        pl.when(first)(start)
        body(*ins, *outs, *scr)
        pl.when(last)(finish)

    res = _call(hosted, name=name + "_comm", grid=grid, in_specs=list(in_specs) + [ANY] * ci,
                out_specs=list(out_specs) + [ANY] * co, out_shape=list(out_shape) + list(c_out),
                scratch_shapes=list(scratch_shapes) + _comm_sems(max(ci, co)),
                compiler_params=_params())(*args, *arrays)
    return res[:n_out], res[n_out:]


def _exchange_plan(bcast=()):
    def plan(s_refs, r_refs, sems):
        na = len(s_refs)
        send_sems, recv_sems, local_sems = sems
        x, y, c = _mesh_pos()
        me = 4 * x + 2 * y + c

        def peer(k):
            px = (1 - x) if k & 4 else x
            py = (1 - y) if k & 2 else y
            pc = (1 - c) if k & 1 else c
            return (px, py, pc), 4 * px + 2 * py + pc

        def src_dst(a, pid, slot):
            return (s_refs[a] if a in bcast else s_refs[a].at[pid]), r_refs[a].at[slot]

        def copy(a, k):
            to, pid = peer(k)
            src, dst = src_dst(a, pid, me)
            return pltpu.make_async_remote_copy(
                src_ref=src, dst_ref=dst, send_sem=send_sems.at[a, k - 1], recv_sem=recv_sems.at[a, k - 1],
                device_id=to, device_id_type=pl.DeviceIdType.MESH)

        def arrival(a, k):
            _, pid = peer(k)
            src, dst = src_dst(a, pid, pid)
            return pltpu.make_async_remote_copy(
                src_ref=src, dst_ref=dst, send_sem=send_sems.at[a, k - 1], recv_sem=recv_sems.at[a, k - 1],
                device_id=(x, y, c), device_id_type=pl.DeviceIdType.MESH)

        mine = [pltpu.make_async_copy(*src_dst(a, me, me), local_sems.at[a]) for a in range(na)]
        sends = [copy(a, k) for k in range(1, N_DEV) for a in range(na)]

        def start():
            for cp in mine + sends:
                cp.start()

        def finish():
            for k in range(1, N_DEV):
                for a in range(na):
                    arrival(a, k).wait_recv()
            for cp in sends:
                cp.wait_send()
            for cp in mine:
                cp.wait()

        return start, finish

    return plan


def _exchange_out(slabs, bcast=()):
    return [jax.ShapeDtypeStruct(((N_DEV,) + s.shape) if a in bcast else s.shape, s.dtype)
            for a, s in enumerate(slabs)]


def grad_exchange(slabs, small):
    arrays = list(slabs) + [small]
    na = len(arrays)
    bcast = (na - 1,)

    def body(*refs):
        start, finish = _exchange_plan(bcast)(refs[:na], refs[na:2 * na], refs[2 * na:])
        start()
        finish()

    return _call(
        body,
        name="grad_exchange",
        out_shape=_exchange_out(arrays, bcast),
        in_specs=[ANY] * na,
        out_specs=[ANY] * na,
        scratch_shapes=_comm_sems(na),
    )(*arrays)


ADAM_BLOCK = 256 * 1024


def sum_adamw(recv, w, m, v, layer, prev, name):
    depth, rows, cols = w.shape
    tr = _tile(rows, max(8, ADAM_BLOCK // cols // 8 * 8), 8)
    c1 = 1.0 / (1.0 - ADAM_B1 ** ADAM_STEP)
    c2 = 1.0 / (1.0 - ADAM_B2 ** ADAM_STEP)

    def body(r_ref, w_ref, m_ref, v_ref, *rest):
        g_ref, d_ref, mo_ref, vo_ref = rest[-4:]
        g = r_ref[0].astype(F32)
        for s in range(1, N_DEV):
            g = g + r_ref[s].astype(F32)
        mn = ADAM_B1 * m_ref[0] + (1.0 - ADAM_B1) * g
        vn = ADAM_B2 * v_ref[0] + (1.0 - ADAM_B2) * (g * g)
        g_ref[0] = g
        mo_ref[0] = mn
        vo_ref[0] = vn
        d_ref[0] = -ADAM_LR * ((mn * c1) / (jnp.sqrt(vn * c2) + ADAM_EPS) + ADAM_WD * w_ref[0])

    blk = pl.BlockSpec((1, tr, cols), lambda i: (layer, i, 0))
    out = jax.ShapeDtypeStruct((depth, rows, cols), F32)
    in_specs = [pl.BlockSpec((N_DEV, tr, cols), lambda i: (0, i, 0)), blk, blk, blk]
    args = [recv, w, m, v]
    extra = {}
    if prev is not None:
        in_specs += [ANY] * 4
        args += list(prev)
        extra["input_output_aliases"] = {4 + j: j for j in range(4)}
    return _call(
        body,
        name=name,
        grid=(rows // tr,),
        in_specs=in_specs,
        out_specs=[blk, blk, blk, blk],
        out_shape=[out, out, out, out],
        compiler_params=_params(),
        **extra,
    )(*args)


def _attn_bias_vector(rel_bias):
    h = rel_bias.shape[0]
    n_far = BAND - MAX_REL
    n_near = BAND + CHUNK - 1 - n_far
    far = jnp.broadcast_to(rel_bias[:, 2 * MAX_REL:], (h, n_far))
    near = rel_bias[:, 2 * MAX_REL - n_near:2 * MAX_REL][:, ::-1]
    pad = jnp.zeros((h, 2 * QBLK - n_far - n_near), F32)
    return jnp.concatenate([far, near, pad], axis=1)[:, None, :]


def _s5_prepare(lre, lim, ldt, bre, bim, cre, cim):
    g, p = lre.shape
    lr = jnp.minimum(lre, -1e-4)
    dt = jnp.exp(ldt)[:, None]
    mag = jnp.exp(lr * dt)
    ar = mag * jnp.cos(lim * dt)
    ai = mag * jnp.sin(lim * dt)
    den = lr * lr + lim * lim
    coef_r = ((ar - 1.0) * lr + ai * lim) / den
    coef_i = (ai * lr - (ar - 1.0) * lim) / den
    bbar_r = coef_r[..., None] * bre - coef_i[..., None] * bim
    bbar_i = coef_r[..., None] * bim + coef_i[..., None] * bre
    eye = jnp.eye(g, dtype=F32)
    bd_in = lambda b: jnp.einsum("gpc,gh->gchp", b, eye).reshape(g * S5_GROUP, g * p)
    bd_out = lambda c: jnp.einsum("gcp,gh->gphc", c, eye).reshape(g * p, g * S5_GROUP)
    return (ar.reshape(1, g * p), ai.reshape(1, g * p), bd_in(bbar_r), bd_in(bbar_i), bd_out(cre), bd_out(cim))


SHARDED = ("ffn1_w_up", "ffn1_w_down", "w_in", "s5_w_glu", "w_br_s5", "w_br_attn", "conv_w_dw", "w_br_conv",
           "w_out", "ffn2_w_up", "ffn2_w_down")
WEIGHTS = ("ffn1_norm", "ffn1_w_up", "ffn1_w_down", "mix_norm", "w_in", "b_gate", "s5_lambda_re", "s5_lambda_im",
           "s5_log_dt", "s5_b_re", "s5_b_im", "s5_c_re", "s5_c_im", "s5_d", "s5_w_glu", "w_br_s5", "attn_q_gain",
           "attn_k_gain", "attn_rel_bias", "w_br_attn", "conv_w_dw", "conv_b_dw", "conv_ln_g", "conv_ln_b",
           "w_br_conv", "w_out", "ffn2_norm", "ffn2_w_up", "ffn2_w_down")
SMALL = tuple(nm for nm in WEIGHTS if nm not in SHARDED)
SMALL_LANES = 1024


def _cols_full(g):
    _, depth, k, nn = g.shape
    return g.transpose(1, 2, 0, 3).reshape(depth, k, N_DEV * nn)


def _rows_full(g):
    _, depth, r, cc = g.shape
    return g.transpose(1, 0, 2, 3).reshape(depth, N_DEV * r, cc)


def _cols_slabs(gfull):
    depth, k, c8 = gfull.shape
    return gfull.reshape(depth, k, N_DEV, c8 // N_DEV).transpose(0, 2, 1, 3)


def _small_pack(t):
    flat = jnp.concatenate([t[nm].reshape(-1) for nm in SMALL])
    rows = -(-flat.shape[0] // SMALL_LANES)
    rows = -(-rows // 8) * 8
    return jnp.pad(flat, (0, rows * SMALL_LANES - flat.shape[0])).reshape(rows, SMALL_LANES)


def _small_unpack(flat, like):
    flat = flat.reshape(-1)
    out, off = {}, 0
    for nm in SMALL:
        out[nm] = flat[off:off + like[nm].size].reshape(like[nm].shape)
        off += like[nm].size
    return out


def _step(x, target, w, m, v):
    bsz, seq, d = x.shape
    n = bsz * seq
    depth = w["ffn1_norm"].shape[0]
    da, dss, dc = d // 2, d // 4, d // 4
    gp =dss // S5_GROUP * S5_STATE
    mid = 3 * da + 2 * dc
    q0, k0, v0, z0, u0 = 3 * d, 3 * d + da, 3 * d + 2 * da, 3 * d + 3 * da, 3 * d + mid
    to_kernel_cols = lambda a: jnp.concatenate([a[..., dss + mid:], a[..., dss:dss + mid], a[..., :dss]], axis=-1)
    to_ref_cols = lambda a: jnp.concatenate([a[..., 3 * d + mid:], a[..., 3 * d:3 * d + mid], a[..., :3 * d]], axis=-1)

    def shard(key):
        nm, sl = (key[0], slice(key[1], key[1] + 1)) if isinstance(key, tuple) else (key, slice(None))
        return w[nm][sl] if nm == "conv_w_dw" else w[nm][sl].astype(BF16)

    mixer_names = ("w_in", "s5_w_glu", "w_br_s5", "w_br_attn", "w_br_conv", "conv_w_dw", "w_out")
    ffn1_of = lambda l: (("ffn1_w_up", l), ("ffn1_w_down", l))
    later_ffn1 = tuple(k for l in range(1, depth) for k in ffn1_of(l))
    gather_of = lambda keys: (_gather_plan, [shard(k) for k in keys], _gather_out([shard(k) for k in keys]))
    full = {}

    def take(keys, arrays):
        for key, g in zip(keys, arrays):
            nm = key[0] if isinstance(key, tuple) else key
            if nm in ("ffn1_w_up", "ffn2_w_up"):
                full[key] = g
            elif nm in ("ffn1_w_down", "ffn2_w_down", "w_out"):
                full[key] = _rows_full(g)
            elif nm == "w_in":
                full[key] = to_kernel_cols(_cols_full(g))
            else:
                full[key] = _cols_full(g)

    take(ffn1_of(0), all_gather([shard(k) for k in ffn1_of(0)]))
    nff = full[("ffn1_w_up", 0)].shape[3]

    row = lambda a: a.reshape(1, -1)
    saved = []
    xin = x.reshape(n, d)
    for l in range(depth):
        s = {"x0": xin}
        f1_up, f1_down = ffn1_of(l)
        (s["x1"], s["a1"], s["b1"]), got = ffn_fwd(xin, row(w["ffn1_norm"][l]), full[f1_up], full[f1_down], 0,
                                                   comm=gather_of(mixer_names) if l == 0 else None)
        if l == 0:
            take(mixer_names, got)
            conv_w = jnp.pad(full["conv_w_dw"], ((0, 0), (0, HALO - CONV_W), (0, 0)))
        (proj,), got = proj_fwd(s["x1"], row(w["mix_norm"][l]), full["w_in"], l,
                                comm=gather_of(later_ffn1) if l == 0 and later_ffn1 else None)
        if l == 0:
            take(later_ffn1, got)
        s["proj"] = proj
        prep_in = (w["s5_lambda_re"][l], w["s5_lambda_im"][l], w["s5_log_dt"][l], w["s5_b_re"][l], w["s5_b_im"][l],
                   w["s5_c_re"][l], w["s5_c_im"][l])
        (ar, ai, bm_r, bm_i, cm_r, cm_i), s["prep_vjp"] = jax.vjp(_s5_prepare, *prep_in)
        s["s5p"] = (ar, ai, bm_r.astype(BF16), bm_i.astype(BF16), cm_r.astype(BF16), cm_i.astype(BF16),
                    row(w["s5_d"][l]), full["s5_w_glu"], l)
        s["so"], s["xr"], s["xi"], s["ypre"] = s5_fwd(proj, u0 // dss, seq, *s["s5p"])
        tv, s["tv_vjp"] = jax.vjp(_attn_bias_vector, w["attn_rel_bias"][l])
        pair_gain = lambda g: jnp.tile(row(g), (1, 2))
        s["attnp"] = (q0, k0, v0, tv, pair_gain(w["attn_q_gain"][l]), pair_gain(w["attn_k_gain"][l]))
        (s["ao"],), got = attn_fwd(proj, *s["attnp"], seq,
                                   comm=gather_of(("ffn2_w_up", "ffn2_w_down")) if l == 0 else None)
        if l == 0:
            take(("ffn2_w_up", "ffn2_w_down"), got)
        s["convp"] = (conv_w[l], row(w["conv_b_dw"][l]), row(w["conv_ln_g"][l]), row(w["conv_ln_b"][l]))
        s["h1"], s["co"] = conv_fwd(proj, z0 // (2 * dc), seq, *s["convp"])
        s["mergep"] = (row(w["b_gate"][l]), s["so"], s["ao"], s["co"], full["w_br_s5"], full["w_br_attn"],
                       full["w_br_conv"], full["w_out"], l)
        s["x2"] = merge_fwd(s["x1"], proj, *s["mergep"])
        (xin, s["a2"], s["b2"]), _ = ffn_fwd(s["x2"], row(w["ffn2_norm"][l]), full["ffn2_w_up"], full["ffn2_w_down"], l)
        saved.append(s)

    dx, loss = loss_head(xin, target.reshape(n, d))
    loss = lax.psum(loss[0, 0], ("x", "y", "c"))

    small_g = {nm: [None] * depth for nm in SMALL}
    slab = {}

    recv = {}

    def exchange_of(names, l):
        arrays = [slab.pop(nm) for nm in names]
        return [(nm, l) for nm in names], (_exchange_plan(), arrays, _exchange_out(arrays))

    def ffn_grads(which, l, hn, dyb, dab_a, dab_b, act, send_up_now=False):
        up, down = which + "_w_up", which + "_w_down"
        up4, down4 = (1, N_DEV, d, nff), (1, FF_CHUNKS, nff, d)
        half = wgrad(hn, dab_a, "wg_ffn_up", up4, "col")
        slab[up] = wgrad(hn, dab_b, "wg_ffn_up", up4, "col", g0=FF_CHUNKS, prev=half).reshape(up4[1:])
        if send_up_now:
            keys, comm = exchange_of((up,), l)
            (dn,), got = wgrad(act, dyb, "wg_ffn_down", down4, "row", comm=comm)
            recv.update(zip(keys, got))
        else:
            dn = wgrad(act, dyb, "wg_ffn_down", down4, "row")
        slab[down] = dn.reshape(N_DEV, nff // 2, d)

    pending = None
    for l in reversed(range(depth)):
        s = saved[l]
        outs, got = ffn_bwd(dx, s["x2"], row(w["ffn2_norm"][l]), s["a2"], s["b2"],
                            full["ffn2_w_up"], full["ffn2_w_down"], l, comm=pending[1] if pending else None)
        if pending:
            recv.update(zip(pending[0], got))
        dx, dg, hn, dyb, dab_a, dab_b, act = outs
        small_g["ffn2_norm"][l] = dg
        ffn_grads("ffn2", l, hn, dyb, dab_a, dab_b, act)

        dlog, dso, dao, dco, dbg, mg, dxb, dys, dya, dyc = merge_bwd(dx, s["proj"], *s["mergep"])
        small_g["b_gate"][l] = dbg
        slab["w_out"] = wgrad(mg, dxb, "wg_out", (1, N_DEV, d // N_DEV, d), "row").reshape(N_DEV, d // N_DEV, d)
        for nm, act_in, dy_br in (("w_br_s5", s["so"], dys), ("w_br_attn", s["ao"], dya), ("w_br_conv", s["co"], dyc)):
            k_in = act_in.shape[1]
            slab[nm] = wgrad(act_in, dy_br, "wg_" + nm, (1, N_DEV, k_in, d // N_DEV), "col").reshape(
                N_DEV, k_in, d // N_DEV)

        dz, dwdw, dbdw, dlng, dlnb = conv_bwd(dco, s["h1"], s["proj"], z0 // (2 * dc), seq, *s["convp"])
        slab["conv_w_dw"] = _cols_slabs(dwdw[None, :CONV_W])[0].astype(BF16)
        small_g["conv_b_dw"][l], small_g["conv_ln_g"][l], small_g["conv_ln_b"][l] = dbdw, dlng, dlnb

        keys, comm = exchange_of(("ffn2_w_up", "ffn2_w_down"), l)
        outs, got = attn_bwd(dao, s["proj"], *s["attnp"], seq, comm=comm)
        recv.update(zip(keys, got))
        dq, dk, dv, dtv, dgq, dgk = outs
        small_g["attn_q_gain"][l] = dgq[:, :HEAD] + dgq[:, HEAD:]
        small_g["attn_k_gain"][l] = dgk[:, :HEAD] + dgk[:, HEAD:]
        small_g["attn_rel_bias"][l] = s["tv_vjp"](dtv)[0]

        du, gr, gi, dyb5, glb, dzb, dar, dai, dd = s5_bwd(dso, s["ypre"], s["proj"], u0 // dss, s["xr"], s["xi"],
                                                          seq, *s["s5p"])
        small_g["s5_d"][l] = dd
        one = lambda k1, k2: (1, 1, k1, k2)
        slab["s5_w_glu"] = _cols_slabs(wgrad(glb, dzb, "wg_s5_glu", one(dss, 2 * dss), "col")[0])[0]
        dcm_r = wgrad(s["xr"], dyb5, "wg_s5_c", one(gp, dss), "col", dtype=F32)[0, 0]
        dcm_i = -wgrad(s["xi"], dyb5, "wg_s5_c", one(gp, dss), "col", dtype=F32)[0, 0]
        dbm_r = wgrad(s["proj"], gr, "wg_s5_b", one(dss, gp), "col", a_cols=(u0, dss), dtype=F32)[0, 0]
        dbm_i = wgrad(s["proj"], gi, "wg_s5_b", one(dss, gp), "col", a_cols=(u0, dss), dtype=F32)[0, 0]
        pg = s["prep_vjp"]((dar, dai, dbm_r, dbm_i, dcm_r, dcm_i))
        for nm, gval in zip(("s5_lambda_re", "s5_lambda_im", "s5_log_dt", "s5_b_re", "s5_b_im", "s5_c_re", "s5_c_im"), pg):
            small_g[nm][l] = gval

        dproj = jnp.concatenate([dlog, dq.astype(BF16), dk.astype(BF16), dv.astype(BF16),
                                 dz.astype(BF16), du.astype(BF16)], axis=1)
        dx, dgm, hn = proj_bwd(dproj, dx, s["x1"], row(w["mix_norm"][l]), full["w_in"], l)
        small_g["mix_norm"][l] = dgm
        slab["w_in"] = _cols_slabs(to_ref_cols(wgrad(hn, dproj, "wg_in", one(d, 3 * d + mid + dss), "col")[0]))[0]

        keys, comm = exchange_of(mixer_names, l)
        f1_up, f1_down = ffn1_of(l)
        outs, got = ffn_bwd(dx, s["x0"], row(w["ffn1_norm"][l]), s["a1"], s["b1"],
                            full[f1_up], full[f1_down], 0, comm=comm)
        recv.update(zip(keys, got))
        dx, dg, hn, dyb, dab_a, dab_b, act = outs
        small_g["ffn1_norm"][l] = dg
        ffn_grads("ffn1", l, hn, dyb, dab_a, dab_b, act, send_up_now=(l == 0))
        pending = exchange_of(("ffn1_w_down",) if l == 0 else ("ffn1_w_up", "ffn1_w_down"), l)

    small_flat = _small_pack({nm: jnp.stack([g.reshape(w[nm].shape[1:]) for g in small_g[nm]]) for nm in SMALL})
    *got, recv_small = grad_exchange(pending[1][1], small_flat)
    recv.update(zip(pending[0], got))

    outs = {}
    for nm in SHARDED:
        shp = w[nm].shape
        as3 = lambda t: t.reshape(shp[0], -1, shp[-1])
        bufs = None
        for l in reversed(range(depth)):
            bufs = sum_adamw(recv[(nm, l)], as3(w[nm]), as3(m[nm]), as3(v[nm]), l, bufs, "adamw_" + nm)
        outs[nm] = [b.reshape(shp) for b in bufs]
    packed = sum_adamw(recv_small, _small_pack(w)[None], _small_pack(m)[None], _small_pack(v)[None], 0, None,
                       "adamw_small")
    unpacked = [_small_unpack(p, w) for p in packed]
    for nm in SMALL:
        outs[nm] = [u[nm] for u in unpacked]
    return loss, dx.reshape(x.shape), outs


def kernel(x, ffn1_norm, ffn1_w_up, ffn1_w_down, mix_norm, w_in, b_gate, s5_lambda_re, s5_lambda_im, s5_log_dt, s5_b_re, s5_b_im, s5_c_re, s5_c_im, s5_d, s5_w_glu, w_br_s5, attn_q_gain, attn_k_gain, attn_rel_bias, w_br_attn, conv_w_dw, conv_b_dw, conv_ln_g, conv_ln_b, w_br_conv, w_out, ffn2_norm, ffn2_w_up, ffn2_w_down, loss_target, m_ffn1_norm, m_ffn1_w_up, m_ffn1_w_down, m_mix_norm, m_w_in, m_b_gate, m_s5_lambda_re, m_s5_lambda_im, m_s5_log_dt, m_s5_b_re, m_s5_b_im, m_s5_c_re, m_s5_c_im, m_s5_d, m_s5_w_glu, m_w_br_s5, m_attn_q_gain, m_attn_k_gain, m_attn_rel_bias, m_w_br_attn, m_conv_w_dw, m_conv_b_dw, m_conv_ln_g, m_conv_ln_b, m_w_br_conv, m_w_out, m_ffn2_norm, m_ffn2_w_up, m_ffn2_w_down, v_ffn1_norm, v_ffn1_w_up, v_ffn1_w_down, v_mix_norm, v_w_in, v_b_gate, v_s5_lambda_re, v_s5_lambda_im, v_s5_log_dt, v_s5_b_re, v_s5_b_im, v_s5_c_re, v_s5_c_im, v_s5_d, v_s5_w_glu, v_w_br_s5, v_attn_q_gain, v_attn_k_gain, v_attn_rel_bias, v_w_br_attn, v_conv_w_dw, v_conv_b_dw, v_conv_ln_g, v_conv_ln_b, v_w_br_conv, v_w_out, v_ffn2_norm, v_ffn2_w_up, v_ffn2_w_down):
    args = locals()
    w = {nm: args[nm] for nm in WEIGHTS}
    m = {nm: args["m_" + nm] for nm in WEIGHTS}
    v = {nm: args["v_" + nm] for nm in WEIGHTS}
    loss, gx, outs = _step(x, loss_target, w, m, v)
    return (loss, gx, *[outs[nm][0] for nm in WEIGHTS], *[outs[nm][1] for nm in WEIGHTS],
            *[outs[nm][2] for nm in WEIGHTS], *[outs[nm][3] for nm in WEIGHTS])
```

```python
import functools
import math

import numpy as np
import jax
import jax.numpy as jnp
from jax import lax
from jax.experimental import pallas as pl
from jax.experimental.pallas import tpu as pltpu

F32 = jnp.float32
BF16 = jnp.bfloat16

CHUNK = 64
N_LEFT = 8
QBLK = CHUNK * N_LEFT
HEAD = 64
MAX_REL = 128
S5_GROUP = 16
S5_STATE = 64
CONV_W = 31
HALO = 32
EPS = 1e-6
NEG = -1e30
ADAM_LR, ADAM_B1, ADAM_B2, ADAM_EPS, ADAM_WD, ADAM_STEP = 0.001, 0.9, 0.999, 1e-08, 0.01, 10
N_DEV = 8
VMEM_LIMIT = 56 * 1024 * 1024


def _call(body, **kw):
    return pl.pallas_call(body, **kw)


def _params(**kw):
    return pltpu.CompilerParams(vmem_limit_bytes=VMEM_LIMIT, **kw)


def _tile(n, cap, unit=128):
    if n <= cap:
        return n
    d = (cap // unit) * unit
    while d >= unit:
        if n % d == 0:
            return d
        d -= unit
    raise ValueError(f"no tile for {n} under {cap}")


def _dot(a, b):
    return jnp.dot(a, b, preferred_element_type=F32)


def _dot_nt(a, b):
    return lax.dot_general(a, b, (((1,), (1,)), ((), ())), preferred_element_type=F32)


def _dot_tn(a, b):
    return lax.dot_general(a, b, (((0,), (0,)), ((), ())), preferred_element_type=F32)


def _sig(x):
    return 1.0 / (1.0 + jnp.exp(-x))


def _rms_fwd(x, g):
    rs = lax.rsqrt(jnp.mean(x * x, axis=-1, keepdims=True) + EPS)
    xhat = x * rs
    return xhat * g, xhat, rs


def _rms_bwd(dh, xhat, rs, g):
    dxh = dh * g
    dx = rs * (dxh - xhat * jnp.mean(dxh * xhat, axis=-1, keepdims=True))
    return dx, dh * xhat


_GELU_C = math.sqrt(2.0 / math.pi)


def _gelu(x):
    return 0.5 * x * (1.0 + jnp.tanh(_GELU_C * (x + 0.044715 * x * x * x)))


def _gelu_grad(x):
    t = jnp.tanh(_GELU_C * (x + 0.044715 * x * x * x))
    return 0.5 * (1.0 + t) + 0.5 * x * (1.0 - t * t) * _GELU_C * (1.0 + 3.0 * 0.044715 * x * x)


def _acc_out(ref, val, first):
    @pl.when(first)
    def _():
        ref[...] = val

    @pl.when(jnp.logical_not(first))
    def _():
        ref[...] += val


FF_CHUNKS = N_DEV // 2


def ffn_fwd(x, g, w_up, w_down, layer, comm=None):
    n, d = x.shape
    nn = w_up.shape[3]
    tm = _tile(n, 1024, 8)

    def body(x_ref, g_ref, wa_ref, wb_ref, wd_ref, xo_ref, a_ref, b_ref, hn_ref, acc_ref):
        j = pl.program_id(1)

        @pl.when(j == 0)
        def _():
            h, _, _ = _rms_fwd(x_ref[...], g_ref[...])
            hn_ref[...] = h.astype(BF16)
            acc_ref[...] = jnp.zeros_like(acc_ref)

        hn = hn_ref[...]
        a = _dot(hn, wa_ref[0, 0])
        b = _dot(hn, wb_ref[0, 0])
        a_ref[0] = a.astype(BF16)
        b_ref[0] = b.astype(BF16)
        act = a * _sig(a) * b
        acc_ref[...] += _dot(act.astype(BF16), wd_ref[0])

        @pl.when(j == FF_CHUNKS - 1)
        def _():
            xo_ref[...] = x_ref[...] + 0.5 * acc_ref[...]

    return _hosted_call(
        body, comm,
        name="ffn_fwd",
        grid=(n // tm, FF_CHUNKS),
        in_specs=[
            pl.BlockSpec((tm, d), lambda i, j: (i, 0)),
            pl.BlockSpec((1, d), lambda i, j: (0, 0)),
            pl.BlockSpec((1, 1, d, nn), lambda i, j: (j, layer, 0, 0)),
            pl.BlockSpec((1, 1, d, nn), lambda i, j: (j + FF_CHUNKS, layer, 0, 0)),
            pl.BlockSpec((1, nn, d), lambda i, j: (layer, j, 0)),
        ],
        out_specs=[
            pl.BlockSpec((tm, d), lambda i, j: (i, 0)),
            pl.BlockSpec((1, tm, nn), lambda i, j: (j, i, 0)),
            pl.BlockSpec((1, tm, nn), lambda i, j: (j, i, 0)),
        ],
        out_shape=[
            jax.ShapeDtypeStruct((n, d), F32),
            jax.ShapeDtypeStruct((FF_CHUNKS, n, nn), BF16),
            jax.ShapeDtypeStruct((FF_CHUNKS, n, nn), BF16),
        ],
        scratch_shapes=[pltpu.VMEM((tm, d), BF16), pltpu.VMEM((tm, d), F32)],
        args=(x, g, w_up, w_up, w_down),
    )


def ffn_bwd(dy, x, g, a, b, w_up, w_down, layer, comm=None):
    n, d = x.shape
    nn = w_up.shape[3]
    tm = _tile(n, 512, 8)

    def body(dy_ref, x_ref, g_ref, a_ref, b_ref, wa_ref, wb_ref, wd_ref,
             dx_ref, dg_ref, hn_ref, dyb_ref, da_ref, db_ref, act_ref, dyb_s, dh_ref):
        i, j = pl.program_id(0), pl.program_id(1)

        @pl.when(j == 0)
        def _():
            h, _, _ = _rms_fwd(x_ref[...], g_ref[...])
            hn_ref[...] = h.astype(BF16)
            dyb = (0.5 * dy_ref[...]).astype(BF16)
            dyb_ref[...] = dyb
            dyb_s[...] = dyb
            dh_ref[...] = jnp.zeros_like(dh_ref)

        dact = _dot_nt(dyb_s[...], wd_ref[0])
        a32 = a_ref[0].astype(F32)
        b32 = b_ref[0].astype(F32)
        s = _sig(a32)
        sil = a32 * s
        da = (dact * b32 * (s * (1.0 + a32 * (1.0 - s)))).astype(BF16)
        db = (dact * sil).astype(BF16)
        da_ref[0] = da
        db_ref[0] = db
        act_ref[0] = (sil * b32).astype(BF16)
        dh_ref[...] += _dot_nt(da, wa_ref[0, 0]) + _dot_nt(db, wb_ref[0, 0])

        @pl.when(j == FF_CHUNKS - 1)
        def _():
            gg = g_ref[...]
            _, xhat, rs = _rms_fwd(x_ref[...], gg)
            dxn, dgr = _rms_bwd(dh_ref[...], xhat, rs, gg)
            dx_ref[...] = dy_ref[...] + dxn
            _acc_out(dg_ref, jnp.sum(dgr, axis=0, keepdims=True), i == 0)

    tok = pl.BlockSpec((tm, d), lambda i, j: (i, 0))
    chunk = pl.BlockSpec((1, tm, nn), lambda i, j: (j, i, 0))
    vec = pl.BlockSpec((1, d), lambda i, j: (0, 0))
    chunks = jax.ShapeDtypeStruct((FF_CHUNKS, n, nn), BF16)
    return _hosted_call(
        body, comm,
        name="ffn_bwd",
        grid=(n // tm, FF_CHUNKS),
        in_specs=[
            tok, tok, vec, chunk, chunk,
            pl.BlockSpec((1, 1, d, nn), lambda i, j: (j, layer, 0, 0)),
            pl.BlockSpec((1, 1, d, nn), lambda i, j: (j + FF_CHUNKS, layer, 0, 0)),
            pl.BlockSpec((1, nn, d), lambda i, j: (layer, j, 0)),
        ],
        out_specs=[tok, vec, tok, tok, chunk, chunk, chunk],
        out_shape=[
            jax.ShapeDtypeStruct((n, d), F32),
            jax.ShapeDtypeStruct((1, d), F32),
            jax.ShapeDtypeStruct((n, d), BF16),
            jax.ShapeDtypeStruct((n, d), BF16),
            chunks, chunks, chunks,
        ],
        scratch_shapes=[pltpu.VMEM((tm, d), BF16), pltpu.VMEM((tm, d), F32)],
        args=(dy, x, g, a, b, w_up, w_up, w_down),
    )


def wgrad(a, b, name, out4, mode, *, g0=0, layer=0, prev=None, a_cols=None, dtype=BF16, comm=None):
    a3 = a if a.ndim == 3 else a[None]
    b3 = b if b.ndim == 3 else b[None]
    sa, n, ka = a3.shape
    sb, _, kb = b3.shape
    a0 = 0
    if a_cols is not None:
        a0, ka = a_cols
    k1, k2 = sa * ka, sb * kb
    depth, groups, rr, cc = out4
    t1 = _tile(math.gcd(ka, rr), 1024)
    t2 = _tile(math.gcd(kb, cc), 1024)
    tn = _tile(n, 1024, 8)
    gpb = 1
    if mode == "col":
        assert rr == k1 and k2 % cc == 0 and g0 + k2 // cc <= groups
        if kb % cc == 0 and cc < kb <= 1024 and g0 % (kb // cc) == 0:
            t2, gpb = kb, kb // cc
        per = max(cc // t2, 1)
        oblock = (1, gpb, t1, min(t2, cc))
        omap = lambda i, j, k: (layer, (g0 + j // per) // gpb, i, j % per)
    else:
        assert cc == k2 and k1 % rr == 0 and g0 + k1 // rr <= groups
        if ka % rr == 0 and rr < ka <= 1024 and g0 % (ka // rr) == 0:
            t1, gpb = ka, ka // rr
        per = max(rr // t1, 1)
        oblock = (1, gpb, min(t1, rr), t2)
        omap = lambda i, j, k: (layer, (g0 + i // per) // gpb, i % per, j)
    na, nb = ka // t1, kb // t2
    nk = n // tn

    def body(a_ref, b_ref, *rest):
        o_ref, acc_ref = rest[-2], rest[-1]
        k = pl.program_id(2)

        @pl.when(k == 0)
        def _():
            acc_ref[...] = jnp.zeros_like(acc_ref)

        acc_ref[...] += _dot_tn(a_ref[0].astype(BF16), b_ref[0].astype(BF16))

        @pl.when(k == nk - 1)
        def _():
            for g in range(gpb):
                if gpb == 1:
                    o_ref[0, 0] = acc_ref[...].astype(dtype)
                elif mode == "col":
                    o_ref[0, g] = acc_ref[:, g * cc:(g + 1) * cc].astype(dtype)
                else:
                    o_ref[0, g] = acc_ref[g * rr:(g + 1) * rr, :].astype(dtype)

    in_specs = [
        pl.BlockSpec((1, tn, t1), lambda i, j, k: (i // na, k, a0 // t1 + i % na)),
        pl.BlockSpec((1, tn, t2), lambda i, j, k: (j // nb, k, j % nb)),
    ]
    args = [a3, b3]
    if comm is not None:
        assert prev is None
        return _hosted_call(body, comm, name=name, grid=(k1 // t1, k2 // t2, nk), in_specs=in_specs,
                            out_specs=[pl.BlockSpec(oblock, omap)], out_shape=[jax.ShapeDtypeStruct(out4, dtype)],
                            scratch_shapes=[pltpu.VMEM((t1, t2), F32)], args=args)
    extra = {}
    if prev is not None:
        in_specs.append(pl.BlockSpec(memory_space=pl.ANY))
        args.append(prev)
        extra["input_output_aliases"] = {2: 0}
    return _call(
        body,
        name=name,
        grid=(k1 // t1, k2 // t2, nk),
        in_specs=in_specs,
        out_specs=pl.BlockSpec(oblock, omap),
        out_shape=jax.ShapeDtypeStruct(out4, dtype),
        scratch_shapes=[pltpu.VMEM((t1, t2), F32)],
        compiler_params=_params(),
        **extra,
    )(*args)


def proj_fwd(x, g, w, layer, comm=None):
    n, d = x.shape
    c = w.shape[2]
    tm, tc = _tile(n, 1024, 8), _tile(c, 768)

    def body(x_ref, g_ref, w_ref, o_ref, hn_ref):
        @pl.when(pl.program_id(1) == 0)
        def _():
            h, _, _ = _rms_fwd(x_ref[...], g_ref[...])
            hn_ref[...] = h.astype(BF16)

        o_ref[...] = _dot(hn_ref[...], w_ref[0])

    return _hosted_call(
        body, comm,
        name="proj_fwd",
        grid=(n // tm, c // tc),
        in_specs=[
            pl.BlockSpec((tm, d), lambda i, j: (i, 0)),
            pl.BlockSpec((1, d), lambda i, j: (0, 0)),
            pl.BlockSpec((1, d, tc), lambda i, j: (layer, 0, j)),
        ],
        out_specs=[pl.BlockSpec((tm, tc), lambda i, j: (i, j))],
        out_shape=[jax.ShapeDtypeStruct((n, c), F32)],
        scratch_shapes=[pltpu.VMEM((tm, d), BF16)],
        args=(x, g, w),
    )


def proj_bwd(dproj, dres, x, g, w, layer):
    n, d = x.shape
    c = w.shape[2]
    tm = _tile(n, 512, 8)

    def body(dp_ref, dr_ref, x_ref, g_ref, w_ref, dx_ref, dg_ref, hn_ref):
        dh = _dot_nt(dp_ref[...], w_ref[0])
        gg = g_ref[...]
        h, xhat, rs = _rms_fwd(x_ref[...], gg)
        hn_ref[...] = h.astype(BF16)
        dxn, dgr = _rms_bwd(dh, xhat, rs, gg)
        dx_ref[...] = dr_ref[...] + dxn
        _acc_out(dg_ref, jnp.sum(dgr, axis=0, keepdims=True), pl.program_id(0) == 0)

    tok = pl.BlockSpec((tm, d), lambda i: (i, 0))
    vec = pl.BlockSpec((1, d), lambda i: (0, 0))
    return _call(
        body,
        name="proj_bwd",
        grid=(n // tm,),
        in_specs=[
            pl.BlockSpec((tm, c), lambda i: (i, 0)), tok, tok, vec,
            pl.BlockSpec((1, d, c), lambda i: (layer, 0, 0), pipeline_mode=pl.Buffered(1)),
        ],
        out_specs=[tok, vec, tok],
        out_shape=[
            jax.ShapeDtypeStruct((n, d), F32),
            jax.ShapeDtypeStruct((1, d), F32),
            jax.ShapeDtypeStruct((n, d), BF16),
        ],
        compiler_params=_params(),
    )(dproj, dres, x, g, w)


S5_TS = 512


def _cmul(ar, ai, br, bi):
    return ar * br - ai * bi, ar * bi + ai * br


def _s5_tables(ar, ai, reverse):
    gp = ar.shape[1]
    if reverse:
        ai = -ai
    a1r, a1i = jnp.broadcast_to(ar, (8, gp)), jnp.broadcast_to(ai, (8, gp))
    a2r, a2i = _cmul(a1r, a1i, a1r, a1i)
    a4r, a4i = _cmul(a2r, a2i, a2r, a2i)
    a8r, a8i = _cmul(a4r, a4i, a4r, a4i)
    row = lax.broadcasted_iota(jnp.int32, (8, gp), 0)
    e = (8 - row) if reverse else (row + 1)
    pr, pi = jnp.ones((8, gp), F32), jnp.zeros((8, gp), F32)
    for bit, (fr, fi) in ((1, (a1r, a1i)), (2, (a2r, a2i)), (4, (a4r, a4i)), (8, (a8r, a8i))):
        nr, ni = _cmul(pr, pi, fr, fi)
        on = (e & bit) != 0
        pr, pi = jnp.where(on, nr, pr), jnp.where(on, ni, pi)
    return (a1r, a1i, a2r, a2i, a4r, a4i, pr, pi)


def _s5_scan(xr_ref, xi_ref, tab_ref, cr_ref, ci_ref, ts, reverse):
    gp = xr_ref.shape[1]
    nt = ts // 8
    row = lax.broadcasted_iota(jnp.int32, (8, gp), 0)

    def shifted(v, s):
        if reverse:
            return jnp.where(row < 8 - s, pltpu.roll(v, 8 - s, 0), 0.0)
        return jnp.where(row >= s, pltpu.roll(v, s, 0), 0.0)

    def step(k, carry):
        cr, ci = carry
        t = (nt - 1 - k) if reverse else k
        r0 = pl.multiple_of(t * 8, 8)
        br = xr_ref[pl.ds(r0, 8), :]
        bi = xi_ref[pl.ds(r0, 8), :]
        for q, s in enumerate((1, 2, 4)):
            fr, fi = tab_ref[2 * q], tab_ref[2 * q + 1]
            sr, si = shifted(br, s), shifted(bi, s)
            mr, mi = _cmul(fr, fi, sr, si)
            br, bi = br + mr, bi + mi
        mr, mi = _cmul(tab_ref[6], tab_ref[7], cr, ci)
        br, bi = br + mr, bi + mi
        xr_ref[pl.ds(r0, 8), :] = br
        xi_ref[pl.ds(r0, 8), :] = bi
        edge = 0 if reverse else 7
        return (jnp.broadcast_to(br[edge:edge + 1, :], (8, gp)),
                jnp.broadcast_to(bi[edge:edge + 1, :], (8, gp)))

    cr, ci = lax.fori_loop(0, nt, step, (cr_ref[...], ci_ref[...]), unroll=2)
    cr_ref[...] = cr
    ci_ref[...] = ci


def s5_fwd(proj, ucol, seq, ar, ai, bm_r, bm_i, cm_r, cm_i, dskip, w_glu, layer):
    n = proj.shape[0]
    ds, gp = bm_r.shape
    ts = min(S5_TS, seq)
    nt = seq // ts

    def body(u_ref, ar_ref, ai_ref, bmr_ref, bmi_ref, cmr_ref, cmi_ref, d_ref, wg_ref,
             out_ref, xr_ref, xi_ref, yp_ref, tab_ref, cr_ref, ci_ref):
        @pl.when(pl.program_id(1) == 0)
        def _():
            for q, v in enumerate(_s5_tables(ar_ref[...], ai_ref[...], False)):
                tab_ref[q] = v
            cr_ref[...] = jnp.zeros_like(cr_ref)
            ci_ref[...] = jnp.zeros_like(ci_ref)

        u = u_ref[...]
        ub = u.astype(BF16)
        xr_ref[...] = _dot(ub, bmr_ref[...])
        xi_ref[...] = _dot(ub, bmi_ref[...])
        _s5_scan(xr_ref, xi_ref, tab_ref, cr_ref, ci_ref, ts, False)
        y = (_dot(xr_ref[...].astype(BF16), cmr_ref[...]) - _dot(xi_ref[...].astype(BF16), cmi_ref[...])
             + d_ref[...] * u)
        yp_ref[...] = y
        z = _dot(_gelu(y).astype(BF16), wg_ref[0])
        out_ref[...] = z[:, :ds] * _sig(z[:, ds:])

    full = lambda shape: pl.BlockSpec(shape, lambda b, t: (0, 0))
    return _call(
        body,
        name="s5_fwd",
        grid=(n // seq, nt),
        in_specs=[
            pl.BlockSpec((ts, ds), lambda b, t: (b * nt + t, ucol)),
            full((1, gp)), full((1, gp)), full((ds, gp)), full((ds, gp)), full((gp, ds)), full((gp, ds)),
            full((1, ds)), pl.BlockSpec((1, ds, 2 * ds), lambda b, t: (layer, 0, 0)),
        ],
        out_specs=[
            pl.BlockSpec((ts, ds), lambda b, t: (b * nt + t, 0)),
            pl.BlockSpec((ts, gp), lambda b, t: (b * nt + t, 0)),
            pl.BlockSpec((ts, gp), lambda b, t: (b * nt + t, 0)),
            pl.BlockSpec((ts, ds), lambda b, t: (b * nt + t, 0)),
        ],
        out_shape=[
            jax.ShapeDtypeStruct((n, ds), F32),
            jax.ShapeDtypeStruct((n, gp), F32),
            jax.ShapeDtypeStruct((n, gp), F32),
            jax.ShapeDtypeStruct((n, ds), F32),
        ],
        scratch_shapes=[pltpu.VMEM((8, 8, gp), F32), pltpu.VMEM((8, gp), F32), pltpu.VMEM((8, gp), F32)],
        compiler_params=_params(),
    )(proj, ar, ai, bm_r, bm_i, cm_r, cm_i, dskip, w_glu)


def s5_bwd(dout, ypre, proj, ucol, xr, xi, seq, ar, ai, bm_r, bm_i, cm_r, cm_i, dskip, w_glu, layer):
    n = proj.shape[0]
    ds, gp = bm_r.shape
    ts = min(S5_TS, seq)
    nt = seq // ts

    def body(do_ref, yp_ref, u_ref, xr_ref, xi_ref, hr_ref, hi_ref, ar_ref, ai_ref, bmr_ref, bmi_ref,
             cmr_ref, cmi_ref, d_ref, wg_ref,
             du_ref, gr_ref, gi_ref, dyb_ref, glb_ref, dzb_ref, dar_ref, dai_ref, dd_ref,
             tab_ref, cr_ref, ci_ref):
        b, t = pl.program_id(0), pl.program_id(1)
        first = jnp.logical_and(b == 0, t == 0)

        @pl.when(t == 0)
        def _():
            for q, v in enumerate(_s5_tables(ar_ref[...], ai_ref[...], True)):
                tab_ref[q] = v
            cr_ref[...] = jnp.zeros_like(cr_ref)
            ci_ref[...] = jnp.zeros_like(ci_ref)

        yp = yp_ref[...]
        u = u_ref[...]
        gl = _gelu(yp).astype(BF16)
        glb_ref[...] = gl
        z = _dot(gl, wg_ref[0])
        za, sg = z[:, :ds], _sig(z[:, ds:])
        do = do_ref[...]
        da = (do * sg).astype(BF16)
        dg = (do * za * sg * (1.0 - sg)).astype(BF16)
        dzb_ref[:, :ds] = da
        dzb_ref[:, ds:] = dg
        dgl = _dot_nt(da, wg_ref[0, :, :ds]) + _dot_nt(dg, wg_ref[0, :, ds:])
        dyp = dgl * _gelu_grad(yp)
        dypb = dyp.astype(BF16)
        dyb_ref[...] = dypb
        _acc_out(dd_ref, jnp.sum(dyp * u, axis=0, keepdims=True), first)

        gr_ref[...] = _dot_nt(dypb, cmr_ref[...])
        gi_ref[...] = -_dot_nt(dypb, cmi_ref[...])
        _s5_scan(gr_ref, gi_ref, tab_ref, cr_ref, ci_ref, ts, True)
        gr, gi = gr_ref[...], gi_ref[...]
        du_ref[...] = d_ref[...] * dyp + _dot_nt(gr.astype(BF16), bmr_ref[...]) + _dot_nt(gi.astype(BF16), bmi_ref[...])

        row = lax.broadcasted_iota(jnp.int32, (ts, gp), 0)
        live = jnp.where(t == nt - 1, 0.0, 1.0)
        pr = jnp.broadcast_to(hr_ref[7:8, :] * live, (ts, gp))
        pi = jnp.broadcast_to(hi_ref[7:8, :] * live, (ts, gp))
        sr = jnp.where(row == 0, pr, pltpu.roll(xr_ref[...], 1, 0))
        si = jnp.where(row == 0, pi, pltpu.roll(xi_ref[...], 1, 0))
        _acc_out(dar_ref, jnp.sum(gr * sr + gi * si, axis=0, keepdims=True), first)
        _acc_out(dai_ref, jnp.sum(gi * sr - gr * si, axis=0, keepdims=True), first)

    full = lambda shape: pl.BlockSpec(shape, lambda b, t: (0, 0))
    blk = lambda w, col=0: pl.BlockSpec((ts, w), lambda b, t: (b * nt + nt - 1 - t, col))
    halo = pl.BlockSpec((8, gp), lambda b, t: (jnp.maximum((b * seq + (nt - 1 - t) * ts) // 8 - 1, 0), 0))
    return _call(
        body,
        name="s5_bwd",
        grid=(n // seq, nt),
        in_specs=[
            blk(ds), blk(ds), blk(ds, ucol), blk(gp), blk(gp), halo, halo,
            full((1, gp)), full((1, gp)), full((ds, gp)), full((ds, gp)), full((gp, ds)), full((gp, ds)),
            full((1, ds)), pl.BlockSpec((1, ds, 2 * ds), lambda b, t: (layer, 0, 0)),
        ],
        out_specs=[
            blk(ds), blk(gp), blk(gp), blk(ds), blk(ds), blk(2 * ds),
            full((1, gp)), full((1, gp)), full((1, ds)),
        ],
        out_shape=[
            jax.ShapeDtypeStruct((n, ds), F32),
            jax.ShapeDtypeStruct((n, gp), F32),
            jax.ShapeDtypeStruct((n, gp), F32),
            jax.ShapeDtypeStruct((n, ds), BF16),
            jax.ShapeDtypeStruct((n, ds), BF16),
            jax.ShapeDtypeStruct((n, 2 * ds), BF16),
            jax.ShapeDtypeStruct((1, gp), F32),
            jax.ShapeDtypeStruct((1, gp), F32),
            jax.ShapeDtypeStruct((1, ds), F32),
        ],
        scratch_shapes=[pltpu.VMEM((8, 8, gp), F32), pltpu.VMEM((8, gp), F32), pltpu.VMEM((8, gp), F32)],
        compiler_params=_params(),
    )(dout, ypre, proj, xr, xi, xr, xi, ar, ai, bm_r, bm_i, cm_r, cm_i, dskip, w_glu)


BAND = QBLK + CHUNK
NCH = QBLK // CHUNK


PAIR = 2 * HEAD


def _attn_specs(nq):
    last = nq - 1
    cur = lambda col0=0: pl.BlockSpec(
        (QBLK, PAIR), lambda p, b, i: (b * nq + jnp.minimum(i, last), col0 // PAIR + p))
    prev = lambda col0=0: pl.BlockSpec(
        (QBLK, PAIR), lambda p, b, i: (b * nq + jnp.maximum(jnp.minimum(i, last) - 1, 0), col0 // PAIR + p))
    vec = pl.BlockSpec((2, 1, 2 * QBLK), lambda p, b, i: (p, 0, 0))
    gain = pl.BlockSpec((1, PAIR), lambda p, b, i: (0, 0))
    return cur, prev, vec, gain


def _pair_masks():
    lane = lax.broadcasted_iota(jnp.int32, (1, PAIR), 1)
    return lane < HEAD, [(lane < HEAD).astype(F32), (lane >= HEAD).astype(F32)]


def _pair_mean(t, low, m0):
    s0 = jnp.sum(t * m0, axis=-1, keepdims=True)
    s1 = jnp.sum(t, axis=-1, keepdims=True) - s0
    return jnp.where(low, s0, s1) * (1.0 / HEAD)


def _pair_rms_fwd(x, g, low, m0):
    rs = lax.rsqrt(_pair_mean(x * x, low, m0) + EPS)
    xhat = x * rs
    return xhat * g, xhat, rs


def _pair_rms_bwd(dh, xhat, rs, g, low, m0):
    dxh = dh * g
    dx = rs * (dxh - xhat * _pair_mean(dxh * xhat, low, m0))
    return dx, dh * xhat


def _attn_build_table(tv, bias_ref, tab_ref):
    w = 2 * QBLK
    for qi in range(CHUNK):
        bias_ref[qi:qi + 1, :] = pltpu.roll(tv, (qi - (CHUNK - 1)) % w, 1)
    bias = bias_ref[...]
    lane = lax.broadcasted_iota(jnp.int32, (CHUNK, w), 1)
    for c in range(NCH):
        rolled = bias if c == 0 else pltpu.roll(bias, CHUNK * c, 1)
        ok = jnp.logical_and(lane >= CHUNK * c, lane < CHUNK * c + BAND)
        tab_ref[CHUNK * c:CHUNK * (c + 1), :] = jnp.where(ok, rolled, NEG)


def _attn_reduce_table(dt_ref, bias_ref):
    w = 2 * QBLK
    acc = dt_ref[0:CHUNK, :]
    for c in range(1, NCH):
        acc = acc + pltpu.roll(dt_ref[CHUNK * c:CHUNK * (c + 1), :], w - CHUNK * c, 1)
    bias_ref[...] = acc
    out = jnp.zeros((1, w), F32)
    for qi in range(CHUNK):
        out = out + pltpu.roll(bias_ref[qi:qi + 1, :], ((CHUNK - 1) - qi) % w, 1)
    return out


HALF = QBLK // 4
KSPAN = QBLK + HALF


def _attn_softmax(s, table, first_block, n_prev):
    s = s * (HEAD ** -0.5) + table
    col = lax.broadcasted_iota(jnp.int32, s.shape, 1)
    s = jnp.where(jnp.logical_and(first_block, col < n_prev), NEG, s)
    e = jnp.exp(s - jnp.max(s, axis=-1, keepdims=True))
    return e * (1.0 / jnp.sum(e, axis=-1, keepdims=True))


def attn_fwd(proj, q0, k0, v0, tv, gq, gk, seq, comm=None):
    n = proj.shape[0]
    heads = tv.shape[0]
    nq = seq // QBLK
    cur, prev, vec, gain = _attn_specs(nq)

    def body(q_ref, kp_ref, kc_ref, vp_ref, vc_ref, tv_ref, gq_ref, gk_ref, o_ref, bias_ref, tab_ref):
        @pl.when(jnp.logical_and(pl.program_id(1) == 0, pl.program_id(2) == 0))
        def _():
            for h in range(2):
                _attn_build_table(tv_ref[h], bias_ref, tab_ref.at[h])

        low, m = _pair_masks()
        qn, _, _ = _pair_rms_fwd(q_ref[...], gq_ref[...], low, m[0])
        kn, _, _ = _pair_rms_fwd(jnp.concatenate([kp_ref[...], kc_ref[...]], axis=0), gk_ref[...], low, m[0])
        knb = kn.astype(BF16)
        v = jnp.concatenate([vp_ref[...], vc_ref[...]], axis=0)
        first_block = pl.program_id(2) == 0
        for r in range(QBLK // HALF):
            rows, cols = slice(r * HALF, (r + 1) * HALF), slice(r * HALF, r * HALF + KSPAN)
            o = jnp.zeros((HALF, PAIR), F32)
            for h in range(2):
                s = _dot_nt((qn[rows] * m[h]).astype(BF16), knb[cols])
                p = _attn_softmax(s, tab_ref[h, rows, cols], first_block, QBLK - r * HALF)
                o = o + _dot(p.astype(BF16), (v[cols] * m[h]).astype(BF16))
            o_ref[rows, :] = o

    return _hosted_call(
        body, comm,
        name="attn_fwd",
        grid=(heads // 2, n // seq, nq),
        in_specs=[cur(q0), prev(k0), cur(k0), prev(v0), cur(v0), vec, gain, gain],
        out_specs=[cur()],
        out_shape=[jax.ShapeDtypeStruct((n, heads * HEAD), F32)],
        scratch_shapes=[pltpu.VMEM((CHUNK, 2 * QBLK), F32), pltpu.VMEM((2, QBLK, 2 * QBLK), F32)],
        args=(proj, proj, proj, proj, proj, tv, gq, gk),
    )


def attn_bwd(do, proj, q0, k0, v0, tv, gq, gk, seq, comm=None):
    n = proj.shape[0]
    heads = tv.shape[0]
    nb = n // seq
    nq = seq // QBLK
    cur, prev, vec, gain = _attn_specs(nq)
    keyblk = pl.BlockSpec((QBLK, PAIR), lambda p, b, i: (b * nq + jnp.maximum(i - 1, 0), p))

    def body(do_ref, q_ref, kp_ref, kc_ref, vp_ref, vc_ref, tv_ref, gq_ref, gk_ref,
             dq_ref, dk_ref, dv_ref, dtv_ref, dgq_ref, dgk_ref, bias_ref, tab_ref, dt_ref, ck_ref, cv_ref,
             dqn_ref, dkn_ref, dvv_ref):
        pp, b, i = pl.program_id(0), pl.program_id(1), pl.program_id(2)
        head_start = jnp.logical_and(b == 0, i == 0)

        @pl.when(head_start)
        def _():
            for h in range(2):
                _attn_build_table(tv_ref[h], bias_ref, tab_ref.at[h])
            dt_ref[...] = jnp.zeros_like(dt_ref)

        @pl.when(i < nq)
        def _():
            low, m = _pair_masks()
            gq_, gk_ = gq_ref[...], gk_ref[...]
            qn, qhat, qrs = _pair_rms_fwd(q_ref[...], gq_, low, m[0])
            kn, khat, krs = _pair_rms_fwd(jnp.concatenate([kp_ref[...], kc_ref[...]], axis=0), gk_, low, m[0])
            knb = kn.astype(BF16)
            vb = jnp.concatenate([vp_ref[...], vc_ref[...]], axis=0).astype(BF16)
            do_ = do_ref[...]
            dkn_ref[...] = jnp.zeros_like(dkn_ref)
            dvv_ref[...] = jnp.zeros_like(dvv_ref)
            for r in range(QBLK // HALF):
                rows, cols = slice(r * HALF, (r + 1) * HALF), slice(r * HALF, r * HALF + KSPAN)
                dqn = jnp.zeros((HALF, PAIR), F32)
                for h in range(2):
                    qh = (qn[rows] * m[h]).astype(BF16)
                    doh = (do_[rows] * m[h]).astype(BF16)
                    p = _attn_softmax(_dot_nt(qh, knb[cols]), tab_ref[h, rows, cols], i == 0, QBLK - r * HALF)
                    dvv_ref[cols, :] += _dot_tn(p.astype(BF16), doh)
                    dp = _dot_nt(doh, vb[cols])
                    ds = p * (dp - jnp.sum(p * dp, axis=-1, keepdims=True))
                    dt_ref[h, rows, cols] += ds
                    dsb = (ds * (HEAD ** -0.5)).astype(BF16)
                    dqn = dqn + _dot(dsb, (kn[cols] * m[h]).astype(BF16))
                    dkn_ref[cols, :] += _dot_tn(dsb, qh)
                dqn_ref[rows, :] = dqn
            dv = dvv_ref[...]
            dq, dgq_rows = _pair_rms_bwd(dqn_ref[...], qhat, qrs, gq_, low, m[0])
            dk, dgk_rows = _pair_rms_bwd(dkn_ref[...], khat, krs, gk_, low, m[0])
            dq_ref[...] = dq

            @pl.when(i == 0)
            def _():
                dk_ref[...] = dk[:QBLK]
                dv_ref[...] = dv[:QBLK]

            @pl.when(i > 0)
            def _():
                dk_ref[...] = ck_ref[...] + dk[:QBLK]
                dv_ref[...] = cv_ref[...] + dv[:QBLK]

            ck_ref[...] = dk[QBLK:]
            cv_ref[...] = dv[QBLK:]
            first = jnp.logical_and(pp == 0, head_start)
            _acc_out(dgq_ref, jnp.sum(dgq_rows, axis=0, keepdims=True), first)
            _acc_out(dgk_ref, jnp.sum(dgk_rows, axis=0, keepdims=True), first)

        @pl.when(i == nq)
        def _():
            dk_ref[...] = ck_ref[...]
            dv_ref[...] = cv_ref[...]

        @pl.when(jnp.logical_and(b == nb - 1, i == nq))
        def _():
            for h in range(2):
                dtv_ref[h] = _attn_reduce_table(dt_ref.at[h], bias_ref)

    tok = jax.ShapeDtypeStruct((n, heads * HEAD), F32)
    return _hosted_call(
        body, comm,
        name="attn_bwd",
        grid=(heads // 2, nb, nq + 1),
        in_specs=[cur(), cur(q0), prev(k0), cur(k0), prev(v0), cur(v0), vec, gain, gain],
        out_specs=[cur(), keyblk, keyblk, vec, gain, gain],
        out_shape=[tok, tok, tok, jax.ShapeDtypeStruct(tv.shape, F32),
                   jax.ShapeDtypeStruct((1, PAIR), F32), jax.ShapeDtypeStruct((1, PAIR), F32)],
        scratch_shapes=[pltpu.VMEM((CHUNK, 2 * QBLK), F32), pltpu.VMEM((2, QBLK, 2 * QBLK), F32),
                        pltpu.VMEM((2, QBLK, 2 * QBLK), F32), pltpu.VMEM((QBLK, PAIR), F32),
                        pltpu.VMEM((QBLK, PAIR), F32), pltpu.VMEM((QBLK, PAIR), F32),
                        pltpu.VMEM((2 * QBLK, PAIR), F32), pltpu.VMEM((2 * QBLK, PAIR), F32)],
        args=(do, proj, proj, proj, proj, proj, tv, gq, gk),
    )


CONV_TC = 512


def _ln_fwd(h1, g, b):
    mu = jnp.mean(h1, axis=-1, keepdims=True)
    xc = h1 - mu
    rs = lax.rsqrt(jnp.mean(xc * xc, axis=-1, keepdims=True) + EPS)
    yhat = xc * rs
    return yhat * g + b, yhat, rs


def _glu(z, dc):
    return z[:, :dc] * _sig(z[:, dc:])


def conv_fwd(proj, zcol, seq, w, bdw, lng, lnb):
    n = proj.shape[0]
    dc = w.shape[1]
    tc = min(CONV_TC, seq)
    nt = seq // tc

    def body(z_ref, zp_ref, w_ref, b_ref, g_ref, lb_ref, h1_ref, o_ref, ext_ref):
        live = jnp.where(pl.program_id(1) == 0, 0.0, 1.0)
        ext_ref[pl.ds(0, HALO), :] = _glu(zp_ref[...], dc) * live
        ext_ref[pl.ds(HALO, tc), :] = _glu(z_ref[...], dc)
        acc = jnp.zeros((tc, dc), F32) + b_ref[...]
        for j in range(CONV_W):
            acc = acc + w_ref[j:j + 1, :] * ext_ref[pl.ds(HALO - (CONV_W - 1) + j, tc), :]
        h1_ref[...] = acc
        ln, _, _ = _ln_fwd(acc, g_ref[...], lb_ref[...])
        o_ref[...] = ln * _sig(ln)

    full = lambda shape: pl.BlockSpec(shape, lambda b, t: (0, 0))
    return _call(
        body,
        name="conv_fwd",
        grid=(n // seq, nt),
        in_specs=[
            pl.BlockSpec((tc, 2 * dc), lambda b, t: (b * nt + t, zcol)),
            pl.BlockSpec((HALO, 2 * dc), lambda b, t: (jnp.maximum((b * seq + t * tc) // HALO - 1, 0), zcol)),
            full((HALO, dc)), full((1, dc)), full((1, dc)), full((1, dc)),
        ],
        out_specs=[
            pl.BlockSpec((tc, dc), lambda b, t: (b * nt + t, 0)),
            pl.BlockSpec((tc, dc), lambda b, t: (b * nt + t, 0)),
        ],
        out_shape=[jax.ShapeDtypeStruct((n, dc), F32), jax.ShapeDtypeStruct((n, dc), F32)],
        scratch_shapes=[pltpu.VMEM((tc + HALO, dc), F32)],
        compiler_params=_params(),
    )(proj, proj, w, bdw, lng, lnb)


def conv_bwd(dco, h1, proj, zcol, seq, w, bdw, lng, lnb):
    n = proj.shape[0]
    dc = w.shape[1]
    tc = min(CONV_TC, seq)
    nt = seq // tc
    nrow = n // HALO

    def body(do_ref, don_ref, h1_ref, h1n_ref, z_ref, zp_ref, w_ref, g_ref, lb_ref,
             dz_ref, dw_ref, db_ref, dg_ref, dlb_ref, ext_ref, dext_ref):
        b, t = pl.program_id(0), pl.program_id(1)
        first = jnp.logical_and(b == 0, t == 0)
        g, lb = g_ref[...], lb_ref[...]

        def dh1_of(do, h1):
            ln, yhat, rs = _ln_fwd(h1, g, lb)
            s = _sig(ln)
            dln = do * (s * (1.0 + ln * (1.0 - s)))
            dyh = dln * g
            dh1 = rs * (dyh - jnp.mean(dyh, axis=-1, keepdims=True)
                        - yhat * jnp.mean(dyh * yhat, axis=-1, keepdims=True))
            return dh1, dln, yhat

        dh1, dln, yhat = dh1_of(do_ref[...], h1_ref[...])
        dh1n, _, _ = dh1_of(don_ref[...], h1n_ref[...])
        _acc_out(dg_ref, jnp.sum(dln * yhat, axis=0, keepdims=True), first)
        _acc_out(dlb_ref, jnp.sum(dln, axis=0, keepdims=True), first)
        _acc_out(db_ref, jnp.sum(dh1, axis=0, keepdims=True), first)

        dext_ref[pl.ds(0, tc), :] = dh1
        dext_ref[pl.ds(tc, HALO), :] = dh1n * jnp.where(t == nt - 1, 0.0, 1.0)
        z = z_ref[...]
        ext_ref[pl.ds(0, HALO), :] = _glu(zp_ref[...], dc) * jnp.where(t == 0, 0.0, 1.0)
        ext_ref[pl.ds(HALO, tc), :] = _glu(z, dc)

        @pl.when(first)
        def _():
            dw_ref[...] = jnp.zeros_like(dw_ref)

        dh0 = jnp.zeros((tc, dc), F32)
        for j in range(CONV_W):
            dh0 = dh0 + w_ref[j:j + 1, :] * dext_ref[pl.ds(CONV_W - 1 - j, tc), :]
            dw_ref[j:j + 1, :] += jnp.sum(dh1 * ext_ref[pl.ds(HALO - (CONV_W - 1) + j, tc), :],
                                          axis=0, keepdims=True)
        za, sg = z[:, :dc], _sig(z[:, dc:])
        dz_ref[:, :dc] = dh0 * sg
        dz_ref[:, dc:] = dh0 * za * sg * (1.0 - sg)

    full = lambda shape: pl.BlockSpec(shape, lambda b, t: (0, 0))
    cur = lambda wd, col=0: pl.BlockSpec((tc, wd), lambda b, t: (b * nt + t, col))
    nxt = pl.BlockSpec((HALO, dc), lambda b, t: (jnp.minimum((b * seq + (t + 1) * tc) // HALO, nrow - 1), 0))
    return _call(
        body,
        name="conv_bwd",
        grid=(n // seq, nt),
        in_specs=[
            cur(dc), nxt, cur(dc), nxt, cur(2 * dc, zcol),
            pl.BlockSpec((HALO, 2 * dc), lambda b, t: (jnp.maximum((b * seq + t * tc) // HALO - 1, 0), zcol)),
            full((HALO, dc)), full((1, dc)), full((1, dc)),
        ],
        out_specs=[cur(2 * dc), full((HALO, dc)), full((1, dc)), full((1, dc)), full((1, dc))],
        out_shape=[
            jax.ShapeDtypeStruct((n, 2 * dc), F32),
            jax.ShapeDtypeStruct((HALO, dc), F32),
            jax.ShapeDtypeStruct((1, dc), F32),
            jax.ShapeDtypeStruct((1, dc), F32),
            jax.ShapeDtypeStruct((1, dc), F32),
        ],
        scratch_shapes=[pltpu.VMEM((tc + HALO, dc), F32), pltpu.VMEM((tc + HALO, dc), F32)],
        compiler_params=_params(),
    )(dco, dco, h1, h1, proj, proj, w, lng, lnb)


def _merge_common(l0, l1, l2, bg, so, ao, co, wbs, wba, wbc, d):
    ys = _dot(so.astype(BF16), wbs)
    ya = _dot(ao.astype(BF16), wba)
    yc = _dot(co.astype(BF16), wbc)
    gs = _sig(l0 + bg[:, :d])
    ga = _sig(l1 + bg[:, d:2 * d])
    gc = _sig(l2 + bg[:, 2 * d:])
    return (ys, ya, yc), (gs, ga, gc)


def _merge_specs(tm, d, dss, da, dc, layer):
    row = lambda w, col=0: pl.BlockSpec((tm, w), lambda i: (i, col))
    full = lambda r, c: pl.BlockSpec((r, c), lambda i: (0, 0))
    stacked = lambda r: pl.BlockSpec((1, r, d), lambda i: (layer, 0, 0))
    acts = [row(d, 0), row(d, 1), row(d, 2), full(1, 3 * d), row(dss), row(da), row(dc)]
    weights = [stacked(dss), stacked(da), stacked(dc), stacked(d)]
    return row, full, acts, weights


def merge_fwd(x, proj, bg, so, ao, co, wbs, wba, wbc, wout, layer):
    n, d = x.shape
    tm = _tile(n, 256, 8)
    row, full, acts, weights = _merge_specs(tm, d, so.shape[1], ao.shape[1], co.shape[1], layer)

    def body(x_ref, l0_ref, l1_ref, l2_ref, bg_ref, so_ref, ao_ref, co_ref,
             wbs_ref, wba_ref, wbc_ref, wo_ref, o_ref):
        (ys, ya, yc), (gs, ga, gc) = _merge_common(
            l0_ref[...], l1_ref[...], l2_ref[...], bg_ref[...], so_ref[...], ao_ref[...], co_ref[...],
            wbs_ref[0], wba_ref[0], wbc_ref[0], d)
        merged = gs * ys + ga * ya + gc * yc
        o_ref[...] = x_ref[...] + _dot(merged.astype(BF16), wo_ref[0])

    return _call(
        body,
        name="merge_fwd",
        grid=(n // tm,),
        in_specs=[row(d)] + acts + weights,
        out_specs=row(d),
        out_shape=jax.ShapeDtypeStruct((n, d), F32),
        compiler_params=_params(),
    )(x, proj, proj, proj, bg, so, ao, co, wbs, wba, wbc, wout)


def merge_bwd(dx, proj, bg, so, ao, co, wbs, wba, wbc, wout, layer):
    n, d = dx.shape
    dss, da, dc = so.shape[1], ao.shape[1], co.shape[1]
    tm = _tile(n, 256, 8)
    row, full, acts, weights = _merge_specs(tm, d, dss, da, dc, layer)

    def body(dx_ref, l0_ref, l1_ref, l2_ref, bg_ref, so_ref, ao_ref, co_ref, wbs_ref, wba_ref, wbc_ref, wo_ref,
             dl_ref, dso_ref, dao_ref, dco_ref, dbg_ref, mg_ref, dxb_ref, dys_ref, dya_ref, dyc_ref):
        wbs, wba, wbc = wbs_ref[0], wba_ref[0], wbc_ref[0]
        (ys, ya, yc), (gs, ga, gc) = _merge_common(
            l0_ref[...], l1_ref[...], l2_ref[...], bg_ref[...], so_ref[...], ao_ref[...], co_ref[...],
            wbs, wba, wbc, d)
        mg_ref[...] = (gs * ys + ga * ya + gc * yc).astype(BF16)
        dxb = dx_ref[...].astype(BF16)
        dxb_ref[...] = dxb
        dm = _dot_nt(dxb, wo_ref[0])
        first = pl.program_id(0) == 0
        for k, (y, g, w, dy_ref, db_ref) in enumerate((
                (ys, gs, wbs, dys_ref, dso_ref), (ya, ga, wba, dya_ref, dao_ref), (yc, gc, wbc, dyc_ref, dco_ref))):
            dl = dm * y * g * (1.0 - g)
            dl_ref[:, k * d:(k + 1) * d] = dl.astype(BF16)
            _acc_out(dbg_ref.at[:, k * d:(k + 1) * d], jnp.sum(dl, axis=0, keepdims=True), first)
            dy = (dm * g).astype(BF16)
            dy_ref[...] = dy
            db_ref[...] = _dot_nt(dy, w)

    bf = lambda w: jax.ShapeDtypeStruct((n, w), BF16)
    return _call(
        body,
        name="merge_bwd",
        grid=(n // tm,),
        in_specs=[row(d)] + acts + weights,
        out_specs=[row(3 * d), row(dss), row(da), row(dc), full(1, 3 * d),
                   row(d), row(d), row(d), row(d), row(d)],
        out_shape=[bf(3 * d), jax.ShapeDtypeStruct((n, dss), F32), jax.ShapeDtypeStruct((n, da), F32),
                   jax.ShapeDtypeStruct((n, dc), F32), jax.ShapeDtypeStruct((1, 3 * d), F32),
                   bf(d), bf(d), bf(d), bf(d), bf(d)],
        compiler_params=_params(),
    )(dx, proj, proj, proj, bg, so, ao, co, wbs, wba, wbc, wout)


def loss_head(y, target):
    n, d = y.shape
    tm = _tile(n, 512, 8)

    def body(y_ref, t_ref, dy_ref, l_ref):
        e = y_ref[...] - t_ref[...]
        dy_ref[...] = e * (1.0 / d)
        part = 0.5 * jnp.sum(jnp.sum(e * e, axis=-1, keepdims=True) * (1.0 / d), axis=0, keepdims=True)
        _acc_out(l_ref, part, pl.program_id(0) == 0)

    return _call(
        body,
        name="loss_head",
        grid=(n // tm,),
        in_specs=[pl.BlockSpec((tm, d), lambda i: (i, 0)), pl.BlockSpec((tm, d), lambda i: (i, 0))],
        out_specs=[pl.BlockSpec((tm, d), lambda i: (i, 0)), pl.BlockSpec((1, 1), lambda i: (0, 0))],
        out_shape=[jax.ShapeDtypeStruct((n, d), F32), jax.ShapeDtypeStruct((1, 1), F32)],
        compiler_params=_params(),
    )(y, target)


def _mesh_pos():
    return lax.axis_index("x"), lax.axis_index("y"), lax.axis_index("c")


ANY = pl.BlockSpec(memory_space=pl.ANY)


def _comm_sems(na):
    return [pltpu.SemaphoreType.DMA((na, 7)), pltpu.SemaphoreType.DMA((na, 7)), pltpu.SemaphoreType.DMA((na,))]


def _gather_plan(x_refs, out_refs, sems):
    na = len(x_refs)
    send_sems, recv_sems, local_sems = sems
    x, y, c = _mesh_pos()
    me, sibling = (x, y, c), (x, y, 1 - c)
    chips = [(1 - x, y), (x, 1 - y), (1 - x, 1 - y)]

    def slot(a, px, py, pc):
        return out_refs[a].at[4 * px + 2 * py + pc]

    def copy(a, k, block, to, src=None):
        return pltpu.make_async_remote_copy(
            src_ref=slot(a, *block) if src is None else src, dst_ref=slot(a, *block),
            send_sem=send_sems.at[a, k], recv_sem=recv_sems.at[a, k],
            device_id=to, device_id_type=pl.DeviceIdType.MESH)

    mine = [pltpu.make_async_copy(x_refs[a], slot(a, *me), local_sems.at[a]) for a in range(na)]
    first = []
    for a in range(na):
        first.append(copy(a, 0, me, sibling, src=x_refs[a]))
        first += [copy(a, 1 + j, me, (*chip, c), src=x_refs[a]) for j, chip in enumerate(chips)]

    def start():
        for cp in mine + first:
            cp.start()

    def finish():
        passed = []
        for j, chip in enumerate(chips):
            for a in range(na):
                copy(a, 1 + j, (*chip, c), me).wait_recv()
                fwd = copy(a, 4 + j, (*chip, c), sibling)
                fwd.start()
                passed.append(fwd)
        for a in range(na):
            copy(a, 0, sibling, me).wait_recv()
            for j, chip in enumerate(chips):
                copy(a, 4 + j, (*chip, 1 - c), me).wait_recv()
        for cp in first + passed:
            cp.wait_send()
        for cp in mine:
            cp.wait()

    return start, finish


def _gather_out(shards):
    return [jax.ShapeDtypeStruct((N_DEV,) + s.shape, s.dtype) for s in shards]


def all_gather(shards):
    na = len(shards)

    def body(*refs):
        start, finish = _gather_plan(refs[:na], refs[na:2 * na], refs[2 * na:])
        start()
        finish()

    return _call(
        body,
        name="all_gather",
        out_shape=_gather_out(shards),
        in_specs=[ANY] * na,
        out_specs=[ANY] * na,
        scratch_shapes=_comm_sems(na),
    )(*shards)


def _hosted_call(body, comm, *, name, grid, in_specs, out_specs, out_shape, scratch_shapes, args):
    if comm is None:
        res = _call(body, name=name, grid=grid, in_specs=in_specs, out_specs=out_specs, out_shape=out_shape,
                    scratch_shapes=scratch_shapes, compiler_params=_params())(*args)
        return res, []
    plan, arrays, c_out = comm
    n_in, n_out, n_scr, ci, co = len(in_specs), len(out_specs), len(scratch_shapes), len(arrays), len(c_out)

    def hosted(*refs):
        ins, cins = refs[:n_in], refs[n_in:n_in + ci]
        o0 = n_in + ci
        outs, couts = refs[o0:o0 + n_out], refs[o0 + n_out:o0 + n_out + co]
        s0 = o0 + n_out + co
        scr, sems = refs[s0:s0 + n_scr], refs[s0 + n_scr:]
        ids = [pl.program_id(ax) for ax in range(len(grid))]
        first = functools.reduce(jnp.logical_and, [i == 0 for i in ids])
        last = functools.reduce(jnp.logical_and, [i == g - 1 for i, g in zip(ids, grid)])
        start, finish = plan(cins, couts, sems)
        pl.when(first)(start)
        body(*ins, *outs, *scr)
        pl.when(last)(finish)

    res = _call(hosted, name=name + "_comm", grid=grid, in_specs=list(in_specs) + [ANY] * ci,
                out_specs=list(out_specs) + [ANY] * co, out_shape=list(out_shape) + list(c_out),
                scratch_shapes=list(scratch_shapes) + _comm_sems(max(ci, co)),
                compiler_params=_params())(*args, *arrays)
    return res[:n_out], res[n_out:]


def _exchange_plan(bcast=()):
    def plan(s_refs, r_refs, sems):
        na = len(s_refs)
        send_sems, recv_sems, local_sems = sems
        x, y, c = _mesh_pos()
        me = 4 * x + 2 * y + c

        def peer(k):
            px = (1 - x) if k & 4 else x
            py = (1 - y) if k & 2 else y
            pc = (1 - c) if k & 1 else c
            return (px, py, pc), 4 * px + 2 * py + pc

        def src_dst(a, pid, slot):
            return (s_refs[a] if a in bcast else s_refs[a].at[pid]), r_refs[a].at[slot]

        def copy(a, k):
            to, pid = peer(k)
            src, dst = src_dst(a, pid, me)
            return pltpu.make_async_remote_copy(
                src_ref=src, dst_ref=dst, send_sem=send_sems.at[a, k - 1], recv_sem=recv_sems.at[a, k - 1],
                device_id=to, device_id_type=pl.DeviceIdType.MESH)

        def arrival(a, k):
            _, pid = peer(k)
            src, dst = src_dst(a, pid, pid)
            return pltpu.make_async_remote_copy(
                src_ref=src, dst_ref=dst, send_sem=send_sems.at[a, k - 1], recv_sem=recv_sems.at[a, k - 1],
                device_id=(x, y, c), device_id_type=pl.DeviceIdType.MESH)

        mine = [pltpu.make_async_copy(*src_dst(a, me, me), local_sems.at[a]) for a in range(na)]
        sends = [copy(a, k) for k in range(1, N_DEV) for a in range(na)]

        def start():
            for cp in mine + sends:
                cp.start()

        def finish():
            for k in range(1, N_DEV):
                for a in range(na):
                    arrival(a, k).wait_recv()
            for cp in sends:
                cp.wait_send()
            for cp in mine:
                cp.wait()

        return start, finish

    return plan


def _exchange_out(slabs, bcast=()):
    return [jax.ShapeDtypeStruct(((N_DEV,) + s.shape) if a in bcast else s.shape, s.dtype)
            for a, s in enumerate(slabs)]


def grad_exchange(slabs, small):
    arrays = list(slabs) + [small]
    na = len(arrays)
    bcast = (na - 1,)

    def body(*refs):
        start, finish = _exchange_plan(bcast)(refs[:na], refs[na:2 * na], refs[2 * na:])
        start()
        finish()

    return _call(
        body,
        name="grad_exchange",
        out_shape=_exchange_out(arrays, bcast),
        in_specs=[ANY] * na,
        out_specs=[ANY] * na,
        scratch_shapes=_comm_sems(na),
    )(*arrays)


ADAM_BLOCK = 256 * 1024


def sum_adamw(recv, w, m, v, layer, prev, name):
    depth, rows, cols = w.shape
    tr = _tile(rows, max(8, ADAM_BLOCK // cols // 8 * 8), 8)
    c1 = 1.0 / (1.0 - ADAM_B1 ** ADAM_STEP)
    c2 = 1.0 / (1.0 - ADAM_B2 ** ADAM_STEP)

    def body(r_ref, w_ref, m_ref, v_ref, *rest):
        g_ref, d_ref, mo_ref, vo_ref = rest[-4:]
        g = r_ref[0].astype(F32)
        for s in range(1, N_DEV):
            g = g + r_ref[s].astype(F32)
        mn = ADAM_B1 * m_ref[0] + (1.0 - ADAM_B1) * g
        vn = ADAM_B2 * v_ref[0] + (1.0 - ADAM_B2) * (g * g)
        g_ref[0] = g
        mo_ref[0] = mn
        vo_ref[0] = vn
        d_ref[0] = -ADAM_LR * ((mn * c1) / (jnp.sqrt(vn * c2) + ADAM_EPS) + ADAM_WD * w_ref[0])

    blk = pl.BlockSpec((1, tr, cols), lambda i: (layer, i, 0))
    out = jax.ShapeDtypeStruct((depth, rows, cols), F32)
    in_specs = [pl.BlockSpec((N_DEV, tr, cols), lambda i: (0, i, 0)), blk, blk, blk]
    args = [recv, w, m, v]
    extra = {}
    if prev is not None:
        in_specs += [ANY] * 4
        args += list(prev)
        extra["input_output_aliases"] = {4 + j: j for j in range(4)}
    return _call(
        body,
        name=name,
        grid=(rows // tr,),
        in_specs=in_specs,
        out_specs=[blk, blk, blk, blk],
        out_shape=[out, out, out, out],
        compiler_params=_params(),
        **extra,
    )(*args)


def _attn_bias_vector(rel_bias):
    h = rel_bias.shape[0]
    n_far = BAND - MAX_REL
    n_near = BAND + CHUNK - 1 - n_far
    far = jnp.broadcast_to(rel_bias[:, 2 * MAX_REL:], (h, n_far))
    near = rel_bias[:, 2 * MAX_REL - n_near:2 * MAX_REL][:, ::-1]
    pad = jnp.zeros((h, 2 * QBLK - n_far - n_near), F32)
    return jnp.concatenate([far, near, pad], axis=1)[:, None, :]


def _s5_prepare(lre, lim, ldt, bre, bim, cre, cim):
    g, p = lre.shape
    lr = jnp.minimum(lre, -1e-4)
    dt = jnp.exp(ldt)[:, None]
    mag = jnp.exp(lr * dt)
    ar = mag * jnp.cos(lim * dt)
    ai = mag * jnp.sin(lim * dt)
    den = lr * lr + lim * lim
    coef_r = ((ar - 1.0) * lr + ai * lim) / den
    coef_i = (ai * lr - (ar - 1.0) * lim) / den
    bbar_r = coef_r[..., None] * bre - coef_i[..., None] * bim
    bbar_i = coef_r[..., None] * bim + coef_i[..., None] * bre
    eye = jnp.eye(g, dtype=F32)
    bd_in = lambda b: jnp.einsum("gpc,gh->gchp", b, eye).reshape(g * S5_GROUP, g * p)
    bd_out = lambda c: jnp.einsum("gcp,gh->gphc", c, eye).reshape(g * p, g * S5_GROUP)
    return (ar.reshape(1, g * p), ai.reshape(1, g * p), bd_in(bbar_r), bd_in(bbar_i), bd_out(cre), bd_out(cim))


SHARDED = ("ffn1_w_up", "ffn1_w_down", "w_in", "s5_w_glu", "w_br_s5", "w_br_attn", "conv_w_dw", "w_br_conv",
           "w_out", "ffn2_w_up", "ffn2_w_down")
WEIGHTS = ("ffn1_norm", "ffn1_w_up", "ffn1_w_down", "mix_norm", "w_in", "b_gate", "s5_lambda_re", "s5_lambda_im",
           "s5_log_dt", "s5_b_re", "s5_b_im", "s5_c_re", "s5_c_im", "s5_d", "s5_w_glu", "w_br_s5", "attn_q_gain",
           "attn_k_gain", "attn_rel_bias", "w_br_attn", "conv_w_dw", "conv_b_dw", "conv_ln_g", "conv_ln_b",
           "w_br_conv", "w_out", "ffn2_norm", "ffn2_w_up", "ffn2_w_down")
SMALL = tuple(nm for nm in WEIGHTS if nm not in SHARDED)
SMALL_LANES = 1024


def _cols_full(g):
    _, depth, k, nn = g.shape
    return g.transpose(1, 2, 0, 3).reshape(depth, k, N_DEV * nn)


def _rows_full(g):
    _, depth, r, cc = g.shape
    return g.transpose(1, 0, 2, 3).reshape(depth, N_DEV * r, cc)


def _cols_slabs(gfull):
    depth, k, c8 = gfull.shape
    return gfull.reshape(depth, k, N_DEV, c8 // N_DEV).transpose(0, 2, 1, 3)


def _small_pack(t):
    flat = jnp.concatenate([t[nm].reshape(-1) for nm in SMALL])
    rows = -(-flat.shape[0] // SMALL_LANES)
    rows = -(-rows // 8) * 8
    return jnp.pad(flat, (0, rows * SMALL_LANES - flat.shape[0])).reshape(rows, SMALL_LANES)


def _small_unpack(flat, like):
    flat = flat.reshape(-1)
    out, off = {}, 0
    for nm in SMALL:
        out[nm] = flat[off:off + like[nm].size].reshape(like[nm].shape)
        off += like[nm].size
    return out


def _step(x, target, w, m, v):
    bsz, seq, d = x.shape
    n = bsz * seq
    depth = w["ffn1_norm"].shape[0]
    da, dss, dc = d // 2, d // 4, d // 4
    gp =dss // S5_GROUP * S5_STATE
    mid = 3 * da + 2 * dc
    q0, k0, v0, z0, u0 = 3 * d, 3 * d + da, 3 * d + 2 * da, 3 * d + 3 * da, 3 * d + mid
    to_kernel_cols = lambda a: jnp.concatenate([a[..., dss + mid:], a[..., dss:dss + mid], a[..., :dss]], axis=-1)
    to_ref_cols = lambda a: jnp.concatenate([a[..., 3 * d + mid:], a[..., 3 * d:3 * d + mid], a[..., :3 * d]], axis=-1)

    def shard(key):
        nm, sl = (key[0], slice(key[1], key[1] + 1)) if isinstance(key, tuple) else (key, slice(None))
        return w[nm][sl] if nm == "conv_w_dw" else w[nm][sl].astype(BF16)

    mixer_names = ("w_in", "s5_w_glu", "w_br_s5", "w_br_attn", "w_br_conv", "conv_w_dw", "w_out")
    ffn1_of = lambda l: (("ffn1_w_up", l), ("ffn1_w_down", l))
    later_ffn1 = tuple(k for l in range(1, depth) for k in ffn1_of(l))
    gather_of = lambda keys: (_gather_plan, [shard(k) for k in keys], _gather_out([shard(k) for k in keys]))
    full = {}

    def take(keys, arrays):
        for key, g in zip(keys, arrays):
            nm = key[0] if isinstance(key, tuple) else key
            if nm in ("ffn1_w_up", "ffn2_w_up"):
                full[key] = g
            elif nm in ("ffn1_w_down", "ffn2_w_down", "w_out"):
                full[key] = _rows_full(g)
            elif nm == "w_in":
                full[key] = to_kernel_cols(_cols_full(g))
            else:
                full[key] = _cols_full(g)

    take(ffn1_of(0), all_gather([shard(k) for k in ffn1_of(0)]))
    nff = full[("ffn1_w_up", 0)].shape[3]

    row = lambda a: a.reshape(1, -1)
    saved = []
    xin = x.reshape(n, d)
    for l in range(depth):
        s = {"x0": xin}
        f1_up, f1_down = ffn1_of(l)
        (s["x1"], s["a1"], s["b1"]), got = ffn_fwd(xin, row(w["ffn1_norm"][l]), full[f1_up], full[f1_down], 0,
                                                   comm=gather_of(mixer_names) if l == 0 else None)
        if l == 0:
            take(mixer_names, got)
            conv_w = jnp.pad(full["conv_w_dw"], ((0, 0), (0, HALO - CONV_W), (0, 0)))
        (proj,), got = proj_fwd(s["x1"], row(w["mix_norm"][l]), full["w_in"], l,
                                comm=gather_of(later_ffn1) if l == 0 and later_ffn1 else None)
        if l == 0:
            take(later_ffn1, got)
        s["proj"] = proj
        prep_in = (w["s5_lambda_re"][l], w["s5_lambda_im"][l], w["s5_log_dt"][l], w["s5_b_re"][l], w["s5_b_im"][l],
                   w["s5_c_re"][l], w["s5_c_im"][l])
        (ar, ai, bm_r, bm_i, cm_r, cm_i), s["prep_vjp"] = jax.vjp(_s5_prepare, *prep_in)
        s["s5p"] = (ar, ai, bm_r.astype(BF16), bm_i.astype(BF16), cm_r.astype(BF16), cm_i.astype(BF16),
                    row(w["s5_d"][l]), full["s5_w_glu"], l)
        s["so"], s["xr"], s["xi"], s["ypre"] = s5_fwd(proj, u0 // dss, seq, *s["s5p"])
        tv, s["tv_vjp"] = jax.vjp(_attn_bias_vector, w["attn_rel_bias"][l])
        pair_gain = lambda g: jnp.tile(row(g), (1, 2))
        s["attnp"] = (q0, k0, v0, tv, pair_gain(w["attn_q_gain"][l]), pair_gain(w["attn_k_gain"][l]))
        (s["ao"],), got = attn_fwd(proj, *s["attnp"], seq,
                                   comm=gather_of(("ffn2_w_up", "ffn2_w_down")) if l == 0 else None)
        if l == 0:
            take(("ffn2_w_up", "ffn2_w_down"), got)
        s["convp"] = (conv_w[l], row(w["conv_b_dw"][l]), row(w["conv_ln_g"][l]), row(w["conv_ln_b"][l]))
        s["h1"], s["co"] = conv_fwd(proj, z0 // (2 * dc), seq, *s["convp"])
        s["mergep"] = (row(w["b_gate"][l]), s["so"], s["ao"], s["co"], full["w_br_s5"], full["w_br_attn"],
                       full["w_br_conv"], full["w_out"], l)
        s["x2"] = merge_fwd(s["x1"], proj, *s["mergep"])
        (xin, s["a2"], s["b2"]), _ = ffn_fwd(s["x2"], row(w["ffn2_norm"][l]), full["ffn2_w_up"], full["ffn2_w_down"], l)
        saved.append(s)

    dx, loss = loss_head(xin, target.reshape(n, d))
    loss = lax.psum(loss[0, 0], ("x", "y", "c"))

    small_g = {nm: [None] * depth for nm in SMALL}
    slab = {}

    recv = {}

    def exchange_of(names, l):
        arrays = [slab.pop(nm) for nm in names]
        return [(nm, l) for nm in names], (_exchange_plan(), arrays, _exchange_out(arrays))

    def ffn_grads(which, l, hn, dyb, dab_a, dab_b, act, send_up_now=False):
        up, down = which + "_w_up", which + "_w_down"
        up4, down4 = (1, N_DEV, d, nff), (1, FF_CHUNKS, nff, d)
        half = wgrad(hn, dab_a, "wg_ffn_up", up4, "col")
        slab[up] = wgrad(hn, dab_b, "wg_ffn_up", up4, "col", g0=FF_CHUNKS, prev=half).reshape(up4[1:])
        if send_up_now:
            keys, comm = exchange_of((up,), l)
            (dn,), got = wgrad(act, dyb, "wg_ffn_down", down4, "row", comm=comm)
            recv.update(zip(keys, got))
        else:
            dn = wgrad(act, dyb, "wg_ffn_down", down4, "row")
        slab[down] = dn.reshape(N_DEV, nff // 2, d)

    pending = None
    for l in reversed(range(depth)):
        s = saved[l]
        outs, got = ffn_bwd(dx, s["x2"], row(w["ffn2_norm"][l]), s["a2"], s["b2"],
                            full["ffn2_w_up"], full["ffn2_w_down"], l, comm=pending[1] if pending else None)
        if pending:
            recv.update(zip(pending[0], got))
        dx, dg, hn, dyb, dab_a, dab_b, act = outs
        small_g["ffn2_norm"][l] = dg
        ffn_grads("ffn2", l, hn, dyb, dab_a, dab_b, act)

        dlog, dso, dao, dco, dbg, mg, dxb, dys, dya, dyc = merge_bwd(dx, s["proj"], *s["mergep"])
        small_g["b_gate"][l] = dbg
        slab["w_out"] = wgrad(mg, dxb, "wg_out", (1, N_DEV, d // N_DEV, d), "row").reshape(N_DEV, d // N_DEV, d)
        for nm, act_in, dy_br in (("w_br_s5", s["so"], dys), ("w_br_attn", s["ao"], dya), ("w_br_conv", s["co"], dyc)):
            k_in = act_in.shape[1]
            slab[nm] = wgrad(act_in, dy_br, "wg_" + nm, (1, N_DEV, k_in, d // N_DEV), "col").reshape(
                N_DEV, k_in, d // N_DEV)

        dz, dwdw, dbdw, dlng, dlnb = conv_bwd(dco, s["h1"], s["proj"], z0 // (2 * dc), seq, *s["convp"])
        slab["conv_w_dw"] = _cols_slabs(dwdw[None, :CONV_W])[0].astype(BF16)
        small_g["conv_b_dw"][l], small_g["conv_ln_g"][l], small_g["conv_ln_b"][l] = dbdw, dlng, dlnb

        keys, comm = exchange_of(("ffn2_w_up", "ffn2_w_down"), l)
        outs, got = attn_bwd(dao, s["proj"], *s["attnp"], seq, comm=comm)
        recv.update(zip(keys, got))
        dq, dk, dv, dtv, dgq, dgk = outs
        small_g["attn_q_gain"][l] = dgq[:, :HEAD] + dgq[:, HEAD:]
        small_g["attn_k_gain"][l] = dgk[:, :HEAD] + dgk[:, HEAD:]
        small_g["attn_rel_bias"][l] = s["tv_vjp"](dtv)[0]

        du, gr, gi, dyb5, glb, dzb, dar, dai, dd = s5_bwd(dso, s["ypre"], s["proj"], u0 // dss, s["xr"], s["xi"],
                                                          seq, *s["s5p"])
        small_g["s5_d"][l] = dd
        one = lambda k1, k2: (1, 1, k1, k2)
        slab["s5_w_glu"] = _cols_slabs(wgrad(glb, dzb, "wg_s5_glu", one(dss, 2 * dss), "col")[0])[0]
        dcm_r = wgrad(s["xr"], dyb5, "wg_s5_c", one(gp, dss), "col", dtype=F32)[0, 0]
        dcm_i = -wgrad(s["xi"], dyb5, "wg_s5_c", one(gp, dss), "col", dtype=F32)[0, 0]
        dbm_r = wgrad(s["proj"], gr, "wg_s5_b", one(dss, gp), "col", a_cols=(u0, dss), dtype=F32)[0, 0]
        dbm_i = wgrad(s["proj"], gi, "wg_s5_b", one(dss, gp), "col", a_cols=(u0, dss), dtype=F32)[0, 0]
        pg = s["prep_vjp"]((dar, dai, dbm_r, dbm_i, dcm_r, dcm_i))
        for nm, gval in zip(("s5_lambda_re", "s5_lambda_im", "s5_log_dt", "s5_b_re", "s5_b_im", "s5_c_re", "s5_c_im"), pg):
            small_g[nm][l] = gval

        dproj = jnp.concatenate([dlog, dq.astype(BF16), dk.astype(BF16), dv.astype(BF16),
                                 dz.astype(BF16), du.astype(BF16)], axis=1)
        dx, dgm, hn = proj_bwd(dproj, dx, s["x1"], row(w["mix_norm"][l]), full["w_in"], l)
        small_g["mix_norm"][l] = dgm
        slab["w_in"] = _cols_slabs(to_ref_cols(wgrad(hn, dproj, "wg_in", one(d, 3 * d + mid + dss), "col")[0]))[0]

        keys, comm = exchange_of(mixer_names, l)
        f1_up, f1_down = ffn1_of(l)
        outs, got = ffn_bwd(dx, s["x0"], row(w["ffn1_norm"][l]), s["a1"], s["b1"],
                            full[f1_up], full[f1_down], 0, comm=comm)
        recv.update(zip(keys, got))
        dx, dg, hn, dyb, dab_a, dab_b, act = outs
        small_g["ffn1_norm"][l] = dg
        ffn_grads("ffn1", l, hn, dyb, dab_a, dab_b, act, send_up_now=(l == 0))
        pending = exchange_of(("ffn1_w_down",) if l == 0 else ("ffn1_w_up", "ffn1_w_down"), l)

    small_flat = _small_pack({nm: jnp.stack([g.reshape(w[nm].shape[1:]) for g in small_g[nm]]) for nm in SMALL})
    *got, recv_small = grad_exchange(pending[1][1], small_flat)
    recv.update(zip(pending[0], got))

    outs = {}
    for nm in SHARDED:
        shp = w[nm].shape
        as3 = lambda t: t.reshape(shp[0], -1, shp[-1])
        bufs = None
        for l in reversed(range(depth)):
            bufs = sum_adamw(recv[(nm, l)], as3(w[nm]), as3(m[nm]), as3(v[nm]), l, bufs, "adamw_" + nm)
        outs[nm] = [b.reshape(shp) for b in bufs]
    packed = sum_adamw(recv_small, _small_pack(w)[None], _small_pack(m)[None], _small_pack(v)[None], 0, None,
                       "adamw_small")
    unpacked = [_small_unpack(p, w) for p in packed]
    for nm in SMALL:
        outs[nm] = [u[nm] for u in unpacked]
    return loss, dx.reshape(x.shape), outs


def kernel(x, ffn1_norm, ffn1_w_up, ffn1_w_down, mix_norm, w_in, b_gate, s5_lambda_re, s5_lambda_im, s5_log_dt, s5_b_re, s5_b_im, s5_c_re, s5_c_im, s5_d, s5_w_glu, w_br_s5, attn_q_gain, attn_k_gain, attn_rel_bias, w_br_attn, conv_w_dw, conv_b_dw, conv_ln_g, conv_ln_b, w_br_conv, w_out, ffn2_norm, ffn2_w_up, ffn2_w_down, loss_target, m_ffn1_norm, m_ffn1_w_up, m_ffn1_w_down, m_mix_norm, m_w_in, m_b_gate, m_s5_lambda_re, m_s5_lambda_im, m_s5_log_dt, m_s5_b_re, m_s5_b_im, m_s5_c_re, m_s5_c_im, m_s5_d, m_s5_w_glu, m_w_br_s5, m_attn_q_gain, m_attn_k_gain, m_attn_rel_bias, m_w_br_attn, m_conv_w_dw, m_conv_b_dw, m_conv_ln_g, m_conv_ln_b, m_w_br_conv, m_w_out, m_ffn2_norm, m_ffn2_w_up, m_ffn2_w_down, v_ffn1_norm, v_ffn1_w_up, v_ffn1_w_down, v_mix_norm, v_w_in, v_b_gate, v_s5_lambda_re, v_s5_lambda_im, v_s5_log_dt, v_s5_b_re, v_s5_b_im, v_s5_c_re, v_s5_c_im, v_s5_d, v_s5_w_glu, v_w_br_s5, v_attn_q_gain, v_attn_k_gain, v_attn_rel_bias, v_w_br_attn, v_conv_w_dw, v_conv_b_dw, v_conv_ln_g, v_conv_ln_b, v_w_br_conv, v_w_out, v_ffn2_norm, v_ffn2_w_up, v_ffn2_w_down):
    args = locals()
    w = {nm: args[nm] for nm in WEIGHTS}
    m = {nm: args["m_" + nm] for nm in WEIGHTS}
    v = {nm: args["v_" + nm] for nm in WEIGHTS}
    loss, gx, outs = _step(x, loss_target, w, m, v)
    return (loss, gx, *[outs[nm][0] for nm in WEIGHTS], *[outs[nm][1] for nm in WEIGHTS],
            *[outs[nm][2] for nm in WEIGHTS], *[outs[nm][3] for nm in WEIGHTS])
```

```python
import functools
import math

import numpy as np
import jax
import jax.numpy as jnp
from jax import lax
from jax.experimental import pallas as pl
from jax.experimental.pallas import tpu as pltpu

F32 = jnp.float32
BF16 = jnp.bfloat16

CHUNK = 64
N_LEFT = 8
QBLK = CHUNK * N_LEFT
HEAD = 64
MAX_REL = 128
S5_GROUP = 16
S5_STATE = 64
CONV_W = 31
HALO = 32
EPS = 1e-6
NEG = -1e30
ADAM_LR, ADAM_B1, ADAM_B2, ADAM_EPS, ADAM_WD, ADAM_STEP = 0.001, 0.9, 0.999, 1e-08, 0.01, 10
N_DEV = 8
VMEM_LIMIT = 56 * 1024 * 1024


def _call(body, **kw):
    return pl.pallas_call(body, **kw)


def _params(**kw):
    return pltpu.CompilerParams(vmem_limit_bytes=VMEM_LIMIT, **kw)


def _tile(n, cap, unit=128):
    if n <= cap:
        return n
    d = (cap // unit) * unit
    while d >= unit:
        if n % d == 0:
            return d
        d -= unit
    raise ValueError(f"no tile for {n} under {cap}")


def _dot(a, b):
    return jnp.dot(a, b, preferred_element_type=F32)


def _dot_nt(a, b):
    return lax.dot_general(a, b, (((1,), (1,)), ((), ())), preferred_element_type=F32)


def _dot_tn(a, b):
    return lax.dot_general(a, b, (((0,), (0,)), ((), ())), preferred_element_type=F32)


def _sig(x):
    return 1.0 / (1.0 + jnp.exp(-x))


def _rms_fwd(x, g):
    rs = lax.rsqrt(jnp.mean(x * x, axis=-1, keepdims=True) + EPS)
    xhat = x * rs
    return xhat * g, xhat, rs


def _rms_bwd(dh, xhat, rs, g):
    dxh = dh * g
    dx = rs * (dxh - xhat * jnp.mean(dxh * xhat, axis=-1, keepdims=True))
    return dx, dh * xhat


_GELU_C = math.sqrt(2.0 / math.pi)


def _gelu(x):
    return 0.5 * x * (1.0 + jnp.tanh(_GELU_C * (x + 0.044715 * x * x * x)))


def _gelu_grad(x):
    t = jnp.tanh(_GELU_C * (x + 0.044715 * x * x * x))
    return 0.5 * (1.0 + t) + 0.5 * x * (1.0 - t * t) * _GELU_C * (1.0 + 3.0 * 0.044715 * x * x)


def _acc_out(ref, val, first):
    @pl.when(first)
    def _():
        ref[...] = val

    @pl.when(jnp.logical_not(first))
    def _():
        ref[...] += val


FF_CHUNKS = N_DEV // 2


def ffn_fwd(x, g, w_up, w_down, layer, comm=None):
    n, d = x.shape
    nn = w_up.shape[3]
    tm = _tile(n, 1024, 8)

    def body(x_ref, g_ref, wa_ref, wb_ref, wd_ref, xo_ref, a_ref, b_ref, hn_ref, acc_ref):
        j = pl.program_id(1)

        @pl.when(j == 0)
        def _():
            h, _, _ = _rms_fwd(x_ref[...], g_ref[...])
            hn_ref[...] = h.astype(BF16)
            acc_ref[...] = jnp.zeros_like(acc_ref)

        hn = hn_ref[...]
        a = _dot(hn, wa_ref[0, 0])
        b = _dot(hn, wb_ref[0, 0])
        a_ref[0] = a.astype(BF16)
        b_ref[0] = b.astype(BF16)
        act = a * _sig(a) * b
        acc_ref[...] += _dot(act.astype(BF16), wd_ref[0])

        @pl.when(j == FF_CHUNKS - 1)
        def _():
            xo_ref[...] = x_ref[...] + 0.5 * acc_ref[...]

    return _hosted_call(
        body, comm,
        name="ffn_fwd",
        grid=(n // tm, FF_CHUNKS),
        in_specs=[
            pl.BlockSpec((tm, d), lambda i, j: (i, 0)),
            pl.BlockSpec((1, d), lambda i, j: (0, 0)),
            pl.BlockSpec((1, 1, d, nn), lambda i, j: (j, layer, 0, 0)),
            pl.BlockSpec((1, 1, d, nn), lambda i, j: (j + FF_CHUNKS, layer, 0, 0)),
            pl.BlockSpec((1, nn, d), lambda i, j: (layer, j, 0)),
        ],
        out_specs=[
            pl.BlockSpec((tm, d), lambda i, j: (i, 0)),
            pl.BlockSpec((1, tm, nn), lambda i, j: (j, i, 0)),
            pl.BlockSpec((1, tm, nn), lambda i, j: (j, i, 0)),
        ],
        out_shape=[
            jax.ShapeDtypeStruct((n, d), F32),
            jax.ShapeDtypeStruct((FF_CHUNKS, n, nn), BF16),
            jax.ShapeDtypeStruct((FF_CHUNKS, n, nn), BF16),
        ],
        scratch_shapes=[pltpu.VMEM((tm, d), BF16), pltpu.VMEM((tm, d), F32)],
        args=(x, g, w_up, w_up, w_down),
    )


def ffn_bwd(dy, x, g, a, b, w_up, w_down, layer, comm=None):
    n, d = x.shape
    nn = w_up.shape[3]
    tm = _tile(n, 512, 8)

    def body(dy_ref, x_ref, g_ref, a_ref, b_ref, wa_ref, wb_ref, wd_ref,
             dx_ref, dg_ref, hn_ref, dyb_ref, da_ref, db_ref, act_ref, dyb_s, dh_ref):
        i, j = pl.program_id(0), pl.program_id(1)

        @pl.when(j == 0)
        def _():
            h, _, _ = _rms_fwd(x_ref[...], g_ref[...])
            hn_ref[...] = h.astype(BF16)
            dyb = (0.5 * dy_ref[...]).astype(BF16)
            dyb_ref[...] = dyb
            dyb_s[...] = dyb
            dh_ref[...] = jnp.zeros_like(dh_ref)

        dact = _dot_nt(dyb_s[...], wd_ref[0])
        a32 = a_ref[0].astype(F32)
        b32 = b_ref[0].astype(F32)
        s = _sig(a32)
        sil = a32 * s
        da = (dact * b32 * (s * (1.0 + a32 * (1.0 - s)))).astype(BF16)
        db = (dact * sil).astype(BF16)
        da_ref[0] = da
        db_ref[0] = db
        act_ref[0] = (sil * b32).astype(BF16)
        dh_ref[...] += _dot_nt(da, wa_ref[0, 0]) + _dot_nt(db, wb_ref[0, 0])

        @pl.when(j == FF_CHUNKS - 1)
        def _():
            gg = g_ref[...]
            _, xhat, rs = _rms_fwd(x_ref[...], gg)
            dxn, dgr = _rms_bwd(dh_ref[...], xhat, rs, gg)
            dx_ref[...] = dy_ref[...] + dxn
            _acc_out(dg_ref, jnp.sum(dgr, axis=0, keepdims=True), i == 0)

    tok = pl.BlockSpec((tm, d), lambda i, j: (i, 0))
    chunk = pl.BlockSpec((1, tm, nn), lambda i, j: (j, i, 0))
    vec = pl.BlockSpec((1, d), lambda i, j: (0, 0))
    chunks = jax.ShapeDtypeStruct((FF_CHUNKS, n, nn), BF16)
    return _hosted_call(
        body, comm,
        name="ffn_bwd",
        grid=(n // tm, FF_CHUNKS),
        in_specs=[
            tok, tok, vec, chunk, chunk,
            pl.BlockSpec((1, 1, d, nn), lambda i, j: (j, layer, 0, 0)),
            pl.BlockSpec((1, 1, d, nn), lambda i, j: (j + FF_CHUNKS, layer, 0, 0)),
            pl.BlockSpec((1, nn, d), lambda i, j: (layer, j, 0)),
        ],
        out_specs=[tok, vec, tok, tok, chunk, chunk, chunk],
        out_shape=[
            jax.ShapeDtypeStruct((n, d), F32),
            jax.ShapeDtypeStruct((1, d), F32),
            jax.ShapeDtypeStruct((n, d), BF16),
            jax.ShapeDtypeStruct((n, d), BF16),
            chunks, chunks, chunks,
        ],
        scratch_shapes=[pltpu.VMEM((tm, d), BF16), pltpu.VMEM((tm, d), F32)],
        args=(dy, x, g, a, b, w_up, w_up, w_down),
    )


def wgrad(a, b, name, out4, mode, *, g0=0, layer=0, prev=None, a_cols=None, dtype=BF16, comm=None):
    a3 = a if a.ndim == 3 else a[None]
    b3 = b if b.ndim == 3 else b[None]
    sa, n, ka = a3.shape
    sb, _, kb = b3.shape
    a0 = 0
    if a_cols is not None:
        a0, ka = a_cols
    k1, k2 = sa * ka, sb * kb
    depth, groups, rr, cc = out4
    t1 = _tile(math.gcd(ka, rr), 1024)
    t2 = _tile(math.gcd(kb, cc), 1024)
    tn = _tile(n, 1024, 8)
    gpb = 1
    if mode == "col":
        assert rr == k1 and k2 % cc == 0 and g0 + k2 // cc <= groups
        if kb % cc == 0 and cc < kb <= 1024 and g0 % (kb // cc) == 0:
            t2, gpb = kb, kb // cc
        per = max(cc // t2, 1)
        oblock = (1, gpb, t1, min(t2, cc))
        omap = lambda i, j, k: (layer, (g0 + j // per) // gpb, i, j % per)
    else:
        assert cc == k2 and k1 % rr == 0 and g0 + k1 // rr <= groups
        if ka % rr == 0 and rr < ka <= 1024 and g0 % (ka // rr) == 0:
            t1, gpb = ka, ka // rr
        per = max(rr // t1, 1)
        oblock = (1, gpb, min(t1, rr), t2)
        omap = lambda i, j, k: (layer, (g0 + i // per) // gpb, i % per, j)
    na, nb = ka // t1, kb // t2
    nk = n // tn

    def body(a_ref, b_ref, *rest):
        o_ref, acc_ref = rest[-2], rest[-1]
        k = pl.program_id(2)

        @pl.when(k == 0)
        def _():
            acc_ref[...] = jnp.zeros_like(acc_ref)

        acc_ref[...] += _dot_tn(a_ref[0].astype(BF16), b_ref[0].astype(BF16))

        @pl.when(k == nk - 1)
        def _():
            for g in range(gpb):
                if gpb == 1:
                    o_ref[0, 0] = acc_ref[...].astype(dtype)
                elif mode == "col":
                    o_ref[0, g] = acc_ref[:, g * cc:(g + 1) * cc].astype(dtype)
                else:
                    o_ref[0, g] = acc_ref[g * rr:(g + 1) * rr, :].astype(dtype)

    in_specs = [
        pl.BlockSpec((1, tn, t1), lambda i, j, k: (i // na, k, a0 // t1 + i % na)),
        pl.BlockSpec((1, tn, t2), lambda i, j, k: (j // nb, k, j % nb)),
    ]
    args = [a3, b3]
    if comm is not None:
        assert prev is None
        return _hosted_call(body, comm, name=name, grid=(k1 // t1, k2 // t2, nk), in_specs=in_specs,
                            out_specs=[pl.BlockSpec(oblock, omap)], out_shape=[jax.ShapeDtypeStruct(out4, dtype)],
                            scratch_shapes=[pltpu.VMEM((t1, t2), F32)], args=args)
    extra = {}
    if prev is not None:
        in_specs.append(pl.BlockSpec(memory_space=pl.ANY))
        args.append(prev)
        extra["input_output_aliases"] = {2: 0}
    return _call(
        body,
        name=name,
        grid=(k1 // t1, k2 // t2, nk),
        in_specs=in_specs,
        out_specs=pl.BlockSpec(oblock, omap),
        out_shape=jax.ShapeDtypeStruct(out4, dtype),
        scratch_shapes=[pltpu.VMEM((t1, t2), F32)],
        compiler_params=_params(),
        **extra,
    )(*args)


def proj_fwd(x, g, w, layer, comm=None):
    n, d = x.shape
    c = w.shape[2]
    tm, tc = _tile(n, 1024, 8), _tile(c, 768)

    def body(x_ref, g_ref, w_ref, o_ref, hn_ref):
        @pl.when(pl.program_id(1) == 0)
        def _():
            h, _, _ = _rms_fwd(x_ref[...], g_ref[...])
            hn_ref[...] = h.astype(BF16)

        o_ref[...] = _dot(hn_ref[...], w_ref[0])

    return _hosted_call(
        body, comm,
        name="proj_fwd",
        grid=(n // tm, c // tc),
        in_specs=[
            pl.BlockSpec((tm, d), lambda i, j: (i, 0)),
            pl.BlockSpec((1, d), lambda i, j: (0, 0)),
            pl.BlockSpec((1, d, tc), lambda i, j: (layer, 0, j)),
        ],
        out_specs=[pl.BlockSpec((tm, tc), lambda i, j: (i, j))],
        out_shape=[jax.ShapeDtypeStruct((n, c), F32)],
        scratch_shapes=[pltpu.VMEM((tm, d), BF16)],
        args=(x, g, w),
    )


def proj_bwd(dproj, dres, x, g, w, layer):
    n, d = x.shape
    c = w.shape[2]
    tm = _tile(n, 512, 8)

    def body(dp_ref, dr_ref, x_ref, g_ref, w_ref, dx_ref, dg_ref, hn_ref):
        dh = _dot_nt(dp_ref[...], w_ref[0])
        gg = g_ref[...]
        h, xhat, rs = _rms_fwd(x_ref[...], gg)
        hn_ref[...] = h.astype(BF16)
        dxn, dgr = _rms_bwd(dh, xhat, rs, gg)
        dx_ref[...] = dr_ref[...] + dxn
        _acc_out(dg_ref, jnp.sum(dgr, axis=0, keepdims=True), pl.program_id(0) == 0)

    tok = pl.BlockSpec((tm, d), lambda i: (i, 0))
    vec = pl.BlockSpec((1, d), lambda i: (0, 0))
    return _call(
        body,
        name="proj_bwd",
        grid=(n // tm,),
        in_specs=[
            pl.BlockSpec((tm, c), lambda i: (i, 0)), tok, tok, vec,
            pl.BlockSpec((1, d, c), lambda i: (layer, 0, 0), pipeline_mode=pl.Buffered(1)),
        ],
        out_specs=[tok, vec, tok],
        out_shape=[
            jax.ShapeDtypeStruct((n, d), F32),
            jax.ShapeDtypeStruct((1, d), F32),
            jax.ShapeDtypeStruct((n, d), BF16),
        ],
        compiler_params=_params(),
    )(dproj, dres, x, g, w)


S5_TS = 512


def _cmul(ar, ai, br, bi):
    return ar * br - ai * bi, ar * bi + ai * br


def _s5_tables(ar, ai, reverse):
    gp = ar.shape[1]
    if reverse:
        ai = -ai
    a1r, a1i = jnp.broadcast_to(ar, (8, gp)), jnp.broadcast_to(ai, (8, gp))
    a2r, a2i = _cmul(a1r, a1i, a1r, a1i)
    a4r, a4i = _cmul(a2r, a2i, a2r, a2i)
    a8r, a8i = _cmul(a4r, a4i, a4r, a4i)
    row = lax.broadcasted_iota(jnp.int32, (8, gp), 0)
    e = (8 - row) if reverse else (row + 1)
    pr, pi = jnp.ones((8, gp), F32), jnp.zeros((8, gp), F32)
    for bit, (fr, fi) in ((1, (a1r, a1i)), (2, (a2r, a2i)), (4, (a4r, a4i)), (8, (a8r, a8i))):
        nr, ni = _cmul(pr, pi, fr, fi)
        on = (e & bit) != 0
        pr, pi = jnp.where(on, nr, pr), jnp.where(on, ni, pi)
    return (a1r, a1i, a2r, a2i, a4r, a4i, pr, pi)


def _s5_scan(xr_ref, xi_ref, tab_ref, cr_ref, ci_ref, ts, reverse):
    gp = xr_ref.shape[1]
    nt = ts // 8
    row = lax.broadcasted_iota(jnp.int32, (8, gp), 0)

    def shifted(v, s):
        if reverse:
            return jnp.where(row < 8 - s, pltpu.roll(v, 8 - s, 0), 0.0)
        return jnp.where(row >= s, pltpu.roll(v, s, 0), 0.0)

    def step(k, carry):
        cr, ci = carry
        t = (nt - 1 - k) if reverse else k
        r0 = pl.multiple_of(t * 8, 8)
        br = xr_ref[pl.ds(r0, 8), :]
        bi = xi_ref[pl.ds(r0, 8), :]
        for q, s in enumerate((1, 2, 4)):
            fr, fi = tab_ref[2 * q], tab_ref[2 * q + 1]
            sr, si = shifted(br, s), shifted(bi, s)
            mr, mi = _cmul(fr, fi, sr, si)
            br, bi = br + mr, bi + mi
        mr, mi = _cmul(tab_ref[6], tab_ref[7], cr, ci)
        br, bi = br + mr, bi + mi
        xr_ref[pl.ds(r0, 8), :] = br
        xi_ref[pl.ds(r0, 8), :] = bi
        edge = 0 if reverse else 7
        return (jnp.broadcast_to(br[edge:edge + 1, :], (8, gp)),
                jnp.broadcast_to(bi[edge:edge + 1, :], (8, gp)))

    cr, ci = lax.fori_loop(0, nt, step, (cr_ref[...], ci_ref[...]), unroll=2)
    cr_ref[...] = cr
    ci_ref[...] = ci


def s5_fwd(proj, ucol, seq, ar, ai, bm_r, bm_i, cm_r, cm_i, dskip, w_glu, layer, comm=None):
    n = proj.shape[0]
    ds, gp = bm_r.shape
    ts = min(S5_TS, seq)
    nt = seq // ts

    def body(u_ref, ar_ref, ai_ref, bmr_ref, bmi_ref, cmr_ref, cmi_ref, d_ref, wg_ref,
             out_ref, xr_ref, xi_ref, yp_ref, tab_ref, cr_ref, ci_ref):
        @pl.when(pl.program_id(1) == 0)
        def _():
            for q, v in enumerate(_s5_tables(ar_ref[...], ai_ref[...], False)):
                tab_ref[q] = v
            cr_ref[...] = jnp.zeros_like(cr_ref)
            ci_ref[...] = jnp.zeros_like(ci_ref)

        u = u_ref[...]
        ub = u.astype(BF16)
        xr_ref[...] = _dot(ub, bmr_ref[...])
        xi_ref[...] = _dot(ub, bmi_ref[...])
        _s5_scan(xr_ref, xi_ref, tab_ref, cr_ref, ci_ref, ts, False)
        y = (_dot(xr_ref[...].astype(BF16), cmr_ref[...]) - _dot(xi_ref[...].astype(BF16), cmi_ref[...])
             + d_ref[...] * u)
        yp_ref[...] = y
        z = _dot(_gelu(y).astype(BF16), wg_ref[0])
        out_ref[...] = z[:, :ds] * _sig(z[:, ds:])

    full = lambda shape: pl.BlockSpec(shape, lambda b, t: (0, 0))
    return _hosted_call(
        body, comm,
        name="s5_fwd",
        grid=(n // seq, nt),
        in_specs=[
            pl.BlockSpec((ts, ds), lambda b, t: (b * nt + t, ucol)),
            full((1, gp)), full((1, gp)), full((ds, gp)), full((ds, gp)), full((gp, ds)), full((gp, ds)),
            full((1, ds)), pl.BlockSpec((1, ds, 2 * ds), lambda b, t: (layer, 0, 0)),
        ],
        out_specs=[
            pl.BlockSpec((ts, ds), lambda b, t: (b * nt + t, 0)),
            pl.BlockSpec((ts, gp), lambda b, t: (b * nt + t, 0)),
            pl.BlockSpec((ts, gp), lambda b, t: (b * nt + t, 0)),
            pl.BlockSpec((ts, ds), lambda b, t: (b * nt + t, 0)),
        ],
        out_shape=[
            jax.ShapeDtypeStruct((n, ds), F32),
            jax.ShapeDtypeStruct((n, gp), F32),
            jax.ShapeDtypeStruct((n, gp), F32),
            jax.ShapeDtypeStruct((n, ds), F32),
        ],
        scratch_shapes=[pltpu.VMEM((8, 8, gp), F32), pltpu.VMEM((8, gp), F32), pltpu.VMEM((8, gp), F32)],
        args=(proj, ar, ai, bm_r, bm_i, cm_r, cm_i, dskip, w_glu),
    )


def s5_bwd(dout, ypre, proj, ucol, xr, xi, seq, ar, ai, bm_r, bm_i, cm_r, cm_i, dskip, w_glu, layer):
    n = proj.shape[0]
    ds, gp = bm_r.shape
    ts = min(S5_TS, seq)
    nt = seq // ts

    def body(do_ref, yp_ref, u_ref, xr_ref, xi_ref, hr_ref, hi_ref, ar_ref, ai_ref, bmr_ref, bmi_ref,
             cmr_ref, cmi_ref, d_ref, wg_ref,
             du_ref, gr_ref, gi_ref, dyb_ref, glb_ref, dzb_ref, dar_ref, dai_ref, dd_ref,
             tab_ref, cr_ref, ci_ref):
        b, t = pl.program_id(0), pl.program_id(1)
        first = jnp.logical_and(b == 0, t == 0)

        @pl.when(t == 0)
        def _():
            for q, v in enumerate(_s5_tables(ar_ref[...], ai_ref[...], True)):
                tab_ref[q] = v
            cr_ref[...] = jnp.zeros_like(cr_ref)
            ci_ref[...] = jnp.zeros_like(ci_ref)

        yp = yp_ref[...]
        u = u_ref[...]
        gl = _gelu(yp).astype(BF16)
        glb_ref[...] = gl
        z = _dot(gl, wg_ref[0])
        za, sg = z[:, :ds], _sig(z[:, ds:])
        do = do_ref[...]
        da = (do * sg).astype(BF16)
        dg = (do * za * sg * (1.0 - sg)).astype(BF16)
        dzb_ref[:, :ds] = da
        dzb_ref[:, ds:] = dg
        dgl = _dot_nt(da, wg_ref[0, :, :ds]) + _dot_nt(dg, wg_ref[0, :, ds:])
        dyp = dgl * _gelu_grad(yp)
        dypb = dyp.astype(BF16)
        dyb_ref[...] = dypb
        _acc_out(dd_ref, jnp.sum(dyp * u, axis=0, keepdims=True), first)

        gr_ref[...] = _dot_nt(dypb, cmr_ref[...])
        gi_ref[...] = -_dot_nt(dypb, cmi_ref[...])
        _s5_scan(gr_ref, gi_ref, tab_ref, cr_ref, ci_ref, ts, True)
        gr, gi = gr_ref[...], gi_ref[...]
        du_ref[...] = d_ref[...] * dyp + _dot_nt(gr.astype(BF16), bmr_ref[...]) + _dot_nt(gi.astype(BF16), bmi_ref[...])

        row = lax.broadcasted_iota(jnp.int32, (ts, gp), 0)
        live = jnp.where(t == nt - 1, 0.0, 1.0)
        pr = jnp.broadcast_to(hr_ref[7:8, :] * live, (ts, gp))
        pi = jnp.broadcast_to(hi_ref[7:8, :] * live, (ts, gp))
        sr = jnp.where(row == 0, pr, pltpu.roll(xr_ref[...], 1, 0))
        si = jnp.where(row == 0, pi, pltpu.roll(xi_ref[...], 1, 0))
        _acc_out(dar_ref, jnp.sum(gr * sr + gi * si, axis=0, keepdims=True), first)
        _acc_out(dai_ref, jnp.sum(gi * sr - gr * si, axis=0, keepdims=True), first)

    full = lambda shape: pl.BlockSpec(shape, lambda b, t: (0, 0))
    blk = lambda w, col=0: pl.BlockSpec((ts, w), lambda b, t: (b * nt + nt - 1 - t, col))
    halo = pl.BlockSpec((8, gp), lambda b, t: (jnp.maximum((b * seq + (nt - 1 - t) * ts) // 8 - 1, 0), 0))
    return _call(
        body,
        name="s5_bwd",
        grid=(n // seq, nt),
        in_specs=[
            blk(ds), blk(ds), blk(ds, ucol), blk(gp), blk(gp), halo, halo,
            full((1, gp)), full((1, gp)), full((ds, gp)), full((ds, gp)), full((gp, ds)), full((gp, ds)),
            full((1, ds)), pl.BlockSpec((1, ds, 2 * ds), lambda b, t: (layer, 0, 0)),
        ],
        out_specs=[
            blk(ds), blk(gp), blk(gp), blk(ds), blk(ds), blk(2 * ds),
            full((1, gp)), full((1, gp)), full((1, ds)),
        ],
        out_shape=[
            jax.ShapeDtypeStruct((n, ds), F32),
            jax.ShapeDtypeStruct((n, gp), F32),
            jax.ShapeDtypeStruct((n, gp), F32),
            jax.ShapeDtypeStruct((n, ds), BF16),
            jax.ShapeDtypeStruct((n, ds), BF16),
            jax.ShapeDtypeStruct((n, 2 * ds), BF16),
            jax.ShapeDtypeStruct((1, gp), F32),
            jax.ShapeDtypeStruct((1, gp), F32),
            jax.ShapeDtypeStruct((1, ds), F32),
        ],
        scratch_shapes=[pltpu.VMEM((8, 8, gp), F32), pltpu.VMEM((8, gp), F32), pltpu.VMEM((8, gp), F32)],
        compiler_params=_params(),
    )(dout, ypre, proj, xr, xi, xr, xi, ar, ai, bm_r, bm_i, cm_r, cm_i, dskip, w_glu)


BAND = QBLK + CHUNK
NCH = QBLK // CHUNK


PAIR = 2 * HEAD


def _attn_specs(nq):
    last = nq - 1
    cur = lambda col0=0: pl.BlockSpec(
        (QBLK, PAIR), lambda p, b, i: (b * nq + jnp.minimum(i, last), col0 // PAIR + p))
    prev = lambda col0=0: pl.BlockSpec(
        (QBLK, PAIR), lambda p, b, i: (b * nq + jnp.maximum(jnp.minimum(i, last) - 1, 0), col0 // PAIR + p))
    vec = pl.BlockSpec((2, 1, 2 * QBLK), lambda p, b, i: (p, 0, 0))
    gain = pl.BlockSpec((1, PAIR), lambda p, b, i: (0, 0))
    return cur, prev, vec, gain


def _pair_masks():
    lane = lax.broadcasted_iota(jnp.int32, (1, PAIR), 1)
    return lane < HEAD, [(lane < HEAD).astype(F32), (lane >= HEAD).astype(F32)]


def _pair_mean(t, low, m0):
    s0 = jnp.sum(t * m0, axis=-1, keepdims=True)
    s1 = jnp.sum(t, axis=-1, keepdims=True) - s0
    return jnp.where(low, s0, s1) * (1.0 / HEAD)


def _pair_rms_fwd(x, g, low, m0):
    rs = lax.rsqrt(_pair_mean(x * x, low, m0) + EPS)
    xhat = x * rs
    return xhat * g, xhat, rs


def _pair_rms_bwd(dh, xhat, rs, g, low, m0):
    dxh = dh * g
    dx = rs * (dxh - xhat * _pair_mean(dxh * xhat, low, m0))
    return dx, dh * xhat


def _attn_build_table(tv, bias_ref, tab_ref):
    w = 2 * QBLK
    for qi in range(CHUNK):
        bias_ref[qi:qi + 1, :] = pltpu.roll(tv, (qi - (CHUNK - 1)) % w, 1)
    bias = bias_ref[...]
    lane = lax.broadcasted_iota(jnp.int32, (CHUNK, w), 1)
    for c in range(NCH):
        rolled = bias if c == 0 else pltpu.roll(bias, CHUNK * c, 1)
        ok = jnp.logical_and(lane >= CHUNK * c, lane < CHUNK * c + BAND)
        tab_ref[CHUNK * c:CHUNK * (c + 1), :] = jnp.where(ok, rolled, NEG)


def _attn_reduce_table(dt_ref, bias_ref):
    w = 2 * QBLK
    acc = dt_ref[0:CHUNK, :]
    for c in range(1, NCH):
        acc = acc + pltpu.roll(dt_ref[CHUNK * c:CHUNK * (c + 1), :], w - CHUNK * c, 1)
    bias_ref[...] = acc
    out = jnp.zeros((1, w), F32)
    for qi in range(CHUNK):
        out = out + pltpu.roll(bias_ref[qi:qi + 1, :], ((CHUNK - 1) - qi) % w, 1)
    return out


HALF = QBLK // 2
KSPAN = QBLK + HALF


def _attn_softmax(s, table, first_block, n_prev):
    s = s * (HEAD ** -0.5) + table
    col = lax.broadcasted_iota(jnp.int32, s.shape, 1)
    s = jnp.where(jnp.logical_and(first_block, col < n_prev), NEG, s)
    e = jnp.exp(s - jnp.max(s, axis=-1, keepdims=True))
    return e * (1.0 / jnp.sum(e, axis=-1, keepdims=True))


def attn_fwd(proj, q0, k0, v0, tv, gq, gk, seq, comm=None):
    n = proj.shape[0]
    heads = tv.shape[0]
    nq = seq // QBLK
    cur, prev, vec, gain = _attn_specs(nq)

    def body(q_ref, kp_ref, kc_ref, vp_ref, vc_ref, tv_ref, gq_ref, gk_ref, o_ref, bias_ref, tab_ref):
        @pl.when(jnp.logical_and(pl.program_id(1) == 0, pl.program_id(2) == 0))
        def _():
            for h in range(2):
                _attn_build_table(tv_ref[h], bias_ref, tab_ref.at[h])

        low, m = _pair_masks()
        qn, _, _ = _pair_rms_fwd(q_ref[...], gq_ref[...], low, m[0])
        kn, _, _ = _pair_rms_fwd(jnp.concatenate([kp_ref[...], kc_ref[...]], axis=0), gk_ref[...], low, m[0])
        knb = kn.astype(BF16)
        v = jnp.concatenate([vp_ref[...], vc_ref[...]], axis=0)
        first_block = pl.program_id(2) == 0
        for r in range(2):
            rows, cols = slice(r * HALF, (r + 1) * HALF), slice(r * HALF, r * HALF + KSPAN)
            o = jnp.zeros((HALF, PAIR), F32)
            for h in range(2):
                s = _dot_nt((qn[rows] * m[h]).astype(BF16), knb[cols])
                p = _attn_softmax(s, tab_ref[h, rows, cols], first_block, QBLK - r * HALF)
                o = o + _dot(p.astype(BF16), (v[cols] * m[h]).astype(BF16))
            o_ref[rows, :] = o

    return _hosted_call(
        body, comm,
        name="attn_fwd",
        grid=(heads // 2, n // seq, nq),
        in_specs=[cur(q0), prev(k0), cur(k0), prev(v0), cur(v0), vec, gain, gain],
        out_specs=[cur()],
        out_shape=[jax.ShapeDtypeStruct((n, heads * HEAD), F32)],
        scratch_shapes=[pltpu.VMEM((CHUNK, 2 * QBLK), F32), pltpu.VMEM((2, QBLK, 2 * QBLK), F32)],
        args=(proj, proj, proj, proj, proj, tv, gq, gk),
    )


def attn_bwd(do, proj, q0, k0, v0, tv, gq, gk, seq, comm=None):
    n = proj.shape[0]
    heads = tv.shape[0]
    nb = n // seq
    nq = seq // QBLK
    cur, prev, vec, gain = _attn_specs(nq)
    keyblk = pl.BlockSpec((QBLK, PAIR), lambda p, b, i: (b * nq + jnp.maximum(i - 1, 0), p))

    def body(do_ref, q_ref, kp_ref, kc_ref, vp_ref, vc_ref, tv_ref, gq_ref, gk_ref,
             dq_ref, dk_ref, dv_ref, dtv_ref, dgq_ref, dgk_ref, bias_ref, tab_ref, dt_ref, ck_ref, cv_ref,
             dqn_ref, dkn_ref, dvv_ref):
        pp, b, i = pl.program_id(0), pl.program_id(1), pl.program_id(2)
        head_start = jnp.logical_and(b == 0, i == 0)

        @pl.when(head_start)
        def _():
            for h in range(2):
                _attn_build_table(tv_ref[h], bias_ref, tab_ref.at[h])
            dt_ref[...] = jnp.zeros_like(dt_ref)

        @pl.when(i < nq)
        def _():
            low, m = _pair_masks()
            gq_, gk_ = gq_ref[...], gk_ref[...]
            qn, qhat, qrs = _pair_rms_fwd(q_ref[...], gq_, low, m[0])
            kn, khat, krs = _pair_rms_fwd(jnp.concatenate([kp_ref[...], kc_ref[...]], axis=0), gk_, low, m[0])
            knb = kn.astype(BF16)
            vb = jnp.concatenate([vp_ref[...], vc_ref[...]], axis=0).astype(BF16)
            do_ = do_ref[...]
            dkn_ref[...] = jnp.zeros_like(dkn_ref)
            dvv_ref[...] = jnp.zeros_like(dvv_ref)
            for r in range(2):
                rows, cols = slice(r * HALF, (r + 1) * HALF), slice(r * HALF, r * HALF + KSPAN)
                dqn = jnp.zeros((HALF, PAIR), F32)
                for h in range(2):
                    qh = (qn[rows] * m[h]).astype(BF16)
                    doh = (do_[rows] * m[h]).astype(BF16)
                    p = _attn_softmax(_dot_nt(qh, knb[cols]), tab_ref[h, rows, cols], i == 0, QBLK - r * HALF)
                    dvv_ref[cols, :] += _dot_tn(p.astype(BF16), doh)
                    dp = _dot_nt(doh, vb[cols])
                    ds = p * (dp - jnp.sum(p * dp, axis=-1, keepdims=True))
                    dt_ref[h, rows, cols] += ds
                    dsb = (ds * (HEAD ** -0.5)).astype(BF16)
                    dqn = dqn + _dot(dsb, (kn[cols] * m[h]).astype(BF16))
                    dkn_ref[cols, :] += _dot_tn(dsb, qh)
                dqn_ref[rows, :] = dqn
            dv = dvv_ref[...]
            dq, dgq_rows = _pair_rms_bwd(dqn_ref[...], qhat, qrs, gq_, low, m[0])
            dk, dgk_rows = _pair_rms_bwd(dkn_ref[...], khat, krs, gk_, low, m[0])
            dq_ref[...] = dq

            @pl.when(i == 0)
            def _():
                dk_ref[...] = dk[:QBLK]
                dv_ref[...] = dv[:QBLK]

            @pl.when(i > 0)
            def _():
                dk_ref[...] = ck_ref[...] + dk[:QBLK]
                dv_ref[...] = cv_ref[...] + dv[:QBLK]

            ck_ref[...] = dk[QBLK:]
            cv_ref[...] = dv[QBLK:]
            first = jnp.logical_and(pp == 0, head_start)
            _acc_out(dgq_ref, jnp.sum(dgq_rows, axis=0, keepdims=True), first)
            _acc_out(dgk_ref, jnp.sum(dgk_rows, axis=0, keepdims=True), first)

        @pl.when(i == nq)
        def _():
            dk_ref[...] = ck_ref[...]
            dv_ref[...] = cv_ref[...]

        @pl.when(jnp.logical_and(b == nb - 1, i == nq))
        def _():
            for h in range(2):
                dtv_ref[h] = _attn_reduce_table(dt_ref.at[h], bias_ref)

    tok = jax.ShapeDtypeStruct((n, heads * HEAD), F32)
    return _hosted_call(
        body, comm,
        name="attn_bwd",
        grid=(heads // 2, nb, nq + 1),
        in_specs=[cur(), cur(q0), prev(k0), cur(k0), prev(v0), cur(v0), vec, gain, gain],
        out_specs=[cur(), keyblk, keyblk, vec, gain, gain],
        out_shape=[tok, tok, tok, jax.ShapeDtypeStruct(tv.shape, F32),
                   jax.ShapeDtypeStruct((1, PAIR), F32), jax.ShapeDtypeStruct((1, PAIR), F32)],
        scratch_shapes=[pltpu.VMEM((CHUNK, 2 * QBLK), F32), pltpu.VMEM((2, QBLK, 2 * QBLK), F32),
                        pltpu.VMEM((2, QBLK, 2 * QBLK), F32), pltpu.VMEM((QBLK, PAIR), F32),
                        pltpu.VMEM((QBLK, PAIR), F32), pltpu.VMEM((QBLK, PAIR), F32),
                        pltpu.VMEM((2 * QBLK, PAIR), F32), pltpu.VMEM((2 * QBLK, PAIR), F32)],
        args=(do, proj, proj, proj, proj, proj, tv, gq, gk),
    )


CONV_TC = 512


def _ln_fwd(h1, g, b):
    mu = jnp.mean(h1, axis=-1, keepdims=True)
    xc = h1 - mu
    rs = lax.rsqrt(jnp.mean(xc * xc, axis=-1, keepdims=True) + EPS)
    yhat = xc * rs
    return yhat * g + b, yhat, rs


def _glu(z, dc):
    return z[:, :dc] * _sig(z[:, dc:])


def conv_fwd(proj, zcol, seq, w, bdw, lng, lnb):
    n = proj.shape[0]
    dc = w.shape[1]
    tc = min(CONV_TC, seq)
    nt = seq // tc

    def body(z_ref, zp_ref, w_ref, b_ref, g_ref, lb_ref, h1_ref, o_ref, ext_ref):
        live = jnp.where(pl.program_id(1) == 0, 0.0, 1.0)
        ext_ref[pl.ds(0, HALO), :] = _glu(zp_ref[...], dc) * live
        ext_ref[pl.ds(HALO, tc), :] = _glu(z_ref[...], dc)
        acc = jnp.zeros((tc, dc), F32) + b_ref[...]
        for j in range(CONV_W):
            acc = acc + w_ref[j:j + 1, :] * ext_ref[pl.ds(HALO - (CONV_W - 1) + j, tc), :]
        h1_ref[...] = acc
        ln, _, _ = _ln_fwd(acc, g_ref[...], lb_ref[...])
        o_ref[...] = ln * _sig(ln)

    full = lambda shape: pl.BlockSpec(shape, lambda b, t: (0, 0))
    return _call(
        body,
        name="conv_fwd",
        grid=(n // seq, nt),
        in_specs=[
            pl.BlockSpec((tc, 2 * dc), lambda b, t: (b * nt + t, zcol)),
            pl.BlockSpec((HALO, 2 * dc), lambda b, t: (jnp.maximum((b * seq + t * tc) // HALO - 1, 0), zcol)),
            full((HALO, dc)), full((1, dc)), full((1, dc)), full((1, dc)),
        ],
        out_specs=[
            pl.BlockSpec((tc, dc), lambda b, t: (b * nt + t, 0)),
            pl.BlockSpec((tc, dc), lambda b, t: (b * nt + t, 0)),
        ],
        out_shape=[jax.ShapeDtypeStruct((n, dc), F32), jax.ShapeDtypeStruct((n, dc), F32)],
        scratch_shapes=[pltpu.VMEM((tc + HALO, dc), F32)],
        compiler_params=_params(),
    )(proj, proj, w, bdw, lng, lnb)


def conv_bwd(dco, h1, proj, zcol, seq, w, bdw, lng, lnb):
    n = proj.shape[0]
    dc = w.shape[1]
    tc = min(CONV_TC, seq)
    nt = seq // tc
    nrow = n // HALO

    def body(do_ref, don_ref, h1_ref, h1n_ref, z_ref, zp_ref, w_ref, g_ref, lb_ref,
             dz_ref, dw_ref, db_ref, dg_ref, dlb_ref, ext_ref, dext_ref):
        b, t = pl.program_id(0), pl.program_id(1)
        first = jnp.logical_and(b == 0, t == 0)
        g, lb = g_ref[...], lb_ref[...]

        def dh1_of(do, h1):
            ln, yhat, rs = _ln_fwd(h1, g, lb)
            s = _sig(ln)
            dln = do * (s * (1.0 + ln * (1.0 - s)))
            dyh = dln * g
            dh1 = rs * (dyh - jnp.mean(dyh, axis=-1, keepdims=True)
                        - yhat * jnp.mean(dyh * yhat, axis=-1, keepdims=True))
            return dh1, dln, yhat

        dh1, dln, yhat = dh1_of(do_ref[...], h1_ref[...])
        dh1n, _, _ = dh1_of(don_ref[...], h1n_ref[...])
        _acc_out(dg_ref, jnp.sum(dln * yhat, axis=0, keepdims=True), first)
        _acc_out(dlb_ref, jnp.sum(dln, axis=0, keepdims=True), first)
        _acc_out(db_ref, jnp.sum(dh1, axis=0, keepdims=True), first)

        dext_ref[pl.ds(0, tc), :] = dh1
        dext_ref[pl.ds(tc, HALO), :] = dh1n * jnp.where(t == nt - 1, 0.0, 1.0)
        z = z_ref[...]
        ext_ref[pl.ds(0, HALO), :] = _glu(zp_ref[...], dc) * jnp.where(t == 0, 0.0, 1.0)
        ext_ref[pl.ds(HALO, tc), :] = _glu(z, dc)

        @pl.when(first)
        def _():
            dw_ref[...] = jnp.zeros_like(dw_ref)

        dh0 = jnp.zeros((tc, dc), F32)
        for j in range(CONV_W):
            dh0 = dh0 + w_ref[j:j + 1, :] * dext_ref[pl.ds(CONV_W - 1 - j, tc), :]
            dw_ref[j:j + 1, :] += jnp.sum(dh1 * ext_ref[pl.ds(HALO - (CONV_W - 1) + j, tc), :],
                                          axis=0, keepdims=True)
        za, sg = z[:, :dc], _sig(z[:, dc:])
        dz_ref[:, :dc] = dh0 * sg
        dz_ref[:, dc:] = dh0 * za * sg * (1.0 - sg)

    full = lambda shape: pl.BlockSpec(shape, lambda b, t: (0, 0))
    cur = lambda wd, col=0: pl.BlockSpec((tc, wd), lambda b, t: (b * nt + t, col))
    nxt = pl.BlockSpec((HALO, dc), lambda b, t: (jnp.minimum((b * seq + (t + 1) * tc) // HALO, nrow - 1), 0))
    return _call(
        body,
        name="conv_bwd",
        grid=(n // seq, nt),
        in_specs=[
            cur(dc), nxt, cur(dc), nxt, cur(2 * dc, zcol),
            pl.BlockSpec((HALO, 2 * dc), lambda b, t: (jnp.maximum((b * seq + t * tc) // HALO - 1, 0), zcol)),
            full((HALO, dc)), full((1, dc)), full((1, dc)),
        ],
        out_specs=[cur(2 * dc), full((HALO, dc)), full((1, dc)), full((1, dc)), full((1, dc))],
        out_shape=[
            jax.ShapeDtypeStruct((n, 2 * dc), F32),
            jax.ShapeDtypeStruct((HALO, dc), F32),
            jax.ShapeDtypeStruct((1, dc), F32),
            jax.ShapeDtypeStruct((1, dc), F32),
            jax.ShapeDtypeStruct((1, dc), F32),
        ],
        scratch_shapes=[pltpu.VMEM((tc + HALO, dc), F32), pltpu.VMEM((tc + HALO, dc), F32)],
        compiler_params=_params(),
    )(dco, dco, h1, h1, proj, proj, w, lng, lnb)


def _merge_common(l0, l1, l2, bg, so, ao, co, wbs, wba, wbc, d):
    ys = _dot(so.astype(BF16), wbs)
    ya = _dot(ao.astype(BF16), wba)
    yc = _dot(co.astype(BF16), wbc)
    gs = _sig(l0 + bg[:, :d])
    ga = _sig(l1 + bg[:, d:2 * d])
    gc = _sig(l2 + bg[:, 2 * d:])
    return (ys, ya, yc), (gs, ga, gc)


def _merge_specs(tm, d, dss, da, dc, layer):
    row = lambda w, col=0: pl.BlockSpec((tm, w), lambda i: (i, col))
    full = lambda r, c: pl.BlockSpec((r, c), lambda i: (0, 0))
    stacked = lambda r: pl.BlockSpec((1, r, d), lambda i: (layer, 0, 0))
    acts = [row(d, 0), row(d, 1), row(d, 2), full(1, 3 * d), row(dss), row(da), row(dc)]
    weights = [stacked(dss), stacked(da), stacked(dc), stacked(d)]
    return row, full, acts, weights


def merge_fwd(x, proj, bg, so, ao, co, wbs, wba, wbc, wout, layer):
    n, d = x.shape
    tm = _tile(n, 256, 8)
    row, full, acts, weights = _merge_specs(tm, d, so.shape[1], ao.shape[1], co.shape[1], layer)

    def body(x_ref, l0_ref, l1_ref, l2_ref, bg_ref, so_ref, ao_ref, co_ref,
             wbs_ref, wba_ref, wbc_ref, wo_ref, o_ref):
        (ys, ya, yc), (gs, ga, gc) = _merge_common(
            l0_ref[...], l1_ref[...], l2_ref[...], bg_ref[...], so_ref[...], ao_ref[...], co_ref[...],
            wbs_ref[0], wba_ref[0], wbc_ref[0], d)
        merged = gs * ys + ga * ya + gc * yc
        o_ref[...] = x_ref[...] + _dot(merged.astype(BF16), wo_ref[0])

    return _call(
        body,
        name="merge_fwd",
        grid=(n // tm,),
        in_specs=[row(d)] + acts + weights,
        out_specs=row(d),
        out_shape=jax.ShapeDtypeStruct((n, d), F32),
        compiler_params=_params(),
    )(x, proj, proj, proj, bg, so, ao, co, wbs, wba, wbc, wout)


def merge_bwd(dx, proj, bg, so, ao, co, wbs, wba, wbc, wout, layer):
    n, d = dx.shape
    dss, da, dc = so.shape[1], ao.shape[1], co.shape[1]
    tm = _tile(n, 256, 8)
    row, full, acts, weights = _merge_specs(tm, d, dss, da, dc, layer)

    def body(dx_ref, l0_ref, l1_ref, l2_ref, bg_ref, so_ref, ao_ref, co_ref, wbs_ref, wba_ref, wbc_ref, wo_ref,
             dl_ref, dso_ref, dao_ref, dco_ref, dbg_ref, mg_ref, dxb_ref, dys_ref, dya_ref, dyc_ref):
        wbs, wba, wbc = wbs_ref[0], wba_ref[0], wbc_ref[0]
        (ys, ya, yc), (gs, ga, gc) = _merge_common(
            l0_ref[...], l1_ref[...], l2_ref[...], bg_ref[...], so_ref[...], ao_ref[...], co_ref[...],
            wbs, wba, wbc, d)
        mg_ref[...] = (gs * ys + ga * ya + gc * yc).astype(BF16)
        dxb = dx_ref[...].astype(BF16)
        dxb_ref[...] = dxb
        dm = _dot_nt(dxb, wo_ref[0])
        first = pl.program_id(0) == 0
        for k, (y, g, w, dy_ref, db_ref) in enumerate((
                (ys, gs, wbs, dys_ref, dso_ref), (ya, ga, wba, dya_ref, dao_ref), (yc, gc, wbc, dyc_ref, dco_ref))):
            dl = dm * y * g * (1.0 - g)
            dl_ref[:, k * d:(k + 1) * d] = dl.astype(BF16)
            _acc_out(dbg_ref.at[:, k * d:(k + 1) * d], jnp.sum(dl, axis=0, keepdims=True), first)
            dy = (dm * g).astype(BF16)
            dy_ref[...] = dy
            db_ref[...] = _dot_nt(dy, w)

    bf = lambda w: jax.ShapeDtypeStruct((n, w), BF16)
    return _call(
        body,
        name="merge_bwd",
        grid=(n // tm,),
        in_specs=[row(d)] + acts + weights,
        out_specs=[row(3 * d), row(dss), row(da), row(dc), full(1, 3 * d),
                   row(d), row(d), row(d), row(d), row(d)],
        out_shape=[bf(3 * d), jax.ShapeDtypeStruct((n, dss), F32), jax.ShapeDtypeStruct((n, da), F32),
                   jax.ShapeDtypeStruct((n, dc), F32), jax.ShapeDtypeStruct((1, 3 * d), F32),
                   bf(d), bf(d), bf(d), bf(d), bf(d)],
        compiler_params=_params(),
    )(dx, proj, proj, proj, bg, so, ao, co, wbs, wba, wbc, wout)


def loss_head(y, target):
    n, d = y.shape
    tm = _tile(n, 512, 8)

    def body(y_ref, t_ref, dy_ref, l_ref):
        e = y_ref[...] - t_ref[...]
        dy_ref[...] = e * (1.0 / d)
        part = 0.5 * jnp.sum(jnp.sum(e * e, axis=-1, keepdims=True) * (1.0 / d), axis=0, keepdims=True)
        _acc_out(l_ref, part, pl.program_id(0) == 0)

    return _call(
        body,
        name="loss_head",
        grid=(n // tm,),
        in_specs=[pl.BlockSpec((tm, d), lambda i: (i, 0)), pl.BlockSpec((tm, d), lambda i: (i, 0))],
        out_specs=[pl.BlockSpec((tm, d), lambda i: (i, 0)), pl.BlockSpec((1, 1), lambda i: (0, 0))],
        out_shape=[jax.ShapeDtypeStruct((n, d), F32), jax.ShapeDtypeStruct((1, 1), F32)],
        compiler_params=_params(),
    )(y, target)


def _mesh_pos():
    return lax.axis_index("x"), lax.axis_index("y"), lax.axis_index("c")


ANY = pl.BlockSpec(memory_space=pl.ANY)


def _comm_sems(na):
    return [pltpu.SemaphoreType.DMA((na, 7)), pltpu.SemaphoreType.DMA((na, 7)), pltpu.SemaphoreType.DMA((na,))]


def _gather_plan(x_refs, out_refs, sems):
    na = len(x_refs)
    send_sems, recv_sems, local_sems = sems
    x, y, c = _mesh_pos()
    me, sibling = (x, y, c), (x, y, 1 - c)
    chips = [(1 - x, y), (x, 1 - y), (1 - x, 1 - y)]

    def slot(a, px, py, pc):
        return out_refs[a].at[4 * px + 2 * py + pc]

    def copy(a, k, block, to, src=None):
        return pltpu.make_async_remote_copy(
            src_ref=slot(a, *block) if src is None else src, dst_ref=slot(a, *block),
            send_sem=send_sems.at[a, k], recv_sem=recv_sems.at[a, k],
            device_id=to, device_id_type=pl.DeviceIdType.MESH)

    mine = [pltpu.make_async_copy(x_refs[a], slot(a, *me), local_sems.at[a]) for a in range(na)]
    first = []
    for a in range(na):
        first.append(copy(a, 0, me, sibling, src=x_refs[a]))
        first += [copy(a, 1 + j, me, (*chip, c), src=x_refs[a]) for j, chip in enumerate(chips)]

    def start():
        for cp in mine + first:
            cp.start()

    def finish():
        passed = []
        for j, chip in enumerate(chips):
            for a in range(na):
                copy(a, 1 + j, (*chip, c), me).wait_recv()
                fwd = copy(a, 4 + j, (*chip, c), sibling)
                fwd.start()
                passed.append(fwd)
        for a in range(na):
            copy(a, 0, sibling, me).wait_recv()
            for j, chip in enumerate(chips):
                copy(a, 4 + j, (*chip, 1 - c), me).wait_recv()
        for cp in first + passed:
            cp.wait_send()
        for cp in mine:
            cp.wait()

    return start, finish


def _gather_out(shards):
    return [jax.ShapeDtypeStruct((N_DEV,) + s.shape, s.dtype) for s in shards]


def all_gather(shards):
    na = len(shards)

    def body(*refs):
        start, finish = _gather_plan(refs[:na], refs[na:2 * na], refs[2 * na:])
        start()
        finish()

    return _call(
        body,
        name="all_gather",
        out_shape=_gather_out(shards),
        in_specs=[ANY] * na,
        out_specs=[ANY] * na,
        scratch_shapes=_comm_sems(na),
    )(*shards)


def _hosted_call(body, comm, *, name, grid, in_specs, out_specs, out_shape, scratch_shapes, args):
    if comm is None:
        res = _call(body, name=name, grid=grid, in_specs=in_specs, out_specs=out_specs, out_shape=out_shape,
                    scratch_shapes=scratch_shapes, compiler_params=_params())(*args)
        return res, []
    plan, arrays, c_out = comm
    n_in, n_out, n_scr, ci, co = len(in_specs), len(out_specs), len(scratch_shapes), len(arrays), len(c_out)

    def hosted(*refs):
        ins, cins = refs[:n_in], refs[n_in:n_in + ci]
        o0 = n_in + ci
        outs, couts = refs[o0:o0 + n_out], refs[o0 + n_out:o0 + n_out + co]
        s0 = o0 + n_out + co
        scr, sems = refs[s0:s0 + n_scr], refs[s0 + n_scr:]
        ids = [pl.program_id(ax) for ax in range(len(grid))]
        first = functools.reduce(jnp.logical_and, [i == 0 for i in ids])
        last = functools.reduce(jnp.logical_and, [i == g - 1 for i, g in zip(ids, grid)])
        start, finish = plan(cins, couts, sems)
        pl.when(first)(start)
        body(*ins, *outs, *scr)
        pl.when(last)(finish)

    res = _call(hosted, name=name + "_comm", grid=grid, in_specs=list(in_specs) + [ANY] * ci,
                out_specs=list(out_specs) + [ANY] * co, out_shape=list(out_shape) + list(c_out),
                scratch_shapes=list(scratch_shapes) + _comm_sems(max(ci, co)),
                compiler_params=_params())(*args, *arrays)
    return res[:n_out], res[n_out:]


def _exchange_plan(bcast=()):
    def plan(s_refs, r_refs, sems):
        na = len(s_refs)
        send_sems, recv_sems, local_sems = sems
        x, y, c = _mesh_pos()
        me = 4 * x + 2 * y + c

        def peer(k):
            px = (1 - x) if k & 4 else x
            py = (1 - y) if k & 2 else y
            pc = (1 - c) if k & 1 else c
            return (px, py, pc), 4 * px + 2 * py + pc

        def src_dst(a, pid, slot):
            return (s_refs[a] if a in bcast else s_refs[a].at[pid]), r_refs[a].at[slot]

        def copy(a, k):
            to, pid = peer(k)
            src, dst = src_dst(a, pid, me)
            return pltpu.make_async_remote_copy(
                src_ref=src, dst_ref=dst, send_sem=send_sems.at[a, k - 1], recv_sem=recv_sems.at[a, k - 1],
                device_id=to, device_id_type=pl.DeviceIdType.MESH)

        def arrival(a, k):
            _, pid = peer(k)
            src, dst = src_dst(a, pid, pid)
            return pltpu.make_async_remote_copy(
                src_ref=src, dst_ref=dst, send_sem=send_sems.at[a, k - 1], recv_sem=recv_sems.at[a, k - 1],
                device_id=(x, y, c), device_id_type=pl.DeviceIdType.MESH)

        mine = [pltpu.make_async_copy(*src_dst(a, me, me), local_sems.at[a]) for a in range(na)]
        sends = [copy(a, k) for k in range(1, N_DEV) for a in range(na)]

        def start():
            for cp in mine + sends:
                cp.start()

        def finish():
            for k in range(1, N_DEV):
                for a in range(na):
                    arrival(a, k).wait_recv()
            for cp in sends:
                cp.wait_send()
            for cp in mine:
                cp.wait()

        return start, finish

    return plan


def _exchange_out(slabs, bcast=()):
    return [jax.ShapeDtypeStruct(((N_DEV,) + s.shape) if a in bcast else s.shape, s.dtype)
            for a, s in enumerate(slabs)]


def grad_exchange(slabs, small):
    arrays = list(slabs) + [small]
    na = len(arrays)
    bcast = (na - 1,)

    def body(*refs):
        start, finish = _exchange_plan(bcast)(refs[:na], refs[na:2 * na], refs[2 * na:])
        start()
        finish()

    return _call(
        body,
        name="grad_exchange",
        out_shape=_exchange_out(arrays, bcast),
        in_specs=[ANY] * na,
        out_specs=[ANY] * na,
        scratch_shapes=_comm_sems(na),
    )(*arrays)


ADAM_BLOCK = 256 * 1024


def sum_adamw(recv, w, m, v, layer, prev, name):
    depth, rows, cols = w.shape
    tr = _tile(rows, max(8, ADAM_BLOCK // cols // 8 * 8), 8)
    c1 = 1.0 / (1.0 - ADAM_B1 ** ADAM_STEP)
    c2 = 1.0 / (1.0 - ADAM_B2 ** ADAM_STEP)

    def body(r_ref, w_ref, m_ref, v_ref, *rest):
        g_ref, d_ref, mo_ref, vo_ref = rest[-4:]
        g = r_ref[0].astype(F32)
        for s in range(1, N_DEV):
            g = g + r_ref[s].astype(F32)
        mn = ADAM_B1 * m_ref[0] + (1.0 - ADAM_B1) * g
        vn = ADAM_B2 * v_ref[0] + (1.0 - ADAM_B2) * (g * g)
        g_ref[0] = g
        mo_ref[0] = mn
        vo_ref[0] = vn
        d_ref[0] = -ADAM_LR * ((mn * c1) / (jnp.sqrt(vn * c2) + ADAM_EPS) + ADAM_WD * w_ref[0])

    blk = pl.BlockSpec((1, tr, cols), lambda i: (layer, i, 0))
    out = jax.ShapeDtypeStruct((depth, rows, cols), F32)
    in_specs = [pl.BlockSpec((N_DEV, tr, cols), lambda i: (0, i, 0)), blk, blk, blk]
    args = [recv, w, m, v]
    extra = {}
    if prev is not None:
        in_specs += [ANY] * 4
        args += list(prev)
        extra["input_output_aliases"] = {4 + j: j for j in range(4)}
    return _call(
        body,
        name=name,
        grid=(rows // tr,),
        in_specs=in_specs,
        out_specs=[blk, blk, blk, blk],
        out_shape=[out, out, out, out],
        compiler_params=_params(),
        **extra,
    )(*args)


def _attn_bias_vector(rel_bias):
    h = rel_bias.shape[0]
    n_far = BAND - MAX_REL
    n_near = BAND + CHUNK - 1 - n_far
    far = jnp.broadcast_to(rel_bias[:, 2 * MAX_REL:], (h, n_far))
    near = rel_bias[:, 2 * MAX_REL - n_near:2 * MAX_REL][:, ::-1]
    pad = jnp.zeros((h, 2 * QBLK - n_far - n_near), F32)
    return jnp.concatenate([far, near, pad], axis=1)[:, None, :]


def _s5_prepare(lre, lim, ldt, bre, bim, cre, cim):
    g, p = lre.shape
    lr = jnp.minimum(lre, -1e-4)
    dt = jnp.exp(ldt)[:, None]
    mag = jnp.exp(lr * dt)
    ar = mag * jnp.cos(lim * dt)
    ai = mag * jnp.sin(lim * dt)
    den = lr * lr + lim * lim
    coef_r = ((ar - 1.0) * lr + ai * lim) / den
    coef_i = (ai * lr - (ar - 1.0) * lim) / den
    bbar_r = coef_r[..., None] * bre - coef_i[..., None] * bim
    bbar_i = coef_r[..., None] * bim + coef_i[..., None] * bre
    eye = jnp.eye(g, dtype=F32)
    bd_in = lambda b: jnp.einsum("gpc,gh->gchp", b, eye).reshape(g * S5_GROUP, g * p)
    bd_out = lambda c: jnp.einsum("gcp,gh->gphc", c, eye).reshape(g * p, g * S5_GROUP)
    return (ar.reshape(1, g * p), ai.reshape(1, g * p), bd_in(bbar_r), bd_in(bbar_i), bd_out(cre), bd_out(cim))


SHARDED = ("ffn1_w_up", "ffn1_w_down", "w_in", "s5_w_glu", "w_br_s5", "w_br_attn", "conv_w_dw", "w_br_conv",
           "w_out", "ffn2_w_up", "ffn2_w_down")
WEIGHTS = ("ffn1_norm", "ffn1_w_up", "ffn1_w_down", "mix_norm", "w_in", "b_gate", "s5_lambda_re", "s5_lambda_im",
           "s5_log_dt", "s5_b_re", "s5_b_im", "s5_c_re", "s5_c_im", "s5_d", "s5_w_glu", "w_br_s5", "attn_q_gain",
           "attn_k_gain", "attn_rel_bias", "w_br_attn", "conv_w_dw", "conv_b_dw", "conv_ln_g", "conv_ln_b",
           "w_br_conv", "w_out", "ffn2_norm", "ffn2_w_up", "ffn2_w_down")
SMALL = tuple(nm for nm in WEIGHTS if nm not in SHARDED)
SMALL_LANES = 1024


def _cols_full(g):
    _, depth, k, nn = g.shape
    return g.transpose(1, 2, 0, 3).reshape(depth, k, N_DEV * nn)


def _rows_full(g):
    _, depth, r, cc = g.shape
    return g.transpose(1, 0, 2, 3).reshape(depth, N_DEV * r, cc)


def _cols_slabs(gfull):
    depth, k, c8 = gfull.shape
    return gfull.reshape(depth, k, N_DEV, c8 // N_DEV).transpose(0, 2, 1, 3)


def _small_pack(t):
    flat = jnp.concatenate([t[nm].reshape(-1) for nm in SMALL])
    rows = -(-flat.shape[0] // SMALL_LANES)
    rows = -(-rows // 8) * 8
    return jnp.pad(flat, (0, rows * SMALL_LANES - flat.shape[0])).reshape(rows, SMALL_LANES)


def _small_unpack(flat, like):
    flat = flat.reshape(-1)
    out, off = {}, 0
    for nm in SMALL:
        out[nm] = flat[off:off + like[nm].size].reshape(like[nm].shape)
        off += like[nm].size
    return out


def _step(x, target, w, m, v):
    bsz, seq, d = x.shape
    n = bsz * seq
    depth = w["ffn1_norm"].shape[0]
    da, dss, dc = d // 2, d // 4, d // 4
    gp =dss // S5_GROUP * S5_STATE
    mid = 3 * da + 2 * dc
    q0, k0, v0, z0, u0 = 3 * d, 3 * d + da, 3 * d + 2 * da, 3 * d + 3 * da, 3 * d + mid
    to_kernel_cols = lambda a: jnp.concatenate([a[..., dss + mid:], a[..., dss:dss + mid], a[..., :dss]], axis=-1)
    to_ref_cols = lambda a: jnp.concatenate([a[..., 3 * d + mid:], a[..., 3 * d:3 * d + mid], a[..., :3 * d]], axis=-1)

    def shard(key):
        nm, sl = (key[0], slice(key[1], key[1] + 1)) if isinstance(key, tuple) else (key, slice(None))
        return w[nm][sl] if nm == "conv_w_dw" else w[nm][sl].astype(BF16)

    mixer_names = ("w_in", "s5_w_glu", "w_br_s5", "w_br_attn", "w_br_conv", "conv_w_dw", "w_out")
    ffn1_of = lambda l: (("ffn1_w_up", l), ("ffn1_w_down", l))
    later_ffn1 = tuple(k for l in range(1, depth) for k in ffn1_of(l))
    gather_of = lambda keys: (_gather_plan, [shard(k) for k in keys], _gather_out([shard(k) for k in keys]))
    full = {}

    def take(keys, arrays):
        for key, g in zip(keys, arrays):
            nm = key[0] if isinstance(key, tuple) else key
            if nm in ("ffn1_w_up", "ffn2_w_up"):
                full[key] = g
            elif nm in ("ffn1_w_down", "ffn2_w_down", "w_out"):
                full[key] = _rows_full(g)
            elif nm == "w_in":
                full[key] = to_kernel_cols(_cols_full(g))
            else:
                full[key] = _cols_full(g)

    take(ffn1_of(0), all_gather([shard(k) for k in ffn1_of(0)]))
    nff = full[("ffn1_w_up", 0)].shape[3]

    row = lambda a: a.reshape(1, -1)
    saved = []
    xin = x.reshape(n, d)
    for l in range(depth):
        s = {"x0": xin}
        f1_up, f1_down = ffn1_of(l)
        (s["x1"], s["a1"], s["b1"]), got = ffn_fwd(xin, row(w["ffn1_norm"][l]), full[f1_up], full[f1_down], 0,
                                                   comm=gather_of(mixer_names) if l == 0 else None)
        if l == 0:
            take(mixer_names, got)
            conv_w = jnp.pad(full["conv_w_dw"], ((0, 0), (0, HALO - CONV_W), (0, 0)))
        (proj,), got = proj_fwd(s["x1"], row(w["mix_norm"][l]), full["w_in"], l,
                                comm=gather_of(later_ffn1) if l == 0 and later_ffn1 else None)
        if l == 0:
            take(later_ffn1, got)
        s["proj"] = proj
        prep_in = (w["s5_lambda_re"][l], w["s5_lambda_im"][l], w["s5_log_dt"][l], w["s5_b_re"][l], w["s5_b_im"][l],
                   w["s5_c_re"][l], w["s5_c_im"][l])
        (ar, ai, bm_r, bm_i, cm_r, cm_i), s["prep_vjp"] = jax.vjp(_s5_prepare, *prep_in)
        s["s5p"] = (ar, ai, bm_r.astype(BF16), bm_i.astype(BF16), cm_r.astype(BF16), cm_i.astype(BF16),
                    row(w["s5_d"][l]), full["s5_w_glu"], l)
        (s["so"], s["xr"], s["xi"], s["ypre"]), got = s5_fwd(proj, u0 // dss, seq, *s["s5p"],
                                                             comm=gather_of(("ffn2_w_down",)) if l == 0 else None)
        if l == 0:
            take(("ffn2_w_down",), got)
        tv, s["tv_vjp"] = jax.vjp(_attn_bias_vector, w["attn_rel_bias"][l])
        pair_gain = lambda g: jnp.tile(row(g), (1, 2))
        s["attnp"] = (q0, k0, v0, tv, pair_gain(w["attn_q_gain"][l]), pair_gain(w["attn_k_gain"][l]))
        (s["ao"],), got = attn_fwd(proj, *s["attnp"], seq,
                                   comm=gather_of(("ffn2_w_up",)) if l == 0 else None)
        if l == 0:
            take(("ffn2_w_up",), got)
        s["convp"] = (conv_w[l], row(w["conv_b_dw"][l]), row(w["conv_ln_g"][l]), row(w["conv_ln_b"][l]))
        s["h1"], s["co"] = conv_fwd(proj, z0 // (2 * dc), seq, *s["convp"])
        s["mergep"] = (row(w["b_gate"][l]), s["so"], s["ao"], s["co"], full["w_br_s5"], full["w_br_attn"],
                       full["w_br_conv"], full["w_out"], l)
        s["x2"] = merge_fwd(s["x1"], proj, *s["mergep"])
        (xin, s["a2"], s["b2"]), _ = ffn_fwd(s["x2"], row(w["ffn2_norm"][l]), full["ffn2_w_up"], full["ffn2_w_down"], l)
        saved.append(s)

    dx, loss = loss_head(xin, target.reshape(n, d))
    loss = lax.psum(loss[0, 0], ("x", "y", "c"))

    small_g = {nm: [None] * depth for nm in SMALL}
    slab = {}

    recv = {}

    def exchange_of(names, l):
        arrays = [slab.pop(nm) for nm in names]
        return [(nm, l) for nm in names], (_exchange_plan(), arrays, _exchange_out(arrays))

    def ffn_grads(which, l, hn, dyb, dab_a, dab_b, act, send_up_now=False):
        up, down = which + "_w_up", which + "_w_down"
        up4, down4 = (1, N_DEV, d, nff), (1, FF_CHUNKS, nff, d)
        half = wgrad(hn, dab_a, "wg_ffn_up", up4, "col")
        slab[up] = wgrad(hn, dab_b, "wg_ffn_up", up4, "col", g0=FF_CHUNKS, prev=half).reshape(up4[1:])
        if send_up_now:
            keys, comm = exchange_of((up,), l)
            (dn,), got = wgrad(act, dyb, "wg_ffn_down", down4, "row", comm=comm)
            recv.update(zip(keys, got))
        else:
            dn = wgrad(act, dyb, "wg_ffn_down", down4, "row")
        slab[down] = dn.reshape(N_DEV, nff // 2, d)

    pending = None
    for l in reversed(range(depth)):
        s = saved[l]
        outs, got = ffn_bwd(dx, s["x2"], row(w["ffn2_norm"][l]), s["a2"], s["b2"],
                            full["ffn2_w_up"], full["ffn2_w_down"], l, comm=pending[1] if pending else None)
        if pending:
            recv.update(zip(pending[0], got))
        dx, dg, hn, dyb, dab_a, dab_b, act = outs
        small_g["ffn2_norm"][l] = dg
        ffn_grads("ffn2", l, hn, dyb, dab_a, dab_b, act)

        dlog, dso, dao, dco, dbg, mg, dxb, dys, dya, dyc = merge_bwd(dx, s["proj"], *s["mergep"])
        small_g["b_gate"][l] = dbg
        slab["w_out"] = wgrad(mg, dxb, "wg_out", (1, N_DEV, d // N_DEV, d), "row").reshape(N_DEV, d // N_DEV, d)
        for nm, act_in, dy_br in (("w_br_s5", s["so"], dys), ("w_br_attn", s["ao"], dya), ("w_br_conv", s["co"], dyc)):
            k_in = act_in.shape[1]
            slab[nm] = wgrad(act_in, dy_br, "wg_" + nm, (1, N_DEV, k_in, d // N_DEV), "col").reshape(
                N_DEV, k_in, d // N_DEV)

        dz, dwdw, dbdw, dlng, dlnb = conv_bwd(dco, s["h1"], s["proj"], z0 // (2 * dc), seq, *s["convp"])
        slab["conv_w_dw"] = _cols_slabs(dwdw[None, :CONV_W])[0].astype(BF16)
        small_g["conv_b_dw"][l], small_g["conv_ln_g"][l], small_g["conv_ln_b"][l] = dbdw, dlng, dlnb

        keys, comm = exchange_of(("ffn2_w_up", "ffn2_w_down"), l)
        outs, got = attn_bwd(dao, s["proj"], *s["attnp"], seq, comm=comm)
        recv.update(zip(keys, got))
        dq, dk, dv, dtv, dgq, dgk = outs
        small_g["attn_q_gain"][l] = dgq[:, :HEAD] + dgq[:, HEAD:]
        small_g["attn_k_gain"][l] = dgk[:, :HEAD] + dgk[:, HEAD:]
        small_g["attn_rel_bias"][l] = s["tv_vjp"](dtv)[0]

        du, gr, gi, dyb5, glb, dzb, dar, dai, dd = s5_bwd(dso, s["ypre"], s["proj"], u0 // dss, s["xr"], s["xi"],
                                                          seq, *s["s5p"])
        small_g["s5_d"][l] = dd
        one = lambda k1, k2: (1, 1, k1, k2)
        slab["s5_w_glu"] = _cols_slabs(wgrad(glb, dzb, "wg_s5_glu", one(dss, 2 * dss), "col")[0])[0]
        dcm_r = wgrad(s["xr"], dyb5, "wg_s5_c", one(gp, dss), "col", dtype=F32)[0, 0]
        dcm_i = -wgrad(s["xi"], dyb5, "wg_s5_c", one(gp, dss), "col", dtype=F32)[0, 0]
        dbm_r = wgrad(s["proj"], gr, "wg_s5_b", one(dss, gp), "col", a_cols=(u0, dss), dtype=F32)[0, 0]
        dbm_i = wgrad(s["proj"], gi, "wg_s5_b", one(dss, gp), "col", a_cols=(u0, dss), dtype=F32)[0, 0]
        pg = s["prep_vjp"]((dar, dai, dbm_r, dbm_i, dcm_r, dcm_i))
        for nm, gval in zip(("s5_lambda_re", "s5_lambda_im", "s5_log_dt", "s5_b_re", "s5_b_im", "s5_c_re", "s5_c_im"), pg):
            small_g[nm][l] = gval

        dproj = jnp.concatenate([dlog, dq.astype(BF16), dk.astype(BF16), dv.astype(BF16),
                                 dz.astype(BF16), du.astype(BF16)], axis=1)
        dx, dgm, hn = proj_bwd(dproj, dx, s["x1"], row(w["mix_norm"][l]), full["w_in"], l)
        small_g["mix_norm"][l] = dgm
        slab["w_in"] = _cols_slabs(to_ref_cols(wgrad(hn, dproj, "wg_in", one(d, 3 * d + mid + dss), "col")[0]))[0]

        keys, comm = exchange_of(mixer_names, l)
        f1_up, f1_down = ffn1_of(l)
        outs, got = ffn_bwd(dx, s["x0"], row(w["ffn1_norm"][l]), s["a1"], s["b1"],
                            full[f1_up], full[f1_down], 0, comm=comm)
        recv.update(zip(keys, got))
        dx, dg, hn, dyb, dab_a, dab_b, act = outs
        small_g["ffn1_norm"][l] = dg
        ffn_grads("ffn1", l, hn, dyb, dab_a, dab_b, act, send_up_now=(l == 0))
        pending = exchange_of(("ffn1_w_down",) if l == 0 else ("ffn1_w_up", "ffn1_w_down"), l)

    small_flat = _small_pack({nm: jnp.stack([g.reshape(w[nm].shape[1:]) for g in small_g[nm]]) for nm in SMALL})
    *got, recv_small = grad_exchange(pending[1][1], small_flat)
    recv.update(zip(pending[0], got))

    outs = {}
    for nm in SHARDED:
        shp = w[nm].shape
        as3 = lambda t: t.reshape(shp[0], -1, shp[-1])
        bufs = None
        for l in reversed(range(depth)):
            bufs = sum_adamw(recv[(nm, l)], as3(w[nm]), as3(m[nm]), as3(v[nm]), l, bufs, "adamw_" + nm)
        outs[nm] = [b.reshape(shp) for b in bufs]
    packed = sum_adamw(recv_small, _small_pack(w)[None], _small_pack(m)[None], _small_pack(v)[None], 0, None,
                       "adamw_small")
    unpacked = [_small_unpack(p, w) for p in packed]
    for nm in SMALL:
        outs[nm] = [u[nm] for u in unpacked]
    return loss, dx.reshape(x.shape), outs


def kernel(x, ffn1_norm, ffn1_w_up, ffn1_w_down, mix_norm, w_in, b_gate, s5_lambda_re, s5_lambda_im, s5_log_dt, s5_b_re, s5_b_im, s5_c_re, s5_c_im, s5_d, s5_w_glu, w_br_s5, attn_q_gain, attn_k_gain, attn_rel_bias, w_br_attn, conv_w_dw, conv_b_dw, conv_ln_g, conv_ln_b, w_br_conv, w_out, ffn2_norm, ffn2_w_up, ffn2_w_down, loss_target, m_ffn1_norm, m_ffn1_w_up, m_ffn1_w_down, m_mix_norm, m_w_in, m_b_gate, m_s5_lambda_re, m_s5_lambda_im, m_s5_log_dt, m_s5_b_re, m_s5_b_im, m_s5_c_re, m_s5_c_im, m_s5_d, m_s5_w_glu, m_w_br_s5, m_attn_q_gain, m_attn_k_gain, m_attn_rel_bias, m_w_br_attn, m_conv_w_dw, m_conv_b_dw, m_conv_ln_g, m_conv_ln_b, m_w_br_conv, m_w_out, m_ffn2_norm, m_ffn2_w_up, m_ffn2_w_down, v_ffn1_norm, v_ffn1_w_up, v_ffn1_w_down, v_mix_norm, v_w_in, v_b_gate, v_s5_lambda_re, v_s5_lambda_im, v_s5_log_dt, v_s5_b_re, v_s5_b_im, v_s5_c_re, v_s5_c_im, v_s5_d, v_s5_w_glu, v_w_br_s5, v_attn_q_gain, v_attn_k_gain, v_attn_rel_bias, v_w_br_attn, v_conv_w_dw, v_conv_b_dw, v_conv_ln_g, v_conv_ln_b, v_w_br_conv, v_w_out, v_ffn2_norm, v_ffn2_w_up, v_ffn2_w_down):
    args = locals()
    w = {nm: args[nm] for nm in WEIGHTS}
    m = {nm: args["m_" + nm] for nm in WEIGHTS}
    v = {nm: args["v_" + nm] for nm in WEIGHTS}
    loss, gx, outs = _step(x, loss_target, w, m, v)
    return (loss, gx, *[outs[nm][0] for nm in WEIGHTS], *[outs[nm][1] for nm in WEIGHTS],
            *[outs[nm][2] for nm in WEIGHTS], *[outs[nm][3] for nm in WEIGHTS])
```
